```python
import math
import jax, jax.numpy as jnp
from jax import lax
import numpy as np

D_MODEL = 1024
BATCH = 4
SEQ = 4096
DEPTH = 2

D_MIX = D_MODEL
GLA_WIDTH = D_MODEL // 4
N_GLA_HEADS = 4
GLA_DV = GLA_WIDTH // N_GLA_HEADS
GLA_DK = GLA_DV // 2
GLA_Q = N_GLA_HEADS * GLA_DK
GLA_V = N_GLA_HEADS * GLA_DV
GLA_RANK = 16
GATE_TEMP = 16.0
GLA_CHUNK = 64
DIFF_WIDTH = D_MODEL // 2
DIFF_DQK = 64
N_DIFF_HEADS = DIFF_WIDTH // (2 * DIFF_DQK)
DIFF_DV = 2 * DIFF_DQK
DIFF_QK = N_DIFF_HEADS * 2 * DIFF_DQK
DIFF_V = N_DIFF_HEADS * DIFF_DV
Q_BLOCK = 128
NUM_BUCKETS = 32
MAX_DISTANCE = 128
CONV_CH = D_MODEL // 4
CONV_K = 31
D_FF = 2816
PLE_DIM = 256
NORM_EPS = 1e-6
IN_SPLIT_SIZES = (GLA_Q, GLA_Q, GLA_V, GLA_V, GLA_RANK, GLA_RANK,
                  DIFF_QK, DIFF_QK, DIFF_V, CONV_CH, CONV_CH)
D_IN = GLA_Q * 2 + GLA_V * 2 + GLA_RANK * 2 + DIFF_QK * 2 + DIFF_V + CONV_CH * 2

kernel_name = "hybrid_gla_diffattn_conformer_encoder"


def rms_norm(x, g):
    xf = x.astype(jnp.float32)
    y = xf * lax.rsqrt(jnp.mean(xf * xf, axis=-1, keepdims=True) + NORM_EPS)
    return (y * g.astype(jnp.float32)).astype(x.dtype)


def layer_norm(x, g, b):
    xf = x.astype(jnp.float32)
    mu = jnp.mean(xf, axis=-1, keepdims=True)
    xc = xf - mu
    y = xc * lax.rsqrt(jnp.mean(xc * xc, axis=-1, keepdims=True) + NORM_EPS)
    return (y * g.astype(jnp.float32) + b.astype(jnp.float32)).astype(x.dtype)


def swiglu(h, w_gate, w_up, w_down):
    return (jax.nn.silu(h @ w_gate) * (h @ w_up)) @ w_down


def gla_chunked(q, k, v, g):
    B, H, L, dk = q.shape
    dv = v.shape[-1]
    n = L // GLA_CHUNK
    q = q.reshape(B, H, n, GLA_CHUNK, dk)
    k = k.reshape(B, H, n, GLA_CHUNK, dk)
    v = v.reshape(B, H, n, GLA_CHUNK, dv)
    g = g.reshape(B, H, n, GLA_CHUNK, dk)
    b = jnp.cumsum(g, axis=3)
    b_last = b[:, :, :, -1:, :]
    q_t = q * jnp.exp(b)
    k_t = k * jnp.exp(-b)
    mask = jnp.tril(jnp.ones((GLA_CHUNK, GLA_CHUNK), dtype=bool))
    a = jnp.einsum('bhncd,bhnjd->bhncj', q_t, k_t)
    a = jnp.where(mask, a, 0.0)
    o_intra = jnp.einsum('bhncj,bhnje->bhnce', a, v)
    k_end = k * jnp.exp(b_last - b)
    d_state = jnp.einsum('bhncd,bhnce->bhnde', k_end, v)
    decay = jnp.exp(b_last[:, :, :, 0, :])

    def step(state, inp):
        dcy, ds = inp
        return dcy[..., None] * state + ds, state

    s0 = jnp.zeros((B, H, dk, dv), jnp.float32)
    _, s_prev = lax.scan(step, s0, (jnp.moveaxis(decay, 2, 0), jnp.moveaxis(d_state, 2, 0)))
    s_prev = jnp.moveaxis(s_prev, 0, 2)
    o_inter = jnp.einsum('bhncd,bhnde->bhnce', q_t, s_prev)
    return (o_intra + o_inter).reshape(B, H, L, dv)


def t5_bucket(rel):
    nb = NUM_BUCKETS // 2
    max_exact = nb // 2
    ret = (rel > 0).astype(jnp.int32) * nb
    n = jnp.abs(rel)
    large = max_exact + (jnp.log(jnp.maximum(n, 1).astype(jnp.float32) / max_exact)
                         / math.log(MAX_DISTANCE / max_exact) * (nb - max_exact)).astype(jnp.int32)
    large = jnp.minimum(large, nb - 1)
    return ret + jnp.where(n < max_exact, n, large)


def diff_attention(q, k, v, rel_bias, lam):
    B, L, H, _, dq = q.shape
    nblk = L // Q_BLOCK
    qb = jnp.moveaxis(q.reshape(B, nblk, Q_BLOCK, H, 2, dq), 1, 0)
    kpos = jnp.arange(L, dtype=jnp.int32)
    vf = v.astype(jnp.float32)

    def one_block(args):
        blk, q_blk = args
        qpos = blk * Q_BLOCK + jnp.arange(Q_BLOCK, dtype=jnp.int32)
        bias = rel_bias[t5_bucket(kpos[None, :] - qpos[:, None])].astype(jnp.float32)
        s = jnp.einsum('bqhmd,bkhmd->bhmqk', q_blk, k).astype(jnp.float32)
        s = s + jnp.transpose(bias, (2, 3, 0, 1))[None]
        pr = jax.nn.softmax(s, axis=-1)
        w = pr[:, :, 0] - lam * pr[:, :, 1]
        return jnp.einsum('bhqk,bkhd->bqhd', w, vf)

    o = lax.map(one_block, (jnp.arange(nblk, dtype=jnp.int32), qb))
    return jnp.moveaxis(o, 0, 1).reshape(B, L, H, -1)


def conformer_conv(a, gate, w, b, g, beta):
    u = a * jax.nn.sigmoid(gate)
    c = lax.conv_general_dilated(u, w[:, None, :].astype(u.dtype), window_strides=(1,),
                                 padding=[(CONV_K // 2, CONV_K // 2)],
                                 dimension_numbers=('NWC', 'WIO', 'NWC'),
                                 feature_group_count=CONV_CH) + b
    return jax.nn.silu(layer_norm(c, g, beta))


def setup_inputs(seed: int = 0) -> dict:
    key = jax.random.key(seed)
    ks = iter(jax.random.split(key, 40))

    def nrm(shape, scale):
        return jax.random.normal(next(ks), shape, jnp.float32) * scale

    def gain(shape):
        return 1.0 + 0.1 * jax.random.normal(next(ks), shape, jnp.float32)

    L = DEPTH
    return {
        "x": nrm((BATCH, SEQ, D_MODEL), 1.0),
        "p": nrm((DEPTH, BATCH, SEQ, PLE_DIM), 1.0),
        "ffn1_norm": gain((L, D_MODEL)),
        "ffn1_w_gate": nrm((L, D_MODEL, D_FF), D_MODEL ** -0.5),
        "ffn1_w_up": nrm((L, D_MODEL, D_FF), D_MODEL ** -0.5),
        "ffn1_w_down": nrm((L, D_FF, D_MODEL), D_FF ** -0.5),
        "mix_norm": gain((L, D_MODEL)),
        "w_in": nrm((L, D_MODEL, D_IN), D_MODEL ** -0.5),
        "w_out": nrm((L, D_MIX, D_MODEL), D_MIX ** -0.5),
        "gla_gk_up_f": nrm((L, GLA_RANK, GLA_Q), GLA_RANK ** -0.5),
        "gla_gk_bias_f": nrm((L, GLA_Q), 0.1),
        "gla_gk_up_b": nrm((L, GLA_RANK, GLA_Q), GLA_RANK ** -0.5),
        "gla_gk_bias_b": nrm((L, GLA_Q), 0.1),
        "gla_out_norm": gain((L, GLA_DV)),
        "diff_q_norm": gain((L, DIFF_DQK)),
        "diff_k_norm": gain((L, DIFF_DQK)),
        "diff_lambda": nrm((L, 4, DIFF_DQK), 0.1),
        "diff_out_norm": gain((L, DIFF_DV)),
        "rel_bias": nrm((NUM_BUCKETS, N_DIFF_HEADS, 2), 0.5),
        "conv_dw_w": nrm((L, CONV_K, CONV_CH), CONV_K ** -0.5),
        "conv_dw_b": nrm((L, CONV_CH), 0.02),
        "conv_norm_g": gain((L, CONV_CH)),
        "conv_norm_b": nrm((L, CONV_CH), 0.02),
        "ffn2_norm": gain((L, D_MODEL)),
        "ffn2_w_gate": nrm((L, D_MODEL, D_FF), D_MODEL ** -0.5),
        "ffn2_w_up": nrm((L, D_MODEL, D_FF), D_MODEL ** -0.5),
        "ffn2_w_down": nrm((L, D_FF, D_MODEL), D_FF ** -0.5),
        "ple_norm": gain((L, D_MODEL)),
        "ple_w_gate": nrm((L, D_MODEL, D_MODEL), D_MODEL ** -0.5),
        "ple_w_proj": nrm((L, PLE_DIM, D_MODEL), PLE_DIM ** -0.5),
    }


def reference(x, p, ffn1_norm, ffn1_w_gate, ffn1_w_up, ffn1_w_down, mix_norm, w_in, w_out,
              gla_gk_up_f, gla_gk_bias_f, gla_gk_up_b, gla_gk_bias_b, gla_out_norm,
              diff_q_norm, diff_k_norm, diff_lambda, diff_out_norm, rel_bias,
              conv_dw_w, conv_dw_b, conv_norm_g, conv_norm_b,
              ffn2_norm, ffn2_w_gate, ffn2_w_up, ffn2_w_down,
              ple_norm, ple_w_gate, ple_w_proj):
    B, L, _ = x.shape
    split_idx = [int(v) for v in np.cumsum(IN_SPLIT_SIZES)[:-1]]

    def gla_heads(t, d):
        return jnp.transpose(t.reshape(B, L, N_GLA_HEADS, d), (0, 2, 1, 3)).astype(jnp.float32)

    for i in range(DEPTH):
        x = x + 0.5 * swiglu(rms_norm(x, ffn1_norm[i]), ffn1_w_gate[i], ffn1_w_up[i], ffn1_w_down[i])

        h = rms_norm(x, mix_norm[i])
        proj = h @ w_in[i]
        (g_q, g_k, g_v, g_o, g_low_f, g_low_b,
         d_q, d_k, d_v, c_a, c_g) = jnp.split(proj, split_idx, axis=-1)

        qh = gla_heads(g_q, GLA_DK) * (GLA_DK ** -0.5)
        kh = gla_heads(g_k, GLA_DK)
        vh = gla_heads(g_v, GLA_DV)
        log_a_f = gla_heads(jax.nn.log_sigmoid(g_low_f @ gla_gk_up_f[i] + gla_gk_bias_f[i]), GLA_DK) / GATE_TEMP
        log_a_b = gla_heads(jax.nn.log_sigmoid(g_low_b @ gla_gk_up_b[i] + gla_gk_bias_b[i]), GLA_DK) / GATE_TEMP
        o_fwd = gla_chunked(qh, kh, vh, log_a_f)
        o_bwd = jnp.flip(gla_chunked(jnp.flip(qh, 2), jnp.flip(kh, 2), jnp.flip(vh, 2),
                                     jnp.flip(log_a_b, 2)), 2)
        o_gla = jnp.transpose(o_fwd + o_bwd, (0, 2, 1, 3))
        y_gla = rms_norm(o_gla, gla_out_norm[i]).reshape(B, L, GLA_V) * jax.nn.silu(g_o.astype(jnp.float32))

        lam_init = 0.8 - 0.6 * math.exp(-0.3 * i)
        lp = diff_lambda[i].astype(jnp.float32)
        lam = jnp.exp(jnp.sum(lp[0] * lp[1])) - jnp.exp(jnp.sum(lp[2] * lp[3])) + lam_init
        dq = rms_norm(d_q.reshape(B, L, N_DIFF_HEADS, 2, DIFF_DQK), diff_q_norm[i]) * (DIFF_DQK ** -0.5)
        dk = rms_norm(d_k.reshape(B, L, N_DIFF_HEADS, 2, DIFF_DQK), diff_k_norm[i])
        dv = d_v.reshape(B, L, N_DIFF_HEADS, DIFF_DV)
        o_diff = diff_attention(dq, dk, dv, rel_bias, lam)
        y_diff = (rms_norm(o_diff, diff_out_norm[i]) * (1.0 - lam_init)).reshape(B, L, DIFF_V)

        y_conv = conformer_conv(c_a, c_g, conv_dw_w[i], conv_dw_b[i], conv_norm_g[i], conv_norm_b[i])

        y_mix = jnp.concatenate([y_gla.astype(x.dtype), y_diff.astype(x.dtype), y_conv.astype(x.dtype)], axis=-1)
        x = x + y_mix @ w_out[i]

        x = x + 0.5 * swiglu(rms_norm(x, ffn2_norm[i]), ffn2_w_gate[i], ffn2_w_up[i], ffn2_w_down[i])

        gate = jax.nn.sigmoid(rms_norm(x, ple_norm[i]) @ ple_w_gate[i])
        x = x + gate * (p[i] @ ple_w_proj[i])
    return x
```

```python
import functools
import math

import jax
import jax.numpy as jnp
from jax import lax
from jax.experimental import pallas as pl
from jax.experimental.pallas import tpu as pltpu

F32 = jnp.float32
BF16 = jnp.bfloat16

D_MODEL = 1024
N_GLA_HEADS = 4
GLA_DK = 32
GLA_DV = 64
GLA_Q = N_GLA_HEADS * GLA_DK
GLA_V = N_GLA_HEADS * GLA_DV
GLA_RANK = 16
GATE_TEMP = 16.0
GLA_CHUNK = 64
DIFF_DQK = 64
N_DIFF_HEADS = 4
DIFF_DV = 128
DIFF_QK = 512
DIFF_V = 512
NUM_BUCKETS = 32
MAX_DISTANCE = 128
CONV_CH = 256
CONV_K = 31
D_FF = 2816
PLE_DIM = 256
NORM_EPS = 1e-6
D_IN = 2848
D_IN_PAD = 2944

TOKEN_TILE = 512
FF_CHUNK = 1408
ATTN_TILE = 256
CONV_ROWS = 128
CONV_PAD = 16
VMEM_LIMIT = 56 * 1024 * 1024


def _cparams(sem):
    return pltpu.CompilerParams(dimension_semantics=sem, vmem_limit_bytes=VMEM_LIMIT)


def _dot(a, b):
    return jnp.dot(a, b, preferred_element_type=F32)


def _dot_nt(a, b):
    return lax.dot_general(a, b, (((1,), (1,)), ((), ())), preferred_element_type=F32)


def _rms_rows(x, gain):
    return x * lax.rsqrt(jnp.mean(x * x, axis=-1, keepdims=True) + NORM_EPS) * gain


def _sigmoid(x):
    return 1.0 / (1.0 + jnp.exp(-x))


def _silu(x):
    return x * _sigmoid(x)


def _const_spec(shape):
    return pl.BlockSpec(shape, lambda *_: (0,) * len(shape), pipeline_mode=pl.Buffered(1))


def _ffn_kernel(x_ref, g_ref, wg_ref, wu_ref, wd_ref, o_ref):
    x = x_ref[...]
    h = _rms_rows(x, g_ref[...]).astype(BF16)
    acc = jnp.zeros(x.shape, F32)
    for c0 in range(0, D_FF, FF_CHUNK):
        gate = _dot(h, wg_ref[:, c0:c0 + FF_CHUNK])
        up = _dot(h, wu_ref[:, c0:c0 + FF_CHUNK])
        a = (_silu(gate) * up).astype(BF16)
        acc = acc + _dot(a, wd_ref[c0:c0 + FF_CHUNK, :])
    o_ref[...] = x + 0.5 * acc


def _ffn(x, gain, wg, wu, wd):
    n = x.shape[0]
    row = pl.BlockSpec((TOKEN_TILE, D_MODEL), lambda i: (i, 0))
    return pl.pallas_call(
        _ffn_kernel,
        grid=(n // TOKEN_TILE,),
        in_specs=[row, _const_spec((1, D_MODEL)), _const_spec((D_MODEL, D_FF)),
                  _const_spec((D_MODEL, D_FF)), _const_spec((D_FF, D_MODEL))],
        out_specs=row,
        out_shape=jax.ShapeDtypeStruct((n, D_MODEL), F32),
        compiler_params=_cparams(("parallel",)),
        name="ffn",
    )(x, gain, wg, wu, wd)


def _block_diag_ones(n, blk, dtype):
    r = lax.broadcasted_iota(jnp.int32, (n, n), 0) // blk
    c = lax.broadcasted_iota(jnp.int32, (n, n), 1) // blk
    return jnp.where(r == c, 1.0, 0.0).astype(dtype)


def _group_mean_sq(x, blk):
    n = x.shape[-1]
    return _dot((x * x).astype(BF16), _block_diag_ones(n, blk, BF16)) * (1.0 / blk)


def _log_sigmoid(z):
    return jnp.minimum(z, 0.0) - jnp.log(1.0 + jnp.exp(-jnp.abs(z)))


def _inproj_kernel(x_ref, g_ref, w_ref, upf_ref, bf_ref, upb_ref, bb_ref, qn_ref, kn_ref,
                   gqk_ref, gg_ref, gv_ref, go_ref, dq_ref, dk_ref, dv_ref, cv_ref):
    h = _rms_rows(x_ref[...], g_ref[...]).astype(BF16)
    proj = _dot(h, w_ref[...])
    gqk_ref[:, :GLA_Q] = proj[:, :GLA_Q] * (GLA_DK ** -0.5)
    gqk_ref[:, GLA_Q:] = proj[:, GLA_Q:2 * GLA_Q]
    gv_ref[...] = proj[:, 256:512].astype(BF16)
    go_ref[...] = _silu(proj[:, 512:768]).astype(BF16)
    low = proj[:, 2816:2944].astype(BF16)
    zf = _dot(low, upf_ref[...]) + bf_ref[...]
    zb = _dot(low, upb_ref[...]) + bb_ref[...]
    gg_ref[:, :GLA_Q] = _log_sigmoid(zf) / GATE_TEMP
    gg_ref[:, GLA_Q:] = _log_sigmoid(zb) / GATE_TEMP
    dq = proj[:, 768:1280]
    dq_ref[...] = (dq * lax.rsqrt(_group_mean_sq(dq, DIFF_DQK) + NORM_EPS) * qn_ref[...]
                   * (DIFF_DQK ** -0.5)).astype(BF16)
    dk = proj[:, 1280:1792]
    dk_ref[...] = (dk * lax.rsqrt(_group_mean_sq(dk, DIFF_DQK) + NORM_EPS) * kn_ref[...]).astype(BF16)
    dv_ref[...] = proj[:, 1792:2304].astype(BF16)
    cv_ref[...] = proj[:, 2304:2816]


def _inproj(x, gain, w, upf, bf, upb, bb, qn, kn):
    n = x.shape[0]
    tm = TOKEN_TILE

    def row(width):
        return pl.BlockSpec((tm, width), lambda i: (i, 0))

    outs = [(2 * GLA_Q, F32), (2 * GLA_Q, F32), (GLA_V, BF16), (GLA_V, BF16),
            (DIFF_QK, BF16), (DIFF_QK, BF16), (DIFF_V, BF16), (2 * CONV_CH, F32)]
    return pl.pallas_call(
        _inproj_kernel,
        grid=(n // tm,),
        in_specs=[row(D_MODEL), _const_spec((1, D_MODEL)), _const_spec((D_MODEL, D_IN_PAD)),
                  _const_spec((128, GLA_Q)), _const_spec((1, GLA_Q)),
                  _const_spec((128, GLA_Q)), _const_spec((1, GLA_Q)),
                  _const_spec((1, DIFF_QK)), _const_spec((1, DIFF_QK))],
        out_specs=[row(wd) for wd, _ in outs],
        out_shape=[jax.ShapeDtypeStruct((n, wd), dt) for wd, dt in outs],
        compiler_params=_cparams(("parallel",)),
        name="inproj",
    )(x, gain, w, upf, bf, upb, bb, qn, kn)


def _gla_kernel(qk_ref, g_ref, v_ref, go_ref, on_ref, y_ref, st_ref):
    C = GLA_CHUNK
    L = qk_ref.shape[0]
    nchunk = L // C
    H = N_GLA_HEADS

    def iota(shape, d):
        return lax.broadcasted_iota(jnp.int32, shape, d)

    ri, ci = iota((C, C), 0), iota((C, C), 1)
    kmask = iota((H * C, GLA_Q), 0) // C == iota((H * C, GLA_Q), 1) // GLA_DK
    vmask = iota((H * C, GLA_V), 0) // C == iota((H * C, GLA_V), 1) // GLA_DV
    smask = iota((GLA_V, GLA_Q), 0) // GLA_DV == iota((GLA_V, GLA_Q), 1) // GLA_DK
    pos_a = iota((C, H * C), 0)
    pos_b = iota((C, H * C), 1) % C
    norm_bd = _block_diag_ones(GLA_V, GLA_DV, BF16)

    def direction(reverse):
        if reverse:
            cum = jnp.where(ci >= ri, 1.0, 0.0).astype(BF16)
            amask = pos_b >= pos_a
            edge = 0
        else:
            cum = jnp.where(ci <= ri, 1.0, 0.0).astype(BF16)
            amask = pos_b <= pos_a
            edge = C - 1
        goff = GLA_Q if reverse else 0

        def chunk(t, _):
            n = (nchunk - 1 - t) if reverse else t
            r0 = pl.multiple_of(n * C, C)
            rows = pl.ds(r0, C)
            q = qk_ref[rows, :GLA_Q]
            k = qk_ref[rows, GLA_Q:]
            g = g_ref[rows, goff:goff + GLA_Q]
            v = v_ref[rows, :].astype(F32)
            g_hi = g.astype(BF16)
            g_lo = (g - g_hi.astype(F32)).astype(BF16)
            b = _dot(cum, g_hi) + _dot(cum, g_lo)
            b_edge = b[edge:edge + 1, :]
            qt = (q * jnp.exp(b)).astype(BF16)
            kt = k * jnp.exp(-b)
            kend = (k * jnp.exp(b_edge - b)).astype(BF16)
            kstack = jnp.where(kmask, jnp.concatenate([kt] * H, axis=0), 0.0).astype(BF16)
            a = jnp.where(amask, _dot_nt(qt, kstack), 0.0).astype(BF16)
            vstack = jnp.where(vmask, jnp.concatenate([v] * H, axis=0), 0.0).astype(BF16)
            s_t = st_ref[...]
            o = _dot(a, vstack) + _dot_nt(qt, s_t.astype(BF16))
            ds = _dot(v.T.astype(BF16), kend)
            st_ref[...] = s_t * jnp.exp(b_edge) + jnp.where(smask, ds, 0.0)
            if reverse:
                y_ref[rows, :] = o
            else:
                tot = o + y_ref[rows, :]
                ms = _dot((tot * tot).astype(BF16), norm_bd) * (1.0 / GLA_DV)
                y_ref[rows, :] = (tot * lax.rsqrt(ms + NORM_EPS) * on_ref[...]
                                  * go_ref[rows, :].astype(F32))
            return 0

        st_ref[...] = jnp.zeros(st_ref.shape, F32)
        lax.fori_loop(0, nchunk, chunk, 0)

    direction(True)
    direction(False)


def _gla(qk, g, v, go, onorm, batch):
    n = qk.shape[0]
    L = n // batch

    def seq(width):
        return pl.BlockSpec((L, width), lambda b: (b, 0))

    return pl.pallas_call(
        _gla_kernel,
        grid=(batch,),
        in_specs=[seq(2 * GLA_Q), seq(2 * GLA_Q), seq(GLA_V), seq(GLA_V), _const_spec((1, GLA_V))],
        out_specs=seq(GLA_V),
        out_shape=jax.ShapeDtypeStruct((n, GLA_V), F32),
        scratch_shapes=[pltpu.VMEM((GLA_V, GLA_Q), F32)],
        compiler_params=_cparams(("parallel",)),
        name="gla",
    )(qk, g, v, go, onorm)


def _bias_kernel(rb_ref, o_ref):
    T = ATTN_TILE
    h = pl.program_id(0)
    nb = NUM_BUCKETS // 2
    max_exact = nb // 2
    for d in range(3):
        rel = (lax.broadcasted_iota(jnp.int32, (T, T), 1) - lax.broadcasted_iota(jnp.int32, (T, T), 0)
               + (d - 1) * T)
        ret = jnp.where(rel > 0, nb, 0)
        n = jnp.abs(rel)
        large = max_exact + (jnp.log(jnp.maximum(n, 1).astype(F32) / max_exact)
                             / math.log(MAX_DISTANCE / max_exact) * (nb - max_exact)).astype(jnp.int32)
        large = jnp.minimum(large, nb - 1)
        bucket = ret + jnp.where(n < max_exact, n, large)
        for m in range(2):
            tile = jnp.zeros((T, T), F32)
            for bkt in range(NUM_BUCKETS):
                tile = jnp.where(bucket == bkt, rb_ref[(bkt * N_DIFF_HEADS + h) * 2 + m], tile)
            o_ref[0, d, m] = tile


def _bias_tiles(rel_bias):
    T = ATTN_TILE
    return pl.pallas_call(
        _bias_kernel,
        grid=(N_DIFF_HEADS,),
        in_specs=[pl.BlockSpec(memory_space=pltpu.SMEM)],
        out_specs=pl.BlockSpec((1, 3, 2, T, T), lambda h: (h, 0, 0, 0, 0)),
        out_shape=jax.ShapeDtypeStruct((N_DIFF_HEADS, 3, 2, T, T), F32),
        compiler_params=_cparams(("parallel",)),
        name="t5_bias",
    )(rel_bias.reshape(-1))


def _attn_kernel(rb_ref, q_ref, k_ref, v_ref, band_ref, lam_ref, on_ref, o_ref,
                 m_ref, l_ref, acc_ref, *, lam_init):
    T = ATTN_TILE
    L = k_ref.shape[0]
    nk = L // T
    h = pl.program_id(1)
    qi = pl.program_id(2)
    q = q_ref[...]
    lane = lax.broadcasted_iota(jnp.int32, q.shape, 1)
    qm = [jnp.where(lane < DIFF_DQK, q, 0).astype(BF16), jnp.where(lane >= DIFF_DQK, q, 0).astype(BF16)]

    m_ref[...] = jnp.full(m_ref.shape, -jnp.inf, F32)
    l_ref[...] = jnp.zeros(l_ref.shape, F32)
    acc_ref[...] = jnp.zeros(acc_ref.shape, F32)

    def update(j, bias_of_map):
        r0 = pl.multiple_of(j * T, T)
        kt = k_ref[pl.ds(r0, T), :]
        vt = v_ref[pl.ds(r0, T), :]
        for m in range(2):
            s = _dot_nt(qm[m], kt) + bias_of_map(m)
            m_old = m_ref[m]
            m_new = jnp.maximum(m_old, jnp.max(s, axis=-1, keepdims=True))
            alpha = jnp.exp(m_old - m_new)
            p = jnp.exp(s - m_new)
            l_ref[m] = alpha * l_ref[m] + jnp.sum(p, axis=-1, keepdims=True)
            acc_ref[m] = alpha * acc_ref[m] + _dot(p.astype(BF16), vt)
            m_ref[m] = m_new

    def far(bucket):
        def body(j, _):
            update(j, lambda m: rb_ref[(bucket * N_DIFF_HEADS + h) * 2 + m])
            return 0
        return body

    lax.fori_loop(0, jnp.maximum(qi - 1, 0), far(NUM_BUCKETS // 2 - 1), 0)
    for d in range(3):
        j = qi + d - 1

        @pl.when(jnp.logical_and(j >= 0, j < nk))
        def _():
            update(j, lambda m: band_ref[0, d, m])

    lax.fori_loop(qi + 2, nk, far(NUM_BUCKETS - 1), 0)

    lp = lam_ref[...]
    lam = (jnp.exp(jnp.sum(lp[0:1] * lp[1:2], axis=-1, keepdims=True))
           - jnp.exp(jnp.sum(lp[2:3] * lp[3:4], axis=-1, keepdims=True)) + lam_init)
    o = acc_ref[0] / l_ref[0] - lam * (acc_ref[1] / l_ref[1])
    o_ref[...] = _rms_rows(o, on_ref[...]) * (1.0 - lam_init)


def _diff_attn(rel_bias_flat, dq, dk, dv, band, lam_p, onorm, batch, lam_init):
    n = dq.shape[0]
    L = n // batch
    T = ATTN_TILE
    nq = L // T
    kv_spec = pl.BlockSpec((L, DIFF_DV), lambda b, h, i: (b, h))
    return pl.pallas_call(
        functools.partial(_attn_kernel, lam_init=lam_init),
        grid=(batch, N_DIFF_HEADS, nq),
        in_specs=[pl.BlockSpec(memory_space=pltpu.SMEM),
                  pl.BlockSpec((T, 2 * DIFF_DQK), lambda b, h, i: (b * nq + i, h)),
                  kv_spec, kv_spec,
                  pl.BlockSpec((1, 3, 2, T, T), lambda b, h, i: (h, 0, 0, 0, 0)),
                  pl.BlockSpec((4, DIFF_DQK), lambda b, h, i: (0, 0)),
                  pl.BlockSpec((1, DIFF_DV), lambda b, h, i: (0, 0))],
        out_specs=pl.BlockSpec((T, DIFF_DV), lambda b, h, i: (b * nq + i, h)),
        out_shape=jax.ShapeDtypeStruct((n, DIFF_V), F32),
        scratch_shapes=[pltpu.VMEM((2, T, 1), F32), pltpu.VMEM((2, T, 1), F32),
                        pltpu.VMEM((2, T, DIFF_DV), F32)],
        compiler_params=_cparams(("parallel", "parallel", "parallel")),
        name="diff_attn",
    )(rel_bias_flat, dq, dk, dv, band, lam_p, onorm)


def _conv_kernel(cv_ref, w_ref, b_ref, g_ref, beta_ref, y_ref, u_ref):
    L = cv_ref.shape[0]
    R = CONV_ROWS
    P = CONV_PAD
    zeros = jnp.zeros((P, CONV_CH), F32)
    u_ref[:P, :] = zeros
    u_ref[P + L:, :] = zeros
    u_ref[P:P + L, :] = cv_ref[:, :CONV_CH] * _sigmoid(cv_ref[:, CONV_CH:])
    off = P - CONV_K // 2

    def tile(t, _):
        r0 = pl.multiple_of(t * R, R)
        win = u_ref[pl.ds(r0, R + 2 * P), :]
        shifted = [win] + [pltpu.roll(win, R + 2 * P - r, 0) for r in range(1, 8)]
        acc = jnp.zeros((R, CONV_CH), F32) + b_ref[...]
        for k in range(CONV_K):
            a8, r = divmod(off + k, 8)
            acc = acc + w_ref[k:k + 1, :] * shifted[r][8 * a8:8 * a8 + R, :]
        mu = jnp.mean(acc, axis=-1, keepdims=True)
        xc = acc - mu
        y = xc * lax.rsqrt(jnp.mean(xc * xc, axis=-1, keepdims=True) + NORM_EPS) * g_ref[...] + beta_ref[...]
        y_ref[pl.ds(r0, R), :] = _silu(y)
        return 0

    lax.fori_loop(0, L // R, tile, 0)


def _conv(cv, w, b, g, beta, batch):
    n = cv.shape[0]
    L = n // batch
    return pl.pallas_call(
        _conv_kernel,
        grid=(batch,),
        in_specs=[pl.BlockSpec((L, 2 * CONV_CH), lambda i: (i, 0)),
                  _const_spec((CONV_K + 1, CONV_CH)), _const_spec((1, CONV_CH)),
                  _const_spec((1, CONV_CH)), _const_spec((1, CONV_CH))],
        out_specs=pl.BlockSpec((L, CONV_CH), lambda i: (i, 0)),
        out_shape=jax.ShapeDtypeStruct((n, CONV_CH), F32),
        scratch_shapes=[pltpu.VMEM((L + 2 * CONV_PAD, CONV_CH), F32)],
        compiler_params=_cparams(("parallel",)),
        name="conv",
    )(cv, w, b, g, beta)


def _outproj_kernel(x_ref, ya_ref, yb_ref, yc_ref, w_ref, o_ref):
    acc = _dot(ya_ref[...].astype(BF16), w_ref[:GLA_V, :])
    acc = acc + _dot(yb_ref[...].astype(BF16), w_ref[GLA_V:GLA_V + DIFF_V, :])
    acc = acc + _dot(yc_ref[...].astype(BF16), w_ref[GLA_V + DIFF_V:, :])
    o_ref[...] = x_ref[...] + acc


def _outproj(x, ya, yb, yc, w):
    n = x.shape[0]
    tm = TOKEN_TILE

    def row(width):
        return pl.BlockSpec((tm, width), lambda i: (i, 0))

    return pl.pallas_call(
        _outproj_kernel,
        grid=(n // tm,),
        in_specs=[row(D_MODEL), row(GLA_V), row(DIFF_V), row(CONV_CH), _const_spec((D_MODEL, D_MODEL))],
        out_specs=row(D_MODEL),
        out_shape=jax.ShapeDtypeStruct((n, D_MODEL), F32),
        compiler_params=_cparams(("parallel",)),
        name="outproj",
    )(x, ya, yb, yc, w)


def _ple_kernel(x_ref, p_ref, g_ref, wg_ref, wp_ref, o_ref):
    x = x_ref[...]
    h = _rms_rows(x, g_ref[...]).astype(BF16)
    gate = _sigmoid(_dot(h, wg_ref[...]))
    o_ref[...] = x + gate * _dot(p_ref[...].astype(BF16), wp_ref[...])


def _ple(x, p, gain, wg, wp):
    n = x.shape[0]
    tm = TOKEN_TILE

    def row(width):
        return pl.BlockSpec((tm, width), lambda i: (i, 0))

    return pl.pallas_call(
        _ple_kernel,
        grid=(n // tm,),
        in_specs=[row(D_MODEL), row(PLE_DIM), _const_spec((1, D_MODEL)),
                  _const_spec((D_MODEL, D_MODEL)), _const_spec((PLE_DIM, D_MODEL))],
        out_specs=row(D_MODEL),
        out_shape=jax.ShapeDtypeStruct((n, D_MODEL), F32),
        compiler_params=_cparams(("parallel",)),
        name="ple",
    )(x, p, gain, wg, wp)


def _reorder_w_in(w):
    parts = [w[:, 0:768], w[:, 800:2848], w[:, 768:800],
             jnp.zeros((D_MODEL, D_IN_PAD - D_IN), w.dtype)]
    return jnp.concatenate(parts, axis=1).astype(BF16)


def _pad_up(up, row0):
    return jnp.zeros((128, GLA_Q), F32).at[row0:row0 + GLA_RANK].set(up).astype(BF16)


def kernel(x, p, ffn1_norm, ffn1_w_gate, ffn1_w_up, ffn1_w_down, mix_norm, w_in, w_out, gla_gk_up_f, gla_gk_bias_f, gla_gk_up_b, gla_gk_bias_b, gla_out_norm, diff_q_norm, diff_k_norm, diff_lambda, diff_out_norm, rel_bias, conv_dw_w, conv_dw_b, conv_norm_g, conv_norm_b, ffn2_norm, ffn2_w_gate, ffn2_w_up, ffn2_w_down, ple_norm, ple_w_gate, ple_w_proj):
    B, L, _ = x.shape
    depth = w_in.shape[0]
    n = B * L
    xs = x.reshape(n, D_MODEL)
    rb_flat = rel_bias.reshape(-1)
    band = _bias_tiles(rel_bias)

    def row(v):
        return v.reshape(1, -1)

    for i in range(depth):
        xs = _ffn(xs, row(ffn1_norm[i]), ffn1_w_gate[i].astype(BF16), ffn1_w_up[i].astype(BF16),
                  ffn1_w_down[i].astype(BF16))

        gqk, gg, gv, go, dq, dk, dv, cv = _inproj(
            xs, row(mix_norm[i]), _reorder_w_in(w_in[i]),
            _pad_up(gla_gk_up_f[i], 0), row(gla_gk_bias_f[i]),
            _pad_up(gla_gk_up_b[i], GLA_RANK), row(gla_gk_bias_b[i]),
            row(jnp.tile(diff_q_norm[i], DIFF_QK // DIFF_DQK)), row(jnp.tile(diff_k_norm[i], DIFF_QK // DIFF_DQK)))

        y_gla = _gla(gqk, gg, gv, go, row(jnp.tile(gla_out_norm[i], N_GLA_HEADS)), B)
        lam_init = 0.8 - 0.6 * math.exp(-0.3 * i)
        y_diff = _diff_attn(rb_flat, dq, dk, dv, band, diff_lambda[i], row(diff_out_norm[i]), B, lam_init)
        conv_w = jnp.concatenate([conv_dw_w[i], jnp.zeros((1, CONV_CH), F32)], axis=0)
        y_conv = _conv(cv, conv_w, row(conv_dw_b[i]), row(conv_norm_g[i]), row(conv_norm_b[i]), B)

        xs = _outproj(xs, y_gla, y_diff, y_conv, w_out[i].astype(BF16))
        xs = _ffn(xs, row(ffn2_norm[i]), ffn2_w_gate[i].astype(BF16), ffn2_w_up[i].astype(BF16),
                  ffn2_w_down[i].astype(BF16))
        xs = _ple(xs, p[i].reshape(n, PLE_DIM), row(ple_norm[i]), ple_w_gate[i].astype(BF16),
                  ple_w_proj[i].astype(BF16))
    return xs.reshape(B, L, D_MODEL)
```

```python
import functools
import math

import jax
import jax.numpy as jnp
from jax import lax
from jax.experimental import pallas as pl
from jax.experimental.pallas import tpu as pltpu

F32 = jnp.float32
BF16 = jnp.bfloat16

D_MODEL = 1024
N_GLA_HEADS = 4
GLA_DK = 32
GLA_DV = 64
GLA_Q = N_GLA_HEADS * GLA_DK
GLA_V = N_GLA_HEADS * GLA_DV
GLA_RANK = 16
GATE_TEMP = 16.0
GLA_CHUNK = 64
DIFF_DQK = 64
N_DIFF_HEADS = 4
DIFF_DV = 128
DIFF_QK = 512
DIFF_V = 512
NUM_BUCKETS = 32
MAX_DISTANCE = 128
CONV_CH = 256
CONV_K = 31
D_FF = 2816
PLE_DIM = 256
NORM_EPS = 1e-6
LOG2E = math.log2(math.e)
D_IN = 2848
D_IN_PAD = 2944

TOKEN_TILE = 512
FF_CHUNK = 1408
ATTN_TILE = 256
CONV_ROWS = 128
CONV_PAD = 16
VMEM_LIMIT = 56 * 1024 * 1024


def _cparams(sem):
    return pltpu.CompilerParams(dimension_semantics=sem, vmem_limit_bytes=VMEM_LIMIT)


def _dot(a, b):
    return jnp.dot(a, b, preferred_element_type=F32)


def _dot_nt(a, b):
    return lax.dot_general(a, b, (((1,), (1,)), ((), ())), preferred_element_type=F32)


def _rms_rows(x, gain):
    return x * lax.rsqrt(jnp.mean(x * x, axis=-1, keepdims=True) + NORM_EPS) * gain


def _sigmoid(x):
    return 1.0 / (1.0 + jnp.exp(-x))


def _silu(x):
    return x * _sigmoid(x)


def _const_spec(shape):
    return pl.BlockSpec(shape, lambda *_: (0,) * len(shape), pipeline_mode=pl.Buffered(1))


def _ffn_kernel(x_ref, g_ref, wg_ref, wu_ref, wd_ref, o_ref):
    x = x_ref[...]
    h = _rms_rows(x, g_ref[...]).astype(BF16)
    acc = jnp.zeros(x.shape, F32)
    for c0 in range(0, D_FF, FF_CHUNK):
        gate = _dot(h, wg_ref[:, c0:c0 + FF_CHUNK])
        up = _dot(h, wu_ref[:, c0:c0 + FF_CHUNK])
        a = (_silu(gate) * up).astype(BF16)
        acc = acc + _dot(a, wd_ref[c0:c0 + FF_CHUNK, :])
    o_ref[...] = x + 0.5 * acc


def _ffn(x, gain, wg, wu, wd):
    n = x.shape[0]
    row = pl.BlockSpec((TOKEN_TILE, D_MODEL), lambda i: (i, 0))
    return pl.pallas_call(
        _ffn_kernel,
        grid=(n // TOKEN_TILE,),
        in_specs=[row, _const_spec((1, D_MODEL)), _const_spec((D_MODEL, D_FF)),
                  _const_spec((D_MODEL, D_FF)), _const_spec((D_FF, D_MODEL))],
        out_specs=row,
        out_shape=jax.ShapeDtypeStruct((n, D_MODEL), F32),
        compiler_params=_cparams(("parallel",)),
        name="ffn",
    )(x, gain, wg, wu, wd)


def _block_diag_ones(n, blk, dtype):
    r = lax.broadcasted_iota(jnp.int32, (n, n), 0) // blk
    c = lax.broadcasted_iota(jnp.int32, (n, n), 1) // blk
    return jnp.where(r == c, 1.0, 0.0).astype(dtype)


def _group_mean_sq(x, blk):
    n = x.shape[-1]
    return _dot((x * x).astype(BF16), _block_diag_ones(n, blk, BF16)) * (1.0 / blk)


def _log_sigmoid(z):
    return jnp.minimum(z, 0.0) - jnp.log(1.0 + jnp.exp(-jnp.abs(z)))


def _inproj_kernel(x_ref, g_ref, w_ref, upf_ref, bf_ref, upb_ref, bb_ref, qn_ref, kn_ref,
                   gqk_ref, gg_ref, gv_ref, go_ref, dq_ref, dk_ref, dv_ref, cv_ref):
    h = _rms_rows(x_ref[...], g_ref[...]).astype(BF16)
    proj = _dot(h, w_ref[...])
    gqk_ref[:, :GLA_Q] = proj[:, :GLA_Q] * (GLA_DK ** -0.5)
    gqk_ref[:, GLA_Q:] = proj[:, GLA_Q:2 * GLA_Q]
    gv_ref[...] = proj[:, 256:512].astype(BF16)
    go_ref[...] = _silu(proj[:, 512:768]).astype(BF16)
    low = proj[:, 2816:2944].astype(BF16)
    zf = _dot(low, upf_ref[...]) + bf_ref[...]
    zb = _dot(low, upb_ref[...]) + bb_ref[...]
    gg_ref[:, :GLA_Q] = _log_sigmoid(zf) / GATE_TEMP
    gg_ref[:, GLA_Q:] = _log_sigmoid(zb) / GATE_TEMP
    dq = proj[:, 768:1280]
    dq_ref[...] = (dq * lax.rsqrt(_group_mean_sq(dq, DIFF_DQK) + NORM_EPS) * qn_ref[...]
                   * (DIFF_DQK ** -0.5 * LOG2E)).astype(BF16)
    dk = proj[:, 1280:1792]
    dk_ref[...] = (dk * lax.rsqrt(_group_mean_sq(dk, DIFF_DQK) + NORM_EPS) * kn_ref[...]).astype(BF16)
    dv_ref[...] = proj[:, 1792:2304].astype(BF16)
    cv_ref[...] = proj[:, 2304:2816]


def _inproj(x, gain, w, upf, bf, upb, bb, qn, kn):
    n = x.shape[0]
    tm = TOKEN_TILE

    def row(width):
        return pl.BlockSpec((tm, width), lambda i: (i, 0))

    outs = [(2 * GLA_Q, F32), (2 * GLA_Q, F32), (GLA_V, BF16), (GLA_V, BF16),
            (DIFF_QK, BF16), (DIFF_QK, BF16), (DIFF_V, BF16), (2 * CONV_CH, F32)]
    return pl.pallas_call(
        _inproj_kernel,
        grid=(n // tm,),
        in_specs=[row(D_MODEL), _const_spec((1, D_MODEL)), _const_spec((D_MODEL, D_IN_PAD)),
                  _const_spec((128, GLA_Q)), _const_spec((1, GLA_Q)),
                  _const_spec((128, GLA_Q)), _const_spec((1, GLA_Q)),
                  _const_spec((1, DIFF_QK)), _const_spec((1, DIFF_QK))],
        out_specs=[row(wd) for wd, _ in outs],
        out_shape=[jax.ShapeDtypeStruct((n, wd), dt) for wd, dt in outs],
        compiler_params=_cparams(("parallel",)),
        name="inproj",
    )(x, gain, w, upf, bf, upb, bb, qn, kn)


def _gla_kernel(qk_ref, g_ref, v_ref, go_ref, on_ref, y_ref, st_ref):
    C = GLA_CHUNK
    L = qk_ref.shape[0]
    nchunk = L // C
    H = N_GLA_HEADS

    def iota(shape, d):
        return lax.broadcasted_iota(jnp.int32, shape, d)

    ri, ci = iota((C, C), 0), iota((C, C), 1)
    kmask = iota((H * C, GLA_Q), 0) // C == iota((H * C, GLA_Q), 1) // GLA_DK
    vmask = iota((H * C, GLA_V), 0) // C == iota((H * C, GLA_V), 1) // GLA_DV
    smask = iota((GLA_V, GLA_Q), 0) // GLA_DV == iota((GLA_V, GLA_Q), 1) // GLA_DK
    pos_a = iota((C, H * C), 0)
    pos_b = iota((C, H * C), 1) % C
    norm_bd = _block_diag_ones(GLA_V, GLA_DV, BF16)

    def direction(reverse):
        if reverse:
            cum = jnp.where(ci >= ri, 1.0, 0.0).astype(BF16)
            amask = pos_b >= pos_a
            edge = 0
        else:
            cum = jnp.where(ci <= ri, 1.0, 0.0).astype(BF16)
            amask = pos_b <= pos_a
            edge = C - 1
        goff = GLA_Q if reverse else 0

        def chunk(t, _):
            n = (nchunk - 1 - t) if reverse else t
            r0 = pl.multiple_of(n * C, C)
            rows = pl.ds(r0, C)
            q = qk_ref[rows, :GLA_Q]
            k = qk_ref[rows, GLA_Q:]
            g = g_ref[rows, goff:goff + GLA_Q]
            v = v_ref[rows, :].astype(F32)
            g_hi = g.astype(BF16)
            g_lo = (g - g_hi.astype(F32)).astype(BF16)
            b = _dot(cum, g_hi) + _dot(cum, g_lo)
            b_edge = b[edge:edge + 1, :]
            qt = (q * jnp.exp(b)).astype(BF16)
            kt = k * jnp.exp(-b)
            kend = (k * jnp.exp(b_edge - b)).astype(BF16)
            kstack = jnp.where(kmask, jnp.concatenate([kt] * H, axis=0), 0.0).astype(BF16)
            a = jnp.where(amask, _dot_nt(qt, kstack), 0.0).astype(BF16)
            vstack = jnp.where(vmask, jnp.concatenate([v] * H, axis=0), 0.0).astype(BF16)
            s_t = st_ref[...]
            o = _dot(a, vstack) + _dot_nt(qt, s_t.astype(BF16))
            ds = _dot(v.T.astype(BF16), kend)
            st_ref[...] = s_t * jnp.exp(b_edge) + jnp.where(smask, ds, 0.0)
            if reverse:
                y_ref[rows, :] = o
            else:
                tot = o + y_ref[rows, :]
                ms = _dot((tot * tot).astype(BF16), norm_bd) * (1.0 / GLA_DV)
                y_ref[rows, :] = (tot * lax.rsqrt(ms + NORM_EPS) * on_ref[...]
                                  * go_ref[rows, :].astype(F32))
            return 0

        st_ref[...] = jnp.zeros(st_ref.shape, F32)
        lax.fori_loop(0, nchunk, chunk, 0)

    direction(True)
    direction(False)


def _gla(qk, g, v, go, onorm, batch):
    n = qk.shape[0]
    L = n // batch

    def seq(width):
        return pl.BlockSpec((L, width), lambda b: (b, 0))

    return pl.pallas_call(
        _gla_kernel,
        grid=(batch,),
        in_specs=[seq(2 * GLA_Q), seq(2 * GLA_Q), seq(GLA_V), seq(GLA_V), _const_spec((1, GLA_V))],
        out_specs=seq(GLA_V),
        out_shape=jax.ShapeDtypeStruct((n, GLA_V), F32),
        scratch_shapes=[pltpu.VMEM((GLA_V, GLA_Q), F32)],
        compiler_params=_cparams(("parallel",)),
        name="gla",
    )(qk, g, v, go, onorm)


def _bias_kernel(rb_ref, o_ref):
    T = ATTN_TILE
    h = pl.program_id(0)
    nb = NUM_BUCKETS // 2
    max_exact = nb // 2
    for d in range(5):
        rel = (lax.broadcasted_iota(jnp.int32, (T, T), 0) - lax.broadcasted_iota(jnp.int32, (T, T), 1)
               + (d - 2) * T)
        ret = jnp.where(rel > 0, nb, 0)
        n = jnp.abs(rel)
        large = max_exact + (jnp.log(jnp.maximum(n, 1).astype(F32) / max_exact)
                             / math.log(MAX_DISTANCE / max_exact) * (nb - max_exact)).astype(jnp.int32)
        large = jnp.minimum(large, nb - 1)
        bucket = ret + jnp.where(n < max_exact, n, large)
        for m in range(2):
            tile = jnp.zeros((T, T), F32)
            for bkt in range(NUM_BUCKETS):
                tile = jnp.where(bucket == bkt, rb_ref[(bkt * N_DIFF_HEADS + h) * 2 + m], tile)
            o_ref[0, d, m] = tile * (-LOG2E)


def _bias_tiles(rel_bias):
    T = ATTN_TILE
    return pl.pallas_call(
        _bias_kernel,
        grid=(N_DIFF_HEADS,),
        in_specs=[pl.BlockSpec(memory_space=pltpu.SMEM)],
        out_specs=pl.BlockSpec((1, 5, 2, T, T), lambda h: (h, 0, 0, 0, 0)),
        out_shape=jax.ShapeDtypeStruct((N_DIFF_HEADS, 5, 2, T, T), F32),
        compiler_params=_cparams(("parallel",)),
        name="t5_bias",
    )(rel_bias.reshape(-1))


def _attn_kernel(rb_ref, q_ref, k_ref, vt_ref, band_ref, lam_ref, on_ref, o_ref, *, lam_init):
    T = ATTN_TILE
    L = k_ref.shape[0]
    nk = L // T
    h = pl.program_id(1)
    qi = pl.program_id(2)
    q = q_ref[...]
    lane = lax.broadcasted_iota(jnp.int32, q.shape, 1)
    qm = [jnp.where(lane < DIFF_DQK, q, 0).astype(BF16), jnp.where(lane >= DIFF_DQK, q, 0).astype(BF16)]

    c_left = [rb_ref[((NUM_BUCKETS // 2 - 1) * N_DIFF_HEADS + h) * 2 + m] * LOG2E for m in range(2)]
    c_right = [rb_ref[((NUM_BUCKETS - 1) * N_DIFF_HEADS + h) * 2 + m] * LOG2E for m in range(2)]
    band_idx = [jnp.where(qi == 0, 4, 1), 2, jnp.where(qi == nk - 1, 0, 3)]
    stats = [(jnp.full((1, T), -jnp.inf, F32), jnp.zeros((1, T), F32), jnp.zeros((DIFF_DV, T), F32))
             for _ in range(2)]

    def tile_start(t):
        j = (qi + (2 + t)) % nk
        return j, pl.multiple_of(j * T, T)

    def scores(t):
        kt = k_ref[pl.ds(tile_start(t)[1], T), :]
        return [_dot_nt(kt, qm[m]) for m in range(2)]

    s_next = scores(0)
    for t in range(nk):
        j, r0 = tile_start(t)
        vt = vt_ref[:, pl.ds(r0, T)]
        s_cur = s_next
        if t + 1 < nk:
            s_next = scores(t + 1)
        for m in range(2):
            m_old, l_old, acc_old = stats[m]
            s = s_cur[m]
            if t >= nk - 3:
                s = s - band_ref[0, band_idx[t - (nk - 3)], m]
                m_new = jnp.maximum(m_old, jnp.max(s, axis=0, keepdims=True))
                shift = m_new
            else:
                c = jnp.where(j < qi, c_left[m], c_right[m])
                m_new = jnp.maximum(m_old, jnp.max(s, axis=0, keepdims=True) + c)
                shift = m_new - c
            alpha = jnp.exp2(m_old - m_new)
            p = jnp.exp2(s - shift)
            l_new = alpha * l_old + jnp.sum(p, axis=0, keepdims=True)
            acc_new = alpha * acc_old + _dot(vt, p.astype(BF16))
            stats[m] = (m_new, l_new, acc_new)

    lp = lam_ref[...]
    lam = (jnp.exp(jnp.sum(lp[0:1] * lp[1:2], axis=-1, keepdims=True))
           - jnp.exp(jnp.sum(lp[2:3] * lp[3:4], axis=-1, keepdims=True)) + lam_init)
    o = stats[0][2] / stats[0][1] - lam * (stats[1][2] / stats[1][1])
    y = o * lax.rsqrt(jnp.mean(o * o, axis=0, keepdims=True) + NORM_EPS) * on_ref[...] * (1.0 - lam_init)
    o_ref[...] = y.T.astype(o_ref.dtype)


def _diff_attn(rel_bias_flat, dq, dk, dvt, band, lam_p, onorm, batch, lam_init):
    n = dq.shape[0]
    L = n // batch
    T = ATTN_TILE
    nq = L // T
    return pl.pallas_call(
        functools.partial(_attn_kernel, lam_init=lam_init),
        grid=(batch, N_DIFF_HEADS, nq),
        in_specs=[pl.BlockSpec(memory_space=pltpu.SMEM),
                  pl.BlockSpec((T, 2 * DIFF_DQK), lambda b, h, i: (b * nq + i, h)),
                  pl.BlockSpec((L, 2 * DIFF_DQK), lambda b, h, i: (b, h)),
                  pl.BlockSpec((DIFF_DV, L), lambda b, h, i: (h, b)),
                  pl.BlockSpec((1, 5, 2, T, T), lambda b, h, i: (h, 0, 0, 0, 0)),
                  pl.BlockSpec((4, DIFF_DQK), lambda b, h, i: (0, 0)),
                  pl.BlockSpec((DIFF_DV, 1), lambda b, h, i: (0, 0))],
        out_specs=pl.BlockSpec((T, DIFF_DV), lambda b, h, i: (b * nq + i, h)),
        out_shape=jax.ShapeDtypeStruct((n, DIFF_V), BF16),
        compiler_params=_cparams(("parallel", "parallel", "parallel")),
        name="diff_attn",
    )(rel_bias_flat, dq, dk, dvt, band, lam_p, onorm)


def _conv_kernel(cv_ref, w_ref, b_ref, g_ref, beta_ref, y_ref, u_ref):
    L = cv_ref.shape[0]
    R = CONV_ROWS
    P = CONV_PAD
    zeros = jnp.zeros((P, CONV_CH), F32)
    u_ref[:P, :] = zeros
    u_ref[P + L:, :] = zeros
    u_ref[P:P + L, :] = cv_ref[:, :CONV_CH] * _sigmoid(cv_ref[:, CONV_CH:])
    off = P - CONV_K // 2

    def tile(t, _):
        r0 = pl.multiple_of(t * R, R)
        win = u_ref[pl.ds(r0, R + 2 * P), :]
        shifted = [win] + [pltpu.roll(win, R + 2 * P - r, 0) for r in range(1, 8)]
        acc = jnp.zeros((R, CONV_CH), F32) + b_ref[...]
        for k in range(CONV_K):
            a8, r = divmod(off + k, 8)
            acc = acc + w_ref[k:k + 1, :] * shifted[r][8 * a8:8 * a8 + R, :]
        mu = jnp.mean(acc, axis=-1, keepdims=True)
        xc = acc - mu
        y = xc * lax.rsqrt(jnp.mean(xc * xc, axis=-1, keepdims=True) + NORM_EPS) * g_ref[...] + beta_ref[...]
        y_ref[pl.ds(r0, R), :] = _silu(y)
        return 0

    lax.fori_loop(0, L // R, tile, 0)


def _conv(cv, w, b, g, beta, batch):
    n = cv.shape[0]
    L = n // batch
    return pl.pallas_call(
        _conv_kernel,
        grid=(batch,),
        in_specs=[pl.BlockSpec((L, 2 * CONV_CH), lambda i: (i, 0)),
                  _const_spec((CONV_K + 1, CONV_CH)), _const_spec((1, CONV_CH)),
                  _const_spec((1, CONV_CH)), _const_spec((1, CONV_CH))],
        out_specs=pl.BlockSpec((L, CONV_CH), lambda i: (i, 0)),
        out_shape=jax.ShapeDtypeStruct((n, CONV_CH), F32),
        scratch_shapes=[pltpu.VMEM((L + 2 * CONV_PAD, CONV_CH), F32)],
        compiler_params=_cparams(("parallel",)),
        name="conv",
    )(cv, w, b, g, beta)


def _outproj_kernel(x_ref, ya_ref, yb_ref, yc_ref, w_ref, o_ref):
    acc = _dot(ya_ref[...].astype(BF16), w_ref[:GLA_V, :])
    acc = acc + _dot(yb_ref[...].astype(BF16), w_ref[GLA_V:GLA_V + DIFF_V, :])
    acc = acc + _dot(yc_ref[...].astype(BF16), w_ref[GLA_V + DIFF_V:, :])
    o_ref[...] = x_ref[...] + acc


def _outproj(x, ya, yb, yc, w):
    n = x.shape[0]
    tm = TOKEN_TILE

    def row(width):
        return pl.BlockSpec((tm, width), lambda i: (i, 0))

    return pl.pallas_call(
        _outproj_kernel,
        grid=(n // tm,),
        in_specs=[row(D_MODEL), row(GLA_V), row(DIFF_V), row(CONV_CH), _const_spec((D_MODEL, D_MODEL))],
        out_specs=row(D_MODEL),
        out_shape=jax.ShapeDtypeStruct((n, D_MODEL), F32),
        compiler_params=_cparams(("parallel",)),
        name="outproj",
    )(x, ya, yb, yc, w)


def _ple_kernel(x_ref, p_ref, g_ref, wg_ref, wp_ref, o_ref):
    x = x_ref[...]
    h = _rms_rows(x, g_ref[...]).astype(BF16)
    gate = _sigmoid(_dot(h, wg_ref[...]))
    o_ref[...] = x + gate * _dot(p_ref[...].astype(BF16), wp_ref[...])


def _ple(x, p, gain, wg, wp):
    n = x.shape[0]
    tm = TOKEN_TILE

    def row(width):
        return pl.BlockSpec((tm, width), lambda i: (i, 0))

    return pl.pallas_call(
        _ple_kernel,
        grid=(n // tm,),
        in_specs=[row(D_MODEL), row(PLE_DIM), _const_spec((1, D_MODEL)),
                  _const_spec((D_MODEL, D_MODEL)), _const_spec((PLE_DIM, D_MODEL))],
        out_specs=row(D_MODEL),
        out_shape=jax.ShapeDtypeStruct((n, D_MODEL), F32),
        compiler_params=_cparams(("parallel",)),
        name="ple",
    )(x, p, gain, wg, wp)


def _reorder_w_in(w):
    parts = [w[:, 0:768], w[:, 800:2848], w[:, 768:800],
             jnp.zeros((D_MODEL, D_IN_PAD - D_IN), w.dtype)]
    return jnp.concatenate(parts, axis=1).astype(BF16)


def _pad_up(up, row0):
    return jnp.zeros((128, GLA_Q), F32).at[row0:row0 + GLA_RANK].set(up).astype(BF16)


def kernel(x, p, ffn1_norm, ffn1_w_gate, ffn1_w_up, ffn1_w_down, mix_norm, w_in, w_out, gla_gk_up_f, gla_gk_bias_f, gla_gk_up_b, gla_gk_bias_b, gla_out_norm, diff_q_norm, diff_k_norm, diff_lambda, diff_out_norm, rel_bias, conv_dw_w, conv_dw_b, conv_norm_g, conv_norm_b, ffn2_norm, ffn2_w_gate, ffn2_w_up, ffn2_w_down, ple_norm, ple_w_gate, ple_w_proj):
    B, L, _ = x.shape
    depth = w_in.shape[0]
    n = B * L
    xs = x.reshape(n, D_MODEL)
    rb_flat = rel_bias.reshape(-1)
    band = _bias_tiles(rel_bias)

    def row(v):
        return v.reshape(1, -1)

    for i in range(depth):
        xs = _ffn(xs, row(ffn1_norm[i]), ffn1_w_gate[i].astype(BF16), ffn1_w_up[i].astype(BF16),
                  ffn1_w_down[i].astype(BF16))

        gqk, gg, gv, go, dq, dk, dv, cv = _inproj(
            xs, row(mix_norm[i]), _reorder_w_in(w_in[i]),
            _pad_up(gla_gk_up_f[i], 0), row(gla_gk_bias_f[i]),
            _pad_up(gla_gk_up_b[i], GLA_RANK), row(gla_gk_bias_b[i]),
            row(jnp.tile(diff_q_norm[i], DIFF_QK // DIFF_DQK)), row(jnp.tile(diff_k_norm[i], DIFF_QK // DIFF_DQK)))

        y_gla = _gla(gqk, gg, gv, go, row(jnp.tile(gla_out_norm[i], N_GLA_HEADS)), B)
        lam_init = 0.8 - 0.6 * math.exp(-0.3 * i)
        y_diff = _diff_attn(rb_flat, dq, dk, dv.T, band, diff_lambda[i], diff_out_norm[i].reshape(-1, 1), B,
                            lam_init)
        conv_w = jnp.concatenate([conv_dw_w[i], jnp.zeros((1, CONV_CH), F32)], axis=0)
        y_conv = _conv(cv, conv_w, row(conv_dw_b[i]), row(conv_norm_g[i]), row(conv_norm_b[i]), B)

        xs = _outproj(xs, y_gla, y_diff, y_conv, w_out[i].astype(BF16))
        xs = _ffn(xs, row(ffn2_norm[i]), ffn2_w_gate[i].astype(BF16), ffn2_w_up[i].astype(BF16),
                  ffn2_w_down[i].astype(BF16))
        xs = _ple(xs, p[i].reshape(n, PLE_DIM), row(ple_norm[i]), ple_w_gate[i].astype(BF16),
                  ple_w_proj[i].astype(BF16))
    return xs.reshape(B, L, D_MODEL)
```

```python
import functools
import math

import jax
import jax.numpy as jnp
from jax import lax
from jax.experimental import pallas as pl
from jax.experimental.pallas import tpu as pltpu

F32 = jnp.float32
BF16 = jnp.bfloat16

D_MODEL = 1024
N_GLA_HEADS = 4
GLA_DK = 32
GLA_DV = 64
GLA_Q = N_GLA_HEADS * GLA_DK
GLA_V = N_GLA_HEADS * GLA_DV
GLA_RANK = 16
GATE_TEMP = 16.0
GLA_CHUNK = 64
DIFF_DQK = 64
N_DIFF_HEADS = 4
DIFF_DV = 128
DIFF_QK = 512
DIFF_V = 512
NUM_BUCKETS = 32
MAX_DISTANCE = 128
CONV_CH = 256
CONV_K = 31
D_FF = 2816
PLE_DIM = 256
NORM_EPS = 1e-6
LOG2E = math.log2(math.e)
EXP2_SAFE_RANGE = 100.0
D_IN = 2848
D_IN_PAD = 2944

TOKEN_TILE = 512
FF_CHUNK = 1408
ATTN_TILE = 256
CONV_ROWS = 128
CONV_PAD = 16
VMEM_LIMIT = 56 * 1024 * 1024


def _cparams(sem):
    return pltpu.CompilerParams(dimension_semantics=sem, vmem_limit_bytes=VMEM_LIMIT)


def _dot(a, b):
    return jnp.dot(a, b, preferred_element_type=F32)


def _dot_nt(a, b):
    return lax.dot_general(a, b, (((1,), (1,)), ((), ())), preferred_element_type=F32)


def _rms_rows(x, gain):
    return x * lax.rsqrt(jnp.mean(x * x, axis=-1, keepdims=True) + NORM_EPS) * gain


def _sigmoid(x):
    return 1.0 / (1.0 + jnp.exp(-x))


def _silu(x):
    return x * _sigmoid(x)


def _const_spec(shape):
    return pl.BlockSpec(shape, lambda *_: (0,) * len(shape), pipeline_mode=pl.Buffered(1))


def _ffn_kernel(x_ref, g_ref, wg_ref, wu_ref, wd_ref, o_ref):
    x = x_ref[...]
    h = _rms_rows(x, g_ref[...]).astype(BF16)
    acc = jnp.zeros(x.shape, F32)
    for c0 in range(0, D_FF, FF_CHUNK):
        gate = _dot(h, wg_ref[:, c0:c0 + FF_CHUNK])
        up = _dot(h, wu_ref[:, c0:c0 + FF_CHUNK])
        a = (_silu(gate) * up).astype(BF16)
        acc = acc + _dot(a, wd_ref[c0:c0 + FF_CHUNK, :])
    o_ref[...] = x + 0.5 * acc


def _ffn(x, gain, wg, wu, wd):
    n = x.shape[0]
    row = pl.BlockSpec((TOKEN_TILE, D_MODEL), lambda i: (i, 0))
    return pl.pallas_call(
        _ffn_kernel,
        grid=(n // TOKEN_TILE,),
        in_specs=[row, _const_spec((1, D_MODEL)), _const_spec((D_MODEL, D_FF)),
                  _const_spec((D_MODEL, D_FF)), _const_spec((D_FF, D_MODEL))],
        out_specs=row,
        out_shape=jax.ShapeDtypeStruct((n, D_MODEL), F32),
        compiler_params=_cparams(("parallel",)),
        name="ffn",
    )(x, gain, wg, wu, wd)


def _block_diag_ones(n, blk, dtype):
    r = lax.broadcasted_iota(jnp.int32, (n, n), 0) // blk
    c = lax.broadcasted_iota(jnp.int32, (n, n), 1) // blk
    return jnp.where(r == c, 1.0, 0.0).astype(dtype)


def _group_mean_sq(x, blk):
    n = x.shape[-1]
    return _dot((x * x).astype(BF16), _block_diag_ones(n, blk, BF16)) * (1.0 / blk)


def _log_sigmoid(z):
    return jnp.minimum(z, 0.0) - jnp.log(1.0 + jnp.exp(-jnp.abs(z)))


def _inproj_kernel(x_ref, g_ref, w_ref, upf_ref, bf_ref, upb_ref, bb_ref, qn_ref, kn_ref,
                   gqk_ref, gg_ref, gv_ref, go_ref, dq_ref, dk_ref, dv_ref, cv_ref):
    h = _rms_rows(x_ref[...], g_ref[...]).astype(BF16)
    proj = _dot(h, w_ref[...])
    gqk_ref[:, :GLA_Q] = proj[:, :GLA_Q] * (GLA_DK ** -0.5)
    gqk_ref[:, GLA_Q:] = proj[:, GLA_Q:2 * GLA_Q]
    gv_ref[...] = proj[:, 256:512].astype(BF16)
    go_ref[...] = _silu(proj[:, 512:768]).astype(BF16)
    low = proj[:, 2816:2944].astype(BF16)
    zf = _dot(low, upf_ref[...]) + bf_ref[...]
    zb = _dot(low, upb_ref[...]) + bb_ref[...]
    gg_ref[:, :GLA_Q] = _log_sigmoid(zf) / GATE_TEMP
    gg_ref[:, GLA_Q:] = _log_sigmoid(zb) / GATE_TEMP
    dq = proj[:, 768:1280]
    dq_ref[...] = (dq * lax.rsqrt(_group_mean_sq(dq, DIFF_DQK) + NORM_EPS) * qn_ref[...]
                   * (DIFF_DQK ** -0.5 * LOG2E)).astype(BF16)
    dk = proj[:, 1280:1792]
    dk_ref[...] = (dk * lax.rsqrt(_group_mean_sq(dk, DIFF_DQK) + NORM_EPS) * kn_ref[...]).astype(BF16)
    dv_ref[...] = proj[:, 1792:2304].astype(BF16)
    cv_ref[...] = proj[:, 2304:2816]


def _inproj(x, gain, w, upf, bf, upb, bb, qn, kn):
    n = x.shape[0]
    tm = TOKEN_TILE

    def row(width):
        return pl.BlockSpec((tm, width), lambda i: (i, 0))

    outs = [(2 * GLA_Q, F32), (2 * GLA_Q, F32), (GLA_V, BF16), (GLA_V, BF16),
            (DIFF_QK, BF16), (DIFF_QK, BF16), (DIFF_V, BF16), (2 * CONV_CH, F32)]
    return pl.pallas_call(
        _inproj_kernel,
        grid=(n // tm,),
        in_specs=[row(D_MODEL), _const_spec((1, D_MODEL)), _const_spec((D_MODEL, D_IN_PAD)),
                  _const_spec((128, GLA_Q)), _const_spec((1, GLA_Q)),
                  _const_spec((128, GLA_Q)), _const_spec((1, GLA_Q)),
                  _const_spec((1, DIFF_QK)), _const_spec((1, DIFF_QK))],
        out_specs=[row(wd) for wd, _ in outs],
        out_shape=[jax.ShapeDtypeStruct((n, wd), dt) for wd, dt in outs],
        compiler_params=_cparams(("parallel",)),
        name="inproj",
    )(x, gain, w, upf, bf, upb, bb, qn, kn)


def _gla_kernel(qk_ref, g_ref, v_ref, go_ref, on_ref, y_ref, st_ref):
    C = GLA_CHUNK
    L = qk_ref.shape[0]
    nchunk = L // C
    H = N_GLA_HEADS

    def iota(shape, d):
        return lax.broadcasted_iota(jnp.int32, shape, d)

    ri, ci = iota((C, C), 0), iota((C, C), 1)
    kmask = iota((H * C, GLA_Q), 0) // C == iota((H * C, GLA_Q), 1) // GLA_DK
    vmask = iota((H * C, GLA_V), 0) // C == iota((H * C, GLA_V), 1) // GLA_DV
    smask = iota((GLA_V, GLA_Q), 0) // GLA_DV == iota((GLA_V, GLA_Q), 1) // GLA_DK
    pos_a = iota((C, H * C), 0)
    pos_b = iota((C, H * C), 1) % C
    norm_bd = _block_diag_ones(GLA_V, GLA_DV, BF16)

    def direction(reverse):
        if reverse:
            cum = jnp.where(ci >= ri, 1.0, 0.0).astype(BF16)
            amask = pos_b >= pos_a
            edge = 0
        else:
            cum = jnp.where(ci <= ri, 1.0, 0.0).astype(BF16)
            amask = pos_b <= pos_a
            edge = C - 1
        goff = GLA_Q if reverse else 0

        def chunk(t, _):
            n = (nchunk - 1 - t) if reverse else t
            r0 = pl.multiple_of(n * C, C)
            rows = pl.ds(r0, C)
            q = qk_ref[rows, :GLA_Q]
            k = qk_ref[rows, GLA_Q:]
            g = g_ref[rows, goff:goff + GLA_Q]
            v = v_ref[rows, :].astype(F32)
            g_hi = g.astype(BF16)
            g_lo = (g - g_hi.astype(F32)).astype(BF16)
            b = _dot(cum, g_hi) + _dot(cum, g_lo)
            b_edge = b[edge:edge + 1, :]
            qt = (q * jnp.exp(b)).astype(BF16)
            kt = k * jnp.exp(-b)
            kend = (k * jnp.exp(b_edge - b)).astype(BF16)
            kstack = jnp.where(kmask, jnp.concatenate([kt] * H, axis=0), 0.0).astype(BF16)
            a = jnp.where(amask, _dot_nt(qt, kstack), 0.0).astype(BF16)
            vstack = jnp.where(vmask, jnp.concatenate([v] * H, axis=0), 0.0).astype(BF16)
            s_t = st_ref[...]
            o = _dot(a, vstack) + _dot_nt(qt, s_t.astype(BF16))
            ds = _dot(v.T.astype(BF16), kend)
            st_ref[...] = s_t * jnp.exp(b_edge) + jnp.where(smask, ds, 0.0)
            if reverse:
                y_ref[rows, :] = o
            else:
                tot = o + y_ref[rows, :]
                ms = _dot((tot * tot).astype(BF16), norm_bd) * (1.0 / GLA_DV)
                y_ref[rows, :] = (tot * lax.rsqrt(ms + NORM_EPS) * on_ref[...]
                                  * go_ref[rows, :].astype(F32))
            return 0

        st_ref[...] = jnp.zeros(st_ref.shape, F32)
        lax.fori_loop(0, nchunk, chunk, 0)

    direction(True)
    direction(False)


def _gla(qk, g, v, go, onorm, batch):
    n = qk.shape[0]
    L = n // batch

    def seq(width):
        return pl.BlockSpec((L, width), lambda b: (b, 0))

    return pl.pallas_call(
        _gla_kernel,
        grid=(batch,),
        in_specs=[seq(2 * GLA_Q), seq(2 * GLA_Q), seq(GLA_V), seq(GLA_V), _const_spec((1, GLA_V))],
        out_specs=seq(GLA_V),
        out_shape=jax.ShapeDtypeStruct((n, GLA_V), F32),
        scratch_shapes=[pltpu.VMEM((GLA_V, GLA_Q), F32)],
        compiler_params=_cparams(("parallel",)),
        name="gla",
    )(qk, g, v, go, onorm)


def _bias_kernel(rb_ref, o_ref):
    T = ATTN_TILE
    h = pl.program_id(0)
    nb = NUM_BUCKETS // 2
    max_exact = nb // 2
    for d in range(5):
        rel = (lax.broadcasted_iota(jnp.int32, (T, T), 0) - lax.broadcasted_iota(jnp.int32, (T, T), 1)
               + (d - 2) * T)
        ret = jnp.where(rel > 0, nb, 0)
        n = jnp.abs(rel)
        large = max_exact + (jnp.log(jnp.maximum(n, 1).astype(F32) / max_exact)
                             / math.log(MAX_DISTANCE / max_exact) * (nb - max_exact)).astype(jnp.int32)
        large = jnp.minimum(large, nb - 1)
        bucket = ret + jnp.where(n < max_exact, n, large)
        for m in range(2):
            tile = jnp.zeros((T, T), F32)
            for bkt in range(NUM_BUCKETS):
                tile = jnp.where(bucket == bkt, rb_ref[(bkt * N_DIFF_HEADS + h) * 2 + m], tile)
            o_ref[0, d, m] = tile * (-LOG2E)


def _bias_tiles(rel_bias):
    T = ATTN_TILE
    return pl.pallas_call(
        _bias_kernel,
        grid=(N_DIFF_HEADS,),
        in_specs=[pl.BlockSpec(memory_space=pltpu.SMEM)],
        out_specs=pl.BlockSpec((1, 5, 2, T, T), lambda h: (h, 0, 0, 0, 0)),
        out_shape=jax.ShapeDtypeStruct((N_DIFF_HEADS, 5, 2, T, T), F32),
        compiler_params=_cparams(("parallel",)),
        name="t5_bias",
    )(rel_bias.reshape(-1))


def _attn_kernel(rb_ref, bound_ref, q_ref, k_ref, vt_ref, band_ref, lam_ref, on_ref, o_ref,
                 m_ref, l_ref, acc_ref, *, lam_init):
    T = ATTN_TILE
    L = k_ref.shape[0]
    nk = L // T
    h = pl.program_id(1)
    qi = pl.program_id(2)
    q = q_ref[...]
    lane = lax.broadcasted_iota(jnp.int32, q.shape, 1)
    qm = [jnp.where(lane < DIFF_DQK, q, 0).astype(BF16), jnp.where(lane >= DIFF_DQK, q, 0).astype(BF16)]

    def side_const(bucket, m):
        return rb_ref[(bucket * N_DIFF_HEADS + h) * 2 + m] * LOG2E

    def tile_start(j):
        return pl.multiple_of(j * T, T)

    unshifted_ok = bound_ref[0] <= EXP2_SAFE_RANGE

    @pl.when(unshifted_ok)
    def _():
        f_left = [jnp.exp2(jnp.full((1, 1), side_const(NUM_BUCKETS // 2 - 1, m), F32)).astype(BF16)
                  for m in range(2)]
        f_right = [jnp.exp2(jnp.full((1, 1), side_const(NUM_BUCKETS - 1, m), F32)).astype(BF16)
                   for m in range(2)]
        band_idx = [jnp.where(qi == 0, 4, 1), 2, jnp.where(qi == nk - 1, 0, 3)]
        l = [jnp.zeros((1, T), F32) for _ in range(2)]
        acc = [jnp.zeros((DIFF_DV, T), F32) for _ in range(2)]

        def scores(t):
            kt = k_ref[pl.ds(tile_start((qi + (2 + t)) % nk), T), :]
            return [_dot_nt(kt, qm[m]) for m in range(2)]

        s_next = scores(0)
        for t in range(nk):
            j = (qi + (2 + t)) % nk
            vt = vt_ref[:, pl.ds(tile_start(j), T)]
            s_cur = s_next
            if t + 1 < nk:
                s_next = scores(t + 1)
            for m in range(2):
                if t >= nk - 3:
                    p = jnp.exp2(s_cur[m] - band_ref[0, band_idx[t - (nk - 3)], m])
                    l[m] = l[m] + jnp.sum(p, axis=0, keepdims=True)
                    acc[m] = acc[m] + _dot(vt, p.astype(BF16))
                else:
                    f = jnp.where(j < qi, f_left[m], f_right[m])
                    p = jnp.exp2(s_cur[m])
                    l[m] = l[m] + f.astype(F32) * jnp.sum(p, axis=0, keepdims=True)
                    acc[m] = acc[m] + _dot(vt * f, p.astype(BF16))
        for m in range(2):
            l_ref[m] = l[m]
            acc_ref[m] = acc[m]

    @pl.when(jnp.logical_not(unshifted_ok))
    def _():
        m_ref[...] = jnp.full(m_ref.shape, -jnp.inf, F32)
        l_ref[...] = jnp.zeros(l_ref.shape, F32)
        acc_ref[...] = jnp.zeros(acc_ref.shape, F32)

        def body(j, _):
            r0 = tile_start(j)
            kt = k_ref[pl.ds(r0, T), :]
            vt = vt_ref[:, pl.ds(r0, T)]
            idx = jnp.clip(j - qi, -2, 2) + 2
            ss = [_dot_nt(kt, qm[m]) for m in range(2)]
            for m in range(2):
                s = ss[m] - band_ref[0, idx, m]
                m_old = m_ref[m]
                m_new = jnp.maximum(m_old, jnp.max(s, axis=0, keepdims=True))
                alpha = jnp.exp2(m_old - m_new)
                p = jnp.exp2(s - m_new)
                l_ref[m] = alpha * l_ref[m] + jnp.sum(p, axis=0, keepdims=True)
                acc_ref[m] = alpha * acc_ref[m] + _dot(vt, p.astype(BF16))
                m_ref[m] = m_new
            return 0

        lax.fori_loop(0, nk, body, 0)

    lp = lam_ref[...]
    lam = (jnp.exp(jnp.sum(lp[0:1] * lp[1:2], axis=-1, keepdims=True))
           - jnp.exp(jnp.sum(lp[2:3] * lp[3:4], axis=-1, keepdims=True)) + lam_init)
    o = acc_ref[0] / l_ref[0] - lam * (acc_ref[1] / l_ref[1])
    y = o * lax.rsqrt(jnp.mean(o * o, axis=0, keepdims=True) + NORM_EPS) * on_ref[...] * (1.0 - lam_init)
    o_ref[...] = y.T.astype(o_ref.dtype)


def _diff_attn(rel_bias_flat, score_bound, dq, dk, dvt, band, lam_p, onorm, batch, lam_init):
    n = dq.shape[0]
    L = n // batch
    T = ATTN_TILE
    nq = L // T
    return pl.pallas_call(
        functools.partial(_attn_kernel, lam_init=lam_init),
        grid=(batch, N_DIFF_HEADS, nq),
        in_specs=[pl.BlockSpec(memory_space=pltpu.SMEM),
                  pl.BlockSpec(memory_space=pltpu.SMEM),
                  pl.BlockSpec((T, 2 * DIFF_DQK), lambda b, h, i: (b * nq + i, h)),
                  pl.BlockSpec((L, 2 * DIFF_DQK), lambda b, h, i: (b, h)),
                  pl.BlockSpec((DIFF_DV, L), lambda b, h, i: (h, b)),
                  pl.BlockSpec((1, 5, 2, T, T), lambda b, h, i: (h, 0, 0, 0, 0)),
                  pl.BlockSpec((4, DIFF_DQK), lambda b, h, i: (0, 0)),
                  pl.BlockSpec((DIFF_DV, 1), lambda b, h, i: (0, 0))],
        out_specs=pl.BlockSpec((T, DIFF_DV), lambda b, h, i: (b * nq + i, h)),
        out_shape=jax.ShapeDtypeStruct((n, DIFF_V), BF16),
        scratch_shapes=[pltpu.VMEM((2, 1, T), F32), pltpu.VMEM((2, 1, T), F32),
                        pltpu.VMEM((2, DIFF_DV, T), F32)],
        compiler_params=_cparams(("parallel", "parallel", "parallel")),
        name="diff_attn",
    )(rel_bias_flat, score_bound, dq, dk, dvt, band, lam_p, onorm)


def _conv_kernel(cv_ref, w_ref, b_ref, g_ref, beta_ref, y_ref, u_ref):
    L = cv_ref.shape[0]
    R = CONV_ROWS
    P = CONV_PAD
    zeros = jnp.zeros((P, CONV_CH), F32)
    u_ref[:P, :] = zeros
    u_ref[P + L:, :] = zeros
    u_ref[P:P + L, :] = cv_ref[:, :CONV_CH] * _sigmoid(cv_ref[:, CONV_CH:])
    off = P - CONV_K // 2

    def tile(t, _):
        r0 = pl.multiple_of(t * R, R)
        win = u_ref[pl.ds(r0, R + 2 * P), :]
        shifted = [win] + [pltpu.roll(win, R + 2 * P - r, 0) for r in range(1, 8)]
        acc = jnp.zeros((R, CONV_CH), F32) + b_ref[...]
        for k in range(CONV_K):
            a8, r = divmod(off + k, 8)
            acc = acc + w_ref[k:k + 1, :] * shifted[r][8 * a8:8 * a8 + R, :]
        mu = jnp.mean(acc, axis=-1, keepdims=True)
        xc = acc - mu
        y = xc * lax.rsqrt(jnp.mean(xc * xc, axis=-1, keepdims=True) + NORM_EPS) * g_ref[...] + beta_ref[...]
        y_ref[pl.ds(r0, R), :] = _silu(y)
        return 0

    lax.fori_loop(0, L // R, tile, 0)


def _conv(cv, w, b, g, beta, batch):
    n = cv.shape[0]
    L = n // batch
    return pl.pallas_call(
        _conv_kernel,
        grid=(batch,),
        in_specs=[pl.BlockSpec((L, 2 * CONV_CH), lambda i: (i, 0)),
                  _const_spec((CONV_K + 1, CONV_CH)), _const_spec((1, CONV_CH)),
                  _const_spec((1, CONV_CH)), _const_spec((1, CONV_CH))],
        out_specs=pl.BlockSpec((L, CONV_CH), lambda i: (i, 0)),
        out_shape=jax.ShapeDtypeStruct((n, CONV_CH), F32),
        scratch_shapes=[pltpu.VMEM((L + 2 * CONV_PAD, CONV_CH), F32)],
        compiler_params=_cparams(("parallel",)),
        name="conv",
    )(cv, w, b, g, beta)


def _outproj_kernel(x_ref, ya_ref, yb_ref, yc_ref, w_ref, o_ref):
    acc = _dot(ya_ref[...].astype(BF16), w_ref[:GLA_V, :])
    acc = acc + _dot(yb_ref[...].astype(BF16), w_ref[GLA_V:GLA_V + DIFF_V, :])
    acc = acc + _dot(yc_ref[...].astype(BF16), w_ref[GLA_V + DIFF_V:, :])
    o_ref[...] = x_ref[...] + acc


def _outproj(x, ya, yb, yc, w):
    n = x.shape[0]
    tm = TOKEN_TILE

    def row(width):
        return pl.BlockSpec((tm, width), lambda i: (i, 0))

    return pl.pallas_call(
        _outproj_kernel,
        grid=(n // tm,),
        in_specs=[row(D_MODEL), row(GLA_V), row(DIFF_V), row(CONV_CH), _const_spec((D_MODEL, D_MODEL))],
        out_specs=row(D_MODEL),
        out_shape=jax.ShapeDtypeStruct((n, D_MODEL), F32),
        compiler_params=_cparams(("parallel",)),
        name="outproj",
    )(x, ya, yb, yc, w)


def _ple_kernel(x_ref, p_ref, g_ref, wg_ref, wp_ref, o_ref):
    x = x_ref[...]
    h = _rms_rows(x, g_ref[...]).astype(BF16)
    gate = _sigmoid(_dot(h, wg_ref[...]))
    o_ref[...] = x + gate * _dot(p_ref[...].astype(BF16), wp_ref[...])


def _ple(x, p, gain, wg, wp):
    n = x.shape[0]
    tm = TOKEN_TILE

    def row(width):
        return pl.BlockSpec((tm, width), lambda i: (i, 0))

    return pl.pallas_call(
        _ple_kernel,
        grid=(n // tm,),
        in_specs=[row(D_MODEL), row(PLE_DIM), _const_spec((1, D_MODEL)),
                  _const_spec((D_MODEL, D_MODEL)), _const_spec((PLE_DIM, D_MODEL))],
        out_specs=row(D_MODEL),
        out_shape=jax.ShapeDtypeStruct((n, D_MODEL), F32),
        compiler_params=_cparams(("parallel",)),
        name="ple",
    )(x, p, gain, wg, wp)


def _reorder_w_in(w):
    parts = [w[:, 0:768], w[:, 800:2848], w[:, 768:800],
             jnp.zeros((D_MODEL, D_IN_PAD - D_IN), w.dtype)]
    return jnp.concatenate(parts, axis=1).astype(BF16)


def _pad_up(up, row0):
    return jnp.zeros((128, GLA_Q), F32).at[row0:row0 + GLA_RANK].set(up).astype(BF16)


def kernel(x, p, ffn1_norm, ffn1_w_gate, ffn1_w_up, ffn1_w_down, mix_norm, w_in, w_out, gla_gk_up_f, gla_gk_bias_f, gla_gk_up_b, gla_gk_bias_b, gla_out_norm, diff_q_norm, diff_k_norm, diff_lambda, diff_out_norm, rel_bias, conv_dw_w, conv_dw_b, conv_norm_g, conv_norm_b, ffn2_norm, ffn2_w_gate, ffn2_w_up, ffn2_w_down, ple_norm, ple_w_gate, ple_w_proj):
    B, L, _ = x.shape
    depth = w_in.shape[0]
    n = B * L
    xs = x.reshape(n, D_MODEL)
    rb_flat = rel_bias.reshape(-1)
    band = _bias_tiles(rel_bias)

    def row(v):
        return v.reshape(1, -1)

    for i in range(depth):
        xs = _ffn(xs, row(ffn1_norm[i]), ffn1_w_gate[i].astype(BF16), ffn1_w_up[i].astype(BF16),
                  ffn1_w_down[i].astype(BF16))

        gqk, gg, gv, go, dq, dk, dv, cv = _inproj(
            xs, row(mix_norm[i]), _reorder_w_in(w_in[i]),
            _pad_up(gla_gk_up_f[i], 0), row(gla_gk_bias_f[i]),
            _pad_up(gla_gk_up_b[i], GLA_RANK), row(gla_gk_bias_b[i]),
            row(jnp.tile(diff_q_norm[i], DIFF_QK // DIFF_DQK)), row(jnp.tile(diff_k_norm[i], DIFF_QK // DIFF_DQK)))

        y_gla = _gla(gqk, gg, gv, go, row(jnp.tile(gla_out_norm[i], N_GLA_HEADS)), B)
        lam_init = 0.8 - 0.6 * math.exp(-0.3 * i)
        score_bound = LOG2E * (DIFF_DQK ** 0.5 * jnp.max(jnp.abs(diff_q_norm[i])) * jnp.max(jnp.abs(diff_k_norm[i]))
                               + jnp.max(jnp.abs(rel_bias)))
        y_diff = _diff_attn(rb_flat, score_bound.reshape(1), dq, dk, dv.T, band, diff_lambda[i],
                            diff_out_norm[i].reshape(-1, 1), B, lam_init)
        conv_w = jnp.concatenate([conv_dw_w[i], jnp.zeros((1, CONV_CH), F32)], axis=0)
        y_conv = _conv(cv, conv_w, row(conv_dw_b[i]), row(conv_norm_g[i]), row(conv_norm_b[i]), B)

        xs = _outproj(xs, y_gla, y_diff, y_conv, w_out[i].astype(BF16))
        xs = _ffn(xs, row(ffn2_norm[i]), ffn2_w_gate[i].astype(BF16), ffn2_w_up[i].astype(BF16),
                  ffn2_w_down[i].astype(BF16))
        xs = _ple(xs, p[i].reshape(n, PLE_DIM), row(ple_norm[i]), ple_w_gate[i].astype(BF16),
                  ple_w_proj[i].astype(BF16))
    return xs.reshape(B, L, D_MODEL)
```

```python
import functools
import math

import jax
import jax.numpy as jnp
from jax import lax
from jax.experimental import pallas as pl
from jax.experimental.pallas import tpu as pltpu

F32 = jnp.float32
BF16 = jnp.bfloat16

D_MODEL = 1024
N_GLA_HEADS = 4
GLA_DK = 32
GLA_DV = 64
GLA_Q = N_GLA_HEADS * GLA_DK
GLA_V = N_GLA_HEADS * GLA_DV
GLA_RANK = 16
GATE_TEMP = 16.0
GLA_CHUNK = 64
GLA_BLOCK = 4
DIFF_DQK = 64
N_DIFF_HEADS = 4
DIFF_DV = 128
DIFF_QK = 512
DIFF_V = 512
NUM_BUCKETS = 32
MAX_DISTANCE = 128
CONV_CH = 256
CONV_K = 31
D_FF = 2816
PLE_DIM = 256
NORM_EPS = 1e-6
LOG2E = math.log2(math.e)
EXP2_SAFE_RANGE = 100.0
D_IN = 2848
D_IN_PAD = 2944

TOKEN_TILE = 512
FF_CHUNK = 1408
ATTN_TILE = 256
CONV_ROWS = 128
CONV_PAD = 16
VMEM_LIMIT = 56 * 1024 * 1024


def _cparams(sem):
    return pltpu.CompilerParams(dimension_semantics=sem, vmem_limit_bytes=VMEM_LIMIT)


def _dot(a, b):
    return jnp.dot(a, b, preferred_element_type=F32)


def _dot_nt(a, b):
    return lax.dot_general(a, b, (((1,), (1,)), ((), ())), preferred_element_type=F32)


def _rms_rows(x, gain):
    return x * lax.rsqrt(jnp.mean(x * x, axis=-1, keepdims=True) + NORM_EPS) * gain


def _sigmoid(x):
    return 1.0 / (1.0 + jnp.exp(-x))


def _silu(x):
    return x * _sigmoid(x)


def _const_spec(shape):
    return pl.BlockSpec(shape, lambda *_: (0,) * len(shape), pipeline_mode=pl.Buffered(1))


def _ffn_kernel(x_ref, g_ref, wg_ref, wu_ref, wd_ref, o_ref):
    x = x_ref[...]
    h = _rms_rows(x, g_ref[...]).astype(BF16)
    acc = jnp.zeros(x.shape, F32)
    for c0 in range(0, D_FF, FF_CHUNK):
        gate = _dot(h, wg_ref[:, c0:c0 + FF_CHUNK])
        up = _dot(h, wu_ref[:, c0:c0 + FF_CHUNK])
        a = (_silu(gate) * up).astype(BF16)
        acc = acc + _dot(a, wd_ref[c0:c0 + FF_CHUNK, :])
    o_ref[...] = x + 0.5 * acc


def _ffn(x, gain, wg, wu, wd):
    n = x.shape[0]
    row = pl.BlockSpec((TOKEN_TILE, D_MODEL), lambda i: (i, 0))
    return pl.pallas_call(
        _ffn_kernel,
        grid=(n // TOKEN_TILE,),
        in_specs=[row, _const_spec((1, D_MODEL)), _const_spec((D_MODEL, D_FF)),
                  _const_spec((D_MODEL, D_FF)), _const_spec((D_FF, D_MODEL))],
        out_specs=row,
        out_shape=jax.ShapeDtypeStruct((n, D_MODEL), F32),
        compiler_params=_cparams(("parallel",)),
        name="ffn",
    )(x, gain, wg, wu, wd)


def _block_diag_ones(n, blk, dtype):
    r = lax.broadcasted_iota(jnp.int32, (n, n), 0) // blk
    c = lax.broadcasted_iota(jnp.int32, (n, n), 1) // blk
    return jnp.where(r == c, 1.0, 0.0).astype(dtype)


def _group_mean_sq(x, blk):
    n = x.shape[-1]
    return _dot((x * x).astype(BF16), _block_diag_ones(n, blk, BF16)) * (1.0 / blk)


def _log_sigmoid(z):
    return jnp.minimum(z, 0.0) - jnp.log(1.0 + jnp.exp(-jnp.abs(z)))


def _inproj_kernel(x_ref, g_ref, w_ref, upf_ref, bf_ref, upb_ref, bb_ref, qn_ref, kn_ref,
                   gqk_ref, gg_ref, gv_ref, go_ref, dq_ref, dk_ref, dv_ref, cv_ref):
    h = _rms_rows(x_ref[...], g_ref[...]).astype(BF16)
    proj = _dot(h, w_ref[...])
    gqk_ref[:, :GLA_Q] = proj[:, :GLA_Q] * (GLA_DK ** -0.5)
    gqk_ref[:, GLA_Q:] = proj[:, GLA_Q:2 * GLA_Q]
    gv_ref[...] = proj[:, 256:512].astype(BF16)
    go_ref[...] = _silu(proj[:, 512:768]).astype(BF16)
    low = proj[:, 2816:2944].astype(BF16)
    zf = _dot(low, upf_ref[...]) + bf_ref[...]
    zb = _dot(low, upb_ref[...]) + bb_ref[...]
    gg_ref[:, :GLA_Q] = _log_sigmoid(zf) / GATE_TEMP
    gg_ref[:, GLA_Q:] = _log_sigmoid(zb) / GATE_TEMP
    dq = proj[:, 768:1280]
    dq_ref[...] = (dq * lax.rsqrt(_group_mean_sq(dq, DIFF_DQK) + NORM_EPS) * qn_ref[...]
                   * (DIFF_DQK ** -0.5 * LOG2E)).astype(BF16)
    dk = proj[:, 1280:1792]
    dk_ref[...] = (dk * lax.rsqrt(_group_mean_sq(dk, DIFF_DQK) + NORM_EPS) * kn_ref[...]).astype(BF16)
    dv_ref[...] = proj[:, 1792:2304].astype(BF16)
    cv_ref[...] = proj[:, 2304:2816]


def _inproj(x, gain, w, upf, bf, upb, bb, qn, kn):
    n = x.shape[0]
    tm = TOKEN_TILE

    def row(width):
        return pl.BlockSpec((tm, width), lambda i: (i, 0))

    outs = [(2 * GLA_Q, F32), (2 * GLA_Q, F32), (GLA_V, BF16), (GLA_V, BF16),
            (DIFF_QK, BF16), (DIFF_QK, BF16), (DIFF_V, BF16), (2 * CONV_CH, F32)]
    return pl.pallas_call(
        _inproj_kernel,
        grid=(n // tm,),
        in_specs=[row(D_MODEL), _const_spec((1, D_MODEL)), _const_spec((D_MODEL, D_IN_PAD)),
                  _const_spec((128, GLA_Q)), _const_spec((1, GLA_Q)),
                  _const_spec((128, GLA_Q)), _const_spec((1, GLA_Q)),
                  _const_spec((1, DIFF_QK)), _const_spec((1, DIFF_QK))],
        out_specs=[row(wd) for wd, _ in outs],
        out_shape=[jax.ShapeDtypeStruct((n, wd), dt) for wd, dt in outs],
        compiler_params=_cparams(("parallel",)),
        name="inproj",
    )(x, gain, w, upf, bf, upb, bb, qn, kn)


def _gla_block_kernel(qk_ref, g_ref, v_ref, go_ref, on_ref, y_ref, of_ref, st_ref):
    C = GLA_CHUNK
    G = GLA_BLOCK
    R = G * C
    L = qk_ref.shape[0]
    nblk = L // R
    H = N_GLA_HEADS

    def iota(shape, d):
        return lax.broadcasted_iota(jnp.int32, shape, d)

    kmask = iota((H * C, GLA_Q), 0) // C == iota((H * C, GLA_Q), 1) // GLA_DK
    vmask = iota((H * C, GLA_V), 0) // C == iota((H * C, GLA_V), 1) // GLA_DV
    smask = iota((GLA_V, GLA_Q), 0) // GLA_DV == iota((GLA_V, GLA_Q), 1) // GLA_DK
    pos_a = iota((C, H * C), 0)
    pos_b = iota((C, H * C), 1) % C
    ri, ci = iota((R, R), 0), iota((R, R), 1)
    same_chunk = ri // C == ci // C
    cums = [jnp.where(jnp.logical_and(same_chunk, ci <= ri), 1.0, 0.0).astype(BF16),
            jnp.where(jnp.logical_and(same_chunk, ci >= ri), 1.0, 0.0).astype(BF16)]
    amasks = [pos_b <= pos_a, pos_b >= pos_a]
    edges = [C - 1, 0]
    orders = [list(range(G)), list(range(G - 1, -1, -1))]

    st_ref[...] = jnp.zeros(st_ref.shape, F32)

    def step(t, _):
        r0 = [pl.multiple_of(t * R, R), pl.multiple_of((nblk - 1 - t) * R, R)]
        rows = [pl.ds(r0[0], R), pl.ds(r0[1], R)]
        dirs = (0, 1)
        b, k, v, qt = {}, {}, {}, {}
        for d in dirs:
            g = g_ref[rows[d], d * GLA_Q:(d + 1) * GLA_Q]
            g_hi = g.astype(BF16)
            g_lo = (g - g_hi.astype(F32)).astype(BF16)
            b[d] = _dot(cums[d], g_hi) + _dot(cums[d], g_lo)
        a, ds, decay = {}, {}, {}
        for d in dirs:
            q = qk_ref[rows[d], :GLA_Q]
            k[d] = qk_ref[rows[d], GLA_Q:]
            v[d] = v_ref[rows[d], :].astype(F32)
            qt[d] = (q * jnp.exp(b[d])).astype(BF16)
            kt = k[d] * jnp.exp(-b[d])
            for c in range(G):
                sl = slice(c * C, (c + 1) * C)
                b_c = b[d][sl]
                b_edge = b_c[edges[d]:edges[d] + 1, :]
                decay[d, c] = jnp.exp(b_edge)
                kend = (k[d][sl] * jnp.exp(b_edge - b_c)).astype(BF16)
                kstack = jnp.where(kmask, jnp.concatenate([kt[sl]] * H, axis=0), 0.0).astype(BF16)
                a[d, c] = jnp.where(amasks[d], _dot_nt(qt[d][sl], kstack), 0.0).astype(BF16)
                ds[d, c] = _dot(v[d][sl].T.astype(BF16), kend)
        for d in dirs:
            s = st_ref[d]
            s_in = {}
            for c in orders[d]:
                s_in[c] = s.astype(BF16)
                s = s * decay[d, c] + jnp.where(smask, ds[d, c], 0.0)
            st_ref[d] = s
            out_ref = y_ref if d else of_ref
            for c in range(G):
                sl = slice(c * C, (c + 1) * C)
                vstack = jnp.where(vmask, jnp.concatenate([v[d][sl]] * H, axis=0), 0.0).astype(BF16)
                o = _dot(a[d, c], vstack) + _dot_nt(qt[d][sl], s_in[c])
                out_ref[pl.ds(pl.multiple_of(r0[d] + c * C, C), C), :] = o
        return 0

    lax.fori_loop(0, nblk, step, 0)

    norm_bd = _block_diag_ones(GLA_V, GLA_DV, BF16)

    def finish(t, _):
        rows = pl.ds(pl.multiple_of(t * R, R), R)
        tot = of_ref[rows, :] + y_ref[rows, :]
        ms = _dot((tot * tot).astype(BF16), norm_bd) * (1.0 / GLA_DV)
        y_ref[rows, :] = tot * lax.rsqrt(ms + NORM_EPS) * on_ref[...] * go_ref[rows, :].astype(F32)
        return 0

    lax.fori_loop(0, nblk, finish, 0)


def _gla(qk, g, v, go, onorm, batch):
    n = qk.shape[0]
    L = n // batch

    def seq(width):
        return pl.BlockSpec((L, width), lambda b: (b, 0))

    return pl.pallas_call(
        _gla_block_kernel,
        grid=(batch,),
        in_specs=[seq(2 * GLA_Q), seq(2 * GLA_Q), seq(GLA_V), seq(GLA_V), _const_spec((1, GLA_V))],
        out_specs=seq(GLA_V),
        out_shape=jax.ShapeDtypeStruct((n, GLA_V), F32),
        scratch_shapes=[pltpu.VMEM((L, GLA_V), F32), pltpu.VMEM((2, GLA_V, GLA_Q), F32)],
        compiler_params=_cparams(("parallel",)),
        name="gla",
    )(qk, g, v, go, onorm)


def _bias_kernel(rb_ref, o_ref):
    T = ATTN_TILE
    h = pl.program_id(0)
    nb = NUM_BUCKETS // 2
    max_exact = nb // 2
    for d in range(5):
        rel = (lax.broadcasted_iota(jnp.int32, (T, T), 0) - lax.broadcasted_iota(jnp.int32, (T, T), 1)
               + (d - 2) * T)
        ret = jnp.where(rel > 0, nb, 0)
        n = jnp.abs(rel)
        large = max_exact + (jnp.log(jnp.maximum(n, 1).astype(F32) / max_exact)
                             / math.log(MAX_DISTANCE / max_exact) * (nb - max_exact)).astype(jnp.int32)
        large = jnp.minimum(large, nb - 1)
        bucket = ret + jnp.where(n < max_exact, n, large)
        for m in range(2):
            tile = jnp.zeros((T, T), F32)
            for bkt in range(NUM_BUCKETS):
                tile = jnp.where(bucket == bkt, rb_ref[(bkt * N_DIFF_HEADS + h) * 2 + m], tile)
            o_ref[0, d, m] = tile * (-LOG2E)


def _bias_tiles(rel_bias):
    T = ATTN_TILE
    return pl.pallas_call(
        _bias_kernel,
        grid=(N_DIFF_HEADS,),
        in_specs=[pl.BlockSpec(memory_space=pltpu.SMEM)],
        out_specs=pl.BlockSpec((1, 5, 2, T, T), lambda h: (h, 0, 0, 0, 0)),
        out_shape=jax.ShapeDtypeStruct((N_DIFF_HEADS, 5, 2, T, T), F32),
        compiler_params=_cparams(("parallel",)),
        name="t5_bias",
    )(rel_bias.reshape(-1))


def _attn_kernel(*refs, lam_init):
    nq = refs[2].shape[0] // ATTN_TILE

    def body(qi, _):
        _attn_query_tile(qi, *refs, lam_init=lam_init)
        return 0

    lax.fori_loop(0, nq, body, 0)


def _attn_query_tile(qi, rb_ref, bound_ref, q_ref, k_ref, vt_ref, band_ref, lam_ref, on_ref, o_ref,
                     m_ref, l_ref, acc_ref, *, lam_init):
    T = ATTN_TILE
    L = k_ref.shape[0]
    nk = L // T
    h = pl.program_id(0)
    q = q_ref[pl.ds(pl.multiple_of(qi * T, T), T), :]
    lane = lax.broadcasted_iota(jnp.int32, q.shape, 1)
    qm = [jnp.where(lane < DIFF_DQK, q, 0).astype(BF16), jnp.where(lane >= DIFF_DQK, q, 0).astype(BF16)]

    def side_const(bucket, m):
        return rb_ref[(bucket * N_DIFF_HEADS + h) * 2 + m] * LOG2E

    def tile_start(j):
        return pl.multiple_of(j * T, T)

    unshifted_ok = bound_ref[0] <= EXP2_SAFE_RANGE

    @pl.when(unshifted_ok)
    def _():
        f_left = [jnp.exp2(jnp.full((1, 1), side_const(NUM_BUCKETS // 2 - 1, m), F32)).astype(BF16)
                  for m in range(2)]
        f_right = [jnp.exp2(jnp.full((1, 1), side_const(NUM_BUCKETS - 1, m), F32)).astype(BF16)
                   for m in range(2)]
        band_idx = [jnp.where(qi == 0, 4, 1), 2, jnp.where(qi == nk - 1, 0, 3)]
        l = [jnp.zeros((1, T), F32) for _ in range(2)]
        acc = [jnp.zeros((DIFF_DV, T), F32) for _ in range(2)]

        def scores(t):
            kt = k_ref[pl.ds(tile_start((qi + (2 + t)) % nk), T), :]
            return [_dot_nt(kt, qm[m]) for m in range(2)]

        s_next = scores(0)
        for t in range(nk):
            j = (qi + (2 + t)) % nk
            vt = vt_ref[:, pl.ds(tile_start(j), T)]
            s_cur = s_next
            if t + 1 < nk:
                s_next = scores(t + 1)
            for m in range(2):
                if t >= nk - 3:
                    p = jnp.exp2(s_cur[m] - band_ref[0, band_idx[t - (nk - 3)], m])
                    l[m] = l[m] + jnp.sum(p, axis=0, keepdims=True)
                    acc[m] = acc[m] + _dot(vt, p.astype(BF16))
                else:
                    f = jnp.where(j < qi, f_left[m], f_right[m])
                    p = jnp.exp2(s_cur[m])
                    l[m] = l[m] + f.astype(F32) * jnp.sum(p, axis=0, keepdims=True)
                    acc[m] = acc[m] + _dot(vt * f, p.astype(BF16))
        for m in range(2):
            l_ref[m] = l[m]
            acc_ref[m] = acc[m]

    @pl.when(jnp.logical_not(unshifted_ok))
    def _():
        m_ref[...] = jnp.full(m_ref.shape, -jnp.inf, F32)
        l_ref[...] = jnp.zeros(l_ref.shape, F32)
        acc_ref[...] = jnp.zeros(acc_ref.shape, F32)

        def body(j, _):
            r0 = tile_start(j)
            kt = k_ref[pl.ds(r0, T), :]
            vt = vt_ref[:, pl.ds(r0, T)]
            idx = jnp.clip(j - qi, -2, 2) + 2
            ss = [_dot_nt(kt, qm[m]) for m in range(2)]
            for m in range(2):
                s = ss[m] - band_ref[0, idx, m]
                m_old = m_ref[m]
                m_new = jnp.maximum(m_old, jnp.max(s, axis=0, keepdims=True))
                alpha = jnp.exp2(m_old - m_new)
                p = jnp.exp2(s - m_new)
                l_ref[m] = alpha * l_ref[m] + jnp.sum(p, axis=0, keepdims=True)
                acc_ref[m] = alpha * acc_ref[m] + _dot(vt, p.astype(BF16))
                m_ref[m] = m_new
            return 0

        lax.fori_loop(0, nk, body, 0)

    lp = lam_ref[...]
    lam = (jnp.exp(jnp.sum(lp[0:1] * lp[1:2], axis=-1, keepdims=True))
           - jnp.exp(jnp.sum(lp[2:3] * lp[3:4], axis=-1, keepdims=True)) + lam_init)
    o = acc_ref[0] / l_ref[0] - lam * (acc_ref[1] / l_ref[1])
    y = o * lax.rsqrt(jnp.mean(o * o, axis=0, keepdims=True) + NORM_EPS) * on_ref[...] * (1.0 - lam_init)
    o_ref[pl.ds(pl.multiple_of(qi * T, T), T), :] = y.T.astype(o_ref.dtype)


def _diff_attn(rel_bias_flat, score_bound, dq, dk, dvt, band, lam_p, onorm, batch, lam_init):
    n = dq.shape[0]
    L = n // batch
    T = ATTN_TILE
    seq = pl.BlockSpec((L, 2 * DIFF_DQK), lambda h, b: (b, h))
    return pl.pallas_call(
        functools.partial(_attn_kernel, lam_init=lam_init),
        grid=(N_DIFF_HEADS, batch),
        in_specs=[pl.BlockSpec(memory_space=pltpu.SMEM),
                  pl.BlockSpec(memory_space=pltpu.SMEM),
                  seq, seq,
                  pl.BlockSpec((DIFF_DV, L), lambda h, b: (h, b)),
                  pl.BlockSpec((1, 5, 2, T, T), lambda h, b: (h, 0, 0, 0, 0)),
                  pl.BlockSpec((4, DIFF_DQK), lambda h, b: (0, 0)),
                  pl.BlockSpec((DIFF_DV, 1), lambda h, b: (0, 0))],
        out_specs=pl.BlockSpec((L, DIFF_DV), lambda h, b: (b, h)),
        out_shape=jax.ShapeDtypeStruct((n, DIFF_V), BF16),
        scratch_shapes=[pltpu.VMEM((2, 1, T), F32), pltpu.VMEM((2, 1, T), F32),
                        pltpu.VMEM((2, DIFF_DV, T), F32)],
        compiler_params=_cparams(("parallel", "parallel")),
        name="diff_attn",
    )(rel_bias_flat, score_bound, dq, dk, dvt, band, lam_p, onorm)


def _conv_kernel(cv_ref, w_ref, b_ref, g_ref, beta_ref, y_ref, u_ref):
    L = cv_ref.shape[0]
    R = CONV_ROWS
    P = CONV_PAD
    zeros = jnp.zeros((P, CONV_CH), F32)
    u_ref[:P, :] = zeros
    u_ref[P + L:, :] = zeros
    u_ref[P:P + L, :] = cv_ref[:, :CONV_CH] * _sigmoid(cv_ref[:, CONV_CH:])
    off = P - CONV_K // 2

    def tile(t, _):
        r0 = pl.multiple_of(t * R, R)
        win = u_ref[pl.ds(r0, R + 2 * P), :]
        shifted = [win] + [pltpu.roll(win, R + 2 * P - r, 0) for r in range(1, 8)]
        acc = jnp.zeros((R, CONV_CH), F32) + b_ref[...]
        for k in range(CONV_K):
            a8, r = divmod(off + k, 8)
            acc = acc + w_ref[k:k + 1, :] * shifted[r][8 * a8:8 * a8 + R, :]
        mu = jnp.mean(acc, axis=-1, keepdims=True)
        xc = acc - mu
        y = xc * lax.rsqrt(jnp.mean(xc * xc, axis=-1, keepdims=True) + NORM_EPS) * g_ref[...] + beta_ref[...]
        y_ref[pl.ds(r0, R), :] = _silu(y)
        return 0

    lax.fori_loop(0, L // R, tile, 0)


def _conv(cv, w, b, g, beta, batch):
    n = cv.shape[0]
    L = n // batch
    return pl.pallas_call(
        _conv_kernel,
        grid=(batch,),
        in_specs=[pl.BlockSpec((L, 2 * CONV_CH), lambda i: (i, 0)),
                  _const_spec((CONV_K + 1, CONV_CH)), _const_spec((1, CONV_CH)),
                  _const_spec((1, CONV_CH)), _const_spec((1, CONV_CH))],
        out_specs=pl.BlockSpec((L, CONV_CH), lambda i: (i, 0)),
        out_shape=jax.ShapeDtypeStruct((n, CONV_CH), F32),
        scratch_shapes=[pltpu.VMEM((L + 2 * CONV_PAD, CONV_CH), F32)],
        compiler_params=_cparams(("parallel",)),
        name="conv",
    )(cv, w, b, g, beta)


def _outproj_kernel(x_ref, ya_ref, yb_ref, yc_ref, w_ref, o_ref):
    acc = _dot(ya_ref[...].astype(BF16), w_ref[:GLA_V, :])
    acc = acc + _dot(yb_ref[...].astype(BF16), w_ref[GLA_V:GLA_V + DIFF_V, :])
    acc = acc + _dot(yc_ref[...].astype(BF16), w_ref[GLA_V + DIFF_V:, :])
    o_ref[...] = x_ref[...] + acc


def _outproj(x, ya, yb, yc, w):
    n = x.shape[0]
    tm = TOKEN_TILE

    def row(width):
        return pl.BlockSpec((tm, width), lambda i: (i, 0))

    return pl.pallas_call(
        _outproj_kernel,
        grid=(n // tm,),
        in_specs=[row(D_MODEL), row(GLA_V), row(DIFF_V), row(CONV_CH), _const_spec((D_MODEL, D_MODEL))],
        out_specs=row(D_MODEL),
        out_shape=jax.ShapeDtypeStruct((n, D_MODEL), F32),
        compiler_params=_cparams(("parallel",)),
        name="outproj",
    )(x, ya, yb, yc, w)


def _ple_kernel(x_ref, p_ref, g_ref, wg_ref, wp_ref, o_ref):
    x = x_ref[...]
    h = _rms_rows(x, g_ref[...]).astype(BF16)
    gate = _sigmoid(_dot(h, wg_ref[...]))
    o_ref[...] = x + gate * _dot(p_ref[...].astype(BF16), wp_ref[...])


def _ple(x, p, gain, wg, wp):
    n = x.shape[0]
    tm = TOKEN_TILE

    def row(width):
        return pl.BlockSpec((tm, width), lambda i: (i, 0))

    return pl.pallas_call(
        _ple_kernel,
        grid=(n // tm,),
        in_specs=[row(D_MODEL), row(PLE_DIM), _const_spec((1, D_MODEL)),
                  _const_spec((D_MODEL, D_MODEL)), _const_spec((PLE_DIM, D_MODEL))],
        out_specs=row(D_MODEL),
        out_shape=jax.ShapeDtypeStruct((n, D_MODEL), F32),
        compiler_params=_cparams(("parallel",)),
        name="ple",
    )(x, p, gain, wg, wp)


def _reorder_w_in(w):
    parts = [w[:, 0:768], w[:, 800:2848], w[:, 768:800],
             jnp.zeros((D_MODEL, D_IN_PAD - D_IN), w.dtype)]
    return jnp.concatenate(parts, axis=1).astype(BF16)


def _pad_up(up, row0):
    return jnp.zeros((128, GLA_Q), F32).at[row0:row0 + GLA_RANK].set(up).astype(BF16)


def kernel(x, p, ffn1_norm, ffn1_w_gate, ffn1_w_up, ffn1_w_down, mix_norm, w_in, w_out, gla_gk_up_f, gla_gk_bias_f, gla_gk_up_b, gla_gk_bias_b, gla_out_norm, diff_q_norm, diff_k_norm, diff_lambda, diff_out_norm, rel_bias, conv_dw_w, conv_dw_b, conv_norm_g, conv_norm_b, ffn2_norm, ffn2_w_gate, ffn2_w_up, ffn2_w_down, ple_norm, ple_w_gate, ple_w_proj):
    B, L, _ = x.shape
    depth = w_in.shape[0]
    n = B * L
    xs = x.reshape(n, D_MODEL)
    rb_flat = rel_bias.reshape(-1)
    band = _bias_tiles(rel_bias)

    def row(v):
        return v.reshape(1, -1)

    for i in range(depth):
        xs = _ffn(xs, row(ffn1_norm[i]), ffn1_w_gate[i].astype(BF16), ffn1_w_up[i].astype(BF16),
                  ffn1_w_down[i].astype(BF16))

        gqk, gg, gv, go, dq, dk, dv, cv = _inproj(
            xs, row(mix_norm[i]), _reorder_w_in(w_in[i]),
            _pad_up(gla_gk_up_f[i], 0), row(gla_gk_bias_f[i]),
            _pad_up(gla_gk_up_b[i], GLA_RANK), row(gla_gk_bias_b[i]),
            row(jnp.tile(diff_q_norm[i], DIFF_QK // DIFF_DQK)), row(jnp.tile(diff_k_norm[i], DIFF_QK // DIFF_DQK)))

        y_gla = _gla(gqk, gg, gv, go, row(jnp.tile(gla_out_norm[i], N_GLA_HEADS)), B)
        lam_init = 0.8 - 0.6 * math.exp(-0.3 * i)
        score_bound = LOG2E * (DIFF_DQK ** 0.5 * jnp.max(jnp.abs(diff_q_norm[i])) * jnp.max(jnp.abs(diff_k_norm[i]))
                               + jnp.max(jnp.abs(rel_bias)))
        y_diff = _diff_attn(rb_flat, score_bound.reshape(1), dq, dk, dv.T, band, diff_lambda[i],
                            diff_out_norm[i].reshape(-1, 1), B, lam_init)
        conv_w = jnp.concatenate([conv_dw_w[i], jnp.zeros((1, CONV_CH), F32)], axis=0)
        y_conv = _conv(cv, conv_w, row(conv_dw_b[i]), row(conv_norm_g[i]), row(conv_norm_b[i]), B)

        xs = _outproj(xs, y_gla, y_diff, y_conv, w_out[i].astype(BF16))
        xs = _ffn(xs, row(ffn2_norm[i]), ffn2_w_gate[i].astype(BF16), ffn2_w_up[i].astype(BF16),
                  ffn2_w_down[i].astype(BF16))
        xs = _ple(xs, p[i].reshape(n, PLE_DIM), row(ple_norm[i]), ple_w_gate[i].astype(BF16),
                  ple_w_proj[i].astype(BF16))
    return xs.reshape(B, L, D_MODEL)
```

```python
import functools
import math

import jax
import jax.numpy as jnp
from jax import lax
from jax.experimental import pallas as pl
from jax.experimental.pallas import tpu as pltpu

F32 = jnp.float32
BF16 = jnp.bfloat16

D_MODEL = 1024
N_GLA_HEADS = 4
GLA_DK = 32
GLA_DV = 64
GLA_Q = N_GLA_HEADS * GLA_DK
GLA_V = N_GLA_HEADS * GLA_DV
GLA_RANK = 16
GATE_TEMP = 16.0
GLA_CHUNK = 64
GLA_BLOCK = 4
DIFF_DQK = 64
N_DIFF_HEADS = 4
DIFF_DV = 128
DIFF_QK = 512
DIFF_V = 512
NUM_BUCKETS = 32
MAX_DISTANCE = 128
CONV_CH = 256
CONV_K = 31
D_FF = 2816
PLE_DIM = 256
NORM_EPS = 1e-6
LOG2E = math.log2(math.e)
EXP2_SAFE_RANGE = 100.0
D_IN = 2848

TOKEN_TILE = 512
FF_CHUNK = 1408
ATTN_TILE = 256
CONV_ROWS = 128
CONV_PAD = 16
VMEM_LIMIT = 56 * 1024 * 1024


def _cparams(sem):
    return pltpu.CompilerParams(dimension_semantics=sem, vmem_limit_bytes=VMEM_LIMIT)


def _dot(a, b):
    return jnp.dot(a, b, preferred_element_type=F32)


def _dot_nt(a, b):
    return lax.dot_general(a, b, (((1,), (1,)), ((), ())), preferred_element_type=F32)


def _rms_rows(x, gain):
    return x * lax.rsqrt(jnp.mean(x * x, axis=-1, keepdims=True) + NORM_EPS) * gain


def _sigmoid(x):
    return 1.0 / (1.0 + jnp.exp(-x))


def _silu(x):
    return x * _sigmoid(x)


def _const_spec(shape):
    return pl.BlockSpec(shape, lambda *_: (0,) * len(shape), pipeline_mode=pl.Buffered(1))


def _swiglu_half_step(x, g_ref, wg_ref, wu_ref, wd_ref):
    h = _rms_rows(x, g_ref[...]).astype(BF16)
    acc = jnp.zeros(x.shape, F32)
    for c0 in range(0, D_FF, FF_CHUNK):
        gate = _dot(h, wg_ref[:, c0:c0 + FF_CHUNK])
        up = _dot(h, wu_ref[:, c0:c0 + FF_CHUNK])
        a = (_silu(gate) * up).astype(BF16)
        acc = acc + _dot(a, wd_ref[c0:c0 + FF_CHUNK, :])
    return x + 0.5 * acc


def _ffn_kernel(x_ref, g_ref, wg_ref, wu_ref, wd_ref, o_ref):
    o_ref[...] = _swiglu_half_step(x_ref[...], g_ref, wg_ref, wu_ref, wd_ref)


def _ffn(x, gain, wg, wu, wd):
    n = x.shape[0]
    row = pl.BlockSpec((TOKEN_TILE, D_MODEL), lambda i: (i, 0))
    return pl.pallas_call(
        _ffn_kernel,
        grid=(n // TOKEN_TILE,),
        in_specs=[row, _const_spec((1, D_MODEL)), _const_spec((D_MODEL, D_FF)),
                  _const_spec((D_MODEL, D_FF)), _const_spec((D_FF, D_MODEL))],
        out_specs=row,
        out_shape=jax.ShapeDtypeStruct((n, D_MODEL), F32),
        compiler_params=_cparams(("parallel",)),
        name="ffn",
    )(x, gain, wg, wu, wd)


def _block_diag_ones(n, blk, dtype):
    r = lax.broadcasted_iota(jnp.int32, (n, n), 0) // blk
    c = lax.broadcasted_iota(jnp.int32, (n, n), 1) // blk
    return jnp.where(r == c, 1.0, 0.0).astype(dtype)


def _group_mean_sq(x, blk):
    n = x.shape[-1]
    return _dot((x * x).astype(BF16), _block_diag_ones(n, blk, BF16)) * (1.0 / blk)


def _log_sigmoid(z):
    return jnp.minimum(z, 0.0) - jnp.log(1.0 + jnp.exp(-jnp.abs(z)))


def _inproj_kernel(x_ref, g_ref, w_ref, upf_ref, bf_ref, upb_ref, bb_ref, qn_ref, kn_ref,
                   gqk_ref, gg_ref, gv_ref, go_ref, dq_ref, dk_ref, dvt_ref, cv_ref):
    h = _rms_rows(x_ref[...], g_ref[...]).astype(BF16)
    proj = _dot(h, w_ref[...])
    gqk_ref[:, :GLA_Q] = proj[:, :GLA_Q] * (GLA_DK ** -0.5)
    gqk_ref[:, GLA_Q:] = proj[:, GLA_Q:2 * GLA_Q]
    gv_ref[...] = proj[:, 256:512].astype(BF16)
    go_ref[...] = _silu(proj[:, 512:768]).astype(BF16)
    low = proj[:, 768:896].astype(BF16)
    zf = _dot(low, upf_ref[...]) + bf_ref[...]
    zb = _dot(low, upb_ref[...]) + bb_ref[...]
    gg_ref[:, :GLA_Q] = _log_sigmoid(zf) / GATE_TEMP
    gg_ref[:, GLA_Q:] = _log_sigmoid(zb) / GATE_TEMP
    rest = proj[:, 768 + 2 * GLA_RANK:]
    dq = rest[:, :512]
    dq_ref[...] = (dq * lax.rsqrt(_group_mean_sq(dq, DIFF_DQK) + NORM_EPS) * qn_ref[...]
                   * (DIFF_DQK ** -0.5 * LOG2E)).astype(BF16)
    dk = rest[:, 512:1024]
    dk_ref[...] = (dk * lax.rsqrt(_group_mean_sq(dk, DIFF_DQK) + NORM_EPS) * kn_ref[...]).astype(BF16)
    dvt_ref[...] = rest[:, 1024:1536].T.astype(BF16)
    cv_ref[...] = rest[:, 1536:2048]


def _inproj(x, gain, w, upf, bf, upb, bb, qn, kn):
    n = x.shape[0]
    tm = TOKEN_TILE

    def row(width):
        return pl.BlockSpec((tm, width), lambda i: (i, 0))

    outs = [(2 * GLA_Q, F32), (2 * GLA_Q, F32), (GLA_V, BF16), (GLA_V, BF16),
            (DIFF_QK, BF16), (DIFF_QK, BF16), None, (2 * CONV_CH, F32)]
    dvt_spec = pl.BlockSpec((DIFF_V, tm), lambda i: (0, i))
    return pl.pallas_call(
        _inproj_kernel,
        grid=(n // tm,),
        in_specs=[row(D_MODEL), _const_spec((1, D_MODEL)), _const_spec((D_MODEL, D_IN)),
                  _const_spec((128, GLA_Q)), _const_spec((1, GLA_Q)),
                  _const_spec((128, GLA_Q)), _const_spec((1, GLA_Q)),
                  _const_spec((1, DIFF_QK)), _const_spec((1, DIFF_QK))],
        out_specs=[row(o[0]) if o else dvt_spec for o in outs],
        out_shape=[jax.ShapeDtypeStruct((n, o[0]), o[1]) if o else jax.ShapeDtypeStruct((DIFF_V, n), BF16)
                   for o in outs],
        compiler_params=_cparams(("parallel",)),
        name="inproj",
    )(x, gain, w, upf, bf, upb, bb, qn, kn)


def _gla_block_kernel(qk_ref, g_ref, v_ref, go_ref, on_ref, y_ref, of_ref, st_ref):
    C = GLA_CHUNK
    G = GLA_BLOCK
    R = G * C
    L = qk_ref.shape[0]
    nblk = L // R
    H = N_GLA_HEADS

    def iota(shape, d):
        return lax.broadcasted_iota(jnp.int32, shape, d)

    kmask = iota((H * C, GLA_Q), 0) // C == iota((H * C, GLA_Q), 1) // GLA_DK
    vmask = iota((H * C, GLA_V), 0) // C == iota((H * C, GLA_V), 1) // GLA_DV
    smask = iota((GLA_V, GLA_Q), 0) // GLA_DV == iota((GLA_V, GLA_Q), 1) // GLA_DK
    pos_a = iota((C, H * C), 0)
    pos_b = iota((C, H * C), 1) % C
    ri, ci = iota((R, R), 0), iota((R, R), 1)
    same_chunk = ri // C == ci // C
    cums = [jnp.where(jnp.logical_and(same_chunk, ci <= ri), 1.0, 0.0).astype(BF16),
            jnp.where(jnp.logical_and(same_chunk, ci >= ri), 1.0, 0.0).astype(BF16)]
    amasks = [pos_b <= pos_a, pos_b >= pos_a]
    edges = [C - 1, 0]
    orders = [list(range(G)), list(range(G - 1, -1, -1))]

    st_ref[...] = jnp.zeros(st_ref.shape, F32)

    def step(t, _):
        r0 = [pl.multiple_of(t * R, R), pl.multiple_of((nblk - 1 - t) * R, R)]
        rows = [pl.ds(r0[0], R), pl.ds(r0[1], R)]
        dirs = (0, 1)
        b, k, v, qt = {}, {}, {}, {}
        for d in dirs:
            g = g_ref[rows[d], d * GLA_Q:(d + 1) * GLA_Q]
            g_hi = g.astype(BF16)
            g_lo = (g - g_hi.astype(F32)).astype(BF16)
            b[d] = _dot(cums[d], g_hi) + _dot(cums[d], g_lo)
        a, ds, decay = {}, {}, {}
        for d in dirs:
            q = qk_ref[rows[d], :GLA_Q]
            k[d] = qk_ref[rows[d], GLA_Q:]
            v[d] = v_ref[rows[d], :].astype(F32)
            qt[d] = (q * jnp.exp(b[d])).astype(BF16)
            kt = k[d] * jnp.exp(-b[d])
            for c in range(G):
                sl = slice(c * C, (c + 1) * C)
                b_c = b[d][sl]
                b_edge = b_c[edges[d]:edges[d] + 1, :]
                decay[d, c] = jnp.exp(b_edge)
                kend = (k[d][sl] * jnp.exp(b_edge - b_c)).astype(BF16)
                kstack = jnp.where(kmask, jnp.concatenate([kt[sl]] * H, axis=0), 0.0).astype(BF16)
                a[d, c] = jnp.where(amasks[d], _dot_nt(qt[d][sl], kstack), 0.0).astype(BF16)
                ds[d, c] = _dot(v[d][sl].T.astype(BF16), kend)
        for d in dirs:
            s = st_ref[d]
            s_in = {}
            for c in orders[d]:
                s_in[c] = s.astype(BF16)
                s = s * decay[d, c] + jnp.where(smask, ds[d, c], 0.0)
            st_ref[d] = s
            out_ref = y_ref if d else of_ref
            for c in range(G):
                sl = slice(c * C, (c + 1) * C)
                vstack = jnp.where(vmask, jnp.concatenate([v[d][sl]] * H, axis=0), 0.0).astype(BF16)
                o = _dot(a[d, c], vstack) + _dot_nt(qt[d][sl], s_in[c])
                out_ref[pl.ds(pl.multiple_of(r0[d] + c * C, C), C), :] = o
        return 0

    lax.fori_loop(0, nblk, step, 0)

    norm_bd = _block_diag_ones(GLA_V, GLA_DV, BF16)

    def finish(t, _):
        rows = pl.ds(pl.multiple_of(t * R, R), R)
        tot = of_ref[rows, :] + y_ref[rows, :]
        ms = _dot((tot * tot).astype(BF16), norm_bd) * (1.0 / GLA_DV)
        y_ref[rows, :] = tot * lax.rsqrt(ms + NORM_EPS) * on_ref[...] * go_ref[rows, :].astype(F32)
        return 0

    lax.fori_loop(0, nblk, finish, 0)


def _gla(qk, g, v, go, onorm, batch):
    n = qk.shape[0]
    L = n // batch

    def seq(width):
        return pl.BlockSpec((L, width), lambda b: (b, 0))

    return pl.pallas_call(
        _gla_block_kernel,
        grid=(batch,),
        in_specs=[seq(2 * GLA_Q), seq(2 * GLA_Q), seq(GLA_V), seq(GLA_V), _const_spec((1, GLA_V))],
        out_specs=seq(GLA_V),
        out_shape=jax.ShapeDtypeStruct((n, GLA_V), F32),
        scratch_shapes=[pltpu.VMEM((L, GLA_V), F32), pltpu.VMEM((2, GLA_V, GLA_Q), F32)],
        compiler_params=_cparams(("parallel",)),
        name="gla",
    )(qk, g, v, go, onorm)


def _bias_kernel(rb_ref, o_ref):
    T = ATTN_TILE
    h = pl.program_id(0)
    nb = NUM_BUCKETS // 2
    max_exact = nb // 2
    for d in range(5):
        rel = (lax.broadcasted_iota(jnp.int32, (T, T), 0) - lax.broadcasted_iota(jnp.int32, (T, T), 1)
               + (d - 2) * T)
        ret = jnp.where(rel > 0, nb, 0)
        n = jnp.abs(rel)
        large = max_exact + (jnp.log(jnp.maximum(n, 1).astype(F32) / max_exact)
                             / math.log(MAX_DISTANCE / max_exact) * (nb - max_exact)).astype(jnp.int32)
        large = jnp.minimum(large, nb - 1)
        bucket = ret + jnp.where(n < max_exact, n, large)
        reachable = range(nb) if d < 2 else range(nb, NUM_BUCKETS) if d > 2 else range(NUM_BUCKETS)
        for m in range(2):
            tile = jnp.zeros((T, T), F32)
            for bkt in reachable:
                tile = jnp.where(bucket == bkt, rb_ref[(bkt * N_DIFF_HEADS + h) * 2 + m], tile)
            o_ref[0, d, m] = tile * (-LOG2E)


def _bias_tiles(rel_bias):
    T = ATTN_TILE
    return pl.pallas_call(
        _bias_kernel,
        grid=(N_DIFF_HEADS,),
        in_specs=[pl.BlockSpec(memory_space=pltpu.SMEM)],
        out_specs=pl.BlockSpec((1, 5, 2, T, T), lambda h: (h, 0, 0, 0, 0)),
        out_shape=jax.ShapeDtypeStruct((N_DIFF_HEADS, 5, 2, T, T), F32),
        compiler_params=_cparams(("parallel",)),
        name="t5_bias",
    )(rel_bias.reshape(-1))


def _attn_kernel(*refs, lam_init):
    nq = refs[2].shape[0] // ATTN_TILE

    def body(qi, _):
        _attn_query_tile(qi, *refs, lam_init=lam_init)
        return 0

    lax.fori_loop(0, nq, body, 0)


def _attn_query_tile(qi, rb_ref, bound_ref, q_ref, k_ref, vt_ref, band_ref, lam_ref, on_ref, o_ref,
                     m_ref, l_ref, acc_ref, *, lam_init):
    T = ATTN_TILE
    L = k_ref.shape[0]
    nk = L // T
    h = pl.program_id(0)
    q = q_ref[pl.ds(pl.multiple_of(qi * T, T), T), :]
    lane = lax.broadcasted_iota(jnp.int32, q.shape, 1)
    qm = [jnp.where(lane < DIFF_DQK, q, 0).astype(BF16), jnp.where(lane >= DIFF_DQK, q, 0).astype(BF16)]

    def side_const(bucket, m):
        return rb_ref[(bucket * N_DIFF_HEADS + h) * 2 + m] * LOG2E

    def tile_start(j):
        return pl.multiple_of(j * T, T)

    unshifted_ok = bound_ref[0] <= EXP2_SAFE_RANGE

    @pl.when(unshifted_ok)
    def _():
        f_left = [jnp.exp2(jnp.full((1, 1), side_const(NUM_BUCKETS // 2 - 1, m), F32)).astype(BF16)
                  for m in range(2)]
        f_right = [jnp.exp2(jnp.full((1, 1), side_const(NUM_BUCKETS - 1, m), F32)).astype(BF16)
                   for m in range(2)]
        band_idx = [jnp.where(qi == 0, 4, 1), 2, jnp.where(qi == nk - 1, 0, 3)]
        l = [jnp.zeros((1, T), F32) for _ in range(2)]
        acc = [jnp.zeros((DIFF_DV, T), F32) for _ in range(2)]

        def scores(t):
            kt = k_ref[pl.ds(tile_start((qi + (2 + t)) % nk), T), :]
            return [_dot_nt(kt, qm[m]) for m in range(2)]

        s_next = scores(0)
        for t in range(nk):
            j = (qi + (2 + t)) % nk
            vt = vt_ref[:, pl.ds(tile_start(j), T)]
            s_cur = s_next
            if t + 1 < nk:
                s_next = scores(t + 1)
            for m in range(2):
                if t >= nk - 3:
                    p = jnp.exp2(s_cur[m] - band_ref[0, band_idx[t - (nk - 3)], m])
                    l[m] = l[m] + jnp.sum(p, axis=0, keepdims=True)
                    acc[m] = acc[m] + _dot(vt, p.astype(BF16))
                else:
                    f = jnp.where(j < qi, f_left[m], f_right[m])
                    p = jnp.exp2(s_cur[m])
                    l[m] = l[m] + f.astype(F32) * jnp.sum(p, axis=0, keepdims=True)
                    acc[m] = acc[m] + _dot(vt * f, p.astype(BF16))
        for m in range(2):
            l_ref[m] = l[m]
            acc_ref[m] = acc[m]

    @pl.when(jnp.logical_not(unshifted_ok))
    def _():
        m_ref[...] = jnp.full(m_ref.shape, -jnp.inf, F32)
        l_ref[...] = jnp.zeros(l_ref.shape, F32)
        acc_ref[...] = jnp.zeros(acc_ref.shape, F32)

        def body(j, _):
            r0 = tile_start(j)
            kt = k_ref[pl.ds(r0, T), :]
            vt = vt_ref[:, pl.ds(r0, T)]
            idx = jnp.clip(j - qi, -2, 2) + 2
            ss = [_dot_nt(kt, qm[m]) for m in range(2)]
            for m in range(2):
                s = ss[m] - band_ref[0, idx, m]
                m_old = m_ref[m]
                m_new = jnp.maximum(m_old, jnp.max(s, axis=0, keepdims=True))
                alpha = jnp.exp2(m_old - m_new)
                p = jnp.exp2(s - m_new)
                l_ref[m] = alpha * l_ref[m] + jnp.sum(p, axis=0, keepdims=True)
                acc_ref[m] = alpha * acc_ref[m] + _dot(vt, p.astype(BF16))
                m_ref[m] = m_new
            return 0

        lax.fori_loop(0, nk, body, 0)

    lp = lam_ref[...]
    lam = (jnp.exp(jnp.sum(lp[0:1] * lp[1:2], axis=-1, keepdims=True))
           - jnp.exp(jnp.sum(lp[2:3] * lp[3:4], axis=-1, keepdims=True)) + lam_init)
    o = acc_ref[0] / l_ref[0] - lam * (acc_ref[1] / l_ref[1])
    y = o * lax.rsqrt(jnp.mean(o * o, axis=0, keepdims=True) + NORM_EPS) * on_ref[...] * (1.0 - lam_init)
    o_ref[pl.ds(pl.multiple_of(qi * T, T), T), :] = y.T.astype(o_ref.dtype)


def _diff_attn(rel_bias_flat, score_bound, dq, dk, dvt, band, lam_p, onorm, batch, lam_init):
    n = dq.shape[0]
    L = n // batch
    T = ATTN_TILE
    seq = pl.BlockSpec((L, 2 * DIFF_DQK), lambda h, b: (b, h))
    return pl.pallas_call(
        functools.partial(_attn_kernel, lam_init=lam_init),
        grid=(N_DIFF_HEADS, batch),
        in_specs=[pl.BlockSpec(memory_space=pltpu.SMEM),
                  pl.BlockSpec(memory_space=pltpu.SMEM),
                  seq, seq,
                  pl.BlockSpec((DIFF_DV, L), lambda h, b: (h, b)),
                  pl.BlockSpec((1, 5, 2, T, T), lambda h, b: (h, 0, 0, 0, 0)),
                  pl.BlockSpec((4, DIFF_DQK), lambda h, b: (0, 0)),
                  pl.BlockSpec((DIFF_DV, 1), lambda h, b: (0, 0))],
        out_specs=pl.BlockSpec((L, DIFF_DV), lambda h, b: (b, h)),
        out_shape=jax.ShapeDtypeStruct((n, DIFF_V), BF16),
        scratch_shapes=[pltpu.VMEM((2, 1, T), F32), pltpu.VMEM((2, 1, T), F32),
                        pltpu.VMEM((2, DIFF_DV, T), F32)],
        compiler_params=_cparams(("parallel", "parallel")),
        name="diff_attn",
    )(rel_bias_flat, score_bound, dq, dk, dvt, band, lam_p, onorm)


def _conv_kernel(cv_ref, w_ref, b_ref, g_ref, beta_ref, y_ref, u_ref):
    L = cv_ref.shape[0]
    R = CONV_ROWS
    P = CONV_PAD
    zeros = jnp.zeros((P, CONV_CH), F32)
    u_ref[:P, :] = zeros
    u_ref[P + L:, :] = zeros
    u_ref[P:P + L, :] = cv_ref[:, :CONV_CH] * _sigmoid(cv_ref[:, CONV_CH:])
    off = P - CONV_K // 2

    def tile(t, _):
        r0 = pl.multiple_of(t * R, R)
        win = u_ref[pl.ds(r0, R + 2 * P), :]
        shifted = [win] + [pltpu.roll(win, R + 2 * P - r, 0) for r in range(1, 8)]
        acc = jnp.zeros((R, CONV_CH), F32) + b_ref[...]
        for k in range(CONV_K):
            a8, r = divmod(off + k, 8)
            acc = acc + w_ref[k:k + 1, :] * shifted[r][8 * a8:8 * a8 + R, :]
        mu = jnp.mean(acc, axis=-1, keepdims=True)
        xc = acc - mu
        y = xc * lax.rsqrt(jnp.mean(xc * xc, axis=-1, keepdims=True) + NORM_EPS) * g_ref[...] + beta_ref[...]
        y_ref[pl.ds(r0, R), :] = _silu(y)
        return 0

    lax.fori_loop(0, L // R, tile, 0)


def _conv(cv, w, b, g, beta, batch):
    n = cv.shape[0]
    L = n // batch
    return pl.pallas_call(
        _conv_kernel,
        grid=(batch,),
        in_specs=[pl.BlockSpec((L, 2 * CONV_CH), lambda i: (i, 0)),
                  _const_spec((CONV_K + 1, CONV_CH)), _const_spec((1, CONV_CH)),
                  _const_spec((1, CONV_CH)), _const_spec((1, CONV_CH))],
        out_specs=pl.BlockSpec((L, CONV_CH), lambda i: (i, 0)),
        out_shape=jax.ShapeDtypeStruct((n, CONV_CH), F32),
        scratch_shapes=[pltpu.VMEM((L + 2 * CONV_PAD, CONV_CH), F32)],
        compiler_params=_cparams(("parallel",)),
        name="conv",
    )(cv, w, b, g, beta)


def _post_kernel(x_ref, ya_ref, yb_ref, yc_ref, p_ref, wo_ref, fg_ref, wg_ref, wu_ref, wd_ref,
                 pg_ref, pwg_ref, pwp_ref, o_ref):
    mix = _dot(ya_ref[...].astype(BF16), wo_ref[:GLA_V, :])
    mix = mix + _dot(yb_ref[...].astype(BF16), wo_ref[GLA_V:GLA_V + DIFF_V, :])
    mix = mix + _dot(yc_ref[...].astype(BF16), wo_ref[GLA_V + DIFF_V:, :])
    x = _swiglu_half_step(x_ref[...] + mix, fg_ref, wg_ref, wu_ref, wd_ref)
    h = _rms_rows(x, pg_ref[...]).astype(BF16)
    gate = _sigmoid(_dot(h, pwg_ref[...]))
    o_ref[...] = x + gate * _dot(p_ref[...].astype(BF16), pwp_ref[...])


def _post(x, ya, yb, yc, p, w_out, ffn_gain, wg, wu, wd, ple_gain, ple_wg, ple_wp):
    n = x.shape[0]
    tm = TOKEN_TILE

    def row(width):
        return pl.BlockSpec((tm, width), lambda i: (i, 0))

    return pl.pallas_call(
        _post_kernel,
        grid=(n // tm,),
        in_specs=[row(D_MODEL), row(GLA_V), row(DIFF_V), row(CONV_CH), row(PLE_DIM),
                  _const_spec((D_MODEL, D_MODEL)),
                  _const_spec((1, D_MODEL)), _const_spec((D_MODEL, D_FF)), _const_spec((D_MODEL, D_FF)),
                  _const_spec((D_FF, D_MODEL)),
                  _const_spec((1, D_MODEL)), _const_spec((D_MODEL, D_MODEL)), _const_spec((PLE_DIM, D_MODEL))],
        out_specs=row(D_MODEL),
        out_shape=jax.ShapeDtypeStruct((n, D_MODEL), F32),
        compiler_params=_cparams(("parallel",)),
        name="post",
    )(x, ya, yb, yc, p, w_out, ffn_gain, wg, wu, wd, ple_gain, ple_wg, ple_wp)


def _pad_up(up, row0):
    return jnp.zeros((128, GLA_Q), F32).at[row0:row0 + GLA_RANK].set(up).astype(BF16)


def kernel(x, p, ffn1_norm, ffn1_w_gate, ffn1_w_up, ffn1_w_down, mix_norm, w_in, w_out, gla_gk_up_f, gla_gk_bias_f, gla_gk_up_b, gla_gk_bias_b, gla_out_norm, diff_q_norm, diff_k_norm, diff_lambda, diff_out_norm, rel_bias, conv_dw_w, conv_dw_b, conv_norm_g, conv_norm_b, ffn2_norm, ffn2_w_gate, ffn2_w_up, ffn2_w_down, ple_norm, ple_w_gate, ple_w_proj):
    B, L, _ = x.shape
    depth = w_in.shape[0]
    n = B * L
    xs = x.reshape(n, D_MODEL)
    rb_flat = rel_bias.reshape(-1)
    band = _bias_tiles(rel_bias)

    def row(v):
        return v.reshape(1, -1)

    for i in range(depth):
        xs = _ffn(xs, row(ffn1_norm[i]), ffn1_w_gate[i].astype(BF16), ffn1_w_up[i].astype(BF16),
                  ffn1_w_down[i].astype(BF16))

        gqk, gg, gv, go, dq, dk, dvt, cv = _inproj(
            xs, row(mix_norm[i]), w_in[i].astype(BF16),
            _pad_up(gla_gk_up_f[i], 0), row(gla_gk_bias_f[i]),
            _pad_up(gla_gk_up_b[i], GLA_RANK), row(gla_gk_bias_b[i]),
            row(jnp.tile(diff_q_norm[i], DIFF_QK // DIFF_DQK)), row(jnp.tile(diff_k_norm[i], DIFF_QK // DIFF_DQK)))

        y_gla = _gla(gqk, gg, gv, go, row(jnp.tile(gla_out_norm[i], N_GLA_HEADS)), B)
        lam_init = 0.8 - 0.6 * math.exp(-0.3 * i)
        score_bound = LOG2E * (DIFF_DQK ** 0.5 * jnp.max(jnp.abs(diff_q_norm[i])) * jnp.max(jnp.abs(diff_k_norm[i]))
                               + jnp.max(jnp.abs(rel_bias)))
        y_diff = _diff_attn(rb_flat, score_bound.reshape(1), dq, dk, dvt, band, diff_lambda[i],
                            diff_out_norm[i].reshape(-1, 1), B, lam_init)
        conv_w = jnp.concatenate([conv_dw_w[i], jnp.zeros((1, CONV_CH), F32)], axis=0)
        y_conv = _conv(cv, conv_w, row(conv_dw_b[i]), row(conv_norm_g[i]), row(conv_norm_b[i]), B)

        xs = _post(xs, y_gla, y_diff, y_conv, p[i].reshape(n, PLE_DIM), w_out[i].astype(BF16),
                   row(ffn2_norm[i]), ffn2_w_gate[i].astype(BF16), ffn2_w_up[i].astype(BF16),
                   ffn2_w_down[i].astype(BF16),
                   row(ple_norm[i]), ple_w_gate[i].astype(BF16), ple_w_proj[i].astype(BF16))
    return xs.reshape(B, L, D_MODEL)
```

```python
import functools
import math

import jax
import jax.numpy as jnp
from jax import lax
from jax.experimental import pallas as pl
from jax.experimental.pallas import tpu as pltpu

F32 = jnp.float32
BF16 = jnp.bfloat16

D_MODEL = 1024
N_GLA_HEADS = 4
GLA_DK = 32
GLA_DV = 64
GLA_Q = N_GLA_HEADS * GLA_DK
GLA_V = N_GLA_HEADS * GLA_DV
GLA_RANK = 16
GATE_TEMP = 16.0
GLA_CHUNK = 64
GLA_BLOCK = 4
DIFF_DQK = 64
N_DIFF_HEADS = 4
DIFF_DV = 128
DIFF_QK = 512
DIFF_V = 512
NUM_BUCKETS = 32
MAX_DISTANCE = 128
CONV_CH = 256
CONV_K = 31
D_FF = 2816
PLE_DIM = 256
NORM_EPS = 1e-6
LOG2E = math.log2(math.e)
EXP2_SAFE_RANGE = 100.0
D_IN = 2848

TOKEN_TILE = 512
FF_CHUNK = 1408
ATTN_TILE = 256
CONV_ROWS = 128
CONV_PAD = 16
VMEM_LIMIT = 56 * 1024 * 1024


def _cparams(sem):
    return pltpu.CompilerParams(dimension_semantics=sem, vmem_limit_bytes=VMEM_LIMIT)


def _dot(a, b):
    return jnp.dot(a, b, preferred_element_type=F32)


def _dot_nt(a, b):
    return lax.dot_general(a, b, (((1,), (1,)), ((), ())), preferred_element_type=F32)


def _rms_rows(x, gain):
    return x * lax.rsqrt(jnp.mean(x * x, axis=-1, keepdims=True) + NORM_EPS) * gain


def _sigmoid(x):
    return 1.0 / (1.0 + jnp.exp(-x))


def _silu(x):
    return x * _sigmoid(x)


def _const_spec(shape):
    return pl.BlockSpec(shape, lambda *_: (0,) * len(shape), pipeline_mode=pl.Buffered(1))


def _layer_spec(shape, layer):
    return pl.BlockSpec((None,) + shape, lambda *_: (layer,) + (0,) * len(shape), pipeline_mode=pl.Buffered(1))


def _swiglu_half_step(x, g_ref, wg_ref, wu_ref, wd_ref):
    h = _rms_rows(x, g_ref[...]).astype(BF16)
    acc = jnp.zeros(x.shape, F32)
    for c0 in range(0, D_FF, FF_CHUNK):
        gate = _dot(h, wg_ref[:, c0:c0 + FF_CHUNK])
        up = _dot(h, wu_ref[:, c0:c0 + FF_CHUNK])
        a = (_silu(gate) * up).astype(BF16)
        acc = acc + _dot(a, wd_ref[c0:c0 + FF_CHUNK, :])
    return x + 0.5 * acc


def _ffn_kernel(x_ref, g_ref, wg_ref, wu_ref, wd_ref, o_ref):
    o_ref[...] = _swiglu_half_step(x_ref[...], g_ref, wg_ref, wu_ref, wd_ref)


def _ffn(x, gain, wg, wu, wd, layer):
    n = x.shape[0]
    row = pl.BlockSpec((TOKEN_TILE, D_MODEL), lambda i: (i, 0))
    return pl.pallas_call(
        _ffn_kernel,
        grid=(n // TOKEN_TILE,),
        in_specs=[row, _const_spec((1, D_MODEL)), _layer_spec((D_MODEL, D_FF), layer),
                  _layer_spec((D_MODEL, D_FF), layer), _layer_spec((D_FF, D_MODEL), layer)],
        out_specs=row,
        out_shape=jax.ShapeDtypeStruct((n, D_MODEL), F32),
        compiler_params=_cparams(("parallel",)),
        name="ffn",
    )(x, gain, wg, wu, wd)


def _block_diag_ones(n, blk, dtype):
    r = lax.broadcasted_iota(jnp.int32, (n, n), 0) // blk
    c = lax.broadcasted_iota(jnp.int32, (n, n), 1) // blk
    return jnp.where(r == c, 1.0, 0.0).astype(dtype)


def _group_mean_sq(x, blk):
    n = x.shape[-1]
    return _dot((x * x).astype(BF16), _block_diag_ones(n, blk, BF16)) * (1.0 / blk)


def _log_sigmoid(z):
    return jnp.minimum(z, 0.0) - jnp.log(1.0 + jnp.exp(-jnp.abs(z)))


def _inproj_kernel(x_ref, g_ref, w_ref, upf_ref, bf_ref, upb_ref, bb_ref, qn_ref, kn_ref,
                   gqk_ref, gg_ref, gv_ref, go_ref, dq_ref, dk_ref, dvt_ref, cv_ref):
    h = _rms_rows(x_ref[...], g_ref[...]).astype(BF16)
    proj = _dot(h, w_ref[...])
    gqk_ref[:, :GLA_Q] = proj[:, :GLA_Q] * (GLA_DK ** -0.5)
    gqk_ref[:, GLA_Q:] = proj[:, GLA_Q:2 * GLA_Q]
    gv_ref[...] = proj[:, 256:512].astype(BF16)
    go_ref[...] = _silu(proj[:, 512:768]).astype(BF16)
    low = proj[:, 768:896].astype(BF16)
    zf = _dot(low, upf_ref[...]) + bf_ref[...]
    zb = _dot(low, upb_ref[...]) + bb_ref[...]
    gg_ref[:, :GLA_Q] = _log_sigmoid(zf) / GATE_TEMP
    gg_ref[:, GLA_Q:] = _log_sigmoid(zb) / GATE_TEMP
    rest = proj[:, 768 + 2 * GLA_RANK:]
    dq = rest[:, :512]
    dq_ref[...] = (dq * lax.rsqrt(_group_mean_sq(dq, DIFF_DQK) + NORM_EPS) * qn_ref[...]
                   * (DIFF_DQK ** -0.5 * LOG2E)).astype(BF16)
    dk = rest[:, 512:1024]
    dk_ref[...] = (dk * lax.rsqrt(_group_mean_sq(dk, DIFF_DQK) + NORM_EPS) * kn_ref[...]).astype(BF16)
    dvt_ref[...] = rest[:, 1024:1536].T.astype(BF16)
    cv_ref[...] = rest[:, 1536:2048]


def _inproj(x, gain, w, upf, bf, upb, bb, qn, kn, layer):
    n = x.shape[0]
    tm = TOKEN_TILE

    def row(width):
        return pl.BlockSpec((tm, width), lambda i: (i, 0))

    outs = [(2 * GLA_Q, F32), (2 * GLA_Q, F32), (GLA_V, BF16), (GLA_V, BF16),
            (DIFF_QK, BF16), (DIFF_QK, BF16), None, (2 * CONV_CH, F32)]
    dvt_spec = pl.BlockSpec((DIFF_V, tm), lambda i: (0, i))
    return pl.pallas_call(
        _inproj_kernel,
        grid=(n // tm,),
        in_specs=[row(D_MODEL), _const_spec((1, D_MODEL)), _layer_spec((D_MODEL, D_IN), layer),
                  _const_spec((128, GLA_Q)), _const_spec((1, GLA_Q)),
                  _const_spec((128, GLA_Q)), _const_spec((1, GLA_Q)),
                  _const_spec((1, DIFF_QK)), _const_spec((1, DIFF_QK))],
        out_specs=[row(o[0]) if o else dvt_spec for o in outs],
        out_shape=[jax.ShapeDtypeStruct((n, o[0]), o[1]) if o else jax.ShapeDtypeStruct((DIFF_V, n), BF16)
                   for o in outs],
        compiler_params=_cparams(("parallel",)),
        name="inproj",
    )(x, gain, w, upf, bf, upb, bb, qn, kn)


def _gla_block_kernel(qk_ref, g_ref, v_ref, go_ref, on_ref, y_ref, of_ref, st_ref):
    C = GLA_CHUNK
    G = GLA_BLOCK
    R = G * C
    L = qk_ref.shape[0]
    nblk = L // R
    H = N_GLA_HEADS

    def iota(shape, d):
        return lax.broadcasted_iota(jnp.int32, shape, d)

    kmask = iota((H * C, GLA_Q), 0) // C == iota((H * C, GLA_Q), 1) // GLA_DK
    vmask = iota((H * C, GLA_V), 0) // C == iota((H * C, GLA_V), 1) // GLA_DV
    smask = iota((GLA_V, GLA_Q), 0) // GLA_DV == iota((GLA_V, GLA_Q), 1) // GLA_DK
    pos_a = iota((C, H * C), 0)
    pos_b = iota((C, H * C), 1) % C
    ri, ci = iota((R, R), 0), iota((R, R), 1)
    same_chunk = ri // C == ci // C
    cums = [jnp.where(jnp.logical_and(same_chunk, ci <= ri), 1.0, 0.0).astype(BF16),
            jnp.where(jnp.logical_and(same_chunk, ci >= ri), 1.0, 0.0).astype(BF16)]
    amasks = [pos_b <= pos_a, pos_b >= pos_a]
    edges = [C - 1, 0]
    orders = [list(range(G)), list(range(G - 1, -1, -1))]

    st_ref[...] = jnp.zeros(st_ref.shape, F32)

    def step(t, _):
        r0 = [pl.multiple_of(t * R, R), pl.multiple_of((nblk - 1 - t) * R, R)]
        rows = [pl.ds(r0[0], R), pl.ds(r0[1], R)]
        dirs = (0, 1)
        b, k, v, qt = {}, {}, {}, {}
        for d in dirs:
            g = g_ref[rows[d], d * GLA_Q:(d + 1) * GLA_Q]
            g_hi = g.astype(BF16)
            g_lo = (g - g_hi.astype(F32)).astype(BF16)
            b[d] = _dot(cums[d], g_hi) + _dot(cums[d], g_lo)
        a, ds, decay = {}, {}, {}
        for d in dirs:
            q = qk_ref[rows[d], :GLA_Q]
            k[d] = qk_ref[rows[d], GLA_Q:]
            v[d] = v_ref[rows[d], :].astype(F32)
            qt[d] = (q * jnp.exp(b[d])).astype(BF16)
            kt = k[d] * jnp.exp(-b[d])
            for c in range(G):
                sl = slice(c * C, (c + 1) * C)
                b_c = b[d][sl]
                b_edge = b_c[edges[d]:edges[d] + 1, :]
                decay[d, c] = jnp.exp(b_edge)
                kend = (k[d][sl] * jnp.exp(b_edge - b_c)).astype(BF16)
                kstack = jnp.where(kmask, jnp.concatenate([kt[sl]] * H, axis=0), 0.0).astype(BF16)
                a[d, c] = jnp.where(amasks[d], _dot_nt(qt[d][sl], kstack), 0.0).astype(BF16)
                ds[d, c] = _dot(v[d][sl].T.astype(BF16), kend)
        for d in dirs:
            s = st_ref[d]
            s_in = {}
            for c in orders[d]:
                s_in[c] = s.astype(BF16)
                s = s * decay[d, c] + jnp.where(smask, ds[d, c], 0.0)
            st_ref[d] = s
            out_ref = y_ref if d else of_ref
            for c in range(G):
                sl = slice(c * C, (c + 1) * C)
                vstack = jnp.where(vmask, jnp.concatenate([v[d][sl]] * H, axis=0), 0.0).astype(BF16)
                o = _dot(a[d, c], vstack) + _dot_nt(qt[d][sl], s_in[c])
                out_ref[pl.ds(pl.multiple_of(r0[d] + c * C, C), C), :] = o
        return 0

    lax.fori_loop(0, nblk, step, 0)

    norm_bd = _block_diag_ones(GLA_V, GLA_DV, BF16)

    def finish(t, _):
        rows = pl.ds(pl.multiple_of(t * R, R), R)
        tot = of_ref[rows, :] + y_ref[rows, :]
        ms = _dot((tot * tot).astype(BF16), norm_bd) * (1.0 / GLA_DV)
        y_ref[rows, :] = tot * lax.rsqrt(ms + NORM_EPS) * on_ref[...] * go_ref[rows, :].astype(F32)
        return 0

    lax.fori_loop(0, nblk, finish, 0)


def _gla(qk, g, v, go, onorm, batch):
    n = qk.shape[0]
    L = n // batch

    def seq(width):
        return pl.BlockSpec((L, width), lambda b: (b, 0))

    return pl.pallas_call(
        _gla_block_kernel,
        grid=(batch,),
        in_specs=[seq(2 * GLA_Q), seq(2 * GLA_Q), seq(GLA_V), seq(GLA_V), _const_spec((1, GLA_V))],
        out_specs=seq(GLA_V),
        out_shape=jax.ShapeDtypeStruct((n, GLA_V), F32),
        scratch_shapes=[pltpu.VMEM((L, GLA_V), F32), pltpu.VMEM((2, GLA_V, GLA_Q), F32)],
        compiler_params=_cparams(("parallel",)),
        name="gla",
    )(qk, g, v, go, onorm)


def _bias_kernel(rb_ref, o_ref):
    T = ATTN_TILE
    h = pl.program_id(0)
    nb = NUM_BUCKETS // 2
    max_exact = nb // 2
    for d in range(5):
        rel = (lax.broadcasted_iota(jnp.int32, (T, T), 0) - lax.broadcasted_iota(jnp.int32, (T, T), 1)
               + (d - 2) * T)
        ret = jnp.where(rel > 0, nb, 0)
        n = jnp.abs(rel)
        large = max_exact + (jnp.log(jnp.maximum(n, 1).astype(F32) / max_exact)
                             / math.log(MAX_DISTANCE / max_exact) * (nb - max_exact)).astype(jnp.int32)
        large = jnp.minimum(large, nb - 1)
        bucket = ret + jnp.where(n < max_exact, n, large)
        reachable = range(nb) if d < 2 else range(nb, NUM_BUCKETS) if d > 2 else range(NUM_BUCKETS)
        for m in range(2):
            tile = jnp.zeros((T, T), F32)
            for bkt in reachable:
                tile = jnp.where(bucket == bkt, rb_ref[(bkt * N_DIFF_HEADS + h) * 2 + m], tile)
            o_ref[0, d, m] = tile * (-LOG2E)


def _bias_tiles(rel_bias):
    T = ATTN_TILE
    return pl.pallas_call(
        _bias_kernel,
        grid=(N_DIFF_HEADS,),
        in_specs=[pl.BlockSpec(memory_space=pltpu.SMEM)],
        out_specs=pl.BlockSpec((1, 5, 2, T, T), lambda h: (h, 0, 0, 0, 0)),
        out_shape=jax.ShapeDtypeStruct((N_DIFF_HEADS, 5, 2, T, T), F32),
        compiler_params=_cparams(("parallel",)),
        name="t5_bias",
    )(rel_bias.reshape(-1))


def _attn_kernel(*refs, lam_init):
    bound_ref = refs[1]
    nq = refs[2].shape[0] // ATTN_TILE
    unshifted_ok = bound_ref[0] <= EXP2_SAFE_RANGE

    def sweep(unshifted, unroll):
        def body(qi, _):
            _attn_query_tile(qi, *refs, lam_init=lam_init, unshifted=unshifted)
            return 0
        lax.fori_loop(0, nq, body, 0, unroll=unroll)

    @pl.when(unshifted_ok)
    def _():
        sweep(True, 2)

    @pl.when(jnp.logical_not(unshifted_ok))
    def _():
        sweep(False, 1)


def _attn_query_tile(qi, rb_ref, bound_ref, q_ref, k_ref, vt_ref, band_ref, lam_ref, on_ref, o_ref,
                     m_ref, l_ref, acc_ref, *, lam_init, unshifted):
    T = ATTN_TILE
    L = k_ref.shape[0]
    nk = L // T
    h = pl.program_id(0)
    q = q_ref[pl.ds(pl.multiple_of(qi * T, T), T), :]
    lane = lax.broadcasted_iota(jnp.int32, q.shape, 1)
    qm = [jnp.where(lane < DIFF_DQK, q, 0).astype(BF16), jnp.where(lane >= DIFF_DQK, q, 0).astype(BF16)]

    def side_const(bucket, m):
        return rb_ref[(bucket * N_DIFF_HEADS + h) * 2 + m] * LOG2E

    def tile_start(j):
        return pl.multiple_of(j * T, T)

    def unshifted_softmax():
        f_left = [jnp.exp2(jnp.full((1, 1), side_const(NUM_BUCKETS // 2 - 1, m), F32)).astype(BF16)
                  for m in range(2)]
        f_right = [jnp.exp2(jnp.full((1, 1), side_const(NUM_BUCKETS - 1, m), F32)).astype(BF16)
                   for m in range(2)]
        band_idx = [jnp.where(qi == 0, 4, 1), 2, jnp.where(qi == nk - 1, 0, 3)]
        l = [jnp.zeros((1, T), F32) for _ in range(2)]
        acc = [jnp.zeros((DIFF_DV, T), F32) for _ in range(2)]

        def scores(t):
            kt = k_ref[pl.ds(tile_start((qi + (2 + t)) % nk), T), :]
            return [_dot_nt(kt, qm[m]) for m in range(2)]

        s_next = scores(0)
        for t in range(nk):
            j = (qi + (2 + t)) % nk
            vt = vt_ref[:, pl.ds(tile_start(j), T)]
            s_cur = s_next
            if t + 1 < nk:
                s_next = scores(t + 1)
            for m in range(2):
                if t >= nk - 3:
                    p = jnp.exp2(s_cur[m] - band_ref[0, band_idx[t - (nk - 3)], m])
                    l[m] = l[m] + jnp.sum(p, axis=0, keepdims=True)
                    acc[m] = acc[m] + _dot(vt, p.astype(BF16))
                else:
                    f = jnp.where(j < qi, f_left[m], f_right[m])
                    p = jnp.exp2(s_cur[m])
                    l[m] = l[m] + f.astype(F32) * jnp.sum(p, axis=0, keepdims=True)
                    acc[m] = acc[m] + _dot(vt * f, p.astype(BF16))
        for m in range(2):
            l_ref[m] = l[m]
            acc_ref[m] = acc[m]

    def online_softmax():
        m_ref[...] = jnp.full(m_ref.shape, -jnp.inf, F32)
        l_ref[...] = jnp.zeros(l_ref.shape, F32)
        acc_ref[...] = jnp.zeros(acc_ref.shape, F32)

        def body(j, _):
            r0 = tile_start(j)
            kt = k_ref[pl.ds(r0, T), :]
            vt = vt_ref[:, pl.ds(r0, T)]
            idx = jnp.clip(j - qi, -2, 2) + 2
            ss = [_dot_nt(kt, qm[m]) for m in range(2)]
            for m in range(2):
                s = ss[m] - band_ref[0, idx, m]
                m_old = m_ref[m]
                m_new = jnp.maximum(m_old, jnp.max(s, axis=0, keepdims=True))
                alpha = jnp.exp2(m_old - m_new)
                p = jnp.exp2(s - m_new)
                l_ref[m] = alpha * l_ref[m] + jnp.sum(p, axis=0, keepdims=True)
                acc_ref[m] = alpha * acc_ref[m] + _dot(vt, p.astype(BF16))
                m_ref[m] = m_new
            return 0

        lax.fori_loop(0, nk, body, 0)

    if unshifted:
        unshifted_softmax()
    else:
        online_softmax()

    lp = lam_ref[...]
    lam = (jnp.exp(jnp.sum(lp[0:1] * lp[1:2], axis=-1, keepdims=True))
           - jnp.exp(jnp.sum(lp[2:3] * lp[3:4], axis=-1, keepdims=True)) + lam_init)
    o = acc_ref[0] / l_ref[0] - lam * (acc_ref[1] / l_ref[1])
    y = o * lax.rsqrt(jnp.mean(o * o, axis=0, keepdims=True) + NORM_EPS) * on_ref[...] * (1.0 - lam_init)
    o_ref[pl.ds(pl.multiple_of(qi * T, T), T), :] = y.T.astype(o_ref.dtype)


def _diff_attn(rel_bias_flat, score_bound, dq, dk, dvt, band, lam_p, onorm, batch, lam_init):
    n = dq.shape[0]
    L = n // batch
    T = ATTN_TILE
    seq = pl.BlockSpec((L, 2 * DIFF_DQK), lambda h, b: (b, h))
    return pl.pallas_call(
        functools.partial(_attn_kernel, lam_init=lam_init),
        grid=(N_DIFF_HEADS, batch),
        in_specs=[pl.BlockSpec(memory_space=pltpu.SMEM),
                  pl.BlockSpec(memory_space=pltpu.SMEM),
                  seq, seq,
                  pl.BlockSpec((DIFF_DV, L), lambda h, b: (h, b)),
                  pl.BlockSpec((1, 5, 2, T, T), lambda h, b: (h, 0, 0, 0, 0)),
                  pl.BlockSpec((4, DIFF_DQK), lambda h, b: (0, 0)),
                  pl.BlockSpec((DIFF_DV, 1), lambda h, b: (0, 0))],
        out_specs=pl.BlockSpec((L, DIFF_DV), lambda h, b: (b, h)),
        out_shape=jax.ShapeDtypeStruct((n, DIFF_V), BF16),
        scratch_shapes=[pltpu.VMEM((2, 1, T), F32), pltpu.VMEM((2, 1, T), F32),
                        pltpu.VMEM((2, DIFF_DV, T), F32)],
        compiler_params=_cparams(("parallel", "parallel")),
        name="diff_attn",
    )(rel_bias_flat, score_bound, dq, dk, dvt, band, lam_p, onorm)


def _conv_kernel(cv_ref, w_ref, b_ref, g_ref, beta_ref, y_ref, u_ref):
    L = cv_ref.shape[0]
    R = CONV_ROWS
    P = CONV_PAD
    zeros = jnp.zeros((P, CONV_CH), F32)
    u_ref[:P, :] = zeros
    u_ref[P + L:, :] = zeros
    u_ref[P:P + L, :] = cv_ref[:, :CONV_CH] * _sigmoid(cv_ref[:, CONV_CH:])
    off = P - CONV_K // 2

    def tile(t, _):
        r0 = pl.multiple_of(t * R, R)
        win = u_ref[pl.ds(r0, R + 2 * P), :]
        shifted = [win] + [pltpu.roll(win, R + 2 * P - r, 0) for r in range(1, 8)]
        acc = jnp.zeros((R, CONV_CH), F32) + b_ref[...]
        for k in range(CONV_K):
            a8, r = divmod(off + k, 8)
            acc = acc + w_ref[k:k + 1, :] * shifted[r][8 * a8:8 * a8 + R, :]
        mu = jnp.mean(acc, axis=-1, keepdims=True)
        xc = acc - mu
        y = xc * lax.rsqrt(jnp.mean(xc * xc, axis=-1, keepdims=True) + NORM_EPS) * g_ref[...] + beta_ref[...]
        y_ref[pl.ds(r0, R), :] = _silu(y)
        return 0

    lax.fori_loop(0, L // R, tile, 0)


def _conv(cv, w, b, g, beta, batch):
    n = cv.shape[0]
    L = n // batch
    return pl.pallas_call(
        _conv_kernel,
        grid=(batch,),
        in_specs=[pl.BlockSpec((L, 2 * CONV_CH), lambda i: (i, 0)),
                  _const_spec((CONV_K + 1, CONV_CH)), _const_spec((1, CONV_CH)),
                  _const_spec((1, CONV_CH)), _const_spec((1, CONV_CH))],
        out_specs=pl.BlockSpec((L, CONV_CH), lambda i: (i, 0)),
        out_shape=jax.ShapeDtypeStruct((n, CONV_CH), F32),
        scratch_shapes=[pltpu.VMEM((L + 2 * CONV_PAD, CONV_CH), F32)],
        compiler_params=_cparams(("parallel",)),
        name="conv",
    )(cv, w, b, g, beta)


def _post_kernel(x_ref, ya_ref, yb_ref, yc_ref, p_ref, wo_ref, fg_ref, wg_ref, wu_ref, wd_ref,
                 pg_ref, pwg_ref, pwp_ref, o_ref):
    mix = _dot(ya_ref[...].astype(BF16), wo_ref[:GLA_V, :])
    mix = mix + _dot(yb_ref[...].astype(BF16), wo_ref[GLA_V:GLA_V + DIFF_V, :])
    mix = mix + _dot(yc_ref[...].astype(BF16), wo_ref[GLA_V + DIFF_V:, :])
    x = _swiglu_half_step(x_ref[...] + mix, fg_ref, wg_ref, wu_ref, wd_ref)
    h = _rms_rows(x, pg_ref[...]).astype(BF16)
    gate = _sigmoid(_dot(h, pwg_ref[...]))
    o_ref[...] = x + gate * _dot(p_ref[...].astype(BF16), pwp_ref[...])


def _post(x, ya, yb, yc, p, w_out, ffn_gain, wg, wu, wd, ple_gain, ple_wg, ple_wp, layer):
    n = x.shape[0]
    tm = TOKEN_TILE

    def row(width):
        return pl.BlockSpec((tm, width), lambda i: (i, 0))

    return pl.pallas_call(
        _post_kernel,
        grid=(n // tm,),
        in_specs=[row(D_MODEL), row(GLA_V), row(DIFF_V), row(CONV_CH),
                  pl.BlockSpec((None, tm, PLE_DIM), lambda i: (layer, i, 0)),
                  _layer_spec((D_MODEL, D_MODEL), layer),
                  _const_spec((1, D_MODEL)), _layer_spec((D_MODEL, D_FF), layer),
                  _layer_spec((D_MODEL, D_FF), layer), _layer_spec((D_FF, D_MODEL), layer),
                  _const_spec((1, D_MODEL)), _layer_spec((D_MODEL, D_MODEL), layer),
                  _layer_spec((PLE_DIM, D_MODEL), layer)],
        out_specs=row(D_MODEL),
        out_shape=jax.ShapeDtypeStruct((n, D_MODEL), F32),
        compiler_params=_cparams(("parallel",)),
        name="post",
    )(x, ya, yb, yc, p, w_out, ffn_gain, wg, wu, wd, ple_gain, ple_wg, ple_wp)


def _pad_up(up, row0):
    return jnp.zeros((128, GLA_Q), F32).at[row0:row0 + GLA_RANK].set(up).astype(BF16)


def kernel(x, p, ffn1_norm, ffn1_w_gate, ffn1_w_up, ffn1_w_down, mix_norm, w_in, w_out, gla_gk_up_f, gla_gk_bias_f, gla_gk_up_b, gla_gk_bias_b, gla_out_norm, diff_q_norm, diff_k_norm, diff_lambda, diff_out_norm, rel_bias, conv_dw_w, conv_dw_b, conv_norm_g, conv_norm_b, ffn2_norm, ffn2_w_gate, ffn2_w_up, ffn2_w_down, ple_norm, ple_w_gate, ple_w_proj):
    B, L, _ = x.shape
    depth = w_in.shape[0]
    n = B * L
    xs = x.reshape(n, D_MODEL)
    rb_flat = rel_bias.reshape(-1)
    band = _bias_tiles(rel_bias)

    def row(v):
        return v.reshape(1, -1)

    ffn1_w = [w.astype(BF16) for w in (ffn1_w_gate, ffn1_w_up, ffn1_w_down)]
    ffn2_w = [w.astype(BF16) for w in (ffn2_w_gate, ffn2_w_up, ffn2_w_down)]
    w_in_b, w_out_b = w_in.astype(BF16), w_out.astype(BF16)
    ple_wg_b, ple_wp_b = ple_w_gate.astype(BF16), ple_w_proj.astype(BF16)
    p_rows = p.reshape(depth, n, PLE_DIM)

    for i in range(depth):
        xs = _ffn(xs, row(ffn1_norm[i]), *ffn1_w, i)

        gqk, gg, gv, go, dq, dk, dvt, cv = _inproj(
            xs, row(mix_norm[i]), w_in_b,
            _pad_up(gla_gk_up_f[i], 0), row(gla_gk_bias_f[i]),
            _pad_up(gla_gk_up_b[i], GLA_RANK), row(gla_gk_bias_b[i]),
            row(jnp.tile(diff_q_norm[i], DIFF_QK // DIFF_DQK)), row(jnp.tile(diff_k_norm[i], DIFF_QK // DIFF_DQK)),
            i)

        y_gla = _gla(gqk, gg, gv, go, row(jnp.tile(gla_out_norm[i], N_GLA_HEADS)), B)
        lam_init = 0.8 - 0.6 * math.exp(-0.3 * i)
        score_bound = LOG2E * (DIFF_DQK ** 0.5 * jnp.max(jnp.abs(diff_q_norm[i])) * jnp.max(jnp.abs(diff_k_norm[i]))
                               + jnp.max(jnp.abs(rel_bias)))
        y_diff = _diff_attn(rb_flat, score_bound.reshape(1), dq, dk, dvt, band, diff_lambda[i],
                            diff_out_norm[i].reshape(-1, 1), B, lam_init)
        conv_w = jnp.concatenate([conv_dw_w[i], jnp.zeros((1, CONV_CH), F32)], axis=0)
        y_conv = _conv(cv, conv_w, row(conv_dw_b[i]), row(conv_norm_g[i]), row(conv_norm_b[i]), B)

        xs = _post(xs, y_gla, y_diff, y_conv, p_rows, w_out_b, row(ffn2_norm[i]), *ffn2_w,
                   row(ple_norm[i]), ple_wg_b, ple_wp_b, i)
    return xs.reshape(B, L, D_MODEL)
```

```python
import functools
import math

import jax
import jax.numpy as jnp
from jax import lax
from jax.experimental import pallas as pl
from jax.experimental.pallas import tpu as pltpu

F32 = jnp.float32
BF16 = jnp.bfloat16

D_MODEL = 1024
N_GLA_HEADS = 4
GLA_DK = 32
GLA_DV = 64
GLA_Q = N_GLA_HEADS * GLA_DK
GLA_V = N_GLA_HEADS * GLA_DV
GLA_RANK = 16
GATE_TEMP = 16.0
GLA_CHUNK = 64
GLA_BLOCK = 4
DIFF_DQK = 64
N_DIFF_HEADS = 4
DIFF_DV = 128
DIFF_QK = 512
DIFF_V = 512
NUM_BUCKETS = 32
MAX_DISTANCE = 128
CONV_CH = 256
CONV_K = 31
D_FF = 2816
PLE_DIM = 256
NORM_EPS = 1e-6
LOG2E = math.log2(math.e)
EXP2_SAFE_RANGE = 100.0
D_IN = 2848

TOKEN_TILE = 512
INPROJ_TILE = 1024
INPROJ_SUB = 256
FF_CHUNK = 1408
ATTN_TILE = 256
ATTN_LOOKAHEAD = 2
CONV_ROWS = 128
CONV_PAD = 16
VMEM_LIMIT = 56 * 1024 * 1024


def _cparams(sem):
    return pltpu.CompilerParams(dimension_semantics=sem, vmem_limit_bytes=VMEM_LIMIT)


def _dot(a, b):
    return jnp.dot(a, b, preferred_element_type=F32)


def _dot_nt(a, b):
    return lax.dot_general(a, b, (((1,), (1,)), ((), ())), preferred_element_type=F32)


def _rms_rows(x, gain):
    return x * lax.rsqrt(jnp.mean(x * x, axis=-1, keepdims=True) + NORM_EPS) * gain


def _sigmoid(x):
    return 1.0 / (1.0 + jnp.exp(-x))


def _silu(x):
    return x * _sigmoid(x)


def _const_spec(shape):
    return pl.BlockSpec(shape, lambda *_: (0,) * len(shape), pipeline_mode=pl.Buffered(1))


def _layer_spec(shape, layer):
    return pl.BlockSpec((None,) + shape, lambda *_: (layer,) + (0,) * len(shape), pipeline_mode=pl.Buffered(1))


def _swiglu_half_step(x, g_ref, wg_ref, wu_ref, wd_ref):
    h = _rms_rows(x, g_ref[...]).astype(BF16)
    acc = jnp.zeros(x.shape, F32)
    for c0 in range(0, D_FF, FF_CHUNK):
        gate = _dot(h, wg_ref[:, c0:c0 + FF_CHUNK])
        up = _dot(h, wu_ref[:, c0:c0 + FF_CHUNK])
        a = (_silu(gate) * up).astype(BF16)
        acc = acc + _dot(a, wd_ref[c0:c0 + FF_CHUNK, :])
    return x + 0.5 * acc


def _ffn_kernel(x_ref, g_ref, wg_ref, wu_ref, wd_ref, o_ref):
    o_ref[...] = _swiglu_half_step(x_ref[...], g_ref, wg_ref, wu_ref, wd_ref)


def _ffn(x, gain, wg, wu, wd, layer):
    n = x.shape[0]
    row = pl.BlockSpec((TOKEN_TILE, D_MODEL), lambda i: (i, 0))
    return pl.pallas_call(
        _ffn_kernel,
        grid=(n // TOKEN_TILE,),
        in_specs=[row, _const_spec((1, D_MODEL)), _layer_spec((D_MODEL, D_FF), layer),
                  _layer_spec((D_MODEL, D_FF), layer), _layer_spec((D_FF, D_MODEL), layer)],
        out_specs=row,
        out_shape=jax.ShapeDtypeStruct((n, D_MODEL), F32),
        compiler_params=_cparams(("parallel",)),
        name="ffn",
    )(x, gain, wg, wu, wd)


def _block_diag_ones(n, blk, dtype):
    r = lax.broadcasted_iota(jnp.int32, (n, n), 0) // blk
    c = lax.broadcasted_iota(jnp.int32, (n, n), 1) // blk
    return jnp.where(r == c, 1.0, 0.0).astype(dtype)


def _group_mean_sq(x, blk):
    n = x.shape[-1]
    return _dot((x * x).astype(BF16), _block_diag_ones(n, blk, BF16)) * (1.0 / blk)


def _log_sigmoid(z):
    return jnp.minimum(z, 0.0) - jnp.log(1.0 + jnp.exp(-jnp.abs(z)))


def _inproj_kernel(x_ref, g_ref, w_ref, upf_ref, bf_ref, upb_ref, bb_ref, qn_ref, kn_ref,
                   gqk_ref, gg_ref, gv_ref, go_ref, dq_ref, dk_ref, dvt_ref, cv_ref):
    tm = x_ref.shape[0]
    nsub = tm // INPROJ_SUB

    def project(s):
        rows = slice(s * INPROJ_SUB, (s + 1) * INPROJ_SUB)
        h = _rms_rows(x_ref[rows, :], g_ref[...]).astype(BF16)
        return _dot(h, w_ref[...])

    def prepare(s, proj):
        rows = slice(s * INPROJ_SUB, (s + 1) * INPROJ_SUB)
        gqk_ref[rows, :GLA_Q] = proj[:, :GLA_Q] * (GLA_DK ** -0.5)
        gqk_ref[rows, GLA_Q:] = proj[:, GLA_Q:2 * GLA_Q]
        gv_ref[rows, :] = proj[:, 256:512].astype(BF16)
        go_ref[rows, :] = _silu(proj[:, 512:768]).astype(BF16)
        low = proj[:, 768:896].astype(BF16)
        zf = _dot(low, upf_ref[...]) + bf_ref[...]
        zb = _dot(low, upb_ref[...]) + bb_ref[...]
        gg_ref[rows, :GLA_Q] = _log_sigmoid(zf) / GATE_TEMP
        gg_ref[rows, GLA_Q:] = _log_sigmoid(zb) / GATE_TEMP
        rest = proj[:, 768 + 2 * GLA_RANK:]
        dq = rest[:, :512]
        dq_ref[rows, :] = (dq * lax.rsqrt(_group_mean_sq(dq, DIFF_DQK) + NORM_EPS) * qn_ref[...]
                           * (DIFF_DQK ** -0.5 * LOG2E)).astype(BF16)
        dk = rest[:, 512:1024]
        dk_ref[rows, :] = (dk * lax.rsqrt(_group_mean_sq(dk, DIFF_DQK) + NORM_EPS) * kn_ref[...]).astype(BF16)
        dvt_ref[:, rows] = rest[:, 1024:1536].T.astype(BF16)
        cv_ref[rows, :] = rest[:, 1536:2048]

    nxt = project(0)
    for s in range(nsub):
        cur = nxt
        if s + 1 < nsub:
            nxt = project(s + 1)
        prepare(s, cur)


def _inproj(x, gain, w, upf, bf, upb, bb, qn, kn, layer):
    n = x.shape[0]
    tm = INPROJ_TILE

    def row(width):
        return pl.BlockSpec((tm, width), lambda i: (i, 0))

    outs = [(2 * GLA_Q, F32), (2 * GLA_Q, F32), (GLA_V, BF16), (GLA_V, BF16),
            (DIFF_QK, BF16), (DIFF_QK, BF16), None, (2 * CONV_CH, F32)]
    dvt_spec = pl.BlockSpec((DIFF_V, tm), lambda i: (0, i))
    return pl.pallas_call(
        _inproj_kernel,
        grid=(n // tm,),
        in_specs=[row(D_MODEL), _const_spec((1, D_MODEL)), _layer_spec((D_MODEL, D_IN), layer),
                  _const_spec((128, GLA_Q)), _const_spec((1, GLA_Q)),
                  _const_spec((128, GLA_Q)), _const_spec((1, GLA_Q)),
                  _const_spec((1, DIFF_QK)), _const_spec((1, DIFF_QK))],
        out_specs=[row(o[0]) if o else dvt_spec for o in outs],
        out_shape=[jax.ShapeDtypeStruct((n, o[0]), o[1]) if o else jax.ShapeDtypeStruct((DIFF_V, n), BF16)
                   for o in outs],
        compiler_params=_cparams(("parallel",)),
        name="inproj",
    )(x, gain, w, upf, bf, upb, bb, qn, kn)


def _gla_block_kernel(qk_ref, g_ref, v_ref, go_ref, on_ref, y_ref, of_ref, st_ref):
    C = GLA_CHUNK
    G = GLA_BLOCK
    R = G * C
    L = qk_ref.shape[0]
    nblk = L // R
    H = N_GLA_HEADS

    def iota(shape, d):
        return lax.broadcasted_iota(jnp.int32, shape, d)

    kmask = iota((H * C, GLA_Q), 0) // C == iota((H * C, GLA_Q), 1) // GLA_DK
    vmask = iota((H * C, GLA_V), 0) // C == iota((H * C, GLA_V), 1) // GLA_DV
    smask = iota((GLA_V, GLA_Q), 0) // GLA_DV == iota((GLA_V, GLA_Q), 1) // GLA_DK
    pos_a = iota((C, H * C), 0)
    pos_b = iota((C, H * C), 1) % C
    ri, ci = iota((R, R), 0), iota((R, R), 1)
    same_chunk = ri // C == ci // C
    cums = [jnp.where(jnp.logical_and(same_chunk, ci <= ri), 1.0, 0.0).astype(BF16),
            jnp.where(jnp.logical_and(same_chunk, ci >= ri), 1.0, 0.0).astype(BF16)]
    amasks = [pos_b <= pos_a, pos_b >= pos_a]
    edges = [C - 1, 0]
    orders = [list(range(G)), list(range(G - 1, -1, -1))]

    st_ref[...] = jnp.zeros(st_ref.shape, F32)

    def step(t, _):
        r0 = [pl.multiple_of(t * R, R), pl.multiple_of((nblk - 1 - t) * R, R)]
        rows = [pl.ds(r0[0], R), pl.ds(r0[1], R)]
        dirs = (0, 1)
        b, k, v, qt = {}, {}, {}, {}
        for d in dirs:
            g = g_ref[rows[d], d * GLA_Q:(d + 1) * GLA_Q]
            g_hi = g.astype(BF16)
            g_lo = (g - g_hi.astype(F32)).astype(BF16)
            b[d] = _dot(cums[d], g_hi) + _dot(cums[d], g_lo)
        a, ds, decay = {}, {}, {}
        for d in dirs:
            q = qk_ref[rows[d], :GLA_Q]
            k[d] = qk_ref[rows[d], GLA_Q:]
            v[d] = v_ref[rows[d], :].astype(F32)
            qt[d] = (q * jnp.exp(b[d])).astype(BF16)
            kt = k[d] * jnp.exp(-b[d])
            for c in range(G):
                sl = slice(c * C, (c + 1) * C)
                b_c = b[d][sl]
                b_edge = b_c[edges[d]:edges[d] + 1, :]
                decay[d, c] = jnp.exp(b_edge)
                kend = (k[d][sl] * jnp.exp(b_edge - b_c)).astype(BF16)
                kstack = jnp.where(kmask, jnp.concatenate([kt[sl]] * H, axis=0), 0.0).astype(BF16)
                a[d, c] = jnp.where(amasks[d], _dot_nt(qt[d][sl], kstack), 0.0).astype(BF16)
                ds[d, c] = _dot(v[d][sl].T.astype(BF16), kend)
        for d in dirs:
            s = st_ref[d]
            s_in = {}
            for c in orders[d]:
                s_in[c] = s.astype(BF16)
                s = s * decay[d, c] + jnp.where(smask, ds[d, c], 0.0)
            st_ref[d] = s
            out_ref = y_ref if d else of_ref
            for c in range(G):
                sl = slice(c * C, (c + 1) * C)
                vstack = jnp.where(vmask, jnp.concatenate([v[d][sl]] * H, axis=0), 0.0).astype(BF16)
                o = _dot(a[d, c], vstack) + _dot_nt(qt[d][sl], s_in[c])
                out_ref[pl.ds(pl.multiple_of(r0[d] + c * C, C), C), :] = o
        return 0

    lax.fori_loop(0, nblk, step, 0)

    norm_bd = _block_diag_ones(GLA_V, GLA_DV, BF16)

    def finish(t, _):
        rows = pl.ds(pl.multiple_of(t * R, R), R)
        tot = of_ref[rows, :] + y_ref[rows, :]
        ms = _dot((tot * tot).astype(BF16), norm_bd) * (1.0 / GLA_DV)
        y_ref[rows, :] = tot * lax.rsqrt(ms + NORM_EPS) * on_ref[...] * go_ref[rows, :].astype(F32)
        return 0

    lax.fori_loop(0, nblk, finish, 0)


def _gla(qk, g, v, go, onorm, batch):
    n = qk.shape[0]
    L = n // batch

    def seq(width):
        return pl.BlockSpec((L, width), lambda b: (b, 0))

    return pl.pallas_call(
        _gla_block_kernel,
        grid=(batch,),
        in_specs=[seq(2 * GLA_Q), seq(2 * GLA_Q), seq(GLA_V), seq(GLA_V), _const_spec((1, GLA_V))],
        out_specs=seq(GLA_V),
        out_shape=jax.ShapeDtypeStruct((n, GLA_V), F32),
        scratch_shapes=[pltpu.VMEM((L, GLA_V), F32), pltpu.VMEM((2, GLA_V, GLA_Q), F32)],
        compiler_params=_cparams(("parallel",)),
        name="gla",
    )(qk, g, v, go, onorm)


def _bias_kernel(rb_ref, o_ref):
    T = ATTN_TILE
    h = pl.program_id(0)
    nb = NUM_BUCKETS // 2
    max_exact = nb // 2
    for d in range(5):
        rel = (lax.broadcasted_iota(jnp.int32, (T, T), 0) - lax.broadcasted_iota(jnp.int32, (T, T), 1)
               + (d - 2) * T)
        ret = jnp.where(rel > 0, nb, 0)
        n = jnp.abs(rel)
        large = max_exact + (jnp.log(jnp.maximum(n, 1).astype(F32) / max_exact)
                             / math.log(MAX_DISTANCE / max_exact) * (nb - max_exact)).astype(jnp.int32)
        large = jnp.minimum(large, nb - 1)
        bucket = ret + jnp.where(n < max_exact, n, large)
        reachable = range(nb) if d < 2 else range(nb, NUM_BUCKETS) if d > 2 else range(NUM_BUCKETS)
        for m in range(2):
            tile = jnp.zeros((T, T), F32)
            for bkt in reachable:
                tile = jnp.where(bucket == bkt, rb_ref[(bkt * N_DIFF_HEADS + h) * 2 + m], tile)
            o_ref[0, d, m] = tile * (-LOG2E)


def _bias_tiles(rel_bias):
    T = ATTN_TILE
    return pl.pallas_call(
        _bias_kernel,
        grid=(N_DIFF_HEADS,),
        in_specs=[pl.BlockSpec(memory_space=pltpu.SMEM)],
        out_specs=pl.BlockSpec((1, 5, 2, T, T), lambda h: (h, 0, 0, 0, 0)),
        out_shape=jax.ShapeDtypeStruct((N_DIFF_HEADS, 5, 2, T, T), F32),
        compiler_params=_cparams(("parallel",)),
        name="t5_bias",
    )(rel_bias.reshape(-1))


def _attn_kernel(*refs, lam_init):
    bound_ref = refs[1]
    nq = refs[2].shape[0] // ATTN_TILE
    unshifted_ok = bound_ref[0] <= EXP2_SAFE_RANGE

    def sweep(unshifted, unroll):
        def body(qi, _):
            _attn_query_tile(qi, *refs, lam_init=lam_init, unshifted=unshifted)
            return 0
        lax.fori_loop(0, nq, body, 0, unroll=unroll)

    @pl.when(unshifted_ok)
    def _():
        sweep(True, 2)

    @pl.when(jnp.logical_not(unshifted_ok))
    def _():
        sweep(False, 1)


def _attn_query_tile(qi, rb_ref, bound_ref, q_ref, k_ref, vt_ref, band_ref, lam_ref, on_ref, o_ref,
                     m_ref, l_ref, acc_ref, *, lam_init, unshifted):
    T = ATTN_TILE
    L = k_ref.shape[0]
    nk = L // T
    h = pl.program_id(0)
    q = q_ref[pl.ds(pl.multiple_of(qi * T, T), T), :]
    qt = q.astype(F32).T
    chan = lax.broadcasted_iota(jnp.int32, qt.shape, 0)
    qm = [jnp.where(chan < DIFF_DQK, qt, 0.0).astype(BF16), jnp.where(chan >= DIFF_DQK, qt, 0.0).astype(BF16)]

    def side_const(bucket, m):
        return rb_ref[(bucket * N_DIFF_HEADS + h) * 2 + m] * LOG2E

    def tile_start(j):
        return pl.multiple_of(j * T, T)

    def unshifted_softmax():
        f_left = [jnp.exp2(jnp.full((1, 1), side_const(NUM_BUCKETS // 2 - 1, m), F32)).astype(BF16)
                  for m in range(2)]
        f_right = [jnp.exp2(jnp.full((1, 1), side_const(NUM_BUCKETS - 1, m), F32)).astype(BF16)
                   for m in range(2)]
        band_idx = [jnp.where(qi == 0, 4, 1), 2, jnp.where(qi == nk - 1, 0, 3)]
        l = [jnp.zeros((1, T), F32) for _ in range(2)]
        acc = [jnp.zeros((DIFF_DV, T), F32) for _ in range(2)]

        def scores(t):
            kt = k_ref[pl.ds(tile_start((qi + (2 + t)) % nk), T), :]
            return [_dot(kt, qm[m]) for m in range(2)]

        pending = [scores(t) for t in range(ATTN_LOOKAHEAD)]
        for t in range(nk):
            j = (qi + (2 + t)) % nk
            vt = vt_ref[:, pl.ds(tile_start(j), T)]
            s_cur = pending.pop(0)
            if t + ATTN_LOOKAHEAD < nk:
                pending.append(scores(t + ATTN_LOOKAHEAD))
            for m in range(2):
                if t >= nk - 3:
                    p = jnp.exp2(s_cur[m] - band_ref[0, band_idx[t - (nk - 3)], m])
                    l[m] = l[m] + jnp.sum(p, axis=0, keepdims=True)
                    acc[m] = acc[m] + _dot(vt, p.astype(BF16))
                else:
                    f = jnp.where(j < qi, f_left[m], f_right[m])
                    p = jnp.exp2(s_cur[m])
                    l[m] = l[m] + f.astype(F32) * jnp.sum(p, axis=0, keepdims=True)
                    acc[m] = acc[m] + _dot(vt * f, p.astype(BF16))
        for m in range(2):
            l_ref[m] = l[m]
            acc_ref[m] = acc[m]

    def online_softmax():
        m_ref[...] = jnp.full(m_ref.shape, -jnp.inf, F32)
        l_ref[...] = jnp.zeros(l_ref.shape, F32)
        acc_ref[...] = jnp.zeros(acc_ref.shape, F32)

        def body(j, _):
            r0 = tile_start(j)
            kt = k_ref[pl.ds(r0, T), :]
            vt = vt_ref[:, pl.ds(r0, T)]
            idx = jnp.clip(j - qi, -2, 2) + 2
            ss = [_dot(kt, qm[m]) for m in range(2)]
            for m in range(2):
                s = ss[m] - band_ref[0, idx, m]
                m_old = m_ref[m]
                m_new = jnp.maximum(m_old, jnp.max(s, axis=0, keepdims=True))
                alpha = jnp.exp2(m_old - m_new)
                p = jnp.exp2(s - m_new)
                l_ref[m] = alpha * l_ref[m] + jnp.sum(p, axis=0, keepdims=True)
                acc_ref[m] = alpha * acc_ref[m] + _dot(vt, p.astype(BF16))
                m_ref[m] = m_new
            return 0

        lax.fori_loop(0, nk, body, 0)

    if unshifted:
        unshifted_softmax()
    else:
        online_softmax()

    lp = lam_ref[...]
    lam = (jnp.exp(jnp.sum(lp[0:1] * lp[1:2], axis=-1, keepdims=True))
           - jnp.exp(jnp.sum(lp[2:3] * lp[3:4], axis=-1, keepdims=True)) + lam_init)
    o = acc_ref[0] / l_ref[0] - lam * (acc_ref[1] / l_ref[1])
    y = o * lax.rsqrt(jnp.mean(o * o, axis=0, keepdims=True) + NORM_EPS) * on_ref[...] * (1.0 - lam_init)
    o_ref[pl.ds(pl.multiple_of(qi * T, T), T), :] = y.T.astype(o_ref.dtype)


def _diff_attn(rel_bias_flat, score_bound, dq, dk, dvt, band, lam_p, onorm, batch, lam_init):
    n = dq.shape[0]
    L = n // batch
    T = ATTN_TILE
    seq = pl.BlockSpec((L, 2 * DIFF_DQK), lambda h, b: (b, h))
    return pl.pallas_call(
        functools.partial(_attn_kernel, lam_init=lam_init),
        grid=(N_DIFF_HEADS, batch),
        in_specs=[pl.BlockSpec(memory_space=pltpu.SMEM),
                  pl.BlockSpec(memory_space=pltpu.SMEM),
                  seq, seq,
                  pl.BlockSpec((DIFF_DV, L), lambda h, b: (h, b)),
                  pl.BlockSpec((1, 5, 2, T, T), lambda h, b: (h, 0, 0, 0, 0)),
                  pl.BlockSpec((4, DIFF_DQK), lambda h, b: (0, 0)),
                  pl.BlockSpec((DIFF_DV, 1), lambda h, b: (0, 0))],
        out_specs=pl.BlockSpec((L, DIFF_DV), lambda h, b: (b, h)),
        out_shape=jax.ShapeDtypeStruct((n, DIFF_V), BF16),
        scratch_shapes=[pltpu.VMEM((2, 1, T), F32), pltpu.VMEM((2, 1, T), F32),
                        pltpu.VMEM((2, DIFF_DV, T), F32)],
        compiler_params=_cparams(("parallel", "parallel")),
        name="diff_attn",
    )(rel_bias_flat, score_bound, dq, dk, dvt, band, lam_p, onorm)


def _conv_kernel(cv_ref, w_ref, b_ref, g_ref, beta_ref, y_ref, u_ref):
    L = cv_ref.shape[0]
    R = CONV_ROWS
    P = CONV_PAD
    zeros = jnp.zeros((P, CONV_CH), F32)
    u_ref[:P, :] = zeros
    u_ref[P + L:, :] = zeros
    u_ref[P:P + L, :] = cv_ref[:, :CONV_CH] * _sigmoid(cv_ref[:, CONV_CH:])
    off = P - CONV_K // 2

    def tile(t, _):
        r0 = pl.multiple_of(t * R, R)
        win = u_ref[pl.ds(r0, R + 2 * P), :]
        shifted = [win] + [pltpu.roll(win, R + 2 * P - r, 0) for r in range(1, 8)]
        acc = jnp.zeros((R, CONV_CH), F32) + b_ref[...]
        for k in range(CONV_K):
            a8, r = divmod(off + k, 8)
            acc = acc + w_ref[k:k + 1, :] * shifted[r][8 * a8:8 * a8 + R, :]
        mu = jnp.mean(acc, axis=-1, keepdims=True)
        xc = acc - mu
        y = xc * lax.rsqrt(jnp.mean(xc * xc, axis=-1, keepdims=True) + NORM_EPS) * g_ref[...] + beta_ref[...]
        y_ref[pl.ds(r0, R), :] = _silu(y)
        return 0

    lax.fori_loop(0, L // R, tile, 0)


def _conv(cv, w, b, g, beta, batch):
    n = cv.shape[0]
    L = n // batch
    return pl.pallas_call(
        _conv_kernel,
        grid=(batch,),
        in_specs=[pl.BlockSpec((L, 2 * CONV_CH), lambda i: (i, 0)),
                  _const_spec((CONV_K + 1, CONV_CH)), _const_spec((1, CONV_CH)),
                  _const_spec((1, CONV_CH)), _const_spec((1, CONV_CH))],
        out_specs=pl.BlockSpec((L, CONV_CH), lambda i: (i, 0)),
        out_shape=jax.ShapeDtypeStruct((n, CONV_CH), F32),
        scratch_shapes=[pltpu.VMEM((L + 2 * CONV_PAD, CONV_CH), F32)],
        compiler_params=_cparams(("parallel",)),
        name="conv",
    )(cv, w, b, g, beta)


def _post_kernel(x_ref, ya_ref, yb_ref, yc_ref, p_ref, wo_ref, fg_ref, wg_ref, wu_ref, wd_ref,
                 pg_ref, pwg_ref, pwp_ref, o_ref):
    mix = _dot(ya_ref[...].astype(BF16), wo_ref[:GLA_V, :])
    mix = mix + _dot(yb_ref[...].astype(BF16), wo_ref[GLA_V:GLA_V + DIFF_V, :])
    mix = mix + _dot(yc_ref[...].astype(BF16), wo_ref[GLA_V + DIFF_V:, :])
    x = _swiglu_half_step(x_ref[...] + mix, fg_ref, wg_ref, wu_ref, wd_ref)
    h = _rms_rows(x, pg_ref[...]).astype(BF16)
    gate = _sigmoid(_dot(h, pwg_ref[...]))
    o_ref[...] = x + gate * _dot(p_ref[...].astype(BF16), pwp_ref[...])


def _post(x, ya, yb, yc, p, w_out, ffn_gain, wg, wu, wd, ple_gain, ple_wg, ple_wp, layer):
    n = x.shape[0]
    tm = TOKEN_TILE

    def row(width):
        return pl.BlockSpec((tm, width), lambda i: (i, 0))

    return pl.pallas_call(
        _post_kernel,
        grid=(n // tm,),
        in_specs=[row(D_MODEL), row(GLA_V), row(DIFF_V), row(CONV_CH),
                  pl.BlockSpec((None, tm, PLE_DIM), lambda i: (layer, i, 0)),
                  _layer_spec((D_MODEL, D_MODEL), layer),
                  _const_spec((1, D_MODEL)), _layer_spec((D_MODEL, D_FF), layer),
                  _layer_spec((D_MODEL, D_FF), layer), _layer_spec((D_FF, D_MODEL), layer),
                  _const_spec((1, D_MODEL)), _layer_spec((D_MODEL, D_MODEL), layer),
                  _layer_spec((PLE_DIM, D_MODEL), layer)],
        out_specs=row(D_MODEL),
        out_shape=jax.ShapeDtypeStruct((n, D_MODEL), F32),
        compiler_params=_cparams(("parallel",)),
        name="post",
    )(x, ya, yb, yc, p, w_out, ffn_gain, wg, wu, wd, ple_gain, ple_wg, ple_wp)


def _pad_up(up, row0):
    return jnp.zeros((128, GLA_Q), F32).at[row0:row0 + GLA_RANK].set(up).astype(BF16)


def kernel(x, p, ffn1_norm, ffn1_w_gate, ffn1_w_up, ffn1_w_down, mix_norm, w_in, w_out, gla_gk_up_f, gla_gk_bias_f, gla_gk_up_b, gla_gk_bias_b, gla_out_norm, diff_q_norm, diff_k_norm, diff_lambda, diff_out_norm, rel_bias, conv_dw_w, conv_dw_b, conv_norm_g, conv_norm_b, ffn2_norm, ffn2_w_gate, ffn2_w_up, ffn2_w_down, ple_norm, ple_w_gate, ple_w_proj):
    B, L, _ = x.shape
    depth = w_in.shape[0]
    n = B * L
    xs = x.reshape(n, D_MODEL)
    rb_flat = rel_bias.reshape(-1)
    band = _bias_tiles(rel_bias)

    def row(v):
        return v.reshape(1, -1)

    ffn1_w = [w.astype(BF16) for w in (ffn1_w_gate, ffn1_w_up, ffn1_w_down)]
    ffn2_w = [w.astype(BF16) for w in (ffn2_w_gate, ffn2_w_up, ffn2_w_down)]
    w_in_b, w_out_b = w_in.astype(BF16), w_out.astype(BF16)
    ple_wg_b, ple_wp_b = ple_w_gate.astype(BF16), ple_w_proj.astype(BF16)
    p_rows = p.reshape(depth, n, PLE_DIM)

    for i in range(depth):
        xs = _ffn(xs, row(ffn1_norm[i]), *ffn1_w, i)

        gqk, gg, gv, go, dq, dk, dvt, cv = _inproj(
            xs, row(mix_norm[i]), w_in_b,
            _pad_up(gla_gk_up_f[i], 0), row(gla_gk_bias_f[i]),
            _pad_up(gla_gk_up_b[i], GLA_RANK), row(gla_gk_bias_b[i]),
            row(jnp.tile(diff_q_norm[i], DIFF_QK // DIFF_DQK)), row(jnp.tile(diff_k_norm[i], DIFF_QK // DIFF_DQK)),
            i)

        y_gla = _gla(gqk, gg, gv, go, row(jnp.tile(gla_out_norm[i], N_GLA_HEADS)), B)
        lam_init = 0.8 - 0.6 * math.exp(-0.3 * i)
        score_bound = LOG2E * (DIFF_DQK ** 0.5 * jnp.max(jnp.abs(diff_q_norm[i])) * jnp.max(jnp.abs(diff_k_norm[i]))
                               + jnp.max(jnp.abs(rel_bias)))
        y_diff = _diff_attn(rb_flat, score_bound.reshape(1), dq, dk, dvt, band, diff_lambda[i],
                            diff_out_norm[i].reshape(-1, 1), B, lam_init)
        conv_w = jnp.concatenate([conv_dw_w[i], jnp.zeros((1, CONV_CH), F32)], axis=0)
        y_conv = _conv(cv, conv_w, row(conv_dw_b[i]), row(conv_norm_g[i]), row(conv_norm_b[i]), B)

        xs = _post(xs, y_gla, y_diff, y_conv, p_rows, w_out_b, row(ffn2_norm[i]), *ffn2_w,
                   row(ple_norm[i]), ple_wg_b, ple_wp_b, i)
    return xs.reshape(B, L, D_MODEL)
```

```python
import functools
import math

import jax
import jax.numpy as jnp
from jax import lax
from jax.experimental import pallas as pl
from jax.experimental.pallas import tpu as pltpu

F32 = jnp.float32
BF16 = jnp.bfloat16

D_MODEL = 1024
N_GLA_HEADS = 4
GLA_DK = 32
GLA_DV = 64
GLA_Q = N_GLA_HEADS * GLA_DK
GLA_V = N_GLA_HEADS * GLA_DV
GLA_RANK = 16
GATE_TEMP = 16.0
GLA_CHUNK = 64
GLA_BLOCK = 4
DIFF_DQK = 64
N_DIFF_HEADS = 4
DIFF_DV = 128
DIFF_QK = 512
DIFF_V = 512
NUM_BUCKETS = 32
MAX_DISTANCE = 128
CONV_CH = 256
CONV_K = 31
D_FF = 2816
PLE_DIM = 256
NORM_EPS = 1e-6
LOG2E = math.log2(math.e)
EXP2_SAFE_RANGE = 100.0
D_IN = 2848

TOKEN_TILE = 512
INPROJ_TILE = 1024
INPROJ_SUB = 256
FF_CHUNKS = (1536, 1280)
ATTN_TILE = 256
ATTN_LOOKAHEAD = 2
CONV_PAD = 16
VMEM_LIMIT = 56 * 1024 * 1024


def _cparams(sem):
    return pltpu.CompilerParams(dimension_semantics=sem, vmem_limit_bytes=VMEM_LIMIT)


def _dot(a, b):
    return jnp.dot(a, b, preferred_element_type=F32)


def _dot_nt(a, b):
    return lax.dot_general(a, b, (((1,), (1,)), ((), ())), preferred_element_type=F32)


def _rms_rows(x, gain):
    return x * lax.rsqrt(jnp.mean(x * x, axis=-1, keepdims=True) + NORM_EPS) * gain


def _sigmoid(x):
    return 1.0 / (1.0 + jnp.exp(-x))


def _silu(x):
    return x * _sigmoid(x)


def _const_spec(shape):
    return pl.BlockSpec(shape, lambda *_: (0,) * len(shape), pipeline_mode=pl.Buffered(1))


def _layer_spec(shape, layer):
    return pl.BlockSpec((None,) + shape, lambda *_: (layer,) + (0,) * len(shape), pipeline_mode=pl.Buffered(1))


def _swiglu_half_step(x, g_ref, wg_ref, wu_ref, wd_ref):
    h = _rms_rows(x, g_ref[...]).astype(BF16)
    acc = jnp.zeros(x.shape, F32)
    c0 = 0
    for width in FF_CHUNKS:
        gate = _dot(h, wg_ref[:, c0:c0 + width])
        up = _dot(h, wu_ref[:, c0:c0 + width])
        a = (_silu(gate) * up).astype(BF16)
        acc = acc + _dot(a, wd_ref[c0:c0 + width, :])
        c0 += width
    return x + 0.5 * acc


def _ffn_kernel(x_ref, g_ref, wg_ref, wu_ref, wd_ref, o_ref):
    o_ref[...] = _swiglu_half_step(x_ref[...], g_ref, wg_ref, wu_ref, wd_ref)


def _ffn(x, gain, wg, wu, wd, layer):
    n = x.shape[0]
    row = pl.BlockSpec((TOKEN_TILE, D_MODEL), lambda i: (i, 0))
    return pl.pallas_call(
        _ffn_kernel,
        grid=(n // TOKEN_TILE,),
        in_specs=[row, _const_spec((1, D_MODEL)), _layer_spec((D_MODEL, D_FF), layer),
                  _layer_spec((D_MODEL, D_FF), layer), _layer_spec((D_FF, D_MODEL), layer)],
        out_specs=row,
        out_shape=jax.ShapeDtypeStruct((n, D_MODEL), F32),
        compiler_params=_cparams(("parallel",)),
        name="ffn",
    )(x, gain, wg, wu, wd)


def _block_diag_ones(n, blk, dtype):
    r = lax.broadcasted_iota(jnp.int32, (n, n), 0) // blk
    c = lax.broadcasted_iota(jnp.int32, (n, n), 1) // blk
    return jnp.where(r == c, 1.0, 0.0).astype(dtype)


def _group_mean_sq(x, blk):
    n = x.shape[-1]
    return _dot((x * x).astype(BF16), _block_diag_ones(n, blk, BF16)) * (1.0 / blk)


def _log_sigmoid(z):
    return jnp.minimum(z, 0.0) - jnp.log(1.0 + jnp.exp(-jnp.abs(z)))


def _inproj_kernel(x_ref, g_ref, w_ref, upf_ref, bf_ref, upb_ref, bb_ref, qn_ref, kn_ref,
                   gqk_ref, gg_ref, gv_ref, go_ref, dq_ref, dk_ref, dvt_ref, cu_ref):
    tm = x_ref.shape[0]
    nsub = tm // INPROJ_SUB

    def project(s):
        rows = slice(s * INPROJ_SUB, (s + 1) * INPROJ_SUB)
        h = _rms_rows(x_ref[rows, :], g_ref[...]).astype(BF16)
        return _dot(h, w_ref[...])

    def prepare(s, proj):
        rows = slice(s * INPROJ_SUB, (s + 1) * INPROJ_SUB)
        gqk_ref[rows, :GLA_Q] = proj[:, :GLA_Q] * (GLA_DK ** -0.5)
        gqk_ref[rows, GLA_Q:] = proj[:, GLA_Q:2 * GLA_Q]
        gv_ref[rows, :] = proj[:, 256:512].astype(BF16)
        go_ref[rows, :] = _silu(proj[:, 512:768]).astype(BF16)
        low = proj[:, 768:896].astype(BF16)
        zf = _dot(low, upf_ref[...]) + bf_ref[...]
        zb = _dot(low, upb_ref[...]) + bb_ref[...]
        gg_ref[rows, :GLA_Q] = _log_sigmoid(zf) / GATE_TEMP
        gg_ref[rows, GLA_Q:] = _log_sigmoid(zb) / GATE_TEMP
        rest = proj[:, 768 + 2 * GLA_RANK:]
        dq = rest[:, :512]
        dq_ref[rows, :] = (dq * lax.rsqrt(_group_mean_sq(dq, DIFF_DQK) + NORM_EPS) * qn_ref[...]
                           * (DIFF_DQK ** -0.5 * LOG2E)).astype(BF16)
        dk = rest[:, 512:1024]
        dk_ref[rows, :] = (dk * lax.rsqrt(_group_mean_sq(dk, DIFF_DQK) + NORM_EPS) * kn_ref[...]).astype(BF16)
        dvt_ref[:, rows] = rest[:, 1024:1536].T.astype(BF16)
        cu_ref[rows, :] = rest[:, 1536:1792] * _sigmoid(rest[:, 1792:2048])

    nxt = project(0)
    for s in range(nsub):
        cur = nxt
        if s + 1 < nsub:
            nxt = project(s + 1)
        prepare(s, cur)


def _inproj(x, gain, w, upf, bf, upb, bb, qn, kn, layer):
    n = x.shape[0]
    tm = INPROJ_TILE

    def row(width):
        return pl.BlockSpec((tm, width), lambda i: (i, 0))

    outs = [(2 * GLA_Q, F32), (2 * GLA_Q, F32), (GLA_V, BF16), (GLA_V, BF16),
            (DIFF_QK, BF16), (DIFF_QK, BF16), None, (CONV_CH, F32)]
    dvt_spec = pl.BlockSpec((DIFF_V, tm), lambda i: (0, i))
    return pl.pallas_call(
        _inproj_kernel,
        grid=(n // tm,),
        in_specs=[row(D_MODEL), _const_spec((1, D_MODEL)), _layer_spec((D_MODEL, D_IN), layer),
                  _const_spec((128, GLA_Q)), _const_spec((1, GLA_Q)),
                  _const_spec((128, GLA_Q)), _const_spec((1, GLA_Q)),
                  _const_spec((1, DIFF_QK)), _const_spec((1, DIFF_QK))],
        out_specs=[row(o[0]) if o else dvt_spec for o in outs],
        out_shape=[jax.ShapeDtypeStruct((n, o[0]), o[1]) if o else jax.ShapeDtypeStruct((DIFF_V, n), BF16)
                   for o in outs],
        compiler_params=_cparams(("parallel",)),
        name="inproj",
    )(x, gain, w, upf, bf, upb, bb, qn, kn)


def _gla_block_kernel(qk_ref, g_ref, v_ref, go_ref, on_ref, y_ref, of_ref, st_ref):
    C = GLA_CHUNK
    G = GLA_BLOCK
    R = G * C
    L = qk_ref.shape[0]
    nblk = L // R
    H = N_GLA_HEADS

    def iota(shape, d):
        return lax.broadcasted_iota(jnp.int32, shape, d)

    kmask = iota((H * C, GLA_Q), 0) // C == iota((H * C, GLA_Q), 1) // GLA_DK
    vmask = iota((H * C, GLA_V), 0) // C == iota((H * C, GLA_V), 1) // GLA_DV
    smask = iota((GLA_V, GLA_Q), 0) // GLA_DV == iota((GLA_V, GLA_Q), 1) // GLA_DK
    pos_a = iota((C, H * C), 0)
    pos_b = iota((C, H * C), 1) % C
    ri, ci = iota((R, R), 0), iota((R, R), 1)
    same_chunk = ri // C == ci // C
    cums = [jnp.where(jnp.logical_and(same_chunk, ci <= ri), 1.0, 0.0).astype(BF16),
            jnp.where(jnp.logical_and(same_chunk, ci >= ri), 1.0, 0.0).astype(BF16)]
    amasks = [pos_b <= pos_a, pos_b >= pos_a]
    edges = [C - 1, 0]
    orders = [list(range(G)), list(range(G - 1, -1, -1))]

    st_ref[...] = jnp.zeros(st_ref.shape, F32)

    def step(t, _):
        r0 = [pl.multiple_of(t * R, R), pl.multiple_of((nblk - 1 - t) * R, R)]
        rows = [pl.ds(r0[0], R), pl.ds(r0[1], R)]
        dirs = (0, 1)
        b, k, v, qt = {}, {}, {}, {}
        for d in dirs:
            g = g_ref[rows[d], d * GLA_Q:(d + 1) * GLA_Q]
            g_hi = g.astype(BF16)
            g_lo = (g - g_hi.astype(F32)).astype(BF16)
            b[d] = _dot(cums[d], g_hi) + _dot(cums[d], g_lo)
        a, ds, decay = {}, {}, {}
        for d in dirs:
            q = qk_ref[rows[d], :GLA_Q]
            k[d] = qk_ref[rows[d], GLA_Q:]
            v[d] = v_ref[rows[d], :].astype(F32)
            qt[d] = (q * jnp.exp(b[d])).astype(BF16)
            kt = k[d] * jnp.exp(-b[d])
            for c in range(G):
                sl = slice(c * C, (c + 1) * C)
                b_c = b[d][sl]
                b_edge = b_c[edges[d]:edges[d] + 1, :]
                decay[d, c] = jnp.exp(b_edge)
                kend = (k[d][sl] * jnp.exp(b_edge - b_c)).astype(BF16)
                kstack = jnp.where(kmask, jnp.concatenate([kt[sl]] * H, axis=0), 0.0).astype(BF16)
                a[d, c] = jnp.where(amasks[d], _dot_nt(qt[d][sl], kstack), 0.0).astype(BF16)
                ds[d, c] = _dot(v[d][sl].T.astype(BF16), kend)
        for d in dirs:
            s = st_ref[d]
            s_in = {}
            for c in orders[d]:
                s_in[c] = s.astype(BF16)
                s = s * decay[d, c] + jnp.where(smask, ds[d, c], 0.0)
            st_ref[d] = s
            out_ref = y_ref if d else of_ref
            for c in range(G):
                sl = slice(c * C, (c + 1) * C)
                vstack = jnp.where(vmask, jnp.concatenate([v[d][sl]] * H, axis=0), 0.0).astype(BF16)
                o = _dot(a[d, c], vstack) + _dot_nt(qt[d][sl], s_in[c])
                out_ref[pl.ds(pl.multiple_of(r0[d] + c * C, C), C), :] = o
        return 0

    lax.fori_loop(0, nblk, step, 0)

    norm_bd = _block_diag_ones(GLA_V, GLA_DV, BF16)

    def finish(t, _):
        rows = pl.ds(pl.multiple_of(t * R, R), R)
        tot = of_ref[rows, :] + y_ref[rows, :]
        ms = _dot((tot * tot).astype(BF16), norm_bd) * (1.0 / GLA_DV)
        y_ref[rows, :] = tot * lax.rsqrt(ms + NORM_EPS) * on_ref[...] * go_ref[rows, :].astype(F32)
        return 0

    lax.fori_loop(0, nblk, finish, 0)


def _gla(qk, g, v, go, onorm, batch):
    n = qk.shape[0]
    L = n // batch

    def seq(width):
        return pl.BlockSpec((L, width), lambda b: (b, 0))

    return pl.pallas_call(
        _gla_block_kernel,
        grid=(batch,),
        in_specs=[seq(2 * GLA_Q), seq(2 * GLA_Q), seq(GLA_V), seq(GLA_V), _const_spec((1, GLA_V))],
        out_specs=seq(GLA_V),
        out_shape=jax.ShapeDtypeStruct((n, GLA_V), F32),
        scratch_shapes=[pltpu.VMEM((L, GLA_V), F32), pltpu.VMEM((2, GLA_V, GLA_Q), F32)],
        compiler_params=_cparams(("parallel",)),
        name="gla",
    )(qk, g, v, go, onorm)


def _bias_kernel(rb_ref, o_ref):
    T = ATTN_TILE
    h = pl.program_id(0)
    nb = NUM_BUCKETS // 2
    max_exact = nb // 2
    for d in range(5):
        rel = (lax.broadcasted_iota(jnp.int32, (T, T), 0) - lax.broadcasted_iota(jnp.int32, (T, T), 1)
               + (d - 2) * T)
        ret = jnp.where(rel > 0, nb, 0)
        n = jnp.abs(rel)
        large = max_exact + (jnp.log(jnp.maximum(n, 1).astype(F32) / max_exact)
                             / math.log(MAX_DISTANCE / max_exact) * (nb - max_exact)).astype(jnp.int32)
        large = jnp.minimum(large, nb - 1)
        bucket = ret + jnp.where(n < max_exact, n, large)
        reachable = range(nb) if d < 2 else range(nb, NUM_BUCKETS) if d > 2 else range(NUM_BUCKETS)
        for m in range(2):
            tile = jnp.zeros((T, T), F32)
            for bkt in reachable:
                tile = jnp.where(bucket == bkt, rb_ref[(bkt * N_DIFF_HEADS + h) * 2 + m], tile)
            o_ref[0, d, m] = tile * (-LOG2E)


def _bias_tiles(rel_bias):
    T = ATTN_TILE
    return pl.pallas_call(
        _bias_kernel,
        grid=(N_DIFF_HEADS,),
        in_specs=[pl.BlockSpec(memory_space=pltpu.SMEM)],
        out_specs=pl.BlockSpec((1, 5, 2, T, T), lambda h: (h, 0, 0, 0, 0)),
        out_shape=jax.ShapeDtypeStruct((N_DIFF_HEADS, 5, 2, T, T), F32),
        compiler_params=_cparams(("parallel",)),
        name="t5_bias",
    )(rel_bias.reshape(-1))


def _stage_conv_window(u_ref, uw_ref):
    L = u_ref.shape[0]
    P = CONV_PAD
    rows = L // N_DIFF_HEADS
    h = pl.program_id(0)
    start = pl.multiple_of(h * rows, rows)
    uw_ref[P:P + rows, :] = u_ref[pl.ds(start, rows), :]
    above = u_ref[pl.ds(pl.multiple_of(jnp.maximum(start - P, 0), P), P), :]
    uw_ref[:P, :] = jnp.where(h > 0, above, 0.0)
    below = u_ref[pl.ds(pl.multiple_of(jnp.minimum(start + rows, L - P), P), P), :]
    uw_ref[P + rows:, :] = jnp.where(h < N_DIFF_HEADS - 1, below, 0.0)


def _conv_rows(r0, R, uw_ref, w_ref, b_ref, g_ref, beta_ref, y_ref):
    P = CONV_PAD
    off = P - CONV_K // 2
    win = uw_ref[pl.ds(r0, R + 2 * P), :]
    acc = jnp.zeros((R, CONV_CH), F32) + b_ref[...]
    for r in range(8):
        shifted = win if r == 0 else pltpu.roll(win, R + 2 * P - r, 0)
        for k in range(CONV_K):
            if (off + k) % 8 == r:
                a8 = (off + k) // 8
                acc = acc + w_ref[k:k + 1, :] * shifted[8 * a8:8 * a8 + R, :]
    mu = jnp.mean(acc, axis=-1, keepdims=True)
    xc = acc - mu
    y = xc * lax.rsqrt(jnp.mean(xc * xc, axis=-1, keepdims=True) + NORM_EPS) * g_ref[...] + beta_ref[...]
    y_ref[pl.ds(r0, R), :] = _silu(y)


def _attn_kernel(*refs, lam_init):
    bound_ref = refs[1]
    nq = refs[2].shape[0] // ATTN_TILE
    unshifted_ok = bound_ref[0] <= EXP2_SAFE_RANGE
    _stage_conv_window(refs[8], refs[-1])

    def sweep(unshifted, unroll):
        def body(qi, _):
            _attn_query_tile(qi, *refs, lam_init=lam_init, unshifted=unshifted)
            return 0
        lax.fori_loop(0, nq, body, 0, unroll=unroll)

    @pl.when(unshifted_ok)
    def _():
        sweep(True, 2)

    @pl.when(jnp.logical_not(unshifted_ok))
    def _():
        sweep(False, 1)


def _attn_query_tile(qi, rb_ref, bound_ref, q_ref, k_ref, vt_ref, band_ref, lam_ref, on_ref,
                     u_ref, cw_ref, cb_ref, cg_ref, cbeta_ref, o_ref, yc_ref,
                     m_ref, l_ref, acc_ref, uw_ref, *, lam_init, unshifted):
    T = ATTN_TILE
    L = k_ref.shape[0]
    nk = L // T
    h = pl.program_id(0)
    q = q_ref[pl.ds(pl.multiple_of(qi * T, T), T), :]
    qt = q.astype(F32).T
    chan = lax.broadcasted_iota(jnp.int32, qt.shape, 0)
    qm = [jnp.where(chan < DIFF_DQK, qt, 0.0).astype(BF16), jnp.where(chan >= DIFF_DQK, qt, 0.0).astype(BF16)]

    def side_const(bucket, m):
        return rb_ref[(bucket * N_DIFF_HEADS + h) * 2 + m] * LOG2E

    def tile_start(j):
        return pl.multiple_of(j * T, T)

    def unshifted_softmax():
        f_left = [jnp.exp2(jnp.full((1, 1), side_const(NUM_BUCKETS // 2 - 1, m), F32)).astype(BF16)
                  for m in range(2)]
        f_right = [jnp.exp2(jnp.full((1, 1), side_const(NUM_BUCKETS - 1, m), F32)).astype(BF16)
                   for m in range(2)]
        band_idx = [jnp.where(qi == 0, 4, 1), 2, jnp.where(qi == nk - 1, 0, 3)]
        l = [jnp.zeros((1, T), F32) for _ in range(2)]
        acc = [jnp.zeros((DIFF_DV, T), F32) for _ in range(2)]

        def scores(t):
            kt = k_ref[pl.ds(tile_start((qi + (2 + t)) % nk), T), :]
            return [_dot(kt, qm[m]) for m in range(2)]

        pending = [scores(t) for t in range(ATTN_LOOKAHEAD)]
        for t in range(nk):
            j = (qi + (2 + t)) % nk
            vt = vt_ref[:, pl.ds(tile_start(j), T)]
            s_cur = pending.pop(0)
            if t + ATTN_LOOKAHEAD < nk:
                pending.append(scores(t + ATTN_LOOKAHEAD))
            for m in range(2):
                if t >= nk - 3:
                    p = jnp.exp2(s_cur[m] - band_ref[0, band_idx[t - (nk - 3)], m])
                    l[m] = l[m] + jnp.sum(p, axis=0, keepdims=True)
                    acc[m] = acc[m] + _dot(vt, p.astype(BF16))
                else:
                    f = jnp.where(j < qi, f_left[m], f_right[m])
                    p = jnp.exp2(s_cur[m])
                    l[m] = l[m] + f.astype(F32) * jnp.sum(p, axis=0, keepdims=True)
                    acc[m] = acc[m] + _dot(vt * f, p.astype(BF16))
        for m in range(2):
            l_ref[m] = l[m]
            acc_ref[m] = acc[m]

    def online_softmax():
        m_ref[...] = jnp.full(m_ref.shape, -jnp.inf, F32)
        l_ref[...] = jnp.zeros(l_ref.shape, F32)
        acc_ref[...] = jnp.zeros(acc_ref.shape, F32)

        def body(j, _):
            r0 = tile_start(j)
            kt = k_ref[pl.ds(r0, T), :]
            vt = vt_ref[:, pl.ds(r0, T)]
            idx = jnp.clip(j - qi, -2, 2) + 2
            ss = [_dot(kt, qm[m]) for m in range(2)]
            for m in range(2):
                s = ss[m] - band_ref[0, idx, m]
                m_old = m_ref[m]
                m_new = jnp.maximum(m_old, jnp.max(s, axis=0, keepdims=True))
                alpha = jnp.exp2(m_old - m_new)
                p = jnp.exp2(s - m_new)
                l_ref[m] = alpha * l_ref[m] + jnp.sum(p, axis=0, keepdims=True)
                acc_ref[m] = alpha * acc_ref[m] + _dot(vt, p.astype(BF16))
                m_ref[m] = m_new
            return 0

        lax.fori_loop(0, nk, body, 0)

    if unshifted:
        unshifted_softmax()
    else:
        online_softmax()

    conv_rows = L // N_DIFF_HEADS // nk
    _conv_rows(pl.multiple_of(qi * conv_rows, conv_rows), conv_rows, uw_ref, cw_ref, cb_ref, cg_ref, cbeta_ref, yc_ref)
    lp = lam_ref[...]
    lam = (jnp.exp(jnp.sum(lp[0:1] * lp[1:2], axis=-1, keepdims=True))
           - jnp.exp(jnp.sum(lp[2:3] * lp[3:4], axis=-1, keepdims=True)) + lam_init)
    o = acc_ref[0] / l_ref[0] - lam * (acc_ref[1] / l_ref[1])
    y = o * lax.rsqrt(jnp.mean(o * o, axis=0, keepdims=True) + NORM_EPS) * on_ref[...] * (1.0 - lam_init)
    o_ref[pl.ds(pl.multiple_of(qi * T, T), T), :] = y.T.astype(o_ref.dtype)


def _diff_attn(rel_bias_flat, score_bound, dq, dk, dvt, band, lam_p, onorm, cu, conv_w, conv_b, conv_g,
               conv_beta, batch, lam_init):
    n = dq.shape[0]
    L = n // batch
    T = ATTN_TILE
    share = L // N_DIFF_HEADS
    seq = pl.BlockSpec((L, 2 * DIFF_DQK), lambda h, b: (b, h))

    def const(shape):
        return pl.BlockSpec(shape, lambda h, b: (0,) * len(shape))

    return pl.pallas_call(
        functools.partial(_attn_kernel, lam_init=lam_init),
        grid=(N_DIFF_HEADS, batch),
        in_specs=[pl.BlockSpec(memory_space=pltpu.SMEM),
                  pl.BlockSpec(memory_space=pltpu.SMEM),
                  seq, seq,
                  pl.BlockSpec((DIFF_DV, L), lambda h, b: (h, b)),
                  pl.BlockSpec((1, 5, 2, T, T), lambda h, b: (h, 0, 0, 0, 0)),
                  const((4, DIFF_DQK)), const((DIFF_DV, 1)),
                  pl.BlockSpec((L, CONV_CH), lambda h, b: (b, 0)),
                  const((CONV_K + 1, CONV_CH)), const((1, CONV_CH)), const((1, CONV_CH)), const((1, CONV_CH))],
        out_specs=[pl.BlockSpec((L, DIFF_DV), lambda h, b: (b, h)),
                   pl.BlockSpec((share, CONV_CH), lambda h, b: (b * N_DIFF_HEADS + h, 0))],
        out_shape=[jax.ShapeDtypeStruct((n, DIFF_V), BF16), jax.ShapeDtypeStruct((n, CONV_CH), F32)],
        scratch_shapes=[pltpu.VMEM((2, 1, T), F32), pltpu.VMEM((2, 1, T), F32),
                        pltpu.VMEM((2, DIFF_DV, T), F32),
                        pltpu.VMEM((share + 2 * CONV_PAD, CONV_CH), F32)],
        compiler_params=_cparams(("parallel", "parallel")),
        name="diff_attn",
    )(rel_bias_flat, score_bound, dq, dk, dvt, band, lam_p, onorm, cu, conv_w, conv_b, conv_g, conv_beta)


def _post_kernel(x_ref, ya_ref, yb_ref, yc_ref, p_ref, wo_ref, fg_ref, wg_ref, wu_ref, wd_ref,
                 pg_ref, pwg_ref, pwp_ref, o_ref):
    mix = _dot(ya_ref[...].astype(BF16), wo_ref[:GLA_V, :])
    mix = mix + _dot(yb_ref[...].astype(BF16), wo_ref[GLA_V:GLA_V + DIFF_V, :])
    mix = mix + _dot(yc_ref[...].astype(BF16), wo_ref[GLA_V + DIFF_V:, :])
    x = _swiglu_half_step(x_ref[...] + mix, fg_ref, wg_ref, wu_ref, wd_ref)
    h = _rms_rows(x, pg_ref[...]).astype(BF16)
    gate = _sigmoid(_dot(h, pwg_ref[...]))
    o_ref[...] = x + gate * _dot(p_ref[...].astype(BF16), pwp_ref[...])


def _post(x, ya, yb, yc, p, w_out, ffn_gain, wg, wu, wd, ple_gain, ple_wg, ple_wp, layer):
    n = x.shape[0]
    tm = TOKEN_TILE

    def row(width):
        return pl.BlockSpec((tm, width), lambda i: (i, 0))

    return pl.pallas_call(
        _post_kernel,
        grid=(n // tm,),
        in_specs=[row(D_MODEL), row(GLA_V), row(DIFF_V), row(CONV_CH),
                  pl.BlockSpec((None, tm, PLE_DIM), lambda i: (layer, i, 0)),
                  _layer_spec((D_MODEL, D_MODEL), layer),
                  _const_spec((1, D_MODEL)), _layer_spec((D_MODEL, D_FF), layer),
                  _layer_spec((D_MODEL, D_FF), layer), _layer_spec((D_FF, D_MODEL), layer),
                  _const_spec((1, D_MODEL)), _layer_spec((D_MODEL, D_MODEL), layer),
                  _layer_spec((PLE_DIM, D_MODEL), layer)],
        out_specs=row(D_MODEL),
        out_shape=jax.ShapeDtypeStruct((n, D_MODEL), F32),
        compiler_params=_cparams(("parallel",)),
        name="post",
    )(x, ya, yb, yc, p, w_out, ffn_gain, wg, wu, wd, ple_gain, ple_wg, ple_wp)


def _pad_up(up, row0):
    return jnp.zeros((128, GLA_Q), F32).at[row0:row0 + GLA_RANK].set(up).astype(BF16)


def kernel(x, p, ffn1_norm, ffn1_w_gate, ffn1_w_up, ffn1_w_down, mix_norm, w_in, w_out, gla_gk_up_f, gla_gk_bias_f, gla_gk_up_b, gla_gk_bias_b, gla_out_norm, diff_q_norm, diff_k_norm, diff_lambda, diff_out_norm, rel_bias, conv_dw_w, conv_dw_b, conv_norm_g, conv_norm_b, ffn2_norm, ffn2_w_gate, ffn2_w_up, ffn2_w_down, ple_norm, ple_w_gate, ple_w_proj):
    B, L, _ = x.shape
    depth = w_in.shape[0]
    n = B * L
    xs = x.reshape(n, D_MODEL)
    rb_flat = rel_bias.reshape(-1)
    band = _bias_tiles(rel_bias)

    def row(v):
        return v.reshape(1, -1)

    ffn1_w = [w.astype(BF16) for w in (ffn1_w_gate, ffn1_w_up, ffn1_w_down)]
    ffn2_w = [w.astype(BF16) for w in (ffn2_w_gate, ffn2_w_up, ffn2_w_down)]
    w_in_b, w_out_b = w_in.astype(BF16), w_out.astype(BF16)
    ple_wg_b, ple_wp_b = ple_w_gate.astype(BF16), ple_w_proj.astype(BF16)
    p_rows = p.reshape(depth, n, PLE_DIM)

    for i in range(depth):
        xs = _ffn(xs, row(ffn1_norm[i]), *ffn1_w, i)

        gqk, gg, gv, go, dq, dk, dvt, cu = _inproj(
            xs, row(mix_norm[i]), w_in_b,
            _pad_up(gla_gk_up_f[i], 0), row(gla_gk_bias_f[i]),
            _pad_up(gla_gk_up_b[i], GLA_RANK), row(gla_gk_bias_b[i]),
            row(jnp.tile(diff_q_norm[i], DIFF_QK // DIFF_DQK)), row(jnp.tile(diff_k_norm[i], DIFF_QK // DIFF_DQK)),
            i)

        y_gla = _gla(gqk, gg, gv, go, row(jnp.tile(gla_out_norm[i], N_GLA_HEADS)), B)
        lam_init = 0.8 - 0.6 * math.exp(-0.3 * i)
        score_bound = LOG2E * (DIFF_DQK ** 0.5 * jnp.max(jnp.abs(diff_q_norm[i])) * jnp.max(jnp.abs(diff_k_norm[i]))
                               + jnp.max(jnp.abs(rel_bias)))
        conv_w = jnp.concatenate([conv_dw_w[i], jnp.zeros((1, CONV_CH), F32)], axis=0)
        y_diff, y_conv = _diff_attn(rb_flat, score_bound.reshape(1), dq, dk, dvt, band, diff_lambda[i],
                                    diff_out_norm[i].reshape(-1, 1), cu, conv_w, row(conv_dw_b[i]),
                                    row(conv_norm_g[i]), row(conv_norm_b[i]), B, lam_init)

        xs = _post(xs, y_gla, y_diff, y_conv, p_rows, w_out_b, row(ffn2_norm[i]), *ffn2_w,
                   row(ple_norm[i]), ple_wg_b, ple_wp_b, i)
    return xs.reshape(B, L, D_MODEL)
```

```python
import functools
import math

import jax
import jax.numpy as jnp
from jax import lax
from jax.experimental import pallas as pl
from jax.experimental.pallas import tpu as pltpu

F32 = jnp.float32
BF16 = jnp.bfloat16

D_MODEL = 1024
N_GLA_HEADS = 4
GLA_DK = 32
GLA_DV = 64
GLA_Q = N_GLA_HEADS * GLA_DK
GLA_V = N_GLA_HEADS * GLA_DV
GLA_RANK = 16
GATE_TEMP = 16.0
GLA_CHUNK = 64
GLA_BLOCK = 4
DIFF_DQK = 64
N_DIFF_HEADS = 4
DIFF_DV = 128
DIFF_QK = 512
DIFF_V = 512
NUM_BUCKETS = 32
MAX_DISTANCE = 128
CONV_CH = 256
CONV_K = 31
D_FF = 2816
PLE_DIM = 256
NORM_EPS = 1e-6
LOG2E = math.log2(math.e)
EXP2_SAFE_RANGE = 100.0
D_IN = 2848

TOKEN_TILE = 512
INPROJ_TILE = 1024
INPROJ_SUB = 256
FF_CHUNKS = (1536, 1280)
ATTN_TILE = 256
ATTN_LOOKAHEAD = 2
ATTN_UNROLL = 4
CONV_PAD = 16
VMEM_LIMIT = 56 * 1024 * 1024
WEIGHT_CHUNK_ROWS_WIDE = 128
WEIGHT_CHUNK_ROWS_TALL = 256


def _cparams(sem):
    return pltpu.CompilerParams(dimension_semantics=sem, vmem_limit_bytes=VMEM_LIMIT)


def _dot(a, b):
    return jnp.dot(a, b, preferred_element_type=F32)


def _dot_nt(a, b):
    return lax.dot_general(a, b, (((1,), (1,)), ((), ())), preferred_element_type=F32)


def _rms_rows(x, gain):
    return x * lax.rsqrt(jnp.mean(x * x, axis=-1, keepdims=True) + NORM_EPS) * gain


def _sigmoid(x):
    return 1.0 / (1.0 + jnp.exp(-x))


def _silu(x):
    return x * _sigmoid(x)


def _const_spec(shape):
    return pl.BlockSpec(shape, lambda *_: (0,) * len(shape), pipeline_mode=pl.Buffered(1))


def _hbm_spec():
    return pl.BlockSpec(memory_space=pl.ANY)


def _stage_scratch(cols):
    rows = WEIGHT_CHUNK_ROWS_WIDE if cols > D_MODEL else WEIGHT_CHUNK_ROWS_TALL
    return [pltpu.VMEM((2, rows, cols), F32), pltpu.SemaphoreType.DMA((2,))]


def _fetch_as_bf16(src_hbm, layer, dst_ref, stage_ref, sem_ref):
    chunk = min(stage_ref.shape[1], dst_ref.shape[0])
    assert dst_ref.shape[0] % chunk == 0
    n = dst_ref.shape[0] // chunk

    def dma(c):
        return pltpu.make_async_copy(src_hbm.at[layer, pl.ds(c * chunk, chunk), :],
                                     stage_ref.at[c % 2, pl.ds(0, chunk), :], sem_ref.at[c % 2])

    dma(0).start()
    for c in range(n):
        if c + 1 < n:
            dma(c + 1).start()
        dma(c).wait()
        dst_ref[c * chunk:(c + 1) * chunk, :] = stage_ref[c % 2, :chunk, :].astype(BF16)


def _swiglu_half_step(x, g_ref, wg_ref, wu_ref, wd_ref):
    h = _rms_rows(x, g_ref[...]).astype(BF16)
    acc = jnp.zeros(x.shape, F32)
    c0 = 0
    for width in FF_CHUNKS:
        gate = _dot(h, wg_ref[:, c0:c0 + width])
        up = _dot(h, wu_ref[:, c0:c0 + width])
        a = (_silu(gate) * up).astype(BF16)
        acc = acc + _dot(a, wd_ref[c0:c0 + width, :])
        c0 += width
    return x + 0.5 * acc


def _ffn_kernel(x_ref, g_ref, wg_hbm, wu_hbm, wd_hbm, o_ref,
                wg_ref, wu_ref, wd_ref, wide_ref, wide_sem, tall_ref, tall_sem, *, layer):
    @pl.when(pl.program_id(0) == 0)
    def _():
        _fetch_as_bf16(wg_hbm, layer, wg_ref, wide_ref, wide_sem)
        _fetch_as_bf16(wu_hbm, layer, wu_ref, wide_ref, wide_sem)
        _fetch_as_bf16(wd_hbm, layer, wd_ref, tall_ref, tall_sem)

    o_ref[...] = _swiglu_half_step(x_ref[...], g_ref, wg_ref, wu_ref, wd_ref)


def _ffn(x, gain, wg, wu, wd, layer):
    n = x.shape[0]
    row = pl.BlockSpec((TOKEN_TILE, D_MODEL), lambda i: (i, 0))
    return pl.pallas_call(
        functools.partial(_ffn_kernel, layer=layer),
        grid=(n // TOKEN_TILE,),
        in_specs=[row, _const_spec((1, D_MODEL)), _hbm_spec(), _hbm_spec(), _hbm_spec()],
        out_specs=row,
        out_shape=jax.ShapeDtypeStruct((n, D_MODEL), F32),
        scratch_shapes=[pltpu.VMEM((D_MODEL, D_FF), BF16), pltpu.VMEM((D_MODEL, D_FF), BF16),
                        pltpu.VMEM((D_FF, D_MODEL), BF16)] + _stage_scratch(D_FF) + _stage_scratch(D_MODEL),
        compiler_params=_cparams(("arbitrary",)),
        name="ffn",
    )(x, gain, wg, wu, wd)


def _block_diag_ones(n, blk, dtype):
    r = lax.broadcasted_iota(jnp.int32, (n, n), 0) // blk
    c = lax.broadcasted_iota(jnp.int32, (n, n), 1) // blk
    return jnp.where(r == c, 1.0, 0.0).astype(dtype)


def _group_mean_sq(x, blk):
    n = x.shape[-1]
    return _dot((x * x).astype(BF16), _block_diag_ones(n, blk, BF16)) * (1.0 / blk)


def _log_sigmoid(z):
    return jnp.minimum(z, 0.0) - jnp.log(1.0 + jnp.exp(-jnp.abs(z)))


def _inproj_kernel(x_ref, g_ref, w_hbm, upf_ref, bf_ref, upb_ref, bb_ref, qn_ref, kn_ref,
                   gqk_ref, gg_ref, gv_ref, go_ref, dq_ref, dk_ref, dvt_ref, cu_ref,
                   w_ref, stage_ref, stage_sem, *, layer):
    @pl.when(pl.program_id(0) == 0)
    def _():
        _fetch_as_bf16(w_hbm, layer, w_ref, stage_ref, stage_sem)

    tm = x_ref.shape[0]
    nsub = tm // INPROJ_SUB

    def project(s):
        rows = slice(s * INPROJ_SUB, (s + 1) * INPROJ_SUB)
        h = _rms_rows(x_ref[rows, :], g_ref[...]).astype(BF16)
        return _dot(h, w_ref[...])

    def prepare(s, proj):
        rows = slice(s * INPROJ_SUB, (s + 1) * INPROJ_SUB)
        gqk_ref[rows, :GLA_Q] = proj[:, :GLA_Q] * (GLA_DK ** -0.5)
        gqk_ref[rows, GLA_Q:] = proj[:, GLA_Q:2 * GLA_Q]
        gv_ref[rows, :] = proj[:, 256:512].astype(BF16)
        go_ref[rows, :] = _silu(proj[:, 512:768]).astype(BF16)
        low = proj[:, 768:896].astype(BF16)
        zf = _dot(low, upf_ref[...]) + bf_ref[...]
        zb = _dot(low, upb_ref[...]) + bb_ref[...]
        gg_ref[rows, :GLA_Q] = _log_sigmoid(zf) / GATE_TEMP
        gg_ref[rows, GLA_Q:] = _log_sigmoid(zb) / GATE_TEMP
        rest = proj[:, 768 + 2 * GLA_RANK:]
        dq = rest[:, :512]
        dq_ref[rows, :] = (dq * lax.rsqrt(_group_mean_sq(dq, DIFF_DQK) + NORM_EPS) * qn_ref[...]
                           * (DIFF_DQK ** -0.5 * LOG2E)).astype(BF16)
        dk = rest[:, 512:1024]
        dk_ref[rows, :] = (dk * lax.rsqrt(_group_mean_sq(dk, DIFF_DQK) + NORM_EPS) * kn_ref[...]).astype(BF16)
        dvt_ref[:, rows] = rest[:, 1024:1536].T.astype(BF16)
        cu_ref[rows, :] = rest[:, 1536:1792] * _sigmoid(rest[:, 1792:2048])

    nxt = project(0)
    for s in range(nsub):
        cur = nxt
        if s + 1 < nsub:
            nxt = project(s + 1)
        prepare(s, cur)


def _inproj(x, gain, w, upf, bf, upb, bb, qn, kn, layer):
    n = x.shape[0]
    tm = INPROJ_TILE

    def row(width):
        return pl.BlockSpec((tm, width), lambda i: (i, 0))

    outs = [(2 * GLA_Q, F32), (2 * GLA_Q, F32), (GLA_V, BF16), (GLA_V, BF16),
            (DIFF_QK, BF16), (DIFF_QK, BF16), None, (CONV_CH, F32)]
    dvt_spec = pl.BlockSpec((DIFF_V, tm), lambda i: (0, i))
    return pl.pallas_call(
        functools.partial(_inproj_kernel, layer=layer),
        grid=(n // tm,),
        in_specs=[row(D_MODEL), _const_spec((1, D_MODEL)), _hbm_spec(),
                  _const_spec((128, GLA_Q)), _const_spec((1, GLA_Q)),
                  _const_spec((128, GLA_Q)), _const_spec((1, GLA_Q)),
                  _const_spec((1, DIFF_QK)), _const_spec((1, DIFF_QK))],
        out_specs=[row(o[0]) if o else dvt_spec for o in outs],
        out_shape=[jax.ShapeDtypeStruct((n, o[0]), o[1]) if o else jax.ShapeDtypeStruct((DIFF_V, n), BF16)
                   for o in outs],
        scratch_shapes=[pltpu.VMEM((D_MODEL, D_IN), BF16)] + _stage_scratch(D_IN),
        compiler_params=_cparams(("arbitrary",)),
        name="inproj",
    )(x, gain, w, upf, bf, upb, bb, qn, kn)


def _gla_block_kernel(qk_ref, g_ref, v_ref, go_ref, on_ref, y_ref, of_ref, st_ref):
    C = GLA_CHUNK
    G = GLA_BLOCK
    R = G * C
    L = qk_ref.shape[0]
    nblk = L // R
    H = N_GLA_HEADS

    def iota(shape, d):
        return lax.broadcasted_iota(jnp.int32, shape, d)

    kmask = iota((H * C, GLA_Q), 0) // C == iota((H * C, GLA_Q), 1) // GLA_DK
    vmask = iota((H * C, GLA_V), 0) // C == iota((H * C, GLA_V), 1) // GLA_DV
    smask = iota((GLA_V, GLA_Q), 0) // GLA_DV == iota((GLA_V, GLA_Q), 1) // GLA_DK
    pos_a = iota((C, H * C), 0)
    pos_b = iota((C, H * C), 1) % C
    ri, ci = iota((R, R), 0), iota((R, R), 1)
    same_chunk = ri // C == ci // C
    cums = [jnp.where(jnp.logical_and(same_chunk, ci <= ri), 1.0, 0.0).astype(BF16),
            jnp.where(jnp.logical_and(same_chunk, ci >= ri), 1.0, 0.0).astype(BF16)]
    amasks = [pos_b <= pos_a, pos_b >= pos_a]
    edges = [C - 1, 0]
    orders = [list(range(G)), list(range(G - 1, -1, -1))]

    st_ref[...] = jnp.zeros(st_ref.shape, F32)

    def step(t, _):
        r0 = [pl.multiple_of(t * R, R), pl.multiple_of((nblk - 1 - t) * R, R)]
        rows = [pl.ds(r0[0], R), pl.ds(r0[1], R)]
        dirs = (0, 1)
        b, k, v, qt = {}, {}, {}, {}
        for d in dirs:
            g = g_ref[rows[d], d * GLA_Q:(d + 1) * GLA_Q]
            g_hi = g.astype(BF16)
            g_lo = (g - g_hi.astype(F32)).astype(BF16)
            b[d] = _dot(cums[d], g_hi) + _dot(cums[d], g_lo)
        a, ds, decay = {}, {}, {}
        for d in dirs:
            q = qk_ref[rows[d], :GLA_Q]
            k[d] = qk_ref[rows[d], GLA_Q:]
            v[d] = v_ref[rows[d], :].astype(F32)
            qt[d] = (q * jnp.exp(b[d])).astype(BF16)
            kt = k[d] * jnp.exp(-b[d])
            for c in range(G):
                sl = slice(c * C, (c + 1) * C)
                b_c = b[d][sl]
                b_edge = b_c[edges[d]:edges[d] + 1, :]
                decay[d, c] = jnp.exp(b_edge)
                kend = (k[d][sl] * jnp.exp(b_edge - b_c)).astype(BF16)
                kstack = jnp.where(kmask, jnp.concatenate([kt[sl]] * H, axis=0), 0.0).astype(BF16)
                a[d, c] = jnp.where(amasks[d], _dot_nt(qt[d][sl], kstack), 0.0).astype(BF16)
                ds[d, c] = _dot(v[d][sl].T.astype(BF16), kend)
        for d in dirs:
            s = st_ref[d]
            s_in = {}
            for c in orders[d]:
                s_in[c] = s.astype(BF16)
                s = s * decay[d, c] + jnp.where(smask, ds[d, c], 0.0)
            st_ref[d] = s
            out_ref = y_ref if d else of_ref
            for c in range(G):
                sl = slice(c * C, (c + 1) * C)
                vstack = jnp.where(vmask, jnp.concatenate([v[d][sl]] * H, axis=0), 0.0).astype(BF16)
                o = _dot(a[d, c], vstack) + _dot_nt(qt[d][sl], s_in[c])
                out_ref[pl.ds(pl.multiple_of(r0[d] + c * C, C), C), :] = o
        return 0

    lax.fori_loop(0, nblk, step, 0)

    norm_bd = _block_diag_ones(GLA_V, GLA_DV, BF16)

    def finish(t, _):
        rows = pl.ds(pl.multiple_of(t * R, R), R)
        tot = of_ref[rows, :] + y_ref[rows, :]
        ms = _dot((tot * tot).astype(BF16), norm_bd) * (1.0 / GLA_DV)
        y_ref[rows, :] = tot * lax.rsqrt(ms + NORM_EPS) * on_ref[...] * go_ref[rows, :].astype(F32)
        return 0

    lax.fori_loop(0, nblk, finish, 0)


def _gla(qk, g, v, go, onorm, batch):
    n = qk.shape[0]
    L = n // batch

    def seq(width):
        return pl.BlockSpec((L, width), lambda b: (b, 0))

    return pl.pallas_call(
        _gla_block_kernel,
        grid=(batch,),
        in_specs=[seq(2 * GLA_Q), seq(2 * GLA_Q), seq(GLA_V), seq(GLA_V), _const_spec((1, GLA_V))],
        out_specs=seq(GLA_V),
        out_shape=jax.ShapeDtypeStruct((n, GLA_V), F32),
        scratch_shapes=[pltpu.VMEM((L, GLA_V), F32), pltpu.VMEM((2, GLA_V, GLA_Q), F32)],
        compiler_params=_cparams(("parallel",)),
        name="gla",
    )(qk, g, v, go, onorm)


def _bias_kernel(rb_ref, o_ref):
    T = ATTN_TILE
    h = pl.program_id(0)
    nb = NUM_BUCKETS // 2
    max_exact = nb // 2
    for d in range(5):
        rel = (lax.broadcasted_iota(jnp.int32, (T, T), 0) - lax.broadcasted_iota(jnp.int32, (T, T), 1)
               + (d - 2) * T)
        ret = jnp.where(rel > 0, nb, 0)
        n = jnp.abs(rel)
        large = max_exact + (jnp.log(jnp.maximum(n, 1).astype(F32) / max_exact)
                             / math.log(MAX_DISTANCE / max_exact) * (nb - max_exact)).astype(jnp.int32)
        large = jnp.minimum(large, nb - 1)
        bucket = ret + jnp.where(n < max_exact, n, large)
        reachable = range(nb) if d < 2 else range(nb, NUM_BUCKETS) if d > 2 else range(NUM_BUCKETS)
        for m in range(2):
            tile = jnp.zeros((T, T), F32)
            for bkt in reachable:
                tile = jnp.where(bucket == bkt, rb_ref[(bkt * N_DIFF_HEADS + h) * 2 + m], tile)
            o_ref[0, d, m] = tile * (-LOG2E)


def _bias_tiles(rel_bias):
    T = ATTN_TILE
    return pl.pallas_call(
        _bias_kernel,
        grid=(N_DIFF_HEADS,),
        in_specs=[pl.BlockSpec(memory_space=pltpu.SMEM)],
        out_specs=pl.BlockSpec((1, 5, 2, T, T), lambda h: (h, 0, 0, 0, 0)),
        out_shape=jax.ShapeDtypeStruct((N_DIFF_HEADS, 5, 2, T, T), F32),
        compiler_params=_cparams(("parallel",)),
        name="t5_bias",
    )(rel_bias.reshape(-1))


def _stage_conv_window(u_ref, uw_ref):
    L = u_ref.shape[0]
    P = CONV_PAD
    rows = L // N_DIFF_HEADS
    h = pl.program_id(0)
    start = pl.multiple_of(h * rows, rows)
    uw_ref[P:P + rows, :] = u_ref[pl.ds(start, rows), :]
    above = u_ref[pl.ds(pl.multiple_of(jnp.maximum(start - P, 0), P), P), :]
    uw_ref[:P, :] = jnp.where(h > 0, above, 0.0)
    below = u_ref[pl.ds(pl.multiple_of(jnp.minimum(start + rows, L - P), P), P), :]
    uw_ref[P + rows:, :] = jnp.where(h < N_DIFF_HEADS - 1, below, 0.0)


def _conv_rows(r0, R, uw_ref, w_ref, b_ref, g_ref, beta_ref, y_ref):
    P = CONV_PAD
    off = P - CONV_K // 2
    win = uw_ref[pl.ds(r0, R + 2 * P), :]
    acc = jnp.zeros((R, CONV_CH), F32) + b_ref[...]
    for r in range(8):
        shifted = win if r == 0 else pltpu.roll(win, R + 2 * P - r, 0)
        for k in range(CONV_K):
            if (off + k) % 8 == r:
                a8 = (off + k) // 8
                acc = acc + w_ref[k:k + 1, :] * shifted[8 * a8:8 * a8 + R, :]
    mu = jnp.mean(acc, axis=-1, keepdims=True)
    xc = acc - mu
    y = xc * lax.rsqrt(jnp.mean(xc * xc, axis=-1, keepdims=True) + NORM_EPS) * g_ref[...] + beta_ref[...]
    y_ref[pl.ds(r0, R), :] = _silu(y)


def _attn_kernel(*refs, lam_init):
    bound_ref = refs[1]
    nq = refs[2].shape[0] // ATTN_TILE
    unshifted_ok = bound_ref[0] <= EXP2_SAFE_RANGE
    _stage_conv_window(refs[8], refs[-1])

    def sweep(unshifted, unroll):
        def body(qi, _):
            _attn_query_tile(qi, *refs, lam_init=lam_init, unshifted=unshifted)
            return 0
        lax.fori_loop(0, nq, body, 0, unroll=unroll)

    @pl.when(unshifted_ok)
    def _():
        sweep(True, ATTN_UNROLL)

    @pl.when(jnp.logical_not(unshifted_ok))
    def _():
        sweep(False, 1)


def _attn_query_tile(qi, rb_ref, bound_ref, q_ref, k_ref, vt_ref, band_ref, lam_ref, on_ref,
                     u_ref, cw_ref, cb_ref, cg_ref, cbeta_ref, o_ref, yc_ref,
                     m_ref, l_ref, acc_ref, uw_ref, *, lam_init, unshifted):
    T = ATTN_TILE
    L = k_ref.shape[0]
    nk = L // T
    h = pl.program_id(0)
    q = q_ref[pl.ds(pl.multiple_of(qi * T, T), T), :]
    qt = q.astype(F32).T
    chan = lax.broadcasted_iota(jnp.int32, qt.shape, 0)
    qm = [jnp.where(chan < DIFF_DQK, qt, 0.0).astype(BF16), jnp.where(chan >= DIFF_DQK, qt, 0.0).astype(BF16)]

    def side_const(bucket, m):
        return rb_ref[(bucket * N_DIFF_HEADS + h) * 2 + m] * LOG2E

    def tile_start(j):
        return pl.multiple_of(j * T, T)

    def unshifted_softmax():
        f_left = [jnp.exp2(jnp.full((1, 1), side_const(NUM_BUCKETS // 2 - 1, m), F32)).astype(BF16)
                  for m in range(2)]
        f_right = [jnp.exp2(jnp.full((1, 1), side_const(NUM_BUCKETS - 1, m), F32)).astype(BF16)
                   for m in range(2)]
        band_idx = [jnp.where(qi == 0, 4, 1), 2, jnp.where(qi == nk - 1, 0, 3)]
        l = [jnp.zeros((1, T), F32) for _ in range(2)]
        acc = [jnp.zeros((DIFF_DV, T), F32) for _ in range(2)]

        def scores(t):
            kt = k_ref[pl.ds(tile_start((qi + (2 + t)) % nk), T), :]
            return [_dot(kt, qm[m]) for m in range(2)]

        pending = [scores(t) for t in range(ATTN_LOOKAHEAD)]
        for t in range(nk):
            j = (qi + (2 + t)) % nk
            vt = vt_ref[:, pl.ds(tile_start(j), T)]
            s_cur = pending.pop(0)
            if t + ATTN_LOOKAHEAD < nk:
                pending.append(scores(t + ATTN_LOOKAHEAD))
            for m in range(2):
                if t >= nk - 3:
                    p = jnp.exp2(s_cur[m] - band_ref[0, band_idx[t - (nk - 3)], m])
                    l[m] = l[m] + jnp.sum(p, axis=0, keepdims=True)
                    acc[m] = acc[m] + _dot(vt, p.astype(BF16))
                else:
                    f = jnp.where(j < qi, f_left[m], f_right[m])
                    p = jnp.exp2(s_cur[m])
                    l[m] = l[m] + f.astype(F32) * jnp.sum(p, axis=0, keepdims=True)
                    acc[m] = acc[m] + _dot(vt * f, p.astype(BF16))
        for m in range(2):
            l_ref[m] = l[m]
            acc_ref[m] = acc[m]

    def online_softmax():
        m_ref[...] = jnp.full(m_ref.shape, -jnp.inf, F32)
        l_ref[...] = jnp.zeros(l_ref.shape, F32)
        acc_ref[...] = jnp.zeros(acc_ref.shape, F32)

        def body(j, _):
            r0 = tile_start(j)
            kt = k_ref[pl.ds(r0, T), :]
            vt = vt_ref[:, pl.ds(r0, T)]
            idx = jnp.clip(j - qi, -2, 2) + 2
            ss = [_dot(kt, qm[m]) for m in range(2)]
            for m in range(2):
                s = ss[m] - band_ref[0, idx, m]
                m_old = m_ref[m]
                m_new = jnp.maximum(m_old, jnp.max(s, axis=0, keepdims=True))
                alpha = jnp.exp2(m_old - m_new)
                p = jnp.exp2(s - m_new)
                l_ref[m] = alpha * l_ref[m] + jnp.sum(p, axis=0, keepdims=True)
                acc_ref[m] = alpha * acc_ref[m] + _dot(vt, p.astype(BF16))
                m_ref[m] = m_new
            return 0

        lax.fori_loop(0, nk, body, 0)

    if unshifted:
        unshifted_softmax()
    else:
        online_softmax()

    conv_rows = L // N_DIFF_HEADS // nk
    _conv_rows(pl.multiple_of(qi * conv_rows, conv_rows), conv_rows, uw_ref, cw_ref, cb_ref, cg_ref, cbeta_ref, yc_ref)
    lp = lam_ref[...]
    lam = (jnp.exp(jnp.sum(lp[0:1] * lp[1:2], axis=-1, keepdims=True))
           - jnp.exp(jnp.sum(lp[2:3] * lp[3:4], axis=-1, keepdims=True)) + lam_init)
    o = acc_ref[0] / l_ref[0] - lam * (acc_ref[1] / l_ref[1])
    y = o * lax.rsqrt(jnp.mean(o * o, axis=0, keepdims=True) + NORM_EPS) * on_ref[...] * (1.0 - lam_init)
    o_ref[pl.ds(pl.multiple_of(qi * T, T), T), :] = y.T.astype(o_ref.dtype)


def _diff_attn(rel_bias_flat, score_bound, dq, dk, dvt, band, lam_p, onorm, cu, conv_w, conv_b, conv_g,
               conv_beta, batch, lam_init):
    n = dq.shape[0]
    L = n // batch
    T = ATTN_TILE
    share = L // N_DIFF_HEADS
    seq = pl.BlockSpec((L, 2 * DIFF_DQK), lambda h, b: (b, h))

    def const(shape):
        return pl.BlockSpec(shape, lambda h, b: (0,) * len(shape))

    return pl.pallas_call(
        functools.partial(_attn_kernel, lam_init=lam_init),
        grid=(N_DIFF_HEADS, batch),
        in_specs=[pl.BlockSpec(memory_space=pltpu.SMEM),
                  pl.BlockSpec(memory_space=pltpu.SMEM),
                  seq, seq,
                  pl.BlockSpec((DIFF_DV, L), lambda h, b: (h, b)),
                  pl.BlockSpec((1, 5, 2, T, T), lambda h, b: (h, 0, 0, 0, 0)),
                  const((4, DIFF_DQK)), const((DIFF_DV, 1)),
                  pl.BlockSpec((L, CONV_CH), lambda h, b: (b, 0)),
                  const((CONV_K + 1, CONV_CH)), const((1, CONV_CH)), const((1, CONV_CH)), const((1, CONV_CH))],
        out_specs=[pl.BlockSpec((L, DIFF_DV), lambda h, b: (b, h)),
                   pl.BlockSpec((share, CONV_CH), lambda h, b: (b * N_DIFF_HEADS + h, 0))],
        out_shape=[jax.ShapeDtypeStruct((n, DIFF_V), BF16), jax.ShapeDtypeStruct((n, CONV_CH), F32)],
        scratch_shapes=[pltpu.VMEM((2, 1, T), F32), pltpu.VMEM((2, 1, T), F32),
                        pltpu.VMEM((2, DIFF_DV, T), F32),
                        pltpu.VMEM((share + 2 * CONV_PAD, CONV_CH), F32)],
        compiler_params=_cparams(("parallel", "parallel")),
        name="diff_attn",
    )(rel_bias_flat, score_bound, dq, dk, dvt, band, lam_p, onorm, cu, conv_w, conv_b, conv_g, conv_beta)


def _post_kernel(x_ref, ya_ref, yb_ref, yc_ref, p_ref, wo_hbm, fg_ref, wg_hbm, wu_hbm, wd_hbm,
                 pg_ref, pwg_hbm, pwp_hbm, o_ref,
                 wo_ref, wg_ref, wu_ref, wd_ref, pwg_ref, pwp_ref, wide_ref, wide_sem, tall_ref, tall_sem,
                 *, layer):
    @pl.when(pl.program_id(0) == 0)
    def _():
        _fetch_as_bf16(wo_hbm, layer, wo_ref, tall_ref, tall_sem)
        _fetch_as_bf16(wg_hbm, layer, wg_ref, wide_ref, wide_sem)
        _fetch_as_bf16(wu_hbm, layer, wu_ref, wide_ref, wide_sem)
        _fetch_as_bf16(wd_hbm, layer, wd_ref, tall_ref, tall_sem)
        _fetch_as_bf16(pwg_hbm, layer, pwg_ref, tall_ref, tall_sem)
        _fetch_as_bf16(pwp_hbm, layer, pwp_ref, tall_ref, tall_sem)

    mix = _dot(ya_ref[...].astype(BF16), wo_ref[:GLA_V, :])
    mix = mix + _dot(yb_ref[...].astype(BF16), wo_ref[GLA_V:GLA_V + DIFF_V, :])
    mix = mix + _dot(yc_ref[...].astype(BF16), wo_ref[GLA_V + DIFF_V:, :])
    x = _swiglu_half_step(x_ref[...] + mix, fg_ref, wg_ref, wu_ref, wd_ref)
    h = _rms_rows(x, pg_ref[...]).astype(BF16)
    gate = _sigmoid(_dot(h, pwg_ref[...]))
    o_ref[...] = x + gate * _dot(p_ref[...].astype(BF16), pwp_ref[...])


def _post(x, ya, yb, yc, p, w_out, ffn_gain, wg, wu, wd, ple_gain, ple_wg, ple_wp, layer):
    n = x.shape[0]
    tm = TOKEN_TILE

    def row(width):
        return pl.BlockSpec((tm, width), lambda i: (i, 0))

    return pl.pallas_call(
        functools.partial(_post_kernel, layer=layer),
        grid=(n // tm,),
        in_specs=[row(D_MODEL), row(GLA_V), row(DIFF_V), row(CONV_CH),
                  pl.BlockSpec((None, tm, PLE_DIM), lambda i: (layer, i, 0)),
                  _hbm_spec(), _const_spec((1, D_MODEL)), _hbm_spec(), _hbm_spec(), _hbm_spec(),
                  _const_spec((1, D_MODEL)), _hbm_spec(), _hbm_spec()],
        out_specs=row(D_MODEL),
        out_shape=jax.ShapeDtypeStruct((n, D_MODEL), F32),
        scratch_shapes=[pltpu.VMEM((D_MODEL, D_MODEL), BF16), pltpu.VMEM((D_MODEL, D_FF), BF16),
                        pltpu.VMEM((D_MODEL, D_FF), BF16), pltpu.VMEM((D_FF, D_MODEL), BF16),
                        pltpu.VMEM((D_MODEL, D_MODEL), BF16), pltpu.VMEM((PLE_DIM, D_MODEL), BF16)]
        + _stage_scratch(D_FF) + _stage_scratch(D_MODEL),
        compiler_params=_cparams(("arbitrary",)),
        name="post",
    )(x, ya, yb, yc, p, w_out, ffn_gain, wg, wu, wd, ple_gain, ple_wg, ple_wp)


def _pad_up(up, row0):
    return jnp.zeros((128, GLA_Q), F32).at[row0:row0 + GLA_RANK].set(up).astype(BF16)


def kernel(x, p, ffn1_norm, ffn1_w_gate, ffn1_w_up, ffn1_w_down, mix_norm, w_in, w_out, gla_gk_up_f, gla_gk_bias_f, gla_gk_up_b, gla_gk_bias_b, gla_out_norm, diff_q_norm, diff_k_norm, diff_lambda, diff_out_norm, rel_bias, conv_dw_w, conv_dw_b, conv_norm_g, conv_norm_b, ffn2_norm, ffn2_w_gate, ffn2_w_up, ffn2_w_down, ple_norm, ple_w_gate, ple_w_proj):
    B, L, _ = x.shape
    depth = w_in.shape[0]
    n = B * L
    xs = x.reshape(n, D_MODEL)
    rb_flat = rel_bias.reshape(-1)
    band = _bias_tiles(rel_bias)

    def row(v):
        return v.reshape(1, -1)

    ffn1_w = (ffn1_w_gate, ffn1_w_up, ffn1_w_down)
    ffn2_w = (ffn2_w_gate, ffn2_w_up, ffn2_w_down)
    p_rows = p.reshape(depth, n, PLE_DIM)

    for i in range(depth):
        xs = _ffn(xs, row(ffn1_norm[i]), *ffn1_w, i)

        gqk, gg, gv, go, dq, dk, dvt, cu = _inproj(
            xs, row(mix_norm[i]), w_in,
            _pad_up(gla_gk_up_f[i], 0), row(gla_gk_bias_f[i]),
            _pad_up(gla_gk_up_b[i], GLA_RANK), row(gla_gk_bias_b[i]),
            row(jnp.tile(diff_q_norm[i], DIFF_QK // DIFF_DQK)), row(jnp.tile(diff_k_norm[i], DIFF_QK // DIFF_DQK)),
            i)

        y_gla = _gla(gqk, gg, gv, go, row(jnp.tile(gla_out_norm[i], N_GLA_HEADS)), B)
        lam_init = 0.8 - 0.6 * math.exp(-0.3 * i)
        score_bound = LOG2E * (DIFF_DQK ** 0.5 * jnp.max(jnp.abs(diff_q_norm[i])) * jnp.max(jnp.abs(diff_k_norm[i]))
                               + jnp.max(jnp.abs(rel_bias)))
        conv_w = jnp.concatenate([conv_dw_w[i], jnp.zeros((1, CONV_CH), F32)], axis=0)
        y_diff, y_conv = _diff_attn(rb_flat, score_bound.reshape(1), dq, dk, dvt, band, diff_lambda[i],
                                    diff_out_norm[i].reshape(-1, 1), cu, conv_w, row(conv_dw_b[i]),
                                    row(conv_norm_g[i]), row(conv_norm_b[i]), B, lam_init)

        xs = _post(xs, y_gla, y_diff, y_conv, p_rows, w_out, row(ffn2_norm[i]), *ffn2_w,
                   row(ple_norm[i]), ple_w_gate, ple_w_proj, i)
    return xs.reshape(B, L, D_MODEL)
```

```python
import functools
import math

import jax
import jax.numpy as jnp
from jax import lax
from jax.experimental import pallas as pl
from jax.experimental.pallas import tpu as pltpu

F32 = jnp.float32
BF16 = jnp.bfloat16

D_MODEL = 1024
N_GLA_HEADS = 4
GLA_DK = 32
GLA_DV = 64
GLA_Q = N_GLA_HEADS * GLA_DK
GLA_V = N_GLA_HEADS * GLA_DV
GLA_RANK = 16
GATE_TEMP = 16.0
GLA_CHUNK = 64
GLA_BLOCK = 4
DIFF_DQK = 64
N_DIFF_HEADS = 4
DIFF_DV = 128
DIFF_QK = 512
DIFF_V = 512
NUM_BUCKETS = 32
MAX_DISTANCE = 128
CONV_CH = 256
CONV_K = 31
D_FF = 2816
PLE_DIM = 256
NORM_EPS = 1e-6
LOG2E = math.log2(math.e)
EXP2_SAFE_RANGE = 100.0
D_IN = 2848

TOKEN_TILE = 512
INPROJ_TILE = 1024
INPROJ_SUB = 256
FF_CHUNKS = (1536, 1280)
ATTN_TILE = 256
ATTN_LOOKAHEAD = 2
ATTN_UNROLL = 4
CONV_PAD = 16
VMEM_LIMIT = 56 * 1024 * 1024
WEIGHT_CHUNK_ROWS_WIDE = 128
WEIGHT_CHUNK_ROWS_TALL = 256
WEIGHT_STAGE_SLOTS = 3


def _cparams(sem):
    return pltpu.CompilerParams(dimension_semantics=sem, vmem_limit_bytes=VMEM_LIMIT)


def _dot(a, b):
    return jnp.dot(a, b, preferred_element_type=F32)


def _dot_nt(a, b):
    return lax.dot_general(a, b, (((1,), (1,)), ((), ())), preferred_element_type=F32)


def _rms_rows(x, gain):
    return x * lax.rsqrt(jnp.mean(x * x, axis=-1, keepdims=True) + NORM_EPS) * gain


def _sigmoid(x):
    return 1.0 / (1.0 + jnp.exp(-x))


def _silu(x):
    return x * _sigmoid(x)


def _const_spec(shape):
    return pl.BlockSpec(shape, lambda *_: (0,) * len(shape), pipeline_mode=pl.Buffered(1))


def _layer_spec(shape, layer):
    return pl.BlockSpec((None,) + shape, lambda *_: (layer,) + (0,) * len(shape), pipeline_mode=pl.Buffered(1))


def _hbm_spec():
    return pl.BlockSpec(memory_space=pl.ANY)


def _stage_scratch(cols):
    rows = WEIGHT_CHUNK_ROWS_WIDE if cols > D_MODEL else WEIGHT_CHUNK_ROWS_TALL
    return [pltpu.VMEM((WEIGHT_STAGE_SLOTS, rows, cols), F32), pltpu.SemaphoreType.DMA((WEIGHT_STAGE_SLOTS,))]


def _fetch_as_bf16(src_hbm, layer, dst_ref, stage_ref, sem_ref):
    slots = stage_ref.shape[0]
    chunk = min(stage_ref.shape[1], dst_ref.shape[0])
    assert dst_ref.shape[0] % chunk == 0
    n = dst_ref.shape[0] // chunk

    def dma(c):
        return pltpu.make_async_copy(src_hbm.at[layer, pl.ds(c * chunk, chunk), :],
                                     stage_ref.at[c % slots, pl.ds(0, chunk), :], sem_ref.at[c % slots])

    for c in range(min(slots - 1, n)):
        dma(c).start()
    for c in range(n):
        if c + slots - 1 < n:
            dma(c + slots - 1).start()
        dma(c).wait()
        dst_ref[c * chunk:(c + 1) * chunk, :] = stage_ref[c % slots, :chunk, :].astype(BF16)


def _swiglu_half_step(x, g_ref, wg_ref, wu_ref, wd_ref):
    h = _rms_rows(x, g_ref[...]).astype(BF16)
    acc = jnp.zeros(x.shape, F32)
    c0 = 0
    for width in FF_CHUNKS:
        gate = _dot(h, wg_ref[:, c0:c0 + width])
        up = _dot(h, wu_ref[:, c0:c0 + width])
        a = (_silu(gate) * up).astype(BF16)
        acc = acc + _dot(a, wd_ref[c0:c0 + width, :])
        c0 += width
    return x + 0.5 * acc


def _ffn_kernel(x_ref, g_ref, wg_hbm, wu_hbm, wd_hbm, o_ref,
                wg_ref, wu_ref, wd_ref, wide_ref, wide_sem, tall_ref, tall_sem, *, layer):
    @pl.when(pl.program_id(0) == 0)
    def _():
        _fetch_as_bf16(wg_hbm, layer, wg_ref, wide_ref, wide_sem)
        _fetch_as_bf16(wu_hbm, layer, wu_ref, wide_ref, wide_sem)
        _fetch_as_bf16(wd_hbm, layer, wd_ref, tall_ref, tall_sem)

    o_ref[...] = _swiglu_half_step(x_ref[...], g_ref, wg_ref, wu_ref, wd_ref)


def _ffn(x, gain, wg, wu, wd, layer):
    n = x.shape[0]
    row = pl.BlockSpec((TOKEN_TILE, D_MODEL), lambda i: (i, 0))
    return pl.pallas_call(
        functools.partial(_ffn_kernel, layer=layer),
        grid=(n // TOKEN_TILE,),
        in_specs=[row, _const_spec((1, D_MODEL)), _hbm_spec(), _hbm_spec(), _hbm_spec()],
        out_specs=row,
        out_shape=jax.ShapeDtypeStruct((n, D_MODEL), F32),
        scratch_shapes=[pltpu.VMEM((D_MODEL, D_FF), BF16), pltpu.VMEM((D_MODEL, D_FF), BF16),
                        pltpu.VMEM((D_FF, D_MODEL), BF16)] + _stage_scratch(D_FF) + _stage_scratch(D_MODEL),
        compiler_params=_cparams(("arbitrary",)),
        name="ffn",
    )(x, gain, wg, wu, wd)


def _block_diag_ones(n, blk, dtype):
    r = lax.broadcasted_iota(jnp.int32, (n, n), 0) // blk
    c = lax.broadcasted_iota(jnp.int32, (n, n), 1) // blk
    return jnp.where(r == c, 1.0, 0.0).astype(dtype)


def _group_mean_sq(x, blk):
    n = x.shape[-1]
    return _dot((x * x).astype(BF16), _block_diag_ones(n, blk, BF16)) * (1.0 / blk)


def _log_sigmoid(z):
    return jnp.minimum(z, 0.0) - jnp.log(1.0 + jnp.exp(-jnp.abs(z)))


def _inproj_kernel(x_ref, g_ref, w_ref, upf_ref, bf_ref, upb_ref, bb_ref, qn_ref, kn_ref,
                   gqk_ref, gg_ref, gv_ref, go_ref, dq_ref, dk_ref, dvt_ref, cu_ref):
    tm = x_ref.shape[0]
    nsub = tm // INPROJ_SUB

    def project(s):
        rows = slice(s * INPROJ_SUB, (s + 1) * INPROJ_SUB)
        h = _rms_rows(x_ref[rows, :], g_ref[...]).astype(BF16)
        return _dot(h, w_ref[...])

    def prepare(s, proj):
        rows = slice(s * INPROJ_SUB, (s + 1) * INPROJ_SUB)
        gqk_ref[rows, :GLA_Q] = proj[:, :GLA_Q] * (GLA_DK ** -0.5)
        gqk_ref[rows, GLA_Q:] = proj[:, GLA_Q:2 * GLA_Q]
        gv_ref[rows, :] = proj[:, 256:512].astype(BF16)
        go_ref[rows, :] = _silu(proj[:, 512:768]).astype(BF16)
        low = proj[:, 768:896].astype(BF16)
        zf = _dot(low, upf_ref[...]) + bf_ref[...]
        zb = _dot(low, upb_ref[...]) + bb_ref[...]
        gg_ref[rows, :GLA_Q] = _log_sigmoid(zf) / GATE_TEMP
        gg_ref[rows, GLA_Q:] = _log_sigmoid(zb) / GATE_TEMP
        rest = proj[:, 768 + 2 * GLA_RANK:]
        dq = rest[:, :512]
        dq_ref[rows, :] = (dq * lax.rsqrt(_group_mean_sq(dq, DIFF_DQK) + NORM_EPS) * qn_ref[...]
                           * (DIFF_DQK ** -0.5 * LOG2E)).astype(BF16)
        dk = rest[:, 512:1024]
        dk_ref[rows, :] = (dk * lax.rsqrt(_group_mean_sq(dk, DIFF_DQK) + NORM_EPS) * kn_ref[...]).astype(BF16)
        dvt_ref[:, rows] = rest[:, 1024:1536].T.astype(BF16)
        cu_ref[rows, :] = rest[:, 1536:1792] * _sigmoid(rest[:, 1792:2048])

    nxt = project(0)
    for s in range(nsub):
        cur = nxt
        if s + 1 < nsub:
            nxt = project(s + 1)
        prepare(s, cur)


def _inproj(x, gain, w, upf, bf, upb, bb, qn, kn, layer):
    n = x.shape[0]
    tm = INPROJ_TILE

    def row(width):
        return pl.BlockSpec((tm, width), lambda i: (i, 0))

    outs = [(2 * GLA_Q, F32), (2 * GLA_Q, F32), (GLA_V, BF16), (GLA_V, BF16),
            (DIFF_QK, BF16), (DIFF_QK, BF16), None, (CONV_CH, F32)]
    dvt_spec = pl.BlockSpec((DIFF_V, tm), lambda i: (0, i))
    return pl.pallas_call(
        _inproj_kernel,
        grid=(n // tm,),
        in_specs=[row(D_MODEL), _const_spec((1, D_MODEL)), _layer_spec((D_MODEL, D_IN), layer),
                  _const_spec((128, GLA_Q)), _const_spec((1, GLA_Q)),
                  _const_spec((128, GLA_Q)), _const_spec((1, GLA_Q)),
                  _const_spec((1, DIFF_QK)), _const_spec((1, DIFF_QK))],
        out_specs=[row(o[0]) if o else dvt_spec for o in outs],
        out_shape=[jax.ShapeDtypeStruct((n, o[0]), o[1]) if o else jax.ShapeDtypeStruct((DIFF_V, n), BF16)
                   for o in outs],
        compiler_params=_cparams(("parallel",)),
        name="inproj",
    )(x, gain, w, upf, bf, upb, bb, qn, kn)


def _gla_block_kernel(qk_ref, g_ref, v_ref, go_ref, on_ref, y_ref, of_ref, st_ref):
    C = GLA_CHUNK
    G = GLA_BLOCK
    R = G * C
    L = qk_ref.shape[0]
    nblk = L // R
    H = N_GLA_HEADS

    def iota(shape, d):
        return lax.broadcasted_iota(jnp.int32, shape, d)

    kmask = iota((H * C, GLA_Q), 0) // C == iota((H * C, GLA_Q), 1) // GLA_DK
    vmask = iota((H * C, GLA_V), 0) // C == iota((H * C, GLA_V), 1) // GLA_DV
    smask = iota((GLA_V, GLA_Q), 0) // GLA_DV == iota((GLA_V, GLA_Q), 1) // GLA_DK
    pos_a = iota((C, H * C), 0)
    pos_b = iota((C, H * C), 1) % C
    ri, ci = iota((R, R), 0), iota((R, R), 1)
    same_chunk = ri // C == ci // C
    cums = [jnp.where(jnp.logical_and(same_chunk, ci <= ri), 1.0, 0.0).astype(BF16),
            jnp.where(jnp.logical_and(same_chunk, ci >= ri), 1.0, 0.0).astype(BF16)]
    amasks = [pos_b <= pos_a, pos_b >= pos_a]
    edges = [C - 1, 0]
    orders = [list(range(G)), list(range(G - 1, -1, -1))]

    st_ref[...] = jnp.zeros(st_ref.shape, F32)

    def step(t, _):
        r0 = [pl.multiple_of(t * R, R), pl.multiple_of((nblk - 1 - t) * R, R)]
        rows = [pl.ds(r0[0], R), pl.ds(r0[1], R)]
        dirs = (0, 1)
        b, k, v, qt = {}, {}, {}, {}
        for d in dirs:
            g = g_ref[rows[d], d * GLA_Q:(d + 1) * GLA_Q]
            g_hi = g.astype(BF16)
            g_lo = (g - g_hi.astype(F32)).astype(BF16)
            b[d] = _dot(cums[d], g_hi) + _dot(cums[d], g_lo)
        a, ds, decay = {}, {}, {}
        for d in dirs:
            q = qk_ref[rows[d], :GLA_Q]
            k[d] = qk_ref[rows[d], GLA_Q:]
            v[d] = v_ref[rows[d], :].astype(F32)
            qt[d] = (q * jnp.exp(b[d])).astype(BF16)
            kt = k[d] * jnp.exp(-b[d])
            for c in range(G):
                sl = slice(c * C, (c + 1) * C)
                b_c = b[d][sl]
                b_edge = b_c[edges[d]:edges[d] + 1, :]
                decay[d, c] = jnp.exp(b_edge)
                kend = (k[d][sl] * jnp.exp(b_edge - b_c)).astype(BF16)
                kstack = jnp.where(kmask, jnp.concatenate([kt[sl]] * H, axis=0), 0.0).astype(BF16)
                a[d, c] = jnp.where(amasks[d], _dot_nt(qt[d][sl], kstack), 0.0).astype(BF16)
                ds[d, c] = _dot(v[d][sl].T.astype(BF16), kend)
        for d in dirs:
            s = st_ref[d]
            s_in = {}
            for c in orders[d]:
                s_in[c] = s.astype(BF16)
                s = s * decay[d, c] + jnp.where(smask, ds[d, c], 0.0)
            st_ref[d] = s
            out_ref = y_ref if d else of_ref
            for c in range(G):
                sl = slice(c * C, (c + 1) * C)
                vstack = jnp.where(vmask, jnp.concatenate([v[d][sl]] * H, axis=0), 0.0).astype(BF16)
                o = _dot(a[d, c], vstack) + _dot_nt(qt[d][sl], s_in[c])
                out_ref[pl.ds(pl.multiple_of(r0[d] + c * C, C), C), :] = o
        return 0

    lax.fori_loop(0, nblk, step, 0)

    norm_bd = _block_diag_ones(GLA_V, GLA_DV, BF16)

    def finish(t, _):
        rows = pl.ds(pl.multiple_of(t * R, R), R)
        tot = of_ref[rows, :] + y_ref[rows, :]
        ms = _dot((tot * tot).astype(BF16), norm_bd) * (1.0 / GLA_DV)
        y_ref[rows, :] = tot * lax.rsqrt(ms + NORM_EPS) * on_ref[...] * go_ref[rows, :].astype(F32)
        return 0

    lax.fori_loop(0, nblk, finish, 0)


def _gla(qk, g, v, go, onorm, batch):
    n = qk.shape[0]
    L = n // batch

    def seq(width):
        return pl.BlockSpec((L, width), lambda b: (b, 0))

    return pl.pallas_call(
        _gla_block_kernel,
        grid=(batch,),
        in_specs=[seq(2 * GLA_Q), seq(2 * GLA_Q), seq(GLA_V), seq(GLA_V), _const_spec((1, GLA_V))],
        out_specs=seq(GLA_V),
        out_shape=jax.ShapeDtypeStruct((n, GLA_V), F32),
        scratch_shapes=[pltpu.VMEM((L, GLA_V), F32), pltpu.VMEM((2, GLA_V, GLA_Q), F32)],
        compiler_params=_cparams(("parallel",)),
        name="gla",
    )(qk, g, v, go, onorm)


def _bias_kernel(rb_ref, o_ref):
    T = ATTN_TILE
    h = pl.program_id(0)
    nb = NUM_BUCKETS // 2
    max_exact = nb // 2
    for d in range(5):
        rel = (lax.broadcasted_iota(jnp.int32, (T, T), 0) - lax.broadcasted_iota(jnp.int32, (T, T), 1)
               + (d - 2) * T)
        ret = jnp.where(rel > 0, nb, 0)
        n = jnp.abs(rel)
        large = max_exact + (jnp.log(jnp.maximum(n, 1).astype(F32) / max_exact)
                             / math.log(MAX_DISTANCE / max_exact) * (nb - max_exact)).astype(jnp.int32)
        large = jnp.minimum(large, nb - 1)
        bucket = ret + jnp.where(n < max_exact, n, large)
        reachable = ([nb - 1] if d == 0 else range(nb) if d == 1 else range(NUM_BUCKETS) if d == 2
                     else range(nb, NUM_BUCKETS) if d == 3 else [NUM_BUCKETS - 1])
        for m in range(2):
            tile = jnp.zeros((T, T), F32)
            for bkt in reachable:
                tile = jnp.where(bucket == bkt, rb_ref[(bkt * N_DIFF_HEADS + h) * 2 + m], tile)
            o_ref[0, d, m] = tile * (-LOG2E)


def _bias_tiles(rel_bias):
    T = ATTN_TILE
    return pl.pallas_call(
        _bias_kernel,
        grid=(N_DIFF_HEADS,),
        in_specs=[pl.BlockSpec(memory_space=pltpu.SMEM)],
        out_specs=pl.BlockSpec((1, 5, 2, T, T), lambda h: (h, 0, 0, 0, 0)),
        out_shape=jax.ShapeDtypeStruct((N_DIFF_HEADS, 5, 2, T, T), F32),
        compiler_params=_cparams(("parallel",)),
        name="t5_bias",
    )(rel_bias.reshape(-1))


def _stage_conv_window(u_ref, uw_ref):
    L = u_ref.shape[0]
    P = CONV_PAD
    rows = L // N_DIFF_HEADS
    h = pl.program_id(0)
    start = pl.multiple_of(h * rows, rows)
    uw_ref[P:P + rows, :] = u_ref[pl.ds(start, rows), :]
    above = u_ref[pl.ds(pl.multiple_of(jnp.maximum(start - P, 0), P), P), :]
    uw_ref[:P, :] = jnp.where(h > 0, above, 0.0)
    below = u_ref[pl.ds(pl.multiple_of(jnp.minimum(start + rows, L - P), P), P), :]
    uw_ref[P + rows:, :] = jnp.where(h < N_DIFF_HEADS - 1, below, 0.0)


def _conv_rows(r0, R, uw_ref, w_ref, b_ref, g_ref, beta_ref, y_ref):
    P = CONV_PAD
    off = P - CONV_K // 2
    win = uw_ref[pl.ds(r0, R + 2 * P), :]
    acc = jnp.zeros((R, CONV_CH), F32) + b_ref[...]
    for r in range(8):
        shifted = win if r == 0 else pltpu.roll(win, R + 2 * P - r, 0)
        for k in range(CONV_K):
            if (off + k) % 8 == r:
                a8 = (off + k) // 8
                acc = acc + w_ref[k:k + 1, :] * shifted[8 * a8:8 * a8 + R, :]
    mu = jnp.mean(acc, axis=-1, keepdims=True)
    xc = acc - mu
    y = xc * lax.rsqrt(jnp.mean(xc * xc, axis=-1, keepdims=True) + NORM_EPS) * g_ref[...] + beta_ref[...]
    y_ref[pl.ds(r0, R), :] = _silu(y)


def _attn_kernel(*refs, lam_init):
    bound_ref = refs[1]
    nq = refs[2].shape[0] // ATTN_TILE
    unshifted_ok = bound_ref[0] <= EXP2_SAFE_RANGE
    _stage_conv_window(refs[8], refs[-1])

    def sweep(unshifted, unroll):
        def body(qi, _):
            _attn_query_tile(qi, *refs, lam_init=lam_init, unshifted=unshifted)
            return 0
        lax.fori_loop(0, nq, body, 0, unroll=unroll)

    @pl.when(unshifted_ok)
    def _():
        sweep(True, ATTN_UNROLL)

    @pl.when(jnp.logical_not(unshifted_ok))
    def _():
        sweep(False, 1)


def _attn_query_tile(qi, rb_ref, bound_ref, q_ref, k_ref, vt_ref, band_ref, lam_ref, on_ref,
                     u_ref, cw_ref, cb_ref, cg_ref, cbeta_ref, o_ref, yc_ref,
                     m_ref, l_ref, acc_ref, uw_ref, *, lam_init, unshifted):
    T = ATTN_TILE
    L = k_ref.shape[0]
    nk = L // T
    h = pl.program_id(0)
    q = q_ref[pl.ds(pl.multiple_of(qi * T, T), T), :]
    qt = q.astype(F32).T
    chan = lax.broadcasted_iota(jnp.int32, qt.shape, 0)
    qm = [jnp.where(chan < DIFF_DQK, qt, 0.0).astype(BF16), jnp.where(chan >= DIFF_DQK, qt, 0.0).astype(BF16)]

    def side_const(bucket, m):
        return rb_ref[(bucket * N_DIFF_HEADS + h) * 2 + m] * LOG2E

    def tile_start(j):
        return pl.multiple_of(j * T, T)

    def unshifted_softmax():
        f_left = [jnp.exp2(jnp.full((1, 1), side_const(NUM_BUCKETS // 2 - 1, m), F32)).astype(BF16)
                  for m in range(2)]
        f_right = [jnp.exp2(jnp.full((1, 1), side_const(NUM_BUCKETS - 1, m), F32)).astype(BF16)
                   for m in range(2)]
        band_idx = [jnp.where(qi == 0, 4, 1), 2, jnp.where(qi == nk - 1, 0, 3)]
        l = [jnp.zeros((1, T), F32) for _ in range(2)]
        acc = [jnp.zeros((DIFF_DV, T), F32) for _ in range(2)]

        def scores(t):
            kt = k_ref[pl.ds(tile_start((qi + (2 + t)) % nk), T), :]
            return [_dot(kt, qm[m]) for m in range(2)]

        pending = [scores(t) for t in range(ATTN_LOOKAHEAD)]
        for t in range(nk):
            j = (qi + (2 + t)) % nk
            vt = vt_ref[:, pl.ds(tile_start(j), T)]
            s_cur = pending.pop(0)
            if t + ATTN_LOOKAHEAD < nk:
                pending.append(scores(t + ATTN_LOOKAHEAD))
            for m in range(2):
                if t >= nk - 3:
                    p = jnp.exp2(s_cur[m] - band_ref[0, band_idx[t - (nk - 3)], m])
                    l[m] = l[m] + jnp.sum(p, axis=0, keepdims=True)
                    acc[m] = acc[m] + _dot(vt, p.astype(BF16))
                else:
                    f = jnp.where(j < qi, f_left[m], f_right[m])
                    p = jnp.exp2(s_cur[m])
                    l[m] = l[m] + f.astype(F32) * jnp.sum(p, axis=0, keepdims=True)
                    acc[m] = acc[m] + _dot(vt * f, p.astype(BF16))
        for m in range(2):
            l_ref[m] = l[m]
            acc_ref[m] = acc[m]

    def online_softmax():
        m_ref[...] = jnp.full(m_ref.shape, -jnp.inf, F32)
        l_ref[...] = jnp.zeros(l_ref.shape, F32)
        acc_ref[...] = jnp.zeros(acc_ref.shape, F32)

        def body(j, _):
            r0 = tile_start(j)
            kt = k_ref[pl.ds(r0, T), :]
            vt = vt_ref[:, pl.ds(r0, T)]
            idx = jnp.clip(j - qi, -2, 2) + 2
            ss = [_dot(kt, qm[m]) for m in range(2)]
            for m in range(2):
                s = ss[m] - band_ref[0, idx, m]
                m_old = m_ref[m]
                m_new = jnp.maximum(m_old, jnp.max(s, axis=0, keepdims=True))
                alpha = jnp.exp2(m_old - m_new)
                p = jnp.exp2(s - m_new)
                l_ref[m] = alpha * l_ref[m] + jnp.sum(p, axis=0, keepdims=True)
                acc_ref[m] = alpha * acc_ref[m] + _dot(vt, p.astype(BF16))
                m_ref[m] = m_new
            return 0

        lax.fori_loop(0, nk, body, 0)

    if unshifted:
        unshifted_softmax()
    else:
        online_softmax()

    conv_rows = L // N_DIFF_HEADS // nk
    _conv_rows(pl.multiple_of(qi * conv_rows, conv_rows), conv_rows, uw_ref, cw_ref, cb_ref, cg_ref, cbeta_ref, yc_ref)
    lp = lam_ref[...]
    lam = (jnp.exp(jnp.sum(lp[0:1] * lp[1:2], axis=-1, keepdims=True))
           - jnp.exp(jnp.sum(lp[2:3] * lp[3:4], axis=-1, keepdims=True)) + lam_init)
    o = acc_ref[0] / l_ref[0] - lam * (acc_ref[1] / l_ref[1])
    y = o * lax.rsqrt(jnp.mean(o * o, axis=0, keepdims=True) + NORM_EPS) * on_ref[...] * (1.0 - lam_init)
    o_ref[pl.ds(pl.multiple_of(qi * T, T), T), :] = y.T.astype(o_ref.dtype)


def _diff_attn(rel_bias_flat, score_bound, dq, dk, dvt, band, lam_p, onorm, cu, conv_w, conv_b, conv_g,
               conv_beta, batch, lam_init):
    n = dq.shape[0]
    L = n // batch
    T = ATTN_TILE
    share = L // N_DIFF_HEADS
    seq = pl.BlockSpec((L, 2 * DIFF_DQK), lambda h, b: (b, h))

    def const(shape):
        return pl.BlockSpec(shape, lambda h, b: (0,) * len(shape))

    return pl.pallas_call(
        functools.partial(_attn_kernel, lam_init=lam_init),
        grid=(N_DIFF_HEADS, batch),
        in_specs=[pl.BlockSpec(memory_space=pltpu.SMEM),
                  pl.BlockSpec(memory_space=pltpu.SMEM),
                  seq, seq,
                  pl.BlockSpec((DIFF_DV, L), lambda h, b: (h, b)),
                  pl.BlockSpec((1, 5, 2, T, T), lambda h, b: (h, 0, 0, 0, 0)),
                  const((4, DIFF_DQK)), const((DIFF_DV, 1)),
                  pl.BlockSpec((L, CONV_CH), lambda h, b: (b, 0)),
                  const((CONV_K + 1, CONV_CH)), const((1, CONV_CH)), const((1, CONV_CH)), const((1, CONV_CH))],
        out_specs=[pl.BlockSpec((L, DIFF_DV), lambda h, b: (b, h)),
                   pl.BlockSpec((share, CONV_CH), lambda h, b: (b * N_DIFF_HEADS + h, 0))],
        out_shape=[jax.ShapeDtypeStruct((n, DIFF_V), BF16), jax.ShapeDtypeStruct((n, CONV_CH), F32)],
        scratch_shapes=[pltpu.VMEM((2, 1, T), F32), pltpu.VMEM((2, 1, T), F32),
                        pltpu.VMEM((2, DIFF_DV, T), F32),
                        pltpu.VMEM((share + 2 * CONV_PAD, CONV_CH), F32)],
        compiler_params=_cparams(("parallel", "parallel")),
        name="diff_attn",
    )(rel_bias_flat, score_bound, dq, dk, dvt, band, lam_p, onorm, cu, conv_w, conv_b, conv_g, conv_beta)


def _post_kernel(x_ref, ya_ref, yb_ref, yc_ref, p_ref, wo_hbm, fg_ref, wg_hbm, wu_hbm, wd_hbm,
                 pg_ref, pwg_hbm, pwp_hbm, o_ref,
                 wo_ref, wg_ref, wu_ref, wd_ref, pwg_ref, pwp_ref, wide_ref, wide_sem, tall_ref, tall_sem,
                 *, layer):
    @pl.when(pl.program_id(0) == 0)
    def _():
        _fetch_as_bf16(wo_hbm, layer, wo_ref, tall_ref, tall_sem)
        _fetch_as_bf16(wg_hbm, layer, wg_ref, wide_ref, wide_sem)
        _fetch_as_bf16(wu_hbm, layer, wu_ref, wide_ref, wide_sem)
        _fetch_as_bf16(wd_hbm, layer, wd_ref, tall_ref, tall_sem)
        _fetch_as_bf16(pwg_hbm, layer, pwg_ref, tall_ref, tall_sem)
        _fetch_as_bf16(pwp_hbm, layer, pwp_ref, tall_ref, tall_sem)

    mix = _dot(ya_ref[...].astype(BF16), wo_ref[:GLA_V, :])
    mix = mix + _dot(yb_ref[...].astype(BF16), wo_ref[GLA_V:GLA_V + DIFF_V, :])
    mix = mix + _dot(yc_ref[...].astype(BF16), wo_ref[GLA_V + DIFF_V:, :])
    x = _swiglu_half_step(x_ref[...] + mix, fg_ref, wg_ref, wu_ref, wd_ref)
    h = _rms_rows(x, pg_ref[...]).astype(BF16)
    gate = _sigmoid(_dot(h, pwg_ref[...]))
    o_ref[...] = x + gate * _dot(p_ref[...].astype(BF16), pwp_ref[...])


def _post(x, ya, yb, yc, p, w_out, ffn_gain, wg, wu, wd, ple_gain, ple_wg, ple_wp, layer):
    n = x.shape[0]
    tm = TOKEN_TILE

    def row(width):
        return pl.BlockSpec((tm, width), lambda i: (i, 0))

    return pl.pallas_call(
        functools.partial(_post_kernel, layer=layer),
        grid=(n // tm,),
        in_specs=[row(D_MODEL), row(GLA_V), row(DIFF_V), row(CONV_CH),
                  pl.BlockSpec((None, tm, PLE_DIM), lambda i: (layer, i, 0)),
                  _hbm_spec(), _const_spec((1, D_MODEL)), _hbm_spec(), _hbm_spec(), _hbm_spec(),
                  _const_spec((1, D_MODEL)), _hbm_spec(), _hbm_spec()],
        out_specs=row(D_MODEL),
        out_shape=jax.ShapeDtypeStruct((n, D_MODEL), F32),
        scratch_shapes=[pltpu.VMEM((D_MODEL, D_MODEL), BF16), pltpu.VMEM((D_MODEL, D_FF), BF16),
                        pltpu.VMEM((D_MODEL, D_FF), BF16), pltpu.VMEM((D_FF, D_MODEL), BF16),
                        pltpu.VMEM((D_MODEL, D_MODEL), BF16), pltpu.VMEM((PLE_DIM, D_MODEL), BF16)]
        + _stage_scratch(D_FF) + _stage_scratch(D_MODEL),
        compiler_params=_cparams(("arbitrary",)),
        name="post",
    )(x, ya, yb, yc, p, w_out, ffn_gain, wg, wu, wd, ple_gain, ple_wg, ple_wp)


def _pad_up(up, row0):
    return jnp.zeros((128, GLA_Q), F32).at[row0:row0 + GLA_RANK].set(up).astype(BF16)


def kernel(x, p, ffn1_norm, ffn1_w_gate, ffn1_w_up, ffn1_w_down, mix_norm, w_in, w_out, gla_gk_up_f, gla_gk_bias_f, gla_gk_up_b, gla_gk_bias_b, gla_out_norm, diff_q_norm, diff_k_norm, diff_lambda, diff_out_norm, rel_bias, conv_dw_w, conv_dw_b, conv_norm_g, conv_norm_b, ffn2_norm, ffn2_w_gate, ffn2_w_up, ffn2_w_down, ple_norm, ple_w_gate, ple_w_proj):
    B, L, _ = x.shape
    depth = w_in.shape[0]
    n = B * L
    xs = x.reshape(n, D_MODEL)
    rb_flat = rel_bias.reshape(-1)
    band = _bias_tiles(rel_bias)

    def row(v):
        return v.reshape(1, -1)

    ffn1_w = (ffn1_w_gate, ffn1_w_up, ffn1_w_down)
    ffn2_w = (ffn2_w_gate, ffn2_w_up, ffn2_w_down)
    w_in_b = w_in.astype(BF16)
    p_rows = p.reshape(depth, n, PLE_DIM)

    for i in range(depth):
        xs = _ffn(xs, row(ffn1_norm[i]), *ffn1_w, i)

        gqk, gg, gv, go, dq, dk, dvt, cu = _inproj(
            xs, row(mix_norm[i]), w_in_b,
            _pad_up(gla_gk_up_f[i], 0), row(gla_gk_bias_f[i]),
            _pad_up(gla_gk_up_b[i], GLA_RANK), row(gla_gk_bias_b[i]),
            row(jnp.tile(diff_q_norm[i], DIFF_QK // DIFF_DQK)), row(jnp.tile(diff_k_norm[i], DIFF_QK // DIFF_DQK)),
            i)

        y_gla = _gla(gqk, gg, gv, go, row(jnp.tile(gla_out_norm[i], N_GLA_HEADS)), B)
        lam_init = 0.8 - 0.6 * math.exp(-0.3 * i)
        score_bound = LOG2E * (DIFF_DQK ** 0.5 * jnp.max(jnp.abs(diff_q_norm[i])) * jnp.max(jnp.abs(diff_k_norm[i]))
                               + jnp.max(jnp.abs(rel_bias)))
        conv_w = jnp.concatenate([conv_dw_w[i], jnp.zeros((1, CONV_CH), F32)], axis=0)
        y_diff, y_conv = _diff_attn(rb_flat, score_bound.reshape(1), dq, dk, dvt, band, diff_lambda[i],
                                    diff_out_norm[i].reshape(-1, 1), cu, conv_w, row(conv_dw_b[i]),
                                    row(conv_norm_g[i]), row(conv_norm_b[i]), B, lam_init)

        xs = _post(xs, y_gla, y_diff, y_conv, p_rows, w_out, row(ffn2_norm[i]), *ffn2_w,
                   row(ple_norm[i]), ple_w_gate, ple_w_proj, i)
    return xs.reshape(B, L, D_MODEL)
```

```python
import functools
import math

import jax
import jax.numpy as jnp
from jax import lax
from jax.experimental import pallas as pl
from jax.experimental.pallas import tpu as pltpu

F32 = jnp.float32
BF16 = jnp.bfloat16

D_MODEL = 1024
N_GLA_HEADS = 4
GLA_DK = 32
GLA_DV = 64
GLA_Q = N_GLA_HEADS * GLA_DK
GLA_V = N_GLA_HEADS * GLA_DV
GLA_RANK = 16
GATE_TEMP = 16.0
GLA_CHUNK = 64
GLA_BLOCK = 8
GLA_CUM_CHUNKS = 4
DIFF_DQK = 64
N_DIFF_HEADS = 4
DIFF_DV = 128
DIFF_QK = 512
DIFF_V = 512
NUM_BUCKETS = 32
MAX_DISTANCE = 128
CONV_CH = 256
CONV_K = 31
D_FF = 2816
PLE_DIM = 256
NORM_EPS = 1e-6
LOG2E = math.log2(math.e)
EXP2_SAFE_RANGE = 100.0
D_IN = 2848

TOKEN_TILE = 512
INPROJ_TILE = 1024
INPROJ_SUB = 256
FF_CHUNKS = (1536, 1280)
ATTN_TILE = 256
ATTN_LOOKAHEAD = 3
ATTN_UNROLL = 4
CONV_PAD = 16
VMEM_LIMIT = 56 * 1024 * 1024
WEIGHT_CHUNK_ROWS_WIDE = 128
WEIGHT_CHUNK_ROWS_TALL = 256
WEIGHT_STAGE_SLOTS = 3


def _cparams(sem):
    return pltpu.CompilerParams(dimension_semantics=sem, vmem_limit_bytes=VMEM_LIMIT)


def _dot(a, b):
    return jnp.dot(a, b, preferred_element_type=F32)


def _dot_nt(a, b):
    return lax.dot_general(a, b, (((1,), (1,)), ((), ())), preferred_element_type=F32)


def _rms_rows(x, gain):
    return x * lax.rsqrt(jnp.mean(x * x, axis=-1, keepdims=True) + NORM_EPS) * gain


def _sigmoid(x):
    return 1.0 / (1.0 + jnp.exp(-x))


def _silu(x):
    return x * _sigmoid(x)


def _const_spec(shape):
    return pl.BlockSpec(shape, lambda *_: (0,) * len(shape), pipeline_mode=pl.Buffered(1))


def _layer_spec(shape, layer):
    return pl.BlockSpec((None,) + shape, lambda *_: (layer,) + (0,) * len(shape), pipeline_mode=pl.Buffered(1))


def _hbm_spec():
    return pl.BlockSpec(memory_space=pl.ANY)


def _stage_scratch(cols):
    rows = WEIGHT_CHUNK_ROWS_WIDE if cols > D_MODEL else WEIGHT_CHUNK_ROWS_TALL
    return [pltpu.VMEM((WEIGHT_STAGE_SLOTS, rows, cols), F32), pltpu.SemaphoreType.DMA((WEIGHT_STAGE_SLOTS,))]


def _fetch_as_bf16(src_hbm, layer, dst_ref, stage_ref, sem_ref):
    slots = stage_ref.shape[0]
    chunk = min(stage_ref.shape[1], dst_ref.shape[0])
    assert dst_ref.shape[0] % chunk == 0
    n = dst_ref.shape[0] // chunk

    def dma(c):
        return pltpu.make_async_copy(src_hbm.at[layer, pl.ds(c * chunk, chunk), :],
                                     stage_ref.at[c % slots, pl.ds(0, chunk), :], sem_ref.at[c % slots])

    for c in range(min(slots - 1, n)):
        dma(c).start()
    for c in range(n):
        if c + slots - 1 < n:
            dma(c + slots - 1).start()
        dma(c).wait()
        dst_ref[c * chunk:(c + 1) * chunk, :] = stage_ref[c % slots, :chunk, :].astype(BF16)


def _swiglu_half_step(x, g_ref, wg_ref, wu_ref, wd_ref):
    h = _rms_rows(x, g_ref[...]).astype(BF16)
    acc = jnp.zeros(x.shape, F32)
    c0 = 0
    for width in FF_CHUNKS:
        gate = _dot(h, wg_ref[:, c0:c0 + width])
        up = _dot(h, wu_ref[:, c0:c0 + width])
        a = (_silu(gate) * up).astype(BF16)
        acc = acc + _dot(a, wd_ref[c0:c0 + width, :])
        c0 += width
    return x + 0.5 * acc


def _ffn_kernel(x_ref, g_ref, wg_hbm, wu_hbm, wd_hbm, o_ref,
                wg_ref, wu_ref, wd_ref, wide_ref, wide_sem, tall_ref, tall_sem, *, layer):
    @pl.when(pl.program_id(0) == 0)
    def _():
        _fetch_as_bf16(wg_hbm, layer, wg_ref, wide_ref, wide_sem)
        _fetch_as_bf16(wu_hbm, layer, wu_ref, wide_ref, wide_sem)
        _fetch_as_bf16(wd_hbm, layer, wd_ref, tall_ref, tall_sem)

    o_ref[...] = _swiglu_half_step(x_ref[...], g_ref, wg_ref, wu_ref, wd_ref)


def _ffn(x, gain, wg, wu, wd, layer):
    n = x.shape[0]
    row = pl.BlockSpec((TOKEN_TILE, D_MODEL), lambda i: (i, 0))
    return pl.pallas_call(
        functools.partial(_ffn_kernel, layer=layer),
        grid=(n // TOKEN_TILE,),
        in_specs=[row, _const_spec((1, D_MODEL)), _hbm_spec(), _hbm_spec(), _hbm_spec()],
        out_specs=row,
        out_shape=jax.ShapeDtypeStruct((n, D_MODEL), F32),
        scratch_shapes=[pltpu.VMEM((D_MODEL, D_FF), BF16), pltpu.VMEM((D_MODEL, D_FF), BF16),
                        pltpu.VMEM((D_FF, D_MODEL), BF16)] + _stage_scratch(D_FF) + _stage_scratch(D_MODEL),
        compiler_params=_cparams(("arbitrary",)),
        name="ffn",
    )(x, gain, wg, wu, wd)


def _block_diag_ones(n, blk, dtype):
    r = lax.broadcasted_iota(jnp.int32, (n, n), 0) // blk
    c = lax.broadcasted_iota(jnp.int32, (n, n), 1) // blk
    return jnp.where(r == c, 1.0, 0.0).astype(dtype)


def _group_mean_sq(x, blk):
    n = x.shape[-1]
    return _dot((x * x).astype(BF16), _block_diag_ones(n, blk, BF16)) * (1.0 / blk)


def _log_sigmoid(z):
    return jnp.minimum(z, 0.0) - jnp.log(1.0 + jnp.exp(-jnp.abs(z)))


def _inproj_kernel(x_ref, g_ref, w_ref, upf_ref, bf_ref, upb_ref, bb_ref, qn_ref, kn_ref,
                   gqk_ref, gg_ref, gv_ref, go_ref, dq_ref, dk_ref, dvt_ref, cu_ref):
    tm = x_ref.shape[0]
    nsub = tm // INPROJ_SUB

    def project(s):
        rows = slice(s * INPROJ_SUB, (s + 1) * INPROJ_SUB)
        h = _rms_rows(x_ref[rows, :], g_ref[...]).astype(BF16)
        return _dot(h, w_ref[...])

    def prepare(s, proj):
        rows = slice(s * INPROJ_SUB, (s + 1) * INPROJ_SUB)
        gqk_ref[rows, :GLA_Q] = proj[:, :GLA_Q] * (GLA_DK ** -0.5)
        gqk_ref[rows, GLA_Q:] = proj[:, GLA_Q:2 * GLA_Q]
        gv_ref[rows, :] = proj[:, 256:512].astype(BF16)
        go_ref[rows, :] = _silu(proj[:, 512:768]).astype(BF16)
        low = proj[:, 768:896].astype(BF16)
        zf = _dot(low, upf_ref[...]) + bf_ref[...]
        zb = _dot(low, upb_ref[...]) + bb_ref[...]
        gg_ref[rows, :GLA_Q] = _log_sigmoid(zf) / GATE_TEMP
        gg_ref[rows, GLA_Q:] = _log_sigmoid(zb) / GATE_TEMP
        rest = proj[:, 768 + 2 * GLA_RANK:]
        dq = rest[:, :512]
        dq_ref[rows, :] = (dq * lax.rsqrt(_group_mean_sq(dq, DIFF_DQK) + NORM_EPS) * qn_ref[...]
                           * (DIFF_DQK ** -0.5 * LOG2E)).astype(BF16)
        dk = rest[:, 512:1024]
        dk_ref[rows, :] = (dk * lax.rsqrt(_group_mean_sq(dk, DIFF_DQK) + NORM_EPS) * kn_ref[...]).astype(BF16)
        dvt_ref[:, rows] = rest[:, 1024:1536].T.astype(BF16)
        cu_ref[rows, :] = rest[:, 1536:1792] * _sigmoid(rest[:, 1792:2048])

    nxt = project(0)
    for s in range(nsub):
        cur = nxt
        if s + 1 < nsub:
            nxt = project(s + 1)
        prepare(s, cur)


def _inproj(x, gain, w, upf, bf, upb, bb, qn, kn, layer):
    n = x.shape[0]
    tm = INPROJ_TILE

    def row(width):
        return pl.BlockSpec((tm, width), lambda i: (i, 0))

    outs = [(2 * GLA_Q, F32), (2 * GLA_Q, F32), (GLA_V, BF16), (GLA_V, BF16),
            (DIFF_QK, BF16), (DIFF_QK, BF16), None, (CONV_CH, F32)]
    dvt_spec = pl.BlockSpec((DIFF_V, tm), lambda i: (0, i))
    return pl.pallas_call(
        _inproj_kernel,
        grid=(n // tm,),
        in_specs=[row(D_MODEL), _const_spec((1, D_MODEL)), _layer_spec((D_MODEL, D_IN), layer),
                  _const_spec((128, GLA_Q)), _const_spec((1, GLA_Q)),
                  _const_spec((128, GLA_Q)), _const_spec((1, GLA_Q)),
                  _const_spec((1, DIFF_QK)), _const_spec((1, DIFF_QK))],
        out_specs=[row(o[0]) if o else dvt_spec for o in outs],
        out_shape=[jax.ShapeDtypeStruct((n, o[0]), o[1]) if o else jax.ShapeDtypeStruct((DIFF_V, n), BF16)
                   for o in outs],
        compiler_params=_cparams(("parallel",)),
        name="inproj",
    )(x, gain, w, upf, bf, upb, bb, qn, kn)


def _gla_block_kernel(qk_ref, g_ref, v_ref, go_ref, on_ref, y_ref, of_ref, st_ref):
    C = GLA_CHUNK
    G = GLA_BLOCK
    R = G * C
    L = qk_ref.shape[0]
    nblk = L // R
    H = N_GLA_HEADS

    def iota(shape, d):
        return lax.broadcasted_iota(jnp.int32, shape, d)

    kmask = iota((H * C, GLA_Q), 0) // C == iota((H * C, GLA_Q), 1) // GLA_DK
    vmask = iota((H * C, GLA_V), 0) // C == iota((H * C, GLA_V), 1) // GLA_DV
    smask = iota((GLA_V, GLA_Q), 0) // GLA_DV == iota((GLA_V, GLA_Q), 1) // GLA_DK
    pos_a = iota((C, H * C), 0)
    pos_b = iota((C, H * C), 1) % C
    RC = GLA_CUM_CHUNKS * C
    ri, ci = iota((RC, RC), 0), iota((RC, RC), 1)
    same_chunk = ri // C == ci // C
    cums = [jnp.where(jnp.logical_and(same_chunk, ci <= ri), 1.0, 0.0).astype(BF16),
            jnp.where(jnp.logical_and(same_chunk, ci >= ri), 1.0, 0.0).astype(BF16)]
    amasks = [pos_b <= pos_a, pos_b >= pos_a]
    edges = [C - 1, 0]
    orders = [list(range(G)), list(range(G - 1, -1, -1))]

    st_ref[...] = jnp.zeros(st_ref.shape, F32)

    def step(t, _):
        r0 = [pl.multiple_of(t * R, R), pl.multiple_of((nblk - 1 - t) * R, R)]
        rows = [pl.ds(r0[0], R), pl.ds(r0[1], R)]
        dirs = (0, 1)
        b, k, v, qt = {}, {}, {}, {}
        for d in dirs:
            g = g_ref[rows[d], d * GLA_Q:(d + 1) * GLA_Q]
            g_hi = g.astype(BF16)
            g_lo = (g - g_hi.astype(F32)).astype(BF16)
            b[d] = jnp.concatenate([_dot(cums[d], g_hi[i:i + RC]) + _dot(cums[d], g_lo[i:i + RC])
                                    for i in range(0, R, RC)], axis=0)
        a, ds, decay = {}, {}, {}
        for d in dirs:
            q = qk_ref[rows[d], :GLA_Q]
            k[d] = qk_ref[rows[d], GLA_Q:]
            v[d] = v_ref[rows[d], :].astype(F32)
            qt[d] = (q * jnp.exp(b[d])).astype(BF16)
            kt = k[d] * jnp.exp(-b[d])
            for c in range(G):
                sl = slice(c * C, (c + 1) * C)
                b_c = b[d][sl]
                b_edge = b_c[edges[d]:edges[d] + 1, :]
                decay[d, c] = jnp.exp(b_edge)
                kend = (k[d][sl] * jnp.exp(b_edge - b_c)).astype(BF16)
                kstack = jnp.where(kmask, jnp.concatenate([kt[sl]] * H, axis=0), 0.0).astype(BF16)
                a[d, c] = jnp.where(amasks[d], _dot_nt(qt[d][sl], kstack), 0.0).astype(BF16)
                ds[d, c] = _dot(v[d][sl].T.astype(BF16), kend)
        for d in dirs:
            s = st_ref[d]
            s_in = {}
            for c in orders[d]:
                s_in[c] = s.astype(BF16)
                s = s * decay[d, c] + jnp.where(smask, ds[d, c], 0.0)
            st_ref[d] = s
            out_ref = y_ref if d else of_ref
            for c in range(G):
                sl = slice(c * C, (c + 1) * C)
                vstack = jnp.where(vmask, jnp.concatenate([v[d][sl]] * H, axis=0), 0.0).astype(BF16)
                o = _dot(a[d, c], vstack) + _dot_nt(qt[d][sl], s_in[c])
                out_ref[pl.ds(pl.multiple_of(r0[d] + c * C, C), C), :] = o
        return 0

    lax.fori_loop(0, nblk, step, 0)

    norm_bd = _block_diag_ones(GLA_V, GLA_DV, BF16)

    def finish(t, _):
        rows = pl.ds(pl.multiple_of(t * R, R), R)
        tot = of_ref[rows, :] + y_ref[rows, :]
        ms = _dot((tot * tot).astype(BF16), norm_bd) * (1.0 / GLA_DV)
        y_ref[rows, :] = tot * lax.rsqrt(ms + NORM_EPS) * on_ref[...] * go_ref[rows, :].astype(F32)
        return 0

    lax.fori_loop(0, nblk, finish, 0)


def _gla(qk, g, v, go, onorm, batch):
    n = qk.shape[0]
    L = n // batch

    def seq(width):
        return pl.BlockSpec((L, width), lambda b: (b, 0))

    return pl.pallas_call(
        _gla_block_kernel,
        grid=(batch,),
        in_specs=[seq(2 * GLA_Q), seq(2 * GLA_Q), seq(GLA_V), seq(GLA_V), _const_spec((1, GLA_V))],
        out_specs=seq(GLA_V),
        out_shape=jax.ShapeDtypeStruct((n, GLA_V), F32),
        scratch_shapes=[pltpu.VMEM((L, GLA_V), F32), pltpu.VMEM((2, GLA_V, GLA_Q), F32)],
        compiler_params=_cparams(("parallel",)),
        name="gla",
    )(qk, g, v, go, onorm)


def _bias_kernel(rb_ref, o_ref):
    T = ATTN_TILE
    h = pl.program_id(0)
    nb = NUM_BUCKETS // 2
    max_exact = nb // 2
    for d in range(5):
        rel = (lax.broadcasted_iota(jnp.int32, (T, T), 0) - lax.broadcasted_iota(jnp.int32, (T, T), 1)
               + (d - 2) * T)
        ret = jnp.where(rel > 0, nb, 0)
        n = jnp.abs(rel)
        large = max_exact + (jnp.log(jnp.maximum(n, 1).astype(F32) / max_exact)
                             / math.log(MAX_DISTANCE / max_exact) * (nb - max_exact)).astype(jnp.int32)
        large = jnp.minimum(large, nb - 1)
        bucket = ret + jnp.where(n < max_exact, n, large)
        reachable = ([nb - 1] if d == 0 else range(nb) if d == 1 else range(NUM_BUCKETS) if d == 2
                     else range(nb, NUM_BUCKETS) if d == 3 else [NUM_BUCKETS - 1])
        for m in range(2):
            tile = jnp.zeros((T, T), F32)
            for bkt in reachable:
                tile = jnp.where(bucket == bkt, rb_ref[(bkt * N_DIFF_HEADS + h) * 2 + m], tile)
            o_ref[0, d, m] = tile * (-LOG2E)


def _bias_tiles(rel_bias):
    T = ATTN_TILE
    return pl.pallas_call(
        _bias_kernel,
        grid=(N_DIFF_HEADS,),
        in_specs=[pl.BlockSpec(memory_space=pltpu.SMEM)],
        out_specs=pl.BlockSpec((1, 5, 2, T, T), lambda h: (h, 0, 0, 0, 0)),
        out_shape=jax.ShapeDtypeStruct((N_DIFF_HEADS, 5, 2, T, T), F32),
        compiler_params=_cparams(("parallel",)),
        name="t5_bias",
    )(rel_bias.reshape(-1))


def _stage_conv_window(u_ref, uw_ref):
    L = u_ref.shape[0]
    P = CONV_PAD
    rows = L // N_DIFF_HEADS
    h = pl.program_id(0)
    start = pl.multiple_of(h * rows, rows)
    uw_ref[P:P + rows, :] = u_ref[pl.ds(start, rows), :]
    above = u_ref[pl.ds(pl.multiple_of(jnp.maximum(start - P, 0), P), P), :]
    uw_ref[:P, :] = jnp.where(h > 0, above, 0.0)
    below = u_ref[pl.ds(pl.multiple_of(jnp.minimum(start + rows, L - P), P), P), :]
    uw_ref[P + rows:, :] = jnp.where(h < N_DIFF_HEADS - 1, below, 0.0)


def _conv_rows(r0, R, uw_ref, w_ref, b_ref, g_ref, beta_ref, y_ref):
    P = CONV_PAD
    off = P - CONV_K // 2
    win = uw_ref[pl.ds(r0, R + 2 * P), :]
    acc = jnp.zeros((R, CONV_CH), F32) + b_ref[...]
    for r in range(8):
        shifted = win if r == 0 else pltpu.roll(win, R + 2 * P - r, 0)
        for k in range(CONV_K):
            if (off + k) % 8 == r:
                a8 = (off + k) // 8
                acc = acc + w_ref[k:k + 1, :] * shifted[8 * a8:8 * a8 + R, :]
    mu = jnp.mean(acc, axis=-1, keepdims=True)
    xc = acc - mu
    y = xc * lax.rsqrt(jnp.mean(xc * xc, axis=-1, keepdims=True) + NORM_EPS) * g_ref[...] + beta_ref[...]
    y_ref[pl.ds(r0, R), :] = _silu(y)


def _attn_kernel(*refs, lam_init):
    bound_ref = refs[1]
    nq = refs[2].shape[0] // ATTN_TILE
    unshifted_ok = bound_ref[0] <= EXP2_SAFE_RANGE
    _stage_conv_window(refs[8], refs[-1])

    def sweep(unshifted, unroll):
        def body(qi, _):
            _attn_query_tile(qi, *refs, lam_init=lam_init, unshifted=unshifted)
            return 0
        lax.fori_loop(0, nq, body, 0, unroll=unroll)

    @pl.when(unshifted_ok)
    def _():
        sweep(True, ATTN_UNROLL)

    @pl.when(jnp.logical_not(unshifted_ok))
    def _():
        sweep(False, 1)


def _attn_query_tile(qi, rb_ref, bound_ref, q_ref, k_ref, vt_ref, band_ref, lam_ref, on_ref,
                     u_ref, cw_ref, cb_ref, cg_ref, cbeta_ref, o_ref, yc_ref,
                     m_ref, l_ref, acc_ref, uw_ref, *, lam_init, unshifted):
    T = ATTN_TILE
    L = k_ref.shape[0]
    nk = L // T
    h = pl.program_id(0)
    q = q_ref[pl.ds(pl.multiple_of(qi * T, T), T), :]
    qt = q.astype(F32).T
    chan = lax.broadcasted_iota(jnp.int32, qt.shape, 0)
    qm = [jnp.where(chan < DIFF_DQK, qt, 0.0).astype(BF16), jnp.where(chan >= DIFF_DQK, qt, 0.0).astype(BF16)]

    def side_const(bucket, m):
        return rb_ref[(bucket * N_DIFF_HEADS + h) * 2 + m] * LOG2E

    def tile_start(j):
        return pl.multiple_of(j * T, T)

    def unshifted_softmax():
        f_left = [jnp.exp2(jnp.full((1, 1), side_const(NUM_BUCKETS // 2 - 1, m), F32)).astype(BF16)
                  for m in range(2)]
        f_right = [jnp.exp2(jnp.full((1, 1), side_const(NUM_BUCKETS - 1, m), F32)).astype(BF16)
                   for m in range(2)]
        band_idx = [jnp.where(qi == 0, 4, 1), 2, jnp.where(qi == nk - 1, 0, 3)]
        l = [jnp.zeros((1, T), F32) for _ in range(2)]
        acc = [jnp.zeros((DIFF_DV, T), F32) for _ in range(2)]

        def scores(t):
            kt = k_ref[pl.ds(tile_start((qi + (2 + t)) % nk), T), :]
            return [_dot(kt, qm[m]) for m in range(2)]

        pending = [scores(t) for t in range(ATTN_LOOKAHEAD)]
        for t in range(nk):
            j = (qi + (2 + t)) % nk
            vt = vt_ref[:, pl.ds(tile_start(j), T)]
            s_cur = pending.pop(0)
            if t + ATTN_LOOKAHEAD < nk:
                pending.append(scores(t + ATTN_LOOKAHEAD))
            for m in range(2):
                if t >= nk - 3:
                    p = jnp.exp2(s_cur[m] - band_ref[0, band_idx[t - (nk - 3)], m])
                    l[m] = l[m] + jnp.sum(p, axis=0, keepdims=True)
                    acc[m] = acc[m] + _dot(vt, p.astype(BF16))
                else:
                    f = jnp.where(j < qi, f_left[m], f_right[m])
                    p = jnp.exp2(s_cur[m])
                    l[m] = l[m] + f.astype(F32) * jnp.sum(p, axis=0, keepdims=True)
                    acc[m] = acc[m] + _dot(vt * f, p.astype(BF16))
        for m in range(2):
            l_ref[m] = l[m]
            acc_ref[m] = acc[m]

    def online_softmax():
        m_ref[...] = jnp.full(m_ref.shape, -jnp.inf, F32)
        l_ref[...] = jnp.zeros(l_ref.shape, F32)
        acc_ref[...] = jnp.zeros(acc_ref.shape, F32)

        def body(j, _):
            r0 = tile_start(j)
            kt = k_ref[pl.ds(r0, T), :]
            vt = vt_ref[:, pl.ds(r0, T)]
            idx = jnp.clip(j - qi, -2, 2) + 2
            ss = [_dot(kt, qm[m]) for m in range(2)]
            for m in range(2):
                s = ss[m] - band_ref[0, idx, m]
                m_old = m_ref[m]
                m_new = jnp.maximum(m_old, jnp.max(s, axis=0, keepdims=True))
                alpha = jnp.exp2(m_old - m_new)
                p = jnp.exp2(s - m_new)
                l_ref[m] = alpha * l_ref[m] + jnp.sum(p, axis=0, keepdims=True)
                acc_ref[m] = alpha * acc_ref[m] + _dot(vt, p.astype(BF16))
                m_ref[m] = m_new
            return 0

        lax.fori_loop(0, nk, body, 0)

    if unshifted:
        unshifted_softmax()
    else:
        online_softmax()

    conv_rows = L // N_DIFF_HEADS // nk
    _conv_rows(pl.multiple_of(qi * conv_rows, conv_rows), conv_rows, uw_ref, cw_ref, cb_ref, cg_ref, cbeta_ref, yc_ref)
    lp = lam_ref[...]
    lam = (jnp.exp(jnp.sum(lp[0:1] * lp[1:2], axis=-1, keepdims=True))
           - jnp.exp(jnp.sum(lp[2:3] * lp[3:4], axis=-1, keepdims=True)) + lam_init)
    o = acc_ref[0] / l_ref[0] - lam * (acc_ref[1] / l_ref[1])
    y = o * lax.rsqrt(jnp.mean(o * o, axis=0, keepdims=True) + NORM_EPS) * on_ref[...] * (1.0 - lam_init)
    o_ref[pl.ds(pl.multiple_of(qi * T, T), T), :] = y.T.astype(o_ref.dtype)


def _diff_attn(rel_bias_flat, score_bound, dq, dk, dvt, band, lam_p, onorm, cu, conv_w, conv_b, conv_g,
               conv_beta, batch, lam_init):
    n = dq.shape[0]
    L = n // batch
    T = ATTN_TILE
    share = L // N_DIFF_HEADS
    seq = pl.BlockSpec((L, 2 * DIFF_DQK), lambda h, b: (b, h))

    def const(shape):
        return pl.BlockSpec(shape, lambda h, b: (0,) * len(shape))

    return pl.pallas_call(
        functools.partial(_attn_kernel, lam_init=lam_init),
        grid=(N_DIFF_HEADS, batch),
        in_specs=[pl.BlockSpec(memory_space=pltpu.SMEM),
                  pl.BlockSpec(memory_space=pltpu.SMEM),
                  seq, seq,
                  pl.BlockSpec((DIFF_DV, L), lambda h, b: (h, b)),
                  pl.BlockSpec((1, 5, 2, T, T), lambda h, b: (h, 0, 0, 0, 0)),
                  const((4, DIFF_DQK)), const((DIFF_DV, 1)),
                  pl.BlockSpec((L, CONV_CH), lambda h, b: (b, 0)),
                  const((CONV_K + 1, CONV_CH)), const((1, CONV_CH)), const((1, CONV_CH)), const((1, CONV_CH))],
        out_specs=[pl.BlockSpec((L, DIFF_DV), lambda h, b: (b, h)),
                   pl.BlockSpec((share, CONV_CH), lambda h, b: (b * N_DIFF_HEADS + h, 0))],
        out_shape=[jax.ShapeDtypeStruct((n, DIFF_V), BF16), jax.ShapeDtypeStruct((n, CONV_CH), F32)],
        scratch_shapes=[pltpu.VMEM((2, 1, T), F32), pltpu.VMEM((2, 1, T), F32),
                        pltpu.VMEM((2, DIFF_DV, T), F32),
                        pltpu.VMEM((share + 2 * CONV_PAD, CONV_CH), F32)],
        compiler_params=_cparams(("parallel", "parallel")),
        name="diff_attn",
    )(rel_bias_flat, score_bound, dq, dk, dvt, band, lam_p, onorm, cu, conv_w, conv_b, conv_g, conv_beta)


def _post_kernel(x_ref, ya_ref, yb_ref, yc_ref, p_ref, wo_hbm, fg_ref, wg_hbm, wu_hbm, wd_hbm,
                 pg_ref, pwg_hbm, pwp_hbm, o_ref,
                 wo_ref, wg_ref, wu_ref, wd_ref, pwg_ref, pwp_ref, wide_ref, wide_sem, tall_ref, tall_sem,
                 *, layer):
    @pl.when(pl.program_id(0) == 0)
    def _():
        _fetch_as_bf16(wo_hbm, layer, wo_ref, tall_ref, tall_sem)
        _fetch_as_bf16(wg_hbm, layer, wg_ref, wide_ref, wide_sem)
        _fetch_as_bf16(wu_hbm, layer, wu_ref, wide_ref, wide_sem)
        _fetch_as_bf16(wd_hbm, layer, wd_ref, tall_ref, tall_sem)
        _fetch_as_bf16(pwg_hbm, layer, pwg_ref, tall_ref, tall_sem)
        _fetch_as_bf16(pwp_hbm, layer, pwp_ref, tall_ref, tall_sem)

    mix = _dot(ya_ref[...].astype(BF16), wo_ref[:GLA_V, :])
    mix = mix + _dot(yb_ref[...].astype(BF16), wo_ref[GLA_V:GLA_V + DIFF_V, :])
    mix = mix + _dot(yc_ref[...].astype(BF16), wo_ref[GLA_V + DIFF_V:, :])
    x = _swiglu_half_step(x_ref[...] + mix, fg_ref, wg_ref, wu_ref, wd_ref)
    h = _rms_rows(x, pg_ref[...]).astype(BF16)
    gate = _sigmoid(_dot(h, pwg_ref[...]))
    o_ref[...] = x + gate * _dot(p_ref[...].astype(BF16), pwp_ref[...])


def _post(x, ya, yb, yc, p, w_out, ffn_gain, wg, wu, wd, ple_gain, ple_wg, ple_wp, layer):
    n = x.shape[0]
    tm = TOKEN_TILE

    def row(width):
        return pl.BlockSpec((tm, width), lambda i: (i, 0))

    return pl.pallas_call(
        functools.partial(_post_kernel, layer=layer),
        grid=(n // tm,),
        in_specs=[row(D_MODEL), row(GLA_V), row(DIFF_V), row(CONV_CH),
                  pl.BlockSpec((None, tm, PLE_DIM), lambda i: (layer, i, 0)),
                  _hbm_spec(), _const_spec((1, D_MODEL)), _hbm_spec(), _hbm_spec(), _hbm_spec(),
                  _const_spec((1, D_MODEL)), _hbm_spec(), _hbm_spec()],
        out_specs=row(D_MODEL),
        out_shape=jax.ShapeDtypeStruct((n, D_MODEL), F32),
        scratch_shapes=[pltpu.VMEM((D_MODEL, D_MODEL), BF16), pltpu.VMEM((D_MODEL, D_FF), BF16),
                        pltpu.VMEM((D_MODEL, D_FF), BF16), pltpu.VMEM((D_FF, D_MODEL), BF16),
                        pltpu.VMEM((D_MODEL, D_MODEL), BF16), pltpu.VMEM((PLE_DIM, D_MODEL), BF16)]
        + _stage_scratch(D_FF) + _stage_scratch(D_MODEL),
        compiler_params=_cparams(("arbitrary",)),
        name="post",
    )(x, ya, yb, yc, p, w_out, ffn_gain, wg, wu, wd, ple_gain, ple_wg, ple_wp)


def _pad_up(up, row0):
    return jnp.zeros((128, GLA_Q), F32).at[row0:row0 + GLA_RANK].set(up).astype(BF16)


def kernel(x, p, ffn1_norm, ffn1_w_gate, ffn1_w_up, ffn1_w_down, mix_norm, w_in, w_out, gla_gk_up_f, gla_gk_bias_f, gla_gk_up_b, gla_gk_bias_b, gla_out_norm, diff_q_norm, diff_k_norm, diff_lambda, diff_out_norm, rel_bias, conv_dw_w, conv_dw_b, conv_norm_g, conv_norm_b, ffn2_norm, ffn2_w_gate, ffn2_w_up, ffn2_w_down, ple_norm, ple_w_gate, ple_w_proj):
    B, L, _ = x.shape
    depth = w_in.shape[0]
    n = B * L
    xs = x.reshape(n, D_MODEL)
    rb_flat = rel_bias.reshape(-1)
    band = _bias_tiles(rel_bias)

    def row(v):
        return v.reshape(1, -1)

    ffn1_w = (ffn1_w_gate, ffn1_w_up, ffn1_w_down)
    ffn2_w = (ffn2_w_gate, ffn2_w_up, ffn2_w_down)
    w_in_b = w_in.astype(BF16)
    p_rows = p.reshape(depth, n, PLE_DIM)

    for i in range(depth):
        xs = _ffn(xs, row(ffn1_norm[i]), *ffn1_w, i)

        gqk, gg, gv, go, dq, dk, dvt, cu = _inproj(
            xs, row(mix_norm[i]), w_in_b,
            _pad_up(gla_gk_up_f[i], 0), row(gla_gk_bias_f[i]),
            _pad_up(gla_gk_up_b[i], GLA_RANK), row(gla_gk_bias_b[i]),
            row(jnp.tile(diff_q_norm[i], DIFF_QK // DIFF_DQK)), row(jnp.tile(diff_k_norm[i], DIFF_QK // DIFF_DQK)),
            i)

        y_gla = _gla(gqk, gg, gv, go, row(jnp.tile(gla_out_norm[i], N_GLA_HEADS)), B)
        lam_init = 0.8 - 0.6 * math.exp(-0.3 * i)
        score_bound = LOG2E * (DIFF_DQK ** 0.5 * jnp.max(jnp.abs(diff_q_norm[i])) * jnp.max(jnp.abs(diff_k_norm[i]))
                               + jnp.max(jnp.abs(rel_bias)))
        conv_w = jnp.concatenate([conv_dw_w[i], jnp.zeros((1, CONV_CH), F32)], axis=0)
        y_diff, y_conv = _diff_attn(rb_flat, score_bound.reshape(1), dq, dk, dvt, band, diff_lambda[i],
                                    diff_out_norm[i].reshape(-1, 1), cu, conv_w, row(conv_dw_b[i]),
                                    row(conv_norm_g[i]), row(conv_norm_b[i]), B, lam_init)

        xs = _post(xs, y_gla, y_diff, y_conv, p_rows, w_out, row(ffn2_norm[i]), *ffn2_w,
                   row(ple_norm[i]), ple_w_gate, ple_w_proj, i)
    return xs.reshape(B, L, D_MODEL)
```

```python
import functools
import math

import jax
import jax.numpy as jnp
from jax import lax
from jax.experimental import pallas as pl
from jax.experimental.pallas import tpu as pltpu

F32 = jnp.float32
BF16 = jnp.bfloat16

D_MODEL = 1024
N_GLA_HEADS = 4
GLA_DK = 32
GLA_DV = 64
GLA_Q = N_GLA_HEADS * GLA_DK
GLA_V = N_GLA_HEADS * GLA_DV
GLA_RANK = 16
GATE_TEMP = 16.0
GLA_CHUNK = 64
GLA_BLOCK = 16
GLA_CUM_CHUNKS = 4
DIFF_DQK = 64
N_DIFF_HEADS = 4
DIFF_DV = 128
DIFF_QK = 512
DIFF_V = 512
NUM_BUCKETS = 32
MAX_DISTANCE = 128
CONV_CH = 256
CONV_K = 31
D_FF = 2816
PLE_DIM = 256
NORM_EPS = 1e-6
LOG2E = math.log2(math.e)
EXP2_SAFE_RANGE = 64.0
D_IN = 2848

TOKEN_TILE = 512
INPROJ_TILE = 1024
INPROJ_SUB = 256
FF_CHUNKS = (1536, 1280)
ATTN_TILE = 256
ATTN_LOOKAHEAD = 3
ATTN_UNROLL = 4
CONV_PAD = 16
VMEM_LIMIT = 56 * 1024 * 1024
WEIGHT_CHUNK_ROWS_WIDE = 128
WEIGHT_CHUNK_ROWS_TALL = 256
WEIGHT_STAGE_SLOTS = 3


def _cparams(sem):
    return pltpu.CompilerParams(dimension_semantics=sem, vmem_limit_bytes=VMEM_LIMIT)


def _dot(a, b):
    return jnp.dot(a, b, preferred_element_type=F32)


def _dot_nt(a, b):
    return lax.dot_general(a, b, (((1,), (1,)), ((), ())), preferred_element_type=F32)


def _rms_rows(x, gain):
    return x * lax.rsqrt(jnp.mean(x * x, axis=-1, keepdims=True) + NORM_EPS) * gain


def _sigmoid(x):
    return 1.0 / (1.0 + jnp.exp(-x))


def _silu(x):
    return x * _sigmoid(x)


def _const_spec(shape):
    return pl.BlockSpec(shape, lambda *_: (0,) * len(shape), pipeline_mode=pl.Buffered(1))


def _layer_spec(shape, layer):
    return pl.BlockSpec((None,) + shape, lambda *_: (layer,) + (0,) * len(shape), pipeline_mode=pl.Buffered(1))


def _hbm_spec():
    return pl.BlockSpec(memory_space=pl.ANY)


def _stage_scratch(cols):
    rows = WEIGHT_CHUNK_ROWS_WIDE if cols > D_MODEL else WEIGHT_CHUNK_ROWS_TALL
    return [pltpu.VMEM((WEIGHT_STAGE_SLOTS, rows, cols), F32), pltpu.SemaphoreType.DMA((WEIGHT_STAGE_SLOTS,))]


def _fetch_as_bf16(src_hbm, layer, dst_ref, stage_ref, sem_ref):
    slots = stage_ref.shape[0]
    chunk = min(stage_ref.shape[1], dst_ref.shape[0])
    assert dst_ref.shape[0] % chunk == 0
    n = dst_ref.shape[0] // chunk

    def dma(c):
        return pltpu.make_async_copy(src_hbm.at[layer, pl.ds(c * chunk, chunk), :],
                                     stage_ref.at[c % slots, pl.ds(0, chunk), :], sem_ref.at[c % slots])

    for c in range(min(slots - 1, n)):
        dma(c).start()
    for c in range(n):
        if c + slots - 1 < n:
            dma(c + slots - 1).start()
        dma(c).wait()
        dst_ref[c * chunk:(c + 1) * chunk, :] = stage_ref[c % slots, :chunk, :].astype(BF16)


def _swiglu_half_step(x, g_ref, wg_ref, wu_ref, wd_ref):
    h = _rms_rows(x, g_ref[...]).astype(BF16)
    acc = jnp.zeros(x.shape, F32)
    c0 = 0
    for width in FF_CHUNKS:
        gate = _dot(h, wg_ref[:, c0:c0 + width])
        up = _dot(h, wu_ref[:, c0:c0 + width])
        a = (_silu(gate) * up).astype(BF16)
        acc = acc + _dot(a, wd_ref[c0:c0 + width, :])
        c0 += width
    return x + 0.5 * acc


def _ffn_kernel(x_ref, g_ref, wg_hbm, wu_hbm, wd_hbm, o_ref,
                wg_ref, wu_ref, wd_ref, wide_ref, wide_sem, tall_ref, tall_sem, *, layer):
    @pl.when(pl.program_id(0) == 0)
    def _():
        _fetch_as_bf16(wg_hbm, layer, wg_ref, wide_ref, wide_sem)
        _fetch_as_bf16(wu_hbm, layer, wu_ref, wide_ref, wide_sem)
        _fetch_as_bf16(wd_hbm, layer, wd_ref, tall_ref, tall_sem)

    o_ref[...] = _swiglu_half_step(x_ref[...], g_ref, wg_ref, wu_ref, wd_ref)


def _ffn(x, gain, wg, wu, wd, layer):
    n = x.shape[0]
    row = pl.BlockSpec((TOKEN_TILE, D_MODEL), lambda i: (i, 0))
    return pl.pallas_call(
        functools.partial(_ffn_kernel, layer=layer),
        grid=(n // TOKEN_TILE,),
        in_specs=[row, _const_spec((1, D_MODEL)), _hbm_spec(), _hbm_spec(), _hbm_spec()],
        out_specs=row,
        out_shape=jax.ShapeDtypeStruct((n, D_MODEL), F32),
        scratch_shapes=[pltpu.VMEM((D_MODEL, D_FF), BF16), pltpu.VMEM((D_MODEL, D_FF), BF16),
                        pltpu.VMEM((D_FF, D_MODEL), BF16)] + _stage_scratch(D_FF) + _stage_scratch(D_MODEL),
        compiler_params=_cparams(("arbitrary",)),
        name="ffn",
    )(x, gain, wg, wu, wd)


def _block_diag_ones(n, blk, dtype):
    r = lax.broadcasted_iota(jnp.int32, (n, n), 0) // blk
    c = lax.broadcasted_iota(jnp.int32, (n, n), 1) // blk
    return jnp.where(r == c, 1.0, 0.0).astype(dtype)


def _group_mean_sq(x, blk):
    n = x.shape[-1]
    return _dot((x * x).astype(BF16), _block_diag_ones(n, blk, BF16)) * (1.0 / blk)


def _log_sigmoid(z):
    return jnp.minimum(z, 0.0) - jnp.log(1.0 + jnp.exp(-jnp.abs(z)))


def _inproj_kernel(x_ref, g_ref, w_ref, upf_ref, bf_ref, upb_ref, bb_ref, qn_ref, kn_ref,
                   gqk_ref, gg_ref, gv_ref, go_ref, dq_ref, dk_ref, dvt_ref, cu_ref):
    tm = x_ref.shape[0]
    nsub = tm // INPROJ_SUB

    def project(s):
        rows = slice(s * INPROJ_SUB, (s + 1) * INPROJ_SUB)
        h = _rms_rows(x_ref[rows, :], g_ref[...]).astype(BF16)
        return _dot(h, w_ref[...])

    def prepare(s, proj):
        rows = slice(s * INPROJ_SUB, (s + 1) * INPROJ_SUB)
        gqk_ref[rows, :GLA_Q] = proj[:, :GLA_Q] * (GLA_DK ** -0.5)
        gqk_ref[rows, GLA_Q:] = proj[:, GLA_Q:2 * GLA_Q]
        gv_ref[rows, :] = proj[:, 256:512].astype(BF16)
        go_ref[rows, :] = _silu(proj[:, 512:768]).astype(BF16)
        low = proj[:, 768:896].astype(BF16)
        zf = _dot(low, upf_ref[...]) + bf_ref[...]
        zb = _dot(low, upb_ref[...]) + bb_ref[...]
        gg_ref[rows, :GLA_Q] = _log_sigmoid(zf) / GATE_TEMP
        gg_ref[rows, GLA_Q:] = _log_sigmoid(zb) / GATE_TEMP
        rest = proj[:, 768 + 2 * GLA_RANK:]
        dq = rest[:, :512]
        dq_ref[rows, :] = (dq * lax.rsqrt(_group_mean_sq(dq, DIFF_DQK) + NORM_EPS) * qn_ref[...]
                           * (DIFF_DQK ** -0.5 * LOG2E)).astype(BF16)
        dk = rest[:, 512:1024]
        dk_ref[rows, :] = (dk * lax.rsqrt(_group_mean_sq(dk, DIFF_DQK) + NORM_EPS) * kn_ref[...]).astype(BF16)
        dvt_ref[:, rows] = rest[:, 1024:1536].T.astype(BF16)
        cu_ref[rows, :] = rest[:, 1536:1792] * _sigmoid(rest[:, 1792:2048])

    nxt = project(0)
    for s in range(nsub):
        cur = nxt
        if s + 1 < nsub:
            nxt = project(s + 1)
        prepare(s, cur)


def _inproj(x, gain, w, upf, bf, upb, bb, qn, kn, layer):
    n = x.shape[0]
    tm = INPROJ_TILE

    def row(width):
        return pl.BlockSpec((tm, width), lambda i: (i, 0))

    outs = [(2 * GLA_Q, F32), (2 * GLA_Q, F32), (GLA_V, BF16), (GLA_V, BF16),
            (DIFF_QK, BF16), (DIFF_QK, BF16), None, (CONV_CH, F32)]
    dvt_spec = pl.BlockSpec((DIFF_V, tm), lambda i: (0, i))
    return pl.pallas_call(
        _inproj_kernel,
        grid=(n // tm,),
        in_specs=[row(D_MODEL), _const_spec((1, D_MODEL)), _layer_spec((D_MODEL, D_IN), layer),
                  _const_spec((128, GLA_Q)), _const_spec((1, GLA_Q)),
                  _const_spec((128, GLA_Q)), _const_spec((1, GLA_Q)),
                  _const_spec((1, DIFF_QK)), _const_spec((1, DIFF_QK))],
        out_specs=[row(o[0]) if o else dvt_spec for o in outs],
        out_shape=[jax.ShapeDtypeStruct((n, o[0]), o[1]) if o else jax.ShapeDtypeStruct((DIFF_V, n), BF16)
                   for o in outs],
        compiler_params=_cparams(("parallel",)),
        name="inproj",
    )(x, gain, w, upf, bf, upb, bb, qn, kn)


def _gla_block_kernel(qk_ref, g_ref, v_ref, go_ref, on_ref, y_ref, of_ref, st_ref):
    C = GLA_CHUNK
    G = GLA_BLOCK
    R = G * C
    L = qk_ref.shape[0]
    nblk = L // R
    H = N_GLA_HEADS

    def iota(shape, d):
        return lax.broadcasted_iota(jnp.int32, shape, d)

    kmask = iota((H * C, GLA_Q), 0) // C == iota((H * C, GLA_Q), 1) // GLA_DK
    vmask = iota((H * C, GLA_V), 0) // C == iota((H * C, GLA_V), 1) // GLA_DV
    smask = iota((GLA_V, GLA_Q), 0) // GLA_DV == iota((GLA_V, GLA_Q), 1) // GLA_DK
    pos_a = iota((C, H * C), 0)
    pos_b = iota((C, H * C), 1) % C
    RC = GLA_CUM_CHUNKS * C
    ri, ci = iota((RC, RC), 0), iota((RC, RC), 1)
    same_chunk = ri // C == ci // C
    cums = [jnp.where(jnp.logical_and(same_chunk, ci <= ri), 1.0, 0.0).astype(BF16),
            jnp.where(jnp.logical_and(same_chunk, ci >= ri), 1.0, 0.0).astype(BF16)]
    amasks = [pos_b <= pos_a, pos_b >= pos_a]
    edges = [C - 1, 0]
    orders = [list(range(G)), list(range(G - 1, -1, -1))]

    st_ref[...] = jnp.zeros(st_ref.shape, F32)

    def step(t, _):
        r0 = [pl.multiple_of(t * R, R), pl.multiple_of((nblk - 1 - t) * R, R)]
        rows = [pl.ds(r0[0], R), pl.ds(r0[1], R)]
        dirs = (0, 1)
        b, k, v, qt = {}, {}, {}, {}
        for d in dirs:
            g = g_ref[rows[d], d * GLA_Q:(d + 1) * GLA_Q]
            g_hi = g.astype(BF16)
            g_lo = (g - g_hi.astype(F32)).astype(BF16)
            b[d] = jnp.concatenate([_dot(cums[d], g_hi[i:i + RC]) + _dot(cums[d], g_lo[i:i + RC])
                                    for i in range(0, R, RC)], axis=0)
        a, ds, decay = {}, {}, {}
        for d in dirs:
            q = qk_ref[rows[d], :GLA_Q]
            k[d] = qk_ref[rows[d], GLA_Q:]
            v[d] = v_ref[rows[d], :].astype(F32)
            qt[d] = (q * jnp.exp(b[d])).astype(BF16)
            kt = k[d] * jnp.exp(-b[d])
            for c in range(G):
                sl = slice(c * C, (c + 1) * C)
                b_c = b[d][sl]
                b_edge = b_c[edges[d]:edges[d] + 1, :]
                decay[d, c] = jnp.exp(b_edge)
                kend = (k[d][sl] * jnp.exp(b_edge - b_c)).astype(BF16)
                kstack = jnp.where(kmask, jnp.concatenate([kt[sl]] * H, axis=0), 0.0).astype(BF16)
                a[d, c] = jnp.where(amasks[d], _dot_nt(qt[d][sl], kstack), 0.0).astype(BF16)
                ds[d, c] = _dot(v[d][sl].T.astype(BF16), kend)
        for d in dirs:
            s = st_ref[d]
            s_in = {}
            for c in orders[d]:
                s_in[c] = s.astype(BF16)
                s = s * decay[d, c] + jnp.where(smask, ds[d, c], 0.0)
            st_ref[d] = s
            out_ref = y_ref if d else of_ref
            for c in range(G):
                sl = slice(c * C, (c + 1) * C)
                vstack = jnp.where(vmask, jnp.concatenate([v[d][sl]] * H, axis=0), 0.0).astype(BF16)
                o = _dot(a[d, c], vstack) + _dot_nt(qt[d][sl], s_in[c])
                out_ref[pl.ds(pl.multiple_of(r0[d] + c * C, C), C), :] = o
        return 0

    lax.fori_loop(0, nblk, step, 0)

    norm_bd = _block_diag_ones(GLA_V, GLA_DV, BF16)

    def finish(t, _):
        rows = pl.ds(pl.multiple_of(t * R, R), R)
        tot = of_ref[rows, :] + y_ref[rows, :]
        ms = _dot((tot * tot).astype(BF16), norm_bd) * (1.0 / GLA_DV)
        y_ref[rows, :] = tot * lax.rsqrt(ms + NORM_EPS) * on_ref[...] * go_ref[rows, :].astype(F32)
        return 0

    lax.fori_loop(0, nblk, finish, 0)


def _gla(qk, g, v, go, onorm, batch):
    n = qk.shape[0]
    L = n // batch

    def seq(width):
        return pl.BlockSpec((L, width), lambda b: (b, 0))

    return pl.pallas_call(
        _gla_block_kernel,
        grid=(batch,),
        in_specs=[seq(2 * GLA_Q), seq(2 * GLA_Q), seq(GLA_V), seq(GLA_V), _const_spec((1, GLA_V))],
        out_specs=seq(GLA_V),
        out_shape=jax.ShapeDtypeStruct((n, GLA_V), F32),
        scratch_shapes=[pltpu.VMEM((L, GLA_V), F32), pltpu.VMEM((2, GLA_V, GLA_Q), F32)],
        compiler_params=_cparams(("parallel",)),
        name="gla",
    )(qk, g, v, go, onorm)


def _bias_kernel(rb_ref, o_ref):
    T = ATTN_TILE
    h = pl.program_id(0)
    nb = NUM_BUCKETS // 2
    max_exact = nb // 2
    for d in range(5):
        rel = (lax.broadcasted_iota(jnp.int32, (T, T), 0) - lax.broadcasted_iota(jnp.int32, (T, T), 1)
               + (d - 2) * T)
        ret = jnp.where(rel > 0, nb, 0)
        n = jnp.abs(rel)
        large = max_exact + (jnp.log(jnp.maximum(n, 1).astype(F32) / max_exact)
                             / math.log(MAX_DISTANCE / max_exact) * (nb - max_exact)).astype(jnp.int32)
        large = jnp.minimum(large, nb - 1)
        bucket = ret + jnp.where(n < max_exact, n, large)
        reachable = ([nb - 1] if d == 0 else range(nb) if d == 1 else range(NUM_BUCKETS) if d == 2
                     else range(nb, NUM_BUCKETS) if d == 3 else [NUM_BUCKETS - 1])
        for m in range(2):
            tile = jnp.zeros((T, T), F32)
            for bkt in reachable:
                tile = jnp.where(bucket == bkt, rb_ref[(bkt * N_DIFF_HEADS + h) * 2 + m], tile)
            o_ref[0, d, m] = tile * (-LOG2E)


def _bias_tiles(rel_bias):
    T = ATTN_TILE
    return pl.pallas_call(
        _bias_kernel,
        grid=(N_DIFF_HEADS,),
        in_specs=[pl.BlockSpec(memory_space=pltpu.SMEM)],
        out_specs=pl.BlockSpec((1, 5, 2, T, T), lambda h: (h, 0, 0, 0, 0)),
        out_shape=jax.ShapeDtypeStruct((N_DIFF_HEADS, 5, 2, T, T), F32),
        compiler_params=_cparams(("parallel",)),
        name="t5_bias",
    )(rel_bias.reshape(-1))


def _stage_conv_window(u_ref, uw_ref):
    L = u_ref.shape[0]
    P = CONV_PAD
    rows = L // N_DIFF_HEADS
    h = pl.program_id(0)
    start = pl.multiple_of(h * rows, rows)
    uw_ref[P:P + rows, :] = u_ref[pl.ds(start, rows), :]
    above = u_ref[pl.ds(pl.multiple_of(jnp.maximum(start - P, 0), P), P), :]
    uw_ref[:P, :] = jnp.where(h > 0, above, 0.0)
    below = u_ref[pl.ds(pl.multiple_of(jnp.minimum(start + rows, L - P), P), P), :]
    uw_ref[P + rows:, :] = jnp.where(h < N_DIFF_HEADS - 1, below, 0.0)


def _conv_rows(r0, R, uw_ref, w_ref, b_ref, g_ref, beta_ref, y_ref):
    P = CONV_PAD
    off = P - CONV_K // 2
    win = uw_ref[pl.ds(r0, R + 2 * P), :]
    acc = jnp.zeros((R, CONV_CH), F32) + b_ref[...]
    for r in range(8):
        shifted = win if r == 0 else pltpu.roll(win, R + 2 * P - r, 0)
        for k in range(CONV_K):
            if (off + k) % 8 == r:
                a8 = (off + k) // 8
                acc = acc + w_ref[k:k + 1, :] * shifted[8 * a8:8 * a8 + R, :]
    mu = jnp.mean(acc, axis=-1, keepdims=True)
    xc = acc - mu
    y = xc * lax.rsqrt(jnp.mean(xc * xc, axis=-1, keepdims=True) + NORM_EPS) * g_ref[...] + beta_ref[...]
    y_ref[pl.ds(r0, R), :] = _silu(y)


def _attn_kernel(*refs, lam_init):
    bound_ref = refs[1]
    nq = refs[2].shape[0] // ATTN_TILE
    unshifted_ok = bound_ref[0] <= EXP2_SAFE_RANGE
    _stage_conv_window(refs[8], refs[-1])

    def sweep(unshifted, unroll):
        def body(qi, _):
            _attn_query_tile(qi, *refs, lam_init=lam_init, unshifted=unshifted)
            return 0
        lax.fori_loop(0, nq, body, 0, unroll=unroll)

    @pl.when(unshifted_ok)
    def _():
        sweep(True, ATTN_UNROLL)

    @pl.when(jnp.logical_not(unshifted_ok))
    def _():
        sweep(False, 1)


def _attn_query_tile(qi, rb_ref, bound_ref, q_ref, k_ref, vt_ref, band_ref, lam_ref, on_ref,
                     u_ref, cw_ref, cb_ref, cg_ref, cbeta_ref, o_ref, yc_ref,
                     m_ref, l_ref, acc_ref, uw_ref, *, lam_init, unshifted):
    T = ATTN_TILE
    L = k_ref.shape[0]
    nk = L // T
    h = pl.program_id(0)
    q = q_ref[pl.ds(pl.multiple_of(qi * T, T), T), :]
    qt = q.astype(F32).T
    chan = lax.broadcasted_iota(jnp.int32, qt.shape, 0)
    qm = [jnp.where(chan < DIFF_DQK, qt, 0.0).astype(BF16), jnp.where(chan >= DIFF_DQK, qt, 0.0).astype(BF16)]

    def side_const(bucket, m):
        return rb_ref[(bucket * N_DIFF_HEADS + h) * 2 + m] * LOG2E

    def tile_start(j):
        return pl.multiple_of(j * T, T)

    def unshifted_softmax():
        f_left = [jnp.exp2(jnp.full((1, 1), side_const(NUM_BUCKETS // 2 - 1, m), F32)).astype(BF16)
                  for m in range(2)]
        f_right = [jnp.exp2(jnp.full((1, 1), side_const(NUM_BUCKETS - 1, m), F32)).astype(BF16)
                   for m in range(2)]
        band_idx = [jnp.where(qi == 0, 4, 1), 2, jnp.where(qi == nk - 1, 0, 3)]
        l = [jnp.zeros((1, T), F32) for _ in range(2)]
        acc = [jnp.zeros((DIFF_DV, T), F32) for _ in range(2)]

        def scores(t):
            kt = k_ref[pl.ds(tile_start((qi + (2 + t)) % nk), T), :]
            return [_dot(kt, qm[m]) for m in range(2)]

        pending = [scores(t) for t in range(ATTN_LOOKAHEAD)]
        for t in range(nk):
            j = (qi + (2 + t)) % nk
            vt = vt_ref[:, pl.ds(tile_start(j), T)]
            s_cur = pending.pop(0)
            if t + ATTN_LOOKAHEAD < nk:
                pending.append(scores(t + ATTN_LOOKAHEAD))
            for m in range(2):
                if t >= nk - 3:
                    p = jnp.exp2(s_cur[m] - band_ref[0, band_idx[t - (nk - 3)], m])
                    l[m] = l[m] + jnp.sum(p, axis=0, keepdims=True)
                    acc[m] = acc[m] + _dot(vt, p.astype(BF16))
                else:
                    f = jnp.where(j < qi, f_left[m], f_right[m])
                    p = jnp.exp2(s_cur[m])
                    l[m] = l[m] + f.astype(F32) * jnp.sum(p, axis=0, keepdims=True)
                    acc[m] = acc[m] + _dot(vt * f, p.astype(BF16))
        for m in range(2):
            l_ref[m] = l[m]
            acc_ref[m] = acc[m]

    def online_softmax():
        m_ref[...] = jnp.full(m_ref.shape, -jnp.inf, F32)
        l_ref[...] = jnp.zeros(l_ref.shape, F32)
        acc_ref[...] = jnp.zeros(acc_ref.shape, F32)

        def body(j, _):
            r0 = tile_start(j)
            kt = k_ref[pl.ds(r0, T), :]
            vt = vt_ref[:, pl.ds(r0, T)]
            idx = jnp.clip(j - qi, -2, 2) + 2
            ss = [_dot(kt, qm[m]) for m in range(2)]
            for m in range(2):
                s = ss[m] - band_ref[0, idx, m]
                m_old = m_ref[m]
                m_new = jnp.maximum(m_old, jnp.max(s, axis=0, keepdims=True))
                alpha = jnp.exp2(m_old - m_new)
                p = jnp.exp2(s - m_new)
                l_ref[m] = alpha * l_ref[m] + jnp.sum(p, axis=0, keepdims=True)
                acc_ref[m] = alpha * acc_ref[m] + _dot(vt, p.astype(BF16))
                m_ref[m] = m_new
            return 0

        lax.fori_loop(0, nk, body, 0)

    if unshifted:
        unshifted_softmax()
    else:
        online_softmax()

    conv_rows = L // N_DIFF_HEADS // nk
    _conv_rows(pl.multiple_of(qi * conv_rows, conv_rows), conv_rows, uw_ref, cw_ref, cb_ref, cg_ref, cbeta_ref, yc_ref)
    lp = lam_ref[...]
    lam = (jnp.exp(jnp.sum(lp[0:1] * lp[1:2], axis=-1, keepdims=True))
           - jnp.exp(jnp.sum(lp[2:3] * lp[3:4], axis=-1, keepdims=True)) + lam_init)
    o = acc_ref[0] / l_ref[0] - lam * (acc_ref[1] / l_ref[1])
    y = o * lax.rsqrt(jnp.mean(o * o, axis=0, keepdims=True) + NORM_EPS) * on_ref[...] * (1.0 - lam_init)
    o_ref[pl.ds(pl.multiple_of(qi * T, T), T), :] = y.T.astype(o_ref.dtype)


def _diff_attn(rel_bias_flat, score_bound, dq, dk, dvt, band, lam_p, onorm, cu, conv_w, conv_b, conv_g,
               conv_beta, batch, lam_init):
    n = dq.shape[0]
    L = n // batch
    T = ATTN_TILE
    share = L // N_DIFF_HEADS
    seq = pl.BlockSpec((L, 2 * DIFF_DQK), lambda h, b: (b, h))

    def const(shape):
        return pl.BlockSpec(shape, lambda h, b: (0,) * len(shape))

    return pl.pallas_call(
        functools.partial(_attn_kernel, lam_init=lam_init),
        grid=(N_DIFF_HEADS, batch),
        in_specs=[pl.BlockSpec(memory_space=pltpu.SMEM),
                  pl.BlockSpec(memory_space=pltpu.SMEM),
                  seq, seq,
                  pl.BlockSpec((DIFF_DV, L), lambda h, b: (h, b)),
                  pl.BlockSpec((1, 5, 2, T, T), lambda h, b: (h, 0, 0, 0, 0)),
                  const((4, DIFF_DQK)), const((DIFF_DV, 1)),
                  pl.BlockSpec((L, CONV_CH), lambda h, b: (b, 0)),
                  const((CONV_K + 1, CONV_CH)), const((1, CONV_CH)), const((1, CONV_CH)), const((1, CONV_CH))],
        out_specs=[pl.BlockSpec((L, DIFF_DV), lambda h, b: (b, h)),
                   pl.BlockSpec((share, CONV_CH), lambda h, b: (b * N_DIFF_HEADS + h, 0))],
        out_shape=[jax.ShapeDtypeStruct((n, DIFF_V), BF16), jax.ShapeDtypeStruct((n, CONV_CH), F32)],
        scratch_shapes=[pltpu.VMEM((2, 1, T), F32), pltpu.VMEM((2, 1, T), F32),
                        pltpu.VMEM((2, DIFF_DV, T), F32),
                        pltpu.VMEM((share + 2 * CONV_PAD, CONV_CH), F32)],
        compiler_params=_cparams(("parallel", "parallel")),
        name="diff_attn",
    )(rel_bias_flat, score_bound, dq, dk, dvt, band, lam_p, onorm, cu, conv_w, conv_b, conv_g, conv_beta)


def _post_kernel(x_ref, ya_ref, yb_ref, yc_ref, p_ref, wo_hbm, fg_ref, wg_hbm, wu_hbm, wd_hbm,
                 pg_ref, pwg_hbm, pwp_hbm, o_ref,
                 wo_ref, wg_ref, wu_ref, wd_ref, pwg_ref, pwp_ref, wide_ref, wide_sem, tall_ref, tall_sem,
                 *, layer):
    @pl.when(pl.program_id(0) == 0)
    def _():
        _fetch_as_bf16(wo_hbm, layer, wo_ref, tall_ref, tall_sem)
        _fetch_as_bf16(wg_hbm, layer, wg_ref, wide_ref, wide_sem)
        _fetch_as_bf16(wu_hbm, layer, wu_ref, wide_ref, wide_sem)
        _fetch_as_bf16(wd_hbm, layer, wd_ref, tall_ref, tall_sem)
        _fetch_as_bf16(pwg_hbm, layer, pwg_ref, tall_ref, tall_sem)
        _fetch_as_bf16(pwp_hbm, layer, pwp_ref, tall_ref, tall_sem)

    mix = _dot(ya_ref[...].astype(BF16), wo_ref[:GLA_V, :])
    mix = mix + _dot(yb_ref[...].astype(BF16), wo_ref[GLA_V:GLA_V + DIFF_V, :])
    mix = mix + _dot(yc_ref[...].astype(BF16), wo_ref[GLA_V + DIFF_V:, :])
    x = _swiglu_half_step(x_ref[...] + mix, fg_ref, wg_ref, wu_ref, wd_ref)
    h = _rms_rows(x, pg_ref[...]).astype(BF16)
    gate = _sigmoid(_dot(h, pwg_ref[...]))
    o_ref[...] = x + gate * _dot(p_ref[...].astype(BF16), pwp_ref[...])


def _post(x, ya, yb, yc, p, w_out, ffn_gain, wg, wu, wd, ple_gain, ple_wg, ple_wp, layer):
    n = x.shape[0]
    tm = TOKEN_TILE

    def row(width):
        return pl.BlockSpec((tm, width), lambda i: (i, 0))

    return pl.pallas_call(
        functools.partial(_post_kernel, layer=layer),
        grid=(n // tm,),
        in_specs=[row(D_MODEL), row(GLA_V), row(DIFF_V), row(CONV_CH),
                  pl.BlockSpec((None, tm, PLE_DIM), lambda i: (layer, i, 0)),
                  _hbm_spec(), _const_spec((1, D_MODEL)), _hbm_spec(), _hbm_spec(), _hbm_spec(),
                  _const_spec((1, D_MODEL)), _hbm_spec(), _hbm_spec()],
        out_specs=row(D_MODEL),
        out_shape=jax.ShapeDtypeStruct((n, D_MODEL), F32),
        scratch_shapes=[pltpu.VMEM((D_MODEL, D_MODEL), BF16), pltpu.VMEM((D_MODEL, D_FF), BF16),
                        pltpu.VMEM((D_MODEL, D_FF), BF16), pltpu.VMEM((D_FF, D_MODEL), BF16),
                        pltpu.VMEM((D_MODEL, D_MODEL), BF16), pltpu.VMEM((PLE_DIM, D_MODEL), BF16)]
        + _stage_scratch(D_FF) + _stage_scratch(D_MODEL),
        compiler_params=_cparams(("arbitrary",)),
        name="post",
    )(x, ya, yb, yc, p, w_out, ffn_gain, wg, wu, wd, ple_gain, ple_wg, ple_wp)


def _pad_up(up, row0):
    return jnp.zeros((128, GLA_Q), F32).at[row0:row0 + GLA_RANK].set(up).astype(BF16)


def kernel(x, p, ffn1_norm, ffn1_w_gate, ffn1_w_up, ffn1_w_down, mix_norm, w_in, w_out, gla_gk_up_f, gla_gk_bias_f, gla_gk_up_b, gla_gk_bias_b, gla_out_norm, diff_q_norm, diff_k_norm, diff_lambda, diff_out_norm, rel_bias, conv_dw_w, conv_dw_b, conv_norm_g, conv_norm_b, ffn2_norm, ffn2_w_gate, ffn2_w_up, ffn2_w_down, ple_norm, ple_w_gate, ple_w_proj):
    B, L, _ = x.shape
    depth = w_in.shape[0]
    n = B * L
    xs = x.reshape(n, D_MODEL)
    rb_flat = rel_bias.reshape(-1)
    band = _bias_tiles(rel_bias)

    def row(v):
        return v.reshape(1, -1)

    ffn1_w = (ffn1_w_gate, ffn1_w_up, ffn1_w_down)
    ffn2_w = (ffn2_w_gate, ffn2_w_up, ffn2_w_down)
    w_in_b = w_in.astype(BF16)
    p_rows = p.reshape(depth, n, PLE_DIM)

    for i in range(depth):
        xs = _ffn(xs, row(ffn1_norm[i]), *ffn1_w, i)

        gqk, gg, gv, go, dq, dk, dvt, cu = _inproj(
            xs, row(mix_norm[i]), w_in_b,
            _pad_up(gla_gk_up_f[i], 0), row(gla_gk_bias_f[i]),
            _pad_up(gla_gk_up_b[i], GLA_RANK), row(gla_gk_bias_b[i]),
            row(jnp.tile(diff_q_norm[i], DIFF_QK // DIFF_DQK)), row(jnp.tile(diff_k_norm[i], DIFF_QK // DIFF_DQK)),
            i)

        y_gla = _gla(gqk, gg, gv, go, row(jnp.tile(gla_out_norm[i], N_GLA_HEADS)), B)
        lam_init = 0.8 - 0.6 * math.exp(-0.3 * i)
        score_bound = LOG2E * (DIFF_DQK ** 0.5 * jnp.max(jnp.abs(diff_q_norm[i])) * jnp.max(jnp.abs(diff_k_norm[i]))
                               + jnp.max(jnp.abs(rel_bias)))
        conv_w = jnp.concatenate([conv_dw_w[i], jnp.zeros((1, CONV_CH), F32)], axis=0)
        y_diff, y_conv = _diff_attn(rb_flat, score_bound.reshape(1), dq, dk, dvt, band, diff_lambda[i],
                                    diff_out_norm[i].reshape(-1, 1), cu, conv_w, row(conv_dw_b[i]),
                                    row(conv_norm_g[i]), row(conv_norm_b[i]), B, lam_init)

        xs = _post(xs, y_gla, y_diff, y_conv, p_rows, w_out, row(ffn2_norm[i]), *ffn2_w,
                   row(ple_norm[i]), ple_w_gate, ple_w_proj, i)
    return xs.reshape(B, L, D_MODEL)
```

```python
import functools
import math

import jax
import jax.numpy as jnp
from jax import lax
from jax.experimental import pallas as pl
from jax.experimental.pallas import tpu as pltpu

F32 = jnp.float32
BF16 = jnp.bfloat16

D_MODEL = 1024
N_GLA_HEADS = 4
GLA_DK = 32
GLA_DV = 64
GLA_Q = N_GLA_HEADS * GLA_DK
GLA_V = N_GLA_HEADS * GLA_DV
GLA_RANK = 16
GATE_TEMP = 16.0
GLA_CHUNK = 64
GLA_BLOCK = 16
GLA_CUM_CHUNKS = 4
DIFF_DQK = 64
N_DIFF_HEADS = 4
DIFF_DV = 128
DIFF_QK = 512
DIFF_V = 512
NUM_BUCKETS = 32
MAX_DISTANCE = 128
CONV_CH = 256
CONV_K = 31
D_FF = 2816
PLE_DIM = 256
NORM_EPS = 1e-6
LOG2E = math.log2(math.e)
EXP2_SAFE_RANGE = 64.0
D_IN = 2848

TOKEN_TILE = 512
INPROJ_TILE = 1024
INPROJ_SUB = 256
FF_CHUNKS = (1536, 1280)
ATTN_TILE = 256
ATTN_LOOKAHEAD = 3
ATTN_UNROLL = 8
CONV_PAD = 16
V7X_VMEM_BYTES = 64 * 1024 * 1024
VMEM_LIMIT = V7X_VMEM_BYTES * 7 // 8
WEIGHT_CHUNK_ROWS_WIDE = 128
WEIGHT_CHUNK_ROWS_TALL = 256
WEIGHT_STAGE_SLOTS = 3


def _cparams(sem):
    return pltpu.CompilerParams(dimension_semantics=sem, vmem_limit_bytes=VMEM_LIMIT)


def _dot(a, b):
    return jnp.dot(a, b, preferred_element_type=F32)


def _dot_nt(a, b):
    return lax.dot_general(a, b, (((1,), (1,)), ((), ())), preferred_element_type=F32)


def _rms_rows(x, gain):
    return x * lax.rsqrt(jnp.mean(x * x, axis=-1, keepdims=True) + NORM_EPS) * gain


def _sigmoid(x):
    return 1.0 / (1.0 + jnp.exp(-x))


def _silu(x):
    return x * _sigmoid(x)


def _const_spec(shape):
    return pl.BlockSpec(shape, lambda *_: (0,) * len(shape), pipeline_mode=pl.Buffered(1))


def _layer_spec(shape, layer):
    return pl.BlockSpec((None,) + shape, lambda *_: (layer,) + (0,) * len(shape), pipeline_mode=pl.Buffered(1))


def _hbm_spec():
    return pl.BlockSpec(memory_space=pl.ANY)


def _stage_scratch(cols):
    rows = WEIGHT_CHUNK_ROWS_WIDE if cols > D_MODEL else WEIGHT_CHUNK_ROWS_TALL
    return [pltpu.VMEM((WEIGHT_STAGE_SLOTS, rows, cols), F32), pltpu.SemaphoreType.DMA((WEIGHT_STAGE_SLOTS,))]


def _fetch_as_bf16(src_hbm, layer, dst_ref, stage_ref, sem_ref):
    slots = stage_ref.shape[0]
    chunk = min(stage_ref.shape[1], dst_ref.shape[0])
    assert dst_ref.shape[0] % chunk == 0
    n = dst_ref.shape[0] // chunk

    def dma(c):
        return pltpu.make_async_copy(src_hbm.at[layer, pl.ds(c * chunk, chunk), :],
                                     stage_ref.at[c % slots, pl.ds(0, chunk), :], sem_ref.at[c % slots])

    for c in range(min(slots - 1, n)):
        dma(c).start()
    for c in range(n):
        if c + slots - 1 < n:
            dma(c + slots - 1).start()
        dma(c).wait()
        dst_ref[c * chunk:(c + 1) * chunk, :] = stage_ref[c % slots, :chunk, :].astype(BF16)


def _swiglu_half_step(x, g_ref, wg_ref, wu_ref, wd_ref):
    h = _rms_rows(x, g_ref[...]).astype(BF16)
    acc = jnp.zeros(x.shape, F32)
    c0 = 0
    for width in FF_CHUNKS:
        gate = _dot(h, wg_ref[:, c0:c0 + width])
        up = _dot(h, wu_ref[:, c0:c0 + width])
        a = (_silu(gate) * up).astype(BF16)
        acc = acc + _dot(a, wd_ref[c0:c0 + width, :])
        c0 += width
    return x + 0.5 * acc


def _ffn_kernel(x_ref, g_ref, wg_hbm, wu_hbm, wd_hbm, o_ref,
                wg_ref, wu_ref, wd_ref, wide_ref, wide_sem, tall_ref, tall_sem, *, layer):
    @pl.when(pl.program_id(0) == 0)
    def _():
        _fetch_as_bf16(wg_hbm, layer, wg_ref, wide_ref, wide_sem)
        _fetch_as_bf16(wu_hbm, layer, wu_ref, wide_ref, wide_sem)
        _fetch_as_bf16(wd_hbm, layer, wd_ref, tall_ref, tall_sem)

    o_ref[...] = _swiglu_half_step(x_ref[...], g_ref, wg_ref, wu_ref, wd_ref)


def _ffn(x, gain, wg, wu, wd, layer):
    n = x.shape[0]
    row = pl.BlockSpec((TOKEN_TILE, D_MODEL), lambda i: (i, 0))
    return pl.pallas_call(
        functools.partial(_ffn_kernel, layer=layer),
        grid=(n // TOKEN_TILE,),
        in_specs=[row, _const_spec((1, D_MODEL)), _hbm_spec(), _hbm_spec(), _hbm_spec()],
        out_specs=row,
        out_shape=jax.ShapeDtypeStruct((n, D_MODEL), F32),
        scratch_shapes=[pltpu.VMEM((D_MODEL, D_FF), BF16), pltpu.VMEM((D_MODEL, D_FF), BF16),
                        pltpu.VMEM((D_FF, D_MODEL), BF16)] + _stage_scratch(D_FF) + _stage_scratch(D_MODEL),
        compiler_params=_cparams(("arbitrary",)),
        name="ffn",
    )(x, gain, wg, wu, wd)


def _block_diag_ones(n, blk, dtype):
    r = lax.broadcasted_iota(jnp.int32, (n, n), 0) // blk
    c = lax.broadcasted_iota(jnp.int32, (n, n), 1) // blk
    return jnp.where(r == c, 1.0, 0.0).astype(dtype)


def _group_mean_sq(x, blk):
    n = x.shape[-1]
    return _dot((x * x).astype(BF16), _block_diag_ones(n, blk, BF16)) * (1.0 / blk)


def _log_sigmoid(z):
    return jnp.minimum(z, 0.0) - jnp.log(1.0 + jnp.exp(-jnp.abs(z)))


def _inproj_kernel(x_ref, g_ref, w_ref, upf_ref, bf_ref, upb_ref, bb_ref, qn_ref, kn_ref,
                   gqk_ref, gg_ref, gv_ref, go_ref, dq_ref, dk_ref, dvt_ref, cu_ref):
    tm = x_ref.shape[0]
    nsub = tm // INPROJ_SUB

    def project(s):
        rows = slice(s * INPROJ_SUB, (s + 1) * INPROJ_SUB)
        h = _rms_rows(x_ref[rows, :], g_ref[...]).astype(BF16)
        return _dot(h, w_ref[...])

    def prepare(s, proj):
        rows = slice(s * INPROJ_SUB, (s + 1) * INPROJ_SUB)
        gqk_ref[rows, :GLA_Q] = proj[:, :GLA_Q] * (GLA_DK ** -0.5)
        gqk_ref[rows, GLA_Q:] = proj[:, GLA_Q:2 * GLA_Q]
        gv_ref[rows, :] = proj[:, 256:512].astype(BF16)
        go_ref[rows, :] = _silu(proj[:, 512:768]).astype(BF16)
        low = proj[:, 768:896].astype(BF16)
        zf = _dot(low, upf_ref[...]) + bf_ref[...]
        zb = _dot(low, upb_ref[...]) + bb_ref[...]
        gg_ref[rows, :GLA_Q] = _log_sigmoid(zf) / GATE_TEMP
        gg_ref[rows, GLA_Q:] = _log_sigmoid(zb) / GATE_TEMP
        rest = proj[:, 768 + 2 * GLA_RANK:]
        dq = rest[:, :512]
        dq_ref[rows, :] = (dq * lax.rsqrt(_group_mean_sq(dq, DIFF_DQK) + NORM_EPS) * qn_ref[...]
                           * (DIFF_DQK ** -0.5 * LOG2E)).astype(BF16)
        dk = rest[:, 512:1024]
        dk_ref[rows, :] = (dk * lax.rsqrt(_group_mean_sq(dk, DIFF_DQK) + NORM_EPS) * kn_ref[...]).astype(BF16)
        dvt_ref[:, rows] = rest[:, 1024:1536].T.astype(BF16)
        cu_ref[rows, :] = rest[:, 1536:1792] * _sigmoid(rest[:, 1792:2048])

    nxt = project(0)
    for s in range(nsub):
        cur = nxt
        if s + 1 < nsub:
            nxt = project(s + 1)
        prepare(s, cur)


def _inproj(x, gain, w, upf, bf, upb, bb, qn, kn, layer):
    n = x.shape[0]
    tm = INPROJ_TILE

    def row(width):
        return pl.BlockSpec((tm, width), lambda i: (i, 0))

    outs = [(2 * GLA_Q, F32), (2 * GLA_Q, F32), (GLA_V, BF16), (GLA_V, BF16),
            (DIFF_QK, BF16), (DIFF_QK, BF16), None, (CONV_CH, F32)]
    dvt_spec = pl.BlockSpec((DIFF_V, tm), lambda i: (0, i))
    return pl.pallas_call(
        _inproj_kernel,
        grid=(n // tm,),
        in_specs=[row(D_MODEL), _const_spec((1, D_MODEL)), _layer_spec((D_MODEL, D_IN), layer),
                  _const_spec((128, GLA_Q)), _const_spec((1, GLA_Q)),
                  _const_spec((128, GLA_Q)), _const_spec((1, GLA_Q)),
                  _const_spec((1, DIFF_QK)), _const_spec((1, DIFF_QK))],
        out_specs=[row(o[0]) if o else dvt_spec for o in outs],
        out_shape=[jax.ShapeDtypeStruct((n, o[0]), o[1]) if o else jax.ShapeDtypeStruct((DIFF_V, n), BF16)
                   for o in outs],
        compiler_params=_cparams(("parallel",)),
        name="inproj",
    )(x, gain, w, upf, bf, upb, bb, qn, kn)


def _gla_block_kernel(qk_ref, g_ref, v_ref, go_ref, on_ref, y_ref, of_ref, st_ref):
    C = GLA_CHUNK
    G = GLA_BLOCK
    R = G * C
    L = qk_ref.shape[0]
    nblk = L // R
    H = N_GLA_HEADS

    def iota(shape, d):
        return lax.broadcasted_iota(jnp.int32, shape, d)

    kmask = iota((H * C, GLA_Q), 0) // C == iota((H * C, GLA_Q), 1) // GLA_DK
    vmask = iota((H * C, GLA_V), 0) // C == iota((H * C, GLA_V), 1) // GLA_DV
    smask = iota((GLA_V, GLA_Q), 0) // GLA_DV == iota((GLA_V, GLA_Q), 1) // GLA_DK
    pos_a = iota((C, H * C), 0)
    pos_b = iota((C, H * C), 1) % C
    RC = GLA_CUM_CHUNKS * C
    ri, ci = iota((RC, RC), 0), iota((RC, RC), 1)
    same_chunk = ri // C == ci // C
    cums = [jnp.where(jnp.logical_and(same_chunk, ci <= ri), 1.0, 0.0).astype(BF16),
            jnp.where(jnp.logical_and(same_chunk, ci >= ri), 1.0, 0.0).astype(BF16)]
    amasks = [pos_b <= pos_a, pos_b >= pos_a]
    edges = [C - 1, 0]
    orders = [list(range(G)), list(range(G - 1, -1, -1))]

    st_ref[...] = jnp.zeros(st_ref.shape, F32)

    def step(t, _):
        r0 = [pl.multiple_of(t * R, R), pl.multiple_of((nblk - 1 - t) * R, R)]
        rows = [pl.ds(r0[0], R), pl.ds(r0[1], R)]
        dirs = (0, 1)
        b, k, v, qt = {}, {}, {}, {}
        for d in dirs:
            g = g_ref[rows[d], d * GLA_Q:(d + 1) * GLA_Q]
            g_hi = g.astype(BF16)
            g_lo = (g - g_hi.astype(F32)).astype(BF16)
            b[d] = jnp.concatenate([_dot(cums[d], g_hi[i:i + RC]) + _dot(cums[d], g_lo[i:i + RC])
                                    for i in range(0, R, RC)], axis=0)
        a, ds, decay = {}, {}, {}
        for d in dirs:
            q = qk_ref[rows[d], :GLA_Q]
            k[d] = qk_ref[rows[d], GLA_Q:]
            v[d] = v_ref[rows[d], :].astype(F32)
            qt[d] = (q * jnp.exp(b[d])).astype(BF16)
            kt = k[d] * jnp.exp(-b[d])
            for c in range(G):
                sl = slice(c * C, (c + 1) * C)
                b_c = b[d][sl]
                b_edge = b_c[edges[d]:edges[d] + 1, :]
                decay[d, c] = jnp.exp(b_edge)
                kend = (k[d][sl] * jnp.exp(b_edge - b_c)).astype(BF16)
                kstack = jnp.where(kmask, jnp.concatenate([kt[sl]] * H, axis=0), 0.0).astype(BF16)
                a[d, c] = jnp.where(amasks[d], _dot_nt(qt[d][sl], kstack), 0.0).astype(BF16)
                ds[d, c] = _dot(v[d][sl].T.astype(BF16), kend)
        for d in dirs:
            s = st_ref[d]
            s_in = {}
            for c in orders[d]:
                s_in[c] = s.astype(BF16)
                s = s * decay[d, c] + jnp.where(smask, ds[d, c], 0.0)
            st_ref[d] = s
            out_ref = y_ref if d else of_ref
            for c in range(G):
                sl = slice(c * C, (c + 1) * C)
                vstack = jnp.where(vmask, jnp.concatenate([v[d][sl]] * H, axis=0), 0.0).astype(BF16)
                o = _dot(a[d, c], vstack) + _dot_nt(qt[d][sl], s_in[c])
                out_ref[pl.ds(pl.multiple_of(r0[d] + c * C, C), C), :] = o
        return 0

    lax.fori_loop(0, nblk, step, 0)

    norm_bd = _block_diag_ones(GLA_V, GLA_DV, BF16)

    def finish(t, _):
        rows = pl.ds(pl.multiple_of(t * R, R), R)
        tot = of_ref[rows, :] + y_ref[rows, :]
        ms = _dot((tot * tot).astype(BF16), norm_bd) * (1.0 / GLA_DV)
        y_ref[rows, :] = tot * lax.rsqrt(ms + NORM_EPS) * on_ref[...] * go_ref[rows, :].astype(F32)
        return 0

    lax.fori_loop(0, nblk, finish, 0)


def _gla(qk, g, v, go, onorm, batch):
    n = qk.shape[0]
    L = n // batch

    def seq(width):
        return pl.BlockSpec((L, width), lambda b: (b, 0))

    return pl.pallas_call(
        _gla_block_kernel,
        grid=(batch,),
        in_specs=[seq(2 * GLA_Q), seq(2 * GLA_Q), seq(GLA_V), seq(GLA_V), _const_spec((1, GLA_V))],
        out_specs=seq(GLA_V),
        out_shape=jax.ShapeDtypeStruct((n, GLA_V), F32),
        scratch_shapes=[pltpu.VMEM((L, GLA_V), F32), pltpu.VMEM((2, GLA_V, GLA_Q), F32)],
        compiler_params=_cparams(("parallel",)),
        name="gla",
    )(qk, g, v, go, onorm)


def _bias_kernel(rb_ref, o_ref):
    T = ATTN_TILE
    h = pl.program_id(0)
    nb = NUM_BUCKETS // 2
    max_exact = nb // 2
    for d in range(5):
        rel = (lax.broadcasted_iota(jnp.int32, (T, T), 0) - lax.broadcasted_iota(jnp.int32, (T, T), 1)
               + (d - 2) * T)
        ret = jnp.where(rel > 0, nb, 0)
        n = jnp.abs(rel)
        large = max_exact + (jnp.log(jnp.maximum(n, 1).astype(F32) / max_exact)
                             / math.log(MAX_DISTANCE / max_exact) * (nb - max_exact)).astype(jnp.int32)
        large = jnp.minimum(large, nb - 1)
        bucket = ret + jnp.where(n < max_exact, n, large)
        reachable = ([nb - 1] if d == 0 else range(nb) if d == 1 else range(NUM_BUCKETS) if d == 2
                     else range(nb, NUM_BUCKETS) if d == 3 else [NUM_BUCKETS - 1])
        for m in range(2):
            tile = jnp.zeros((T, T), F32)
            for bkt in reachable:
                tile = jnp.where(bucket == bkt, rb_ref[(bkt * N_DIFF_HEADS + h) * 2 + m], tile)
            o_ref[0, d, m] = tile * (-LOG2E)


def _bias_tiles(rel_bias):
    T = ATTN_TILE
    return pl.pallas_call(
        _bias_kernel,
        grid=(N_DIFF_HEADS,),
        in_specs=[pl.BlockSpec(memory_space=pltpu.SMEM)],
        out_specs=pl.BlockSpec((1, 5, 2, T, T), lambda h: (h, 0, 0, 0, 0)),
        out_shape=jax.ShapeDtypeStruct((N_DIFF_HEADS, 5, 2, T, T), F32),
        compiler_params=_cparams(("parallel",)),
        name="t5_bias",
    )(rel_bias.reshape(-1))


def _stage_conv_window(u_ref, uw_ref):
    L = u_ref.shape[0]
    P = CONV_PAD
    rows = L // N_DIFF_HEADS
    h = pl.program_id(0)
    start = pl.multiple_of(h * rows, rows)
    uw_ref[P:P + rows, :] = u_ref[pl.ds(start, rows), :]
    above = u_ref[pl.ds(pl.multiple_of(jnp.maximum(start - P, 0), P), P), :]
    uw_ref[:P, :] = jnp.where(h > 0, above, 0.0)
    below = u_ref[pl.ds(pl.multiple_of(jnp.minimum(start + rows, L - P), P), P), :]
    uw_ref[P + rows:, :] = jnp.where(h < N_DIFF_HEADS - 1, below, 0.0)


def _conv_rows(r0, R, uw_ref, w_ref, b_ref, g_ref, beta_ref, y_ref):
    P = CONV_PAD
    off = P - CONV_K // 2
    win = uw_ref[pl.ds(r0, R + 2 * P), :]
    acc = jnp.zeros((R, CONV_CH), F32) + b_ref[...]
    for r in range(8):
        shifted = win if r == 0 else pltpu.roll(win, R + 2 * P - r, 0)
        for k in range(CONV_K):
            if (off + k) % 8 == r:
                a8 = (off + k) // 8
                acc = acc + w_ref[k:k + 1, :] * shifted[8 * a8:8 * a8 + R, :]
    mu = jnp.mean(acc, axis=-1, keepdims=True)
    xc = acc - mu
    y = xc * lax.rsqrt(jnp.mean(xc * xc, axis=-1, keepdims=True) + NORM_EPS) * g_ref[...] + beta_ref[...]
    y_ref[pl.ds(r0, R), :] = _silu(y)


def _attn_kernel(*refs, lam_init):
    bound_ref = refs[1]
    nq = refs[2].shape[0] // ATTN_TILE
    unshifted_ok = bound_ref[0] <= EXP2_SAFE_RANGE
    _stage_conv_window(refs[8], refs[-1])

    def sweep(unshifted, unroll):
        def body(qi, _):
            _attn_query_tile(qi, *refs, lam_init=lam_init, unshifted=unshifted)
            return 0
        lax.fori_loop(0, nq, body, 0, unroll=unroll)

    @pl.when(unshifted_ok)
    def _():
        sweep(True, ATTN_UNROLL)

    @pl.when(jnp.logical_not(unshifted_ok))
    def _():
        sweep(False, 1)


def _attn_query_tile(qi, rb_ref, bound_ref, q_ref, k_ref, vt_ref, band_ref, lam_ref, on_ref,
                     u_ref, cw_ref, cb_ref, cg_ref, cbeta_ref, o_ref, yc_ref,
                     m_ref, l_ref, acc_ref, uw_ref, *, lam_init, unshifted):
    T = ATTN_TILE
    L = k_ref.shape[0]
    nk = L // T
    h = pl.program_id(0)
    q = q_ref[pl.ds(pl.multiple_of(qi * T, T), T), :]
    qt = q.astype(F32).T
    chan = lax.broadcasted_iota(jnp.int32, qt.shape, 0)
    qm = [jnp.where(chan < DIFF_DQK, qt, 0.0).astype(BF16), jnp.where(chan >= DIFF_DQK, qt, 0.0).astype(BF16)]

    def side_const(bucket, m):
        return rb_ref[(bucket * N_DIFF_HEADS + h) * 2 + m] * LOG2E

    def tile_start(j):
        return pl.multiple_of(j * T, T)

    def unshifted_softmax():
        f_left = [jnp.exp2(jnp.full((1, 1), side_const(NUM_BUCKETS // 2 - 1, m), F32)).astype(BF16)
                  for m in range(2)]
        f_right = [jnp.exp2(jnp.full((1, 1), side_const(NUM_BUCKETS - 1, m), F32)).astype(BF16)
                   for m in range(2)]
        band_idx = [jnp.where(qi == 0, 4, 1), 2, jnp.where(qi == nk - 1, 0, 3)]
        l = [jnp.zeros((1, T), F32) for _ in range(2)]
        acc = [jnp.zeros((DIFF_DV, T), F32) for _ in range(2)]

        def scores(t):
            kt = k_ref[pl.ds(tile_start((qi + (2 + t)) % nk), T), :]
            return [_dot(kt, qm[m]) for m in range(2)]

        pending = [scores(t) for t in range(ATTN_LOOKAHEAD)]
        for t in range(nk):
            j = (qi + (2 + t)) % nk
            vt = vt_ref[:, pl.ds(tile_start(j), T)]
            s_cur = pending.pop(0)
            if t + ATTN_LOOKAHEAD < nk:
                pending.append(scores(t + ATTN_LOOKAHEAD))
            for m in range(2):
                if t >= nk - 3:
                    p = jnp.exp2(s_cur[m] - band_ref[0, band_idx[t - (nk - 3)], m])
                    l[m] = l[m] + jnp.sum(p, axis=0, keepdims=True)
                    acc[m] = acc[m] + _dot(vt, p.astype(BF16))
                else:
                    f = jnp.where(j < qi, f_left[m], f_right[m])
                    p = jnp.exp2(s_cur[m])
                    l[m] = l[m] + f.astype(F32) * jnp.sum(p, axis=0, keepdims=True)
                    acc[m] = acc[m] + _dot(vt * f, p.astype(BF16))
        for m in range(2):
            l_ref[m] = l[m]
            acc_ref[m] = acc[m]

    def online_softmax():
        m_ref[...] = jnp.full(m_ref.shape, -jnp.inf, F32)
        l_ref[...] = jnp.zeros(l_ref.shape, F32)
        acc_ref[...] = jnp.zeros(acc_ref.shape, F32)

        def body(j, _):
            r0 = tile_start(j)
            kt = k_ref[pl.ds(r0, T), :]
            vt = vt_ref[:, pl.ds(r0, T)]
            idx = jnp.clip(j - qi, -2, 2) + 2
            ss = [_dot(kt, qm[m]) for m in range(2)]
            for m in range(2):
                s = ss[m] - band_ref[0, idx, m]
                m_old = m_ref[m]
                m_new = jnp.maximum(m_old, jnp.max(s, axis=0, keepdims=True))
                alpha = jnp.exp2(m_old - m_new)
                p = jnp.exp2(s - m_new)
                l_ref[m] = alpha * l_ref[m] + jnp.sum(p, axis=0, keepdims=True)
                acc_ref[m] = alpha * acc_ref[m] + _dot(vt, p.astype(BF16))
                m_ref[m] = m_new
            return 0

        lax.fori_loop(0, nk, body, 0)

    if unshifted:
        unshifted_softmax()
    else:
        online_softmax()

    conv_rows = L // N_DIFF_HEADS // nk
    _conv_rows(pl.multiple_of(qi * conv_rows, conv_rows), conv_rows, uw_ref, cw_ref, cb_ref, cg_ref, cbeta_ref, yc_ref)
    lp = lam_ref[...]
    lam = (jnp.exp(jnp.sum(lp[0:1] * lp[1:2], axis=-1, keepdims=True))
           - jnp.exp(jnp.sum(lp[2:3] * lp[3:4], axis=-1, keepdims=True)) + lam_init)
    o = acc_ref[0] * (1.0 / l_ref[0]) - acc_ref[1] * (lam / l_ref[1])
    y = o * lax.rsqrt(jnp.mean(o * o, axis=0, keepdims=True) + NORM_EPS) * on_ref[...] * (1.0 - lam_init)
    o_ref[pl.ds(pl.multiple_of(qi * T, T), T), :] = y.T.astype(o_ref.dtype)


def _diff_attn(rel_bias_flat, score_bound, dq, dk, dvt, band, lam_p, onorm, cu, conv_w, conv_b, conv_g,
               conv_beta, batch, lam_init):
    n = dq.shape[0]
    L = n // batch
    T = ATTN_TILE
    share = L // N_DIFF_HEADS
    seq = pl.BlockSpec((L, 2 * DIFF_DQK), lambda h, b: (b, h))

    def const(shape):
        return pl.BlockSpec(shape, lambda h, b: (0,) * len(shape))

    return pl.pallas_call(
        functools.partial(_attn_kernel, lam_init=lam_init),
        grid=(N_DIFF_HEADS, batch),
        in_specs=[pl.BlockSpec(memory_space=pltpu.SMEM),
                  pl.BlockSpec(memory_space=pltpu.SMEM),
                  seq, seq,
                  pl.BlockSpec((DIFF_DV, L), lambda h, b: (h, b)),
                  pl.BlockSpec((1, 5, 2, T, T), lambda h, b: (h, 0, 0, 0, 0)),
                  const((4, DIFF_DQK)), const((DIFF_DV, 1)),
                  pl.BlockSpec((L, CONV_CH), lambda h, b: (b, 0)),
                  const((CONV_K + 1, CONV_CH)), const((1, CONV_CH)), const((1, CONV_CH)), const((1, CONV_CH))],
        out_specs=[pl.BlockSpec((L, DIFF_DV), lambda h, b: (b, h)),
                   pl.BlockSpec((share, CONV_CH), lambda h, b: (b * N_DIFF_HEADS + h, 0))],
        out_shape=[jax.ShapeDtypeStruct((n, DIFF_V), BF16), jax.ShapeDtypeStruct((n, CONV_CH), F32)],
        scratch_shapes=[pltpu.VMEM((2, 1, T), F32), pltpu.VMEM((2, 1, T), F32),
                        pltpu.VMEM((2, DIFF_DV, T), F32),
                        pltpu.VMEM((share + 2 * CONV_PAD, CONV_CH), F32)],
        compiler_params=_cparams(("parallel", "parallel")),
        name="diff_attn",
    )(rel_bias_flat, score_bound, dq, dk, dvt, band, lam_p, onorm, cu, conv_w, conv_b, conv_g, conv_beta)


def _post_kernel(x_ref, ya_ref, yb_ref, yc_ref, p_ref, wo_hbm, fg_ref, wg_hbm, wu_hbm, wd_hbm,
                 pg_ref, pwg_hbm, pwp_hbm, o_ref,
                 wo_ref, wg_ref, wu_ref, wd_ref, pwg_ref, pwp_ref, wide_ref, wide_sem, tall_ref, tall_sem,
                 *, layer):
    @pl.when(pl.program_id(0) == 0)
    def _():
        _fetch_as_bf16(wo_hbm, layer, wo_ref, tall_ref, tall_sem)
        _fetch_as_bf16(wg_hbm, layer, wg_ref, wide_ref, wide_sem)
        _fetch_as_bf16(wu_hbm, layer, wu_ref, wide_ref, wide_sem)
        _fetch_as_bf16(wd_hbm, layer, wd_ref, tall_ref, tall_sem)
        _fetch_as_bf16(pwg_hbm, layer, pwg_ref, tall_ref, tall_sem)
        _fetch_as_bf16(pwp_hbm, layer, pwp_ref, tall_ref, tall_sem)

    mix = _dot(ya_ref[...].astype(BF16), wo_ref[:GLA_V, :])
    mix = mix + _dot(yb_ref[...].astype(BF16), wo_ref[GLA_V:GLA_V + DIFF_V, :])
    mix = mix + _dot(yc_ref[...].astype(BF16), wo_ref[GLA_V + DIFF_V:, :])
    x = _swiglu_half_step(x_ref[...] + mix, fg_ref, wg_ref, wu_ref, wd_ref)
    h = _rms_rows(x, pg_ref[...]).astype(BF16)
    gate = _sigmoid(_dot(h, pwg_ref[...]))
    o_ref[...] = x + gate * _dot(p_ref[...].astype(BF16), pwp_ref[...])


def _post(x, ya, yb, yc, p, w_out, ffn_gain, wg, wu, wd, ple_gain, ple_wg, ple_wp, layer):
    n = x.shape[0]
    tm = TOKEN_TILE

    def row(width):
        return pl.BlockSpec((tm, width), lambda i: (i, 0))

    return pl.pallas_call(
        functools.partial(_post_kernel, layer=layer),
        grid=(n // tm,),
        in_specs=[row(D_MODEL), row(GLA_V), row(DIFF_V), row(CONV_CH),
                  pl.BlockSpec((None, tm, PLE_DIM), lambda i: (layer, i, 0)),
                  _hbm_spec(), _const_spec((1, D_MODEL)), _hbm_spec(), _hbm_spec(), _hbm_spec(),
                  _const_spec((1, D_MODEL)), _hbm_spec(), _hbm_spec()],
        out_specs=row(D_MODEL),
        out_shape=jax.ShapeDtypeStruct((n, D_MODEL), F32),
        scratch_shapes=[pltpu.VMEM((D_MODEL, D_MODEL), BF16), pltpu.VMEM((D_MODEL, D_FF), BF16),
                        pltpu.VMEM((D_MODEL, D_FF), BF16), pltpu.VMEM((D_FF, D_MODEL), BF16),
                        pltpu.VMEM((D_MODEL, D_MODEL), BF16), pltpu.VMEM((PLE_DIM, D_MODEL), BF16)]
        + _stage_scratch(D_FF) + _stage_scratch(D_MODEL),
        compiler_params=_cparams(("arbitrary",)),
        name="post",
    )(x, ya, yb, yc, p, w_out, ffn_gain, wg, wu, wd, ple_gain, ple_wg, ple_wp)


def _pad_up(up, row0):
    return jnp.zeros((128, GLA_Q), F32).at[row0:row0 + GLA_RANK].set(up).astype(BF16)


def kernel(x, p, ffn1_norm, ffn1_w_gate, ffn1_w_up, ffn1_w_down, mix_norm, w_in, w_out, gla_gk_up_f, gla_gk_bias_f, gla_gk_up_b, gla_gk_bias_b, gla_out_norm, diff_q_norm, diff_k_norm, diff_lambda, diff_out_norm, rel_bias, conv_dw_w, conv_dw_b, conv_norm_g, conv_norm_b, ffn2_norm, ffn2_w_gate, ffn2_w_up, ffn2_w_down, ple_norm, ple_w_gate, ple_w_proj):
    B, L, _ = x.shape
    depth = w_in.shape[0]
    n = B * L
    xs = x.reshape(n, D_MODEL)
    rb_flat = rel_bias.reshape(-1)
    band = _bias_tiles(rel_bias)

    def row(v):
        return v.reshape(1, -1)

    ffn1_w = (ffn1_w_gate, ffn1_w_up, ffn1_w_down)
    ffn2_w = (ffn2_w_gate, ffn2_w_up, ffn2_w_down)
    w_in_b = w_in.astype(BF16)
    p_rows = p.reshape(depth, n, PLE_DIM)

    for i in range(depth):
        xs = _ffn(xs, row(ffn1_norm[i]), *ffn1_w, i)

        gqk, gg, gv, go, dq, dk, dvt, cu = _inproj(
            xs, row(mix_norm[i]), w_in_b,
            _pad_up(gla_gk_up_f[i], 0), row(gla_gk_bias_f[i]),
            _pad_up(gla_gk_up_b[i], GLA_RANK), row(gla_gk_bias_b[i]),
            row(jnp.tile(diff_q_norm[i], DIFF_QK // DIFF_DQK)), row(jnp.tile(diff_k_norm[i], DIFF_QK // DIFF_DQK)),
            i)

        y_gla = _gla(gqk, gg, gv, go, row(jnp.tile(gla_out_norm[i], N_GLA_HEADS)), B)
        lam_init = 0.8 - 0.6 * math.exp(-0.3 * i)
        score_bound = LOG2E * (DIFF_DQK ** 0.5 * jnp.max(jnp.abs(diff_q_norm[i])) * jnp.max(jnp.abs(diff_k_norm[i]))
                               + jnp.max(jnp.abs(rel_bias)))
        conv_w = jnp.concatenate([conv_dw_w[i], jnp.zeros((1, CONV_CH), F32)], axis=0)
        y_diff, y_conv = _diff_attn(rb_flat, score_bound.reshape(1), dq, dk, dvt, band, diff_lambda[i],
                                    diff_out_norm[i].reshape(-1, 1), cu, conv_w, row(conv_dw_b[i]),
                                    row(conv_norm_g[i]), row(conv_norm_b[i]), B, lam_init)

        xs = _post(xs, y_gla, y_diff, y_conv, p_rows, w_out, row(ffn2_norm[i]), *ffn2_w,
                   row(ple_norm[i]), ple_w_gate, ple_w_proj, i)
    return xs.reshape(B, L, D_MODEL)
```

```python
import functools
import math

import jax
import jax.numpy as jnp
from jax import lax
from jax.experimental import pallas as pl
from jax.experimental.pallas import tpu as pltpu

F32 = jnp.float32
BF16 = jnp.bfloat16

D_MODEL = 1024
N_GLA_HEADS = 4
GLA_DK = 32
GLA_DV = 64
GLA_Q = N_GLA_HEADS * GLA_DK
GLA_V = N_GLA_HEADS * GLA_DV
GLA_RANK = 16
GATE_TEMP = 16.0
GLA_CHUNK = 64
GLA_BLOCK = 16
GLA_CUM_CHUNKS = 4
DIFF_DQK = 64
N_DIFF_HEADS = 4
DIFF_DV = 128
DIFF_QK = 512
DIFF_V = 512
NUM_BUCKETS = 32
MAX_DISTANCE = 128
CONV_CH = 256
CONV_K = 31
D_FF = 2816
PLE_DIM = 256
NORM_EPS = 1e-6
LOG2E = math.log2(math.e)
EXP2_SAFE_RANGE = 64.0
D_IN = 2848

TOKEN_TILE = 512
INPROJ_TILE = 1024
INPROJ_SUB = 256
FF_CHUNKS = (1536, 1280)
ATTN_TILE = 256
ATTN_LOOKAHEAD = 3
ATTN_UNROLL = 4
CONV_PAD = 16
V7X_VMEM_BYTES = 64 * 1024 * 1024
VMEM_LIMIT = V7X_VMEM_BYTES * 7 // 8
WEIGHT_CHUNK_ROWS_WIDE = 128
WEIGHT_CHUNK_ROWS_TALL = 256
WEIGHT_STAGE_SLOTS = 3


def _cparams(sem):
    return pltpu.CompilerParams(dimension_semantics=sem, vmem_limit_bytes=VMEM_LIMIT)


def _dot(a, b):
    return jnp.dot(a, b, preferred_element_type=F32)


def _dot_nt(a, b):
    return lax.dot_general(a, b, (((1,), (1,)), ((), ())), preferred_element_type=F32)


def _rms_rows(x, gain):
    return x * lax.rsqrt(jnp.mean(x * x, axis=-1, keepdims=True) + NORM_EPS) * gain


def _sigmoid(x):
    return 1.0 / (1.0 + jnp.exp(-x))


def _silu(x):
    return x * _sigmoid(x)


def _const_spec(shape):
    return pl.BlockSpec(shape, lambda *_: (0,) * len(shape), pipeline_mode=pl.Buffered(1))


def _layer_spec(shape, layer):
    return pl.BlockSpec((None,) + shape, lambda *_: (layer,) + (0,) * len(shape), pipeline_mode=pl.Buffered(1))


def _hbm_spec():
    return pl.BlockSpec(memory_space=pl.ANY)


def _stage_scratch(cols):
    rows = WEIGHT_CHUNK_ROWS_WIDE if cols > D_MODEL else WEIGHT_CHUNK_ROWS_TALL
    return [pltpu.VMEM((WEIGHT_STAGE_SLOTS, rows, cols), F32), pltpu.SemaphoreType.DMA((WEIGHT_STAGE_SLOTS,))]


def _fetch_as_bf16(src_hbm, layer, dst_ref, stage_ref, sem_ref):
    slots = stage_ref.shape[0]
    chunk = min(stage_ref.shape[1], dst_ref.shape[0])
    assert dst_ref.shape[0] % chunk == 0
    n = dst_ref.shape[0] // chunk

    def dma(c):
        return pltpu.make_async_copy(src_hbm.at[layer, pl.ds(c * chunk, chunk), :],
                                     stage_ref.at[c % slots, pl.ds(0, chunk), :], sem_ref.at[c % slots])

    for c in range(min(slots - 1, n)):
        dma(c).start()
    for c in range(n):
        if c + slots - 1 < n:
            dma(c + slots - 1).start()
        dma(c).wait()
        dst_ref[c * chunk:(c + 1) * chunk, :] = stage_ref[c % slots, :chunk, :].astype(BF16)


def _swiglu_half_step(x, g_ref, wg_ref, wu_ref, wd_ref):
    h = _rms_rows(x, g_ref[...]).astype(BF16)
    acc = jnp.zeros(x.shape, F32)
    c0 = 0
    for width in FF_CHUNKS:
        gate = _dot(h, wg_ref[:, c0:c0 + width])
        up = _dot(h, wu_ref[:, c0:c0 + width])
        a = (_silu(gate) * up).astype(BF16)
        acc = acc + _dot(a, wd_ref[c0:c0 + width, :])
        c0 += width
    return x + 0.5 * acc


def _ffn_kernel(x_ref, g_ref, wg_hbm, wu_hbm, wd_hbm, o_ref,
                wg_ref, wu_ref, wd_ref, wide_ref, wide_sem, tall_ref, tall_sem, *, layer):
    @pl.when(pl.program_id(0) == 0)
    def _():
        _fetch_as_bf16(wg_hbm, layer, wg_ref, wide_ref, wide_sem)
        _fetch_as_bf16(wu_hbm, layer, wu_ref, wide_ref, wide_sem)
        _fetch_as_bf16(wd_hbm, layer, wd_ref, tall_ref, tall_sem)

    o_ref[...] = _swiglu_half_step(x_ref[...], g_ref, wg_ref, wu_ref, wd_ref)


def _ffn(x, gain, wg, wu, wd, layer):
    n = x.shape[0]
    row = pl.BlockSpec((TOKEN_TILE, D_MODEL), lambda i: (i, 0))
    return pl.pallas_call(
        functools.partial(_ffn_kernel, layer=layer),
        grid=(n // TOKEN_TILE,),
        in_specs=[row, _const_spec((1, D_MODEL)), _hbm_spec(), _hbm_spec(), _hbm_spec()],
        out_specs=row,
        out_shape=jax.ShapeDtypeStruct((n, D_MODEL), F32),
        scratch_shapes=[pltpu.VMEM((D_MODEL, D_FF), BF16), pltpu.VMEM((D_MODEL, D_FF), BF16),
                        pltpu.VMEM((D_FF, D_MODEL), BF16)] + _stage_scratch(D_FF) + _stage_scratch(D_MODEL),
        compiler_params=_cparams(("arbitrary",)),
        name="ffn",
    )(x, gain, wg, wu, wd)


def _block_diag_ones(n, blk, dtype):
    r = lax.broadcasted_iota(jnp.int32, (n, n), 0) // blk
    c = lax.broadcasted_iota(jnp.int32, (n, n), 1) // blk
    return jnp.where(r == c, 1.0, 0.0).astype(dtype)


def _group_mean_sq(x, blk):
    n = x.shape[-1]
    return _dot((x * x).astype(BF16), _block_diag_ones(n, blk, BF16)) * (1.0 / blk)


def _log_sigmoid(z):
    return jnp.minimum(z, 0.0) - jnp.log(1.0 + jnp.exp(-jnp.abs(z)))


def _inproj_kernel(x_ref, g_ref, w_ref, upf_ref, bf_ref, upb_ref, bb_ref, qn_ref, kn_ref,
                   gqk_ref, gg_ref, gv_ref, go_ref, dq_ref, dk_ref, dvt_ref, cu_ref):
    tm = x_ref.shape[0]
    nsub = tm // INPROJ_SUB

    def project(s):
        rows = slice(s * INPROJ_SUB, (s + 1) * INPROJ_SUB)
        h = _rms_rows(x_ref[rows, :], g_ref[...]).astype(BF16)
        return _dot(h, w_ref[...])

    def prepare(s, proj):
        rows = slice(s * INPROJ_SUB, (s + 1) * INPROJ_SUB)
        gqk_ref[rows, :GLA_Q] = proj[:, :GLA_Q] * (GLA_DK ** -0.5)
        gqk_ref[rows, GLA_Q:] = proj[:, GLA_Q:2 * GLA_Q]
        gv_ref[rows, :] = proj[:, 256:512].astype(BF16)
        go_ref[rows, :] = _silu(proj[:, 512:768]).astype(BF16)
        low = proj[:, 768:896].astype(BF16)
        zf = _dot(low, upf_ref[...]) + bf_ref[...]
        zb = _dot(low, upb_ref[...]) + bb_ref[...]
        gg_ref[rows, :GLA_Q] = _log_sigmoid(zf) / GATE_TEMP
        gg_ref[rows, GLA_Q:] = _log_sigmoid(zb) / GATE_TEMP
        rest = proj[:, 768 + 2 * GLA_RANK:]
        dq = rest[:, :512]
        dq_ref[rows, :] = (dq * lax.rsqrt(_group_mean_sq(dq, DIFF_DQK) + NORM_EPS) * qn_ref[...]
                           * (DIFF_DQK ** -0.5 * LOG2E)).astype(BF16)
        dk = rest[:, 512:1024]
        dk_ref[rows, :] = (dk * lax.rsqrt(_group_mean_sq(dk, DIFF_DQK) + NORM_EPS) * kn_ref[...]).astype(BF16)
        dvt_ref[:, rows] = rest[:, 1024:1536].T.astype(BF16)
        cu_ref[rows, :] = rest[:, 1536:1792] * _sigmoid(rest[:, 1792:2048])

    nxt = project(0)
    for s in range(nsub):
        cur = nxt
        if s + 1 < nsub:
            nxt = project(s + 1)
        prepare(s, cur)


def _inproj(x, gain, w, upf, bf, upb, bb, qn, kn, layer):
    n = x.shape[0]
    tm = INPROJ_TILE

    def row(width):
        return pl.BlockSpec((tm, width), lambda i: (i, 0))

    outs = [(2 * GLA_Q, F32), (2 * GLA_Q, F32), (GLA_V, BF16), (GLA_V, BF16),
            (DIFF_QK, BF16), (DIFF_QK, BF16), None, (CONV_CH, F32)]
    dvt_spec = pl.BlockSpec((DIFF_V, tm), lambda i: (0, i))
    return pl.pallas_call(
        _inproj_kernel,
        grid=(n // tm,),
        in_specs=[row(D_MODEL), _const_spec((1, D_MODEL)), _layer_spec((D_MODEL, D_IN), layer),
                  _const_spec((128, GLA_Q)), _const_spec((1, GLA_Q)),
                  _const_spec((128, GLA_Q)), _const_spec((1, GLA_Q)),
                  _const_spec((1, DIFF_QK)), _const_spec((1, DIFF_QK))],
        out_specs=[row(o[0]) if o else dvt_spec for o in outs],
        out_shape=[jax.ShapeDtypeStruct((n, o[0]), o[1]) if o else jax.ShapeDtypeStruct((DIFF_V, n), BF16)
                   for o in outs],
        compiler_params=_cparams(("parallel",)),
        name="inproj",
    )(x, gain, w, upf, bf, upb, bb, qn, kn)


def _gla_block_kernel(qk_ref, g_ref, v_ref, go_ref, on_ref, y_ref, of_ref, st_ref):
    C = GLA_CHUNK
    G = GLA_BLOCK
    R = G * C
    L = qk_ref.shape[0]
    nblk = L // R
    H = N_GLA_HEADS

    def iota(shape, d):
        return lax.broadcasted_iota(jnp.int32, shape, d)

    kmask = iota((H * C, GLA_Q), 0) // C == iota((H * C, GLA_Q), 1) // GLA_DK
    vmask = iota((H * C, GLA_V), 0) // C == iota((H * C, GLA_V), 1) // GLA_DV
    smask = iota((GLA_V, GLA_Q), 0) // GLA_DV == iota((GLA_V, GLA_Q), 1) // GLA_DK
    pos_a = iota((C, H * C), 0)
    pos_b = iota((C, H * C), 1) % C
    RC = GLA_CUM_CHUNKS * C
    ri, ci = iota((RC, RC), 0), iota((RC, RC), 1)
    same_chunk = ri // C == ci // C
    cums = [jnp.where(jnp.logical_and(same_chunk, ci <= ri), 1.0, 0.0).astype(BF16),
            jnp.where(jnp.logical_and(same_chunk, ci >= ri), 1.0, 0.0).astype(BF16)]
    amasks = [pos_b <= pos_a, pos_b >= pos_a]
    edges = [C - 1, 0]
    orders = [list(range(G)), list(range(G - 1, -1, -1))]

    st_ref[...] = jnp.zeros(st_ref.shape, F32)

    def step(t, _):
        r0 = [pl.multiple_of(t * R, R), pl.multiple_of((nblk - 1 - t) * R, R)]
        rows = [pl.ds(r0[0], R), pl.ds(r0[1], R)]
        dirs = (0, 1)
        b, k, v, qt = {}, {}, {}, {}
        for d in dirs:
            g = g_ref[rows[d], d * GLA_Q:(d + 1) * GLA_Q]
            g_hi = g.astype(BF16)
            g_lo = (g - g_hi.astype(F32)).astype(BF16)
            b[d] = jnp.concatenate([_dot(cums[d], g_hi[i:i + RC]) + _dot(cums[d], g_lo[i:i + RC])
                                    for i in range(0, R, RC)], axis=0)
        a, ds, decay = {}, {}, {}
        for d in dirs:
            q = qk_ref[rows[d], :GLA_Q]
            k[d] = qk_ref[rows[d], GLA_Q:]
            v[d] = v_ref[rows[d], :].astype(F32)
            qt[d] = (q * jnp.exp(b[d])).astype(BF16)
            kt = k[d] * jnp.exp(-b[d])
            for c in range(G):
                sl = slice(c * C, (c + 1) * C)
                b_c = b[d][sl]
                b_edge = b_c[edges[d]:edges[d] + 1, :]
                decay[d, c] = jnp.exp(b_edge)
                kend = (k[d][sl] * jnp.exp(b_edge - b_c)).astype(BF16)
                kstack = jnp.where(kmask, jnp.concatenate([kt[sl]] * H, axis=0), 0.0).astype(BF16)
                a[d, c] = jnp.where(amasks[d], _dot_nt(qt[d][sl], kstack), 0.0).astype(BF16)
                ds[d, c] = _dot(v[d][sl].T.astype(BF16), kend)
        for d in dirs:
            s = st_ref[d]
            s_in = {}
            for c in orders[d]:
                s_in[c] = s.astype(BF16)
                s = s * decay[d, c] + jnp.where(smask, ds[d, c], 0.0)
            st_ref[d] = s
            out_ref = y_ref if d else of_ref
            for c in range(G):
                sl = slice(c * C, (c + 1) * C)
                vstack = jnp.where(vmask, jnp.concatenate([v[d][sl]] * H, axis=0), 0.0).astype(BF16)
                o = _dot(a[d, c], vstack) + _dot_nt(qt[d][sl], s_in[c])
                out_ref[pl.ds(pl.multiple_of(r0[d] + c * C, C), C), :] = o
        return 0

    lax.fori_loop(0, nblk, step, 0)

    norm_bd = _block_diag_ones(GLA_V, GLA_DV, BF16)

    def finish(t, _):
        rows = pl.ds(pl.multiple_of(t * R, R), R)
        tot = of_ref[rows, :] + y_ref[rows, :]
        ms = _dot((tot * tot).astype(BF16), norm_bd) * (1.0 / GLA_DV)
        y_ref[rows, :] = tot * lax.rsqrt(ms + NORM_EPS) * on_ref[...] * go_ref[rows, :].astype(F32)
        return 0

    lax.fori_loop(0, nblk, finish, 0)


def _gla(qk, g, v, go, onorm, batch):
    n = qk.shape[0]
    L = n // batch

    def seq(width):
        return pl.BlockSpec((L, width), lambda b: (b, 0))

    return pl.pallas_call(
        _gla_block_kernel,
        grid=(batch,),
        in_specs=[seq(2 * GLA_Q), seq(2 * GLA_Q), seq(GLA_V), seq(GLA_V), _const_spec((1, GLA_V))],
        out_specs=seq(GLA_V),
        out_shape=jax.ShapeDtypeStruct((n, GLA_V), F32),
        scratch_shapes=[pltpu.VMEM((L, GLA_V), F32), pltpu.VMEM((2, GLA_V, GLA_Q), F32)],
        compiler_params=_cparams(("parallel",)),
        name="gla",
    )(qk, g, v, go, onorm)


def _bias_kernel(rb_ref, o_ref):
    T = ATTN_TILE
    h = pl.program_id(0)
    nb = NUM_BUCKETS // 2
    max_exact = nb // 2
    lane = lax.broadcasted_iota(jnp.int32, (8, 2 * T), 1)
    for d in range(5):
        rel = (d - 2) * T + T - lane
        ret = jnp.where(rel > 0, nb, 0)
        n = jnp.abs(rel)
        large = max_exact + (jnp.log(jnp.maximum(n, 1).astype(F32) / max_exact)
                             / math.log(MAX_DISTANCE / max_exact) * (nb - max_exact)).astype(jnp.int32)
        large = jnp.minimum(large, nb - 1)
        bucket = ret + jnp.where(n < max_exact, n, large)
        for m in range(2):
            row = jnp.zeros((8, 2 * T), F32)
            for bkt in range(NUM_BUCKETS):
                row = jnp.where(bucket == bkt, rb_ref[(bkt * N_DIFF_HEADS + h) * 2 + m], row)
            rows = jnp.broadcast_to(row[0:1, :] * (-LOG2E), (T, 2 * T))
            o_ref[0, d, m] = pltpu.roll(rows, 0, 1, stride=1, stride_axis=0)[:, T:]


def _bias_tiles(rel_bias):
    T = ATTN_TILE
    return pl.pallas_call(
        _bias_kernel,
        grid=(N_DIFF_HEADS,),
        in_specs=[pl.BlockSpec(memory_space=pltpu.SMEM)],
        out_specs=pl.BlockSpec((1, 5, 2, T, T), lambda h: (h, 0, 0, 0, 0)),
        out_shape=jax.ShapeDtypeStruct((N_DIFF_HEADS, 5, 2, T, T), F32),
        compiler_params=_cparams(("parallel",)),
        name="t5_bias",
    )(rel_bias.reshape(-1))


def _stage_conv_window(u_ref, uw_ref):
    L = u_ref.shape[0]
    P = CONV_PAD
    rows = L // N_DIFF_HEADS
    h = pl.program_id(0)
    start = pl.multiple_of(h * rows, rows)
    uw_ref[P:P + rows, :] = u_ref[pl.ds(start, rows), :]
    above = u_ref[pl.ds(pl.multiple_of(jnp.maximum(start - P, 0), P), P), :]
    uw_ref[:P, :] = jnp.where(h > 0, above, 0.0)
    below = u_ref[pl.ds(pl.multiple_of(jnp.minimum(start + rows, L - P), P), P), :]
    uw_ref[P + rows:, :] = jnp.where(h < N_DIFF_HEADS - 1, below, 0.0)


def _conv_rows(r0, R, uw_ref, w_ref, b_ref, g_ref, beta_ref, y_ref):
    P = CONV_PAD
    off = P - CONV_K // 2
    win = uw_ref[pl.ds(r0, R + 2 * P), :]
    acc = jnp.zeros((R, CONV_CH), F32) + b_ref[...]
    for r in range(8):
        shifted = win if r == 0 else pltpu.roll(win, R + 2 * P - r, 0)
        for k in range(CONV_K):
            if (off + k) % 8 == r:
                a8 = (off + k) // 8
                acc = acc + w_ref[k:k + 1, :] * shifted[8 * a8:8 * a8 + R, :]
    mu = jnp.mean(acc, axis=-1, keepdims=True)
    xc = acc - mu
    y = xc * lax.rsqrt(jnp.mean(xc * xc, axis=-1, keepdims=True) + NORM_EPS) * g_ref[...] + beta_ref[...]
    y_ref[pl.ds(r0, R), :] = _silu(y)


def _attn_kernel(*refs, lam_init):
    bound_ref = refs[1]
    nq = refs[2].shape[0] // ATTN_TILE
    unshifted_ok = bound_ref[0] <= EXP2_SAFE_RANGE
    _stage_conv_window(refs[8], refs[-1])

    def sweep(unshifted, unroll):
        def body(qi, _):
            _attn_query_tile(qi, *refs, lam_init=lam_init, unshifted=unshifted)
            return 0
        lax.fori_loop(0, nq, body, 0, unroll=unroll)

    @pl.when(unshifted_ok)
    def _():
        sweep(True, ATTN_UNROLL)

    @pl.when(jnp.logical_not(unshifted_ok))
    def _():
        sweep(False, 1)


def _attn_query_tile(qi, rb_ref, bound_ref, q_ref, k_ref, vt_ref, band_ref, lam_ref, on_ref,
                     u_ref, cw_ref, cb_ref, cg_ref, cbeta_ref, o_ref, yc_ref,
                     m_ref, l_ref, acc_ref, uw_ref, *, lam_init, unshifted):
    T = ATTN_TILE
    L = k_ref.shape[0]
    nk = L // T
    h = pl.program_id(0)
    q = q_ref[pl.ds(pl.multiple_of(qi * T, T), T), :]
    qt = q.astype(F32).T
    chan = lax.broadcasted_iota(jnp.int32, qt.shape, 0)
    qm = [jnp.where(chan < DIFF_DQK, qt, 0.0).astype(BF16), jnp.where(chan >= DIFF_DQK, qt, 0.0).astype(BF16)]

    def side_const(bucket, m):
        return rb_ref[(bucket * N_DIFF_HEADS + h) * 2 + m] * LOG2E

    def tile_start(j):
        return pl.multiple_of(j * T, T)

    def unshifted_softmax():
        f_left = [jnp.exp2(jnp.full((1, 1), side_const(NUM_BUCKETS // 2 - 1, m), F32)).astype(BF16)
                  for m in range(2)]
        f_right = [jnp.exp2(jnp.full((1, 1), side_const(NUM_BUCKETS - 1, m), F32)).astype(BF16)
                   for m in range(2)]
        band_idx = [jnp.where(qi == 0, 4, 1), 2, jnp.where(qi == nk - 1, 0, 3)]
        l = [jnp.zeros((1, T), F32) for _ in range(2)]
        acc = [jnp.zeros((DIFF_DV, T), F32) for _ in range(2)]

        def scores(t):
            kt = k_ref[pl.ds(tile_start((qi + (2 + t)) % nk), T), :]
            return [_dot(kt, qm[m]) for m in range(2)]

        pending = [scores(t) for t in range(ATTN_LOOKAHEAD)]
        for t in range(nk):
            j = (qi + (2 + t)) % nk
            vt = vt_ref[:, pl.ds(tile_start(j), T)]
            s_cur = pending.pop(0)
            if t + ATTN_LOOKAHEAD < nk:
                pending.append(scores(t + ATTN_LOOKAHEAD))
            for m in range(2):
                if t >= nk - 3:
                    p = jnp.exp2(s_cur[m] - band_ref[0, band_idx[t - (nk - 3)], m])
                    l[m] = l[m] + jnp.sum(p, axis=0, keepdims=True)
                    acc[m] = acc[m] + _dot(vt, p.astype(BF16))
                else:
                    f = jnp.where(j < qi, f_left[m], f_right[m])
                    p = jnp.exp2(s_cur[m])
                    l[m] = l[m] + f.astype(F32) * jnp.sum(p, axis=0, keepdims=True)
                    acc[m] = acc[m] + _dot(vt * f, p.astype(BF16))
        for m in range(2):
            l_ref[m] = l[m]
            acc_ref[m] = acc[m]

    def online_softmax():
        m_ref[...] = jnp.full(m_ref.shape, -jnp.inf, F32)
        l_ref[...] = jnp.zeros(l_ref.shape, F32)
        acc_ref[...] = jnp.zeros(acc_ref.shape, F32)

        def body(j, _):
            r0 = tile_start(j)
            kt = k_ref[pl.ds(r0, T), :]
            vt = vt_ref[:, pl.ds(r0, T)]
            idx = jnp.clip(j - qi, -2, 2) + 2
            ss = [_dot(kt, qm[m]) for m in range(2)]
            for m in range(2):
                s = ss[m] - band_ref[0, idx, m]
                m_old = m_ref[m]
                m_new = jnp.maximum(m_old, jnp.max(s, axis=0, keepdims=True))
                alpha = jnp.exp2(m_old - m_new)
                p = jnp.exp2(s - m_new)
                l_ref[m] = alpha * l_ref[m] + jnp.sum(p, axis=0, keepdims=True)
                acc_ref[m] = alpha * acc_ref[m] + _dot(vt, p.astype(BF16))
                m_ref[m] = m_new
            return 0

        lax.fori_loop(0, nk, body, 0)

    if unshifted:
        unshifted_softmax()
    else:
        online_softmax()

    conv_rows = L // N_DIFF_HEADS // nk
    _conv_rows(pl.multiple_of(qi * conv_rows, conv_rows), conv_rows, uw_ref, cw_ref, cb_ref, cg_ref, cbeta_ref, yc_ref)
    lp = lam_ref[...]
    lam = (jnp.exp(jnp.sum(lp[0:1] * lp[1:2], axis=-1, keepdims=True))
           - jnp.exp(jnp.sum(lp[2:3] * lp[3:4], axis=-1, keepdims=True)) + lam_init)
    o = acc_ref[0] * (1.0 / l_ref[0]) - acc_ref[1] * (lam / l_ref[1])
    y = o * lax.rsqrt(jnp.mean(o * o, axis=0, keepdims=True) + NORM_EPS) * on_ref[...] * (1.0 - lam_init)
    o_ref[pl.ds(pl.multiple_of(qi * T, T), T), :] = y.T.astype(o_ref.dtype)


def _diff_attn(rel_bias_flat, score_bound, dq, dk, dvt, band, lam_p, onorm, cu, conv_w, conv_b, conv_g,
               conv_beta, batch, lam_init):
    n = dq.shape[0]
    L = n // batch
    T = ATTN_TILE
    share = L // N_DIFF_HEADS
    seq = pl.BlockSpec((L, 2 * DIFF_DQK), lambda h, b: (b, h))

    def const(shape):
        return pl.BlockSpec(shape, lambda h, b: (0,) * len(shape))

    return pl.pallas_call(
        functools.partial(_attn_kernel, lam_init=lam_init),
        grid=(N_DIFF_HEADS, batch),
        in_specs=[pl.BlockSpec(memory_space=pltpu.SMEM),
                  pl.BlockSpec(memory_space=pltpu.SMEM),
                  seq, seq,
                  pl.BlockSpec((DIFF_DV, L), lambda h, b: (h, b)),
                  pl.BlockSpec((1, 5, 2, T, T), lambda h, b: (h, 0, 0, 0, 0)),
                  const((4, DIFF_DQK)), const((DIFF_DV, 1)),
                  pl.BlockSpec((L, CONV_CH), lambda h, b: (b, 0)),
                  const((CONV_K + 1, CONV_CH)), const((1, CONV_CH)), const((1, CONV_CH)), const((1, CONV_CH))],
        out_specs=[pl.BlockSpec((L, DIFF_DV), lambda h, b: (b, h)),
                   pl.BlockSpec((share, CONV_CH), lambda h, b: (b * N_DIFF_HEADS + h, 0))],
        out_shape=[jax.ShapeDtypeStruct((n, DIFF_V), BF16), jax.ShapeDtypeStruct((n, CONV_CH), F32)],
        scratch_shapes=[pltpu.VMEM((2, 1, T), F32), pltpu.VMEM((2, 1, T), F32),
                        pltpu.VMEM((2, DIFF_DV, T), F32),
                        pltpu.VMEM((share + 2 * CONV_PAD, CONV_CH), F32)],
        compiler_params=_cparams(("parallel", "parallel")),
        name="diff_attn",
    )(rel_bias_flat, score_bound, dq, dk, dvt, band, lam_p, onorm, cu, conv_w, conv_b, conv_g, conv_beta)


def _post_kernel(x_ref, ya_ref, yb_ref, yc_ref, p_ref, wo_hbm, fg_ref, wg_hbm, wu_hbm, wd_hbm,
                 pg_ref, pwg_hbm, pwp_hbm, o_ref,
                 wo_ref, wg_ref, wu_ref, wd_ref, pwg_ref, pwp_ref, wide_ref, wide_sem, tall_ref, tall_sem,
                 *, layer):
    @pl.when(pl.program_id(0) == 0)
    def _():
        _fetch_as_bf16(wo_hbm, layer, wo_ref, tall_ref, tall_sem)
        _fetch_as_bf16(wg_hbm, layer, wg_ref, wide_ref, wide_sem)
        _fetch_as_bf16(wu_hbm, layer, wu_ref, wide_ref, wide_sem)
        _fetch_as_bf16(wd_hbm, layer, wd_ref, tall_ref, tall_sem)
        _fetch_as_bf16(pwg_hbm, layer, pwg_ref, tall_ref, tall_sem)
        _fetch_as_bf16(pwp_hbm, layer, pwp_ref, tall_ref, tall_sem)

    mix = _dot(ya_ref[...].astype(BF16), wo_ref[:GLA_V, :])
    mix = mix + _dot(yb_ref[...].astype(BF16), wo_ref[GLA_V:GLA_V + DIFF_V, :])
    mix = mix + _dot(yc_ref[...].astype(BF16), wo_ref[GLA_V + DIFF_V:, :])
    x = _swiglu_half_step(x_ref[...] + mix, fg_ref, wg_ref, wu_ref, wd_ref)
    h = _rms_rows(x, pg_ref[...]).astype(BF16)
    gate = _sigmoid(_dot(h, pwg_ref[...]))
    o_ref[...] = x + gate * _dot(p_ref[...].astype(BF16), pwp_ref[...])


def _post(x, ya, yb, yc, p, w_out, ffn_gain, wg, wu, wd, ple_gain, ple_wg, ple_wp, layer):
    n = x.shape[0]
    tm = TOKEN_TILE

    def row(width):
        return pl.BlockSpec((tm, width), lambda i: (i, 0))

    return pl.pallas_call(
        functools.partial(_post_kernel, layer=layer),
        grid=(n // tm,),
        in_specs=[row(D_MODEL), row(GLA_V), row(DIFF_V), row(CONV_CH),
                  pl.BlockSpec((None, tm, PLE_DIM), lambda i: (layer, i, 0)),
                  _hbm_spec(), _const_spec((1, D_MODEL)), _hbm_spec(), _hbm_spec(), _hbm_spec(),
                  _const_spec((1, D_MODEL)), _hbm_spec(), _hbm_spec()],
        out_specs=row(D_MODEL),
        out_shape=jax.ShapeDtypeStruct((n, D_MODEL), F32),
        scratch_shapes=[pltpu.VMEM((D_MODEL, D_MODEL), BF16), pltpu.VMEM((D_MODEL, D_FF), BF16),
                        pltpu.VMEM((D_MODEL, D_FF), BF16), pltpu.VMEM((D_FF, D_MODEL), BF16),
                        pltpu.VMEM((D_MODEL, D_MODEL), BF16), pltpu.VMEM((PLE_DIM, D_MODEL), BF16)]
        + _stage_scratch(D_FF) + _stage_scratch(D_MODEL),
        compiler_params=_cparams(("arbitrary",)),
        name="post",
    )(x, ya, yb, yc, p, w_out, ffn_gain, wg, wu, wd, ple_gain, ple_wg, ple_wp)


def _pad_up(up, row0):
    return jnp.zeros((128, GLA_Q), F32).at[row0:row0 + GLA_RANK].set(up).astype(BF16)


def kernel(x, p, ffn1_norm, ffn1_w_gate, ffn1_w_up, ffn1_w_down, mix_norm, w_in, w_out, gla_gk_up_f, gla_gk_bias_f, gla_gk_up_b, gla_gk_bias_b, gla_out_norm, diff_q_norm, diff_k_norm, diff_lambda, diff_out_norm, rel_bias, conv_dw_w, conv_dw_b, conv_norm_g, conv_norm_b, ffn2_norm, ffn2_w_gate, ffn2_w_up, ffn2_w_down, ple_norm, ple_w_gate, ple_w_proj):
    B, L, _ = x.shape
    depth = w_in.shape[0]
    n = B * L
    xs = x.reshape(n, D_MODEL)
    rb_flat = rel_bias.reshape(-1)
    band = _bias_tiles(rel_bias)

    def row(v):
        return v.reshape(1, -1)

    ffn1_w = (ffn1_w_gate, ffn1_w_up, ffn1_w_down)
    ffn2_w = (ffn2_w_gate, ffn2_w_up, ffn2_w_down)
    w_in_b = w_in.astype(BF16)
    p_rows = p.reshape(depth, n, PLE_DIM)

    for i in range(depth):
        xs = _ffn(xs, row(ffn1_norm[i]), *ffn1_w, i)

        gqk, gg, gv, go, dq, dk, dvt, cu = _inproj(
            xs, row(mix_norm[i]), w_in_b,
            _pad_up(gla_gk_up_f[i], 0), row(gla_gk_bias_f[i]),
            _pad_up(gla_gk_up_b[i], GLA_RANK), row(gla_gk_bias_b[i]),
            row(jnp.tile(diff_q_norm[i], DIFF_QK // DIFF_DQK)), row(jnp.tile(diff_k_norm[i], DIFF_QK // DIFF_DQK)),
            i)

        y_gla = _gla(gqk, gg, gv, go, row(jnp.tile(gla_out_norm[i], N_GLA_HEADS)), B)
        lam_init = 0.8 - 0.6 * math.exp(-0.3 * i)
        score_bound = LOG2E * (DIFF_DQK ** 0.5 * jnp.max(jnp.abs(diff_q_norm[i])) * jnp.max(jnp.abs(diff_k_norm[i]))
                               + jnp.max(jnp.abs(rel_bias)))
        conv_w = jnp.concatenate([conv_dw_w[i], jnp.zeros((1, CONV_CH), F32)], axis=0)
        y_diff, y_conv = _diff_attn(rb_flat, score_bound.reshape(1), dq, dk, dvt, band, diff_lambda[i],
                                    diff_out_norm[i].reshape(-1, 1), cu, conv_w, row(conv_dw_b[i]),
                                    row(conv_norm_g[i]), row(conv_norm_b[i]), B, lam_init)

        xs = _post(xs, y_gla, y_diff, y_conv, p_rows, w_out, row(ffn2_norm[i]), *ffn2_w,
                   row(ple_norm[i]), ple_w_gate, ple_w_proj, i)
    return xs.reshape(B, L, D_MODEL)
```

```python
import functools
import math

import jax
import jax.numpy as jnp
from jax import lax
from jax.experimental import pallas as pl
from jax.experimental.pallas import tpu as pltpu

F32 = jnp.float32
BF16 = jnp.bfloat16

D_MODEL = 1024
N_GLA_HEADS = 4
GLA_DK = 32
GLA_DV = 64
GLA_Q = N_GLA_HEADS * GLA_DK
GLA_V = N_GLA_HEADS * GLA_DV
GLA_RANK = 16
GATE_TEMP = 16.0
GLA_CHUNK = 64
GLA_BLOCK = 16
GLA_CUM_CHUNKS = 4
DIFF_DQK = 64
N_DIFF_HEADS = 4
DIFF_DV = 128
DIFF_QK = 512
DIFF_V = 512
NUM_BUCKETS = 32
MAX_DISTANCE = 128
CONV_CH = 256
CONV_K = 31
D_FF = 2816
PLE_DIM = 256
NORM_EPS = 1e-6
LOG2E = math.log2(math.e)
EXP2_SAFE_RANGE = 64.0
D_IN = 2848

TOKEN_TILE = 512
INPROJ_TILE = 1024
INPROJ_SUB = 256
FF_CHUNKS = (1536, 1280)
ATTN_TILE = 256
ATTN_LOOKAHEAD = 3
ATTN_UNROLL = 4
CONV_PAD = 16
V7X_VMEM_BYTES = 64 * 1024 * 1024
VMEM_LIMIT = V7X_VMEM_BYTES * 7 // 8
WEIGHT_CHUNK_ROWS_WIDE = 128
WEIGHT_CHUNK_ROWS_TALL = 256
WEIGHT_STAGE_SLOTS = 3


def _cparams(sem):
    return pltpu.CompilerParams(dimension_semantics=sem, vmem_limit_bytes=VMEM_LIMIT)


def _dot(a, b):
    return jnp.dot(a, b, preferred_element_type=F32)


def _dot_nt(a, b):
    return lax.dot_general(a, b, (((1,), (1,)), ((), ())), preferred_element_type=F32)


def _rms_rows(x, gain):
    return x * lax.rsqrt(jnp.mean(x * x, axis=-1, keepdims=True) + NORM_EPS) * gain


def _sigmoid(x):
    return 1.0 / (1.0 + jnp.exp(-x))


def _silu(x):
    return x * _sigmoid(x)


def _const_spec(shape):
    return pl.BlockSpec(shape, lambda *_: (0,) * len(shape), pipeline_mode=pl.Buffered(1))


def _layer_spec(shape, layer):
    return pl.BlockSpec((None,) + shape, lambda *_: (layer,) + (0,) * len(shape), pipeline_mode=pl.Buffered(1))


def _hbm_spec():
    return pl.BlockSpec(memory_space=pl.ANY)


def _stage_scratch(cols):
    rows = WEIGHT_CHUNK_ROWS_WIDE if cols > D_MODEL else WEIGHT_CHUNK_ROWS_TALL
    return [pltpu.VMEM((WEIGHT_STAGE_SLOTS, rows, cols), F32), pltpu.SemaphoreType.DMA((WEIGHT_STAGE_SLOTS,))]


def _fetch_as_bf16(src_hbm, layer, dst_ref, stage_ref, sem_ref):
    slots = stage_ref.shape[0]
    chunk = min(stage_ref.shape[1], dst_ref.shape[0])
    assert dst_ref.shape[0] % chunk == 0
    n = dst_ref.shape[0] // chunk

    def dma(c):
        return pltpu.make_async_copy(src_hbm.at[layer, pl.ds(c * chunk, chunk), :],
                                     stage_ref.at[c % slots, pl.ds(0, chunk), :], sem_ref.at[c % slots])

    for c in range(min(slots - 1, n)):
        dma(c).start()
    for c in range(n):
        if c + slots - 1 < n:
            dma(c + slots - 1).start()
        dma(c).wait()
        dst_ref[c * chunk:(c + 1) * chunk, :] = stage_ref[c % slots, :chunk, :].astype(BF16)


def _swiglu_half_step(x, g_ref, wg_ref, wu_ref, wd_ref):
    h = _rms_rows(x, g_ref[...]).astype(BF16)
    acc = jnp.zeros(x.shape, F32)
    c0 = 0
    for width in FF_CHUNKS:
        gate = _dot(h, wg_ref[:, c0:c0 + width])
        up = _dot(h, wu_ref[:, c0:c0 + width])
        a = (_silu(gate) * up).astype(BF16)
        acc = acc + _dot(a, wd_ref[c0:c0 + width, :])
        c0 += width
    return x + 0.5 * acc


def _ffn_kernel(x_ref, g_ref, wg_hbm, wu_hbm, wd_hbm, o_ref,
                wg_ref, wu_ref, wd_ref, wide_ref, wide_sem, tall_ref, tall_sem, *, layer):
    @pl.when(pl.program_id(0) == 0)
    def _():
        _fetch_as_bf16(wg_hbm, layer, wg_ref, wide_ref, wide_sem)
        _fetch_as_bf16(wu_hbm, layer, wu_ref, wide_ref, wide_sem)
        _fetch_as_bf16(wd_hbm, layer, wd_ref, tall_ref, tall_sem)

    o_ref[...] = _swiglu_half_step(x_ref[...], g_ref, wg_ref, wu_ref, wd_ref)


def _ffn(x, gain, wg, wu, wd, layer):
    n = x.shape[0]
    row = pl.BlockSpec((TOKEN_TILE, D_MODEL), lambda i: (i, 0))
    return pl.pallas_call(
        functools.partial(_ffn_kernel, layer=layer),
        grid=(n // TOKEN_TILE,),
        in_specs=[row, _const_spec((1, D_MODEL)), _hbm_spec(), _hbm_spec(), _hbm_spec()],
        out_specs=row,
        out_shape=jax.ShapeDtypeStruct((n, D_MODEL), F32),
        scratch_shapes=[pltpu.VMEM((D_MODEL, D_FF), BF16), pltpu.VMEM((D_MODEL, D_FF), BF16),
                        pltpu.VMEM((D_FF, D_MODEL), BF16)] + _stage_scratch(D_FF) + _stage_scratch(D_MODEL),
        compiler_params=_cparams(("arbitrary",)),
        name="ffn",
    )(x, gain, wg, wu, wd)


def _block_diag_ones(n, blk, dtype):
    r = lax.broadcasted_iota(jnp.int32, (n, n), 0) // blk
    c = lax.broadcasted_iota(jnp.int32, (n, n), 1) // blk
    return jnp.where(r == c, 1.0, 0.0).astype(dtype)


def _group_mean_sq(x, blk):
    n = x.shape[-1]
    return _dot((x * x).astype(BF16), _block_diag_ones(n, blk, BF16)) * (1.0 / blk)


def _log_sigmoid(z):
    return jnp.minimum(z, 0.0) - jnp.log(1.0 + jnp.exp(-jnp.abs(z)))


def _inproj_kernel(x_ref, g_ref, w_ref, upf_ref, bf_ref, upb_ref, bb_ref, qn_ref, kn_ref,
                   gqk_ref, gg_ref, gv_ref, go_ref, dq_ref, dk_ref, dvt_ref, cu_ref):
    tm = x_ref.shape[0]
    nsub = tm // INPROJ_SUB

    def project(s):
        rows = slice(s * INPROJ_SUB, (s + 1) * INPROJ_SUB)
        h = _rms_rows(x_ref[rows, :], g_ref[...]).astype(BF16)
        return _dot(h, w_ref[...])

    def prepare(s, proj):
        rows = slice(s * INPROJ_SUB, (s + 1) * INPROJ_SUB)
        gqk_ref[rows, :GLA_Q] = proj[:, :GLA_Q] * (GLA_DK ** -0.5)
        gqk_ref[rows, GLA_Q:] = proj[:, GLA_Q:2 * GLA_Q]
        gv_ref[rows, :] = proj[:, 256:512].astype(BF16)
        go_ref[rows, :] = _silu(proj[:, 512:768]).astype(BF16)
        low = proj[:, 768:896].astype(BF16)
        zf = _dot(low, upf_ref[...]) + bf_ref[...]
        zb = _dot(low, upb_ref[...]) + bb_ref[...]
        gg_ref[rows, :GLA_Q] = _log_sigmoid(zf) / GATE_TEMP
        gg_ref[rows, GLA_Q:] = _log_sigmoid(zb) / GATE_TEMP
        rest = proj[:, 768 + 2 * GLA_RANK:]
        dq = rest[:, :512]
        dq_ref[rows, :] = (dq * lax.rsqrt(_group_mean_sq(dq, DIFF_DQK) + NORM_EPS) * qn_ref[...]
                           * (DIFF_DQK ** -0.5 * LOG2E)).astype(BF16)
        dk = rest[:, 512:1024]
        dk_ref[rows, :] = (dk * lax.rsqrt(_group_mean_sq(dk, DIFF_DQK) + NORM_EPS) * kn_ref[...]).astype(BF16)
        dvt_ref[:, rows] = rest[:, 1024:1536].T.astype(BF16)
        cu_ref[rows, :] = rest[:, 1536:1792] * _sigmoid(rest[:, 1792:2048])

    nxt = project(0)
    for s in range(nsub):
        cur = nxt
        if s + 1 < nsub:
            nxt = project(s + 1)
        prepare(s, cur)


def _inproj(x, gain, w, upf, bf, upb, bb, qn, kn, layer):
    n = x.shape[0]
    tm = INPROJ_TILE

    def row(width):
        return pl.BlockSpec((tm, width), lambda i: (i, 0))

    outs = [(2 * GLA_Q, F32), (2 * GLA_Q, F32), (GLA_V, BF16), (GLA_V, BF16),
            (DIFF_QK, BF16), (DIFF_QK, BF16), None, (CONV_CH, F32)]
    dvt_spec = pl.BlockSpec((DIFF_V, tm), lambda i: (0, i))
    return pl.pallas_call(
        _inproj_kernel,
        grid=(n // tm,),
        in_specs=[row(D_MODEL), _const_spec((1, D_MODEL)), _layer_spec((D_MODEL, D_IN), layer),
                  _const_spec((128, GLA_Q)), _const_spec((1, GLA_Q)),
                  _const_spec((128, GLA_Q)), _const_spec((1, GLA_Q)),
                  _const_spec((1, DIFF_QK)), _const_spec((1, DIFF_QK))],
        out_specs=[row(o[0]) if o else dvt_spec for o in outs],
        out_shape=[jax.ShapeDtypeStruct((n, o[0]), o[1]) if o else jax.ShapeDtypeStruct((DIFF_V, n), BF16)
                   for o in outs],
        compiler_params=_cparams(("parallel",)),
        name="inproj",
    )(x, gain, w, upf, bf, upb, bb, qn, kn)


def _gla_block_kernel(qk_ref, g_ref, v_ref, go_ref, on_ref, y_ref, of_ref, st_ref):
    C = GLA_CHUNK
    G = GLA_BLOCK
    R = G * C
    L = qk_ref.shape[0]
    nblk = L // R
    H = N_GLA_HEADS

    def iota(shape, d):
        return lax.broadcasted_iota(jnp.int32, shape, d)

    kmask = iota((H * C, GLA_Q), 0) // C == iota((H * C, GLA_Q), 1) // GLA_DK
    vmask = iota((H * C, GLA_V), 0) // C == iota((H * C, GLA_V), 1) // GLA_DV
    smask = iota((GLA_V, GLA_Q), 0) // GLA_DV == iota((GLA_V, GLA_Q), 1) // GLA_DK
    pos_a = iota((C, H * C), 0)
    pos_b = iota((C, H * C), 1) % C
    RC = GLA_CUM_CHUNKS * C
    ri, ci = iota((RC, RC), 0), iota((RC, RC), 1)
    same_chunk = ri // C == ci // C
    cums = [jnp.where(jnp.logical_and(same_chunk, ci <= ri), 1.0, 0.0).astype(BF16),
            jnp.where(jnp.logical_and(same_chunk, ci >= ri), 1.0, 0.0).astype(BF16)]
    amasks = [pos_b <= pos_a, pos_b >= pos_a]
    edges = [C - 1, 0]
    orders = [list(range(G)), list(range(G - 1, -1, -1))]

    st_ref[...] = jnp.zeros(st_ref.shape, F32)

    def step(t, _):
        r0 = [pl.multiple_of(t * R, R), pl.multiple_of((nblk - 1 - t) * R, R)]
        rows = [pl.ds(r0[0], R), pl.ds(r0[1], R)]
        dirs = (0, 1)
        b, k, v, qt = {}, {}, {}, {}
        for d in dirs:
            g = g_ref[rows[d], d * GLA_Q:(d + 1) * GLA_Q]
            g_hi = g.astype(BF16)
            g_lo = (g - g_hi.astype(F32)).astype(BF16)
            b[d] = jnp.concatenate([_dot(cums[d], g_hi[i:i + RC]) + _dot(cums[d], g_lo[i:i + RC])
                                    for i in range(0, R, RC)], axis=0)
        a, ds, decay = {}, {}, {}
        for d in dirs:
            q = qk_ref[rows[d], :GLA_Q]
            k[d] = qk_ref[rows[d], GLA_Q:]
            v[d] = v_ref[rows[d], :].astype(F32)
            qt[d] = (q * jnp.exp(b[d])).astype(BF16)
            kt = k[d] * jnp.exp(-b[d])
            for c in range(G):
                sl = slice(c * C, (c + 1) * C)
                b_c = b[d][sl]
                b_edge = b_c[edges[d]:edges[d] + 1, :]
                decay[d, c] = jnp.exp(b_edge)
                kend = (k[d][sl] * jnp.exp(b_edge - b_c)).astype(BF16)
                kstack = jnp.where(kmask, jnp.concatenate([kt[sl]] * H, axis=0), 0.0).astype(BF16)
                a[d, c] = jnp.where(amasks[d], _dot_nt(qt[d][sl], kstack), 0.0).astype(BF16)
                ds[d, c] = _dot(v[d][sl].T.astype(BF16), kend)
        for d in dirs:
            s = st_ref[d]
            s_in = {}
            for c in orders[d]:
                s_in[c] = s.astype(BF16)
                s = s * decay[d, c] + jnp.where(smask, ds[d, c], 0.0)
            st_ref[d] = s
            out_ref = y_ref if d else of_ref
            for c in range(G):
                sl = slice(c * C, (c + 1) * C)
                vstack = jnp.where(vmask, jnp.concatenate([v[d][sl]] * H, axis=0), 0.0).astype(BF16)
                o = _dot(a[d, c], vstack) + _dot_nt(qt[d][sl], s_in[c])
                out_ref[pl.ds(pl.multiple_of(r0[d] + c * C, C), C), :] = o
        return 0

    lax.fori_loop(0, nblk, step, 0)

    norm_bd = _block_diag_ones(GLA_V, GLA_DV, BF16)

    def finish(t, _):
        rows = pl.ds(pl.multiple_of(t * R, R), R)
        tot = of_ref[rows, :] + y_ref[rows, :]
        ms = _dot((tot * tot).astype(BF16), norm_bd) * (1.0 / GLA_DV)
        y_ref[rows, :] = tot * lax.rsqrt(ms + NORM_EPS) * on_ref[...] * go_ref[rows, :].astype(F32)
        return 0

    lax.fori_loop(0, nblk, finish, 0)


def _gla(qk, g, v, go, onorm, batch):
    n = qk.shape[0]
    L = n // batch

    def seq(width):
        return pl.BlockSpec((L, width), lambda b: (b, 0))

    return pl.pallas_call(
        _gla_block_kernel,
        grid=(batch,),
        in_specs=[seq(2 * GLA_Q), seq(2 * GLA_Q), seq(GLA_V), seq(GLA_V), _const_spec((1, GLA_V))],
        out_specs=seq(GLA_V),
        out_shape=jax.ShapeDtypeStruct((n, GLA_V), F32),
        scratch_shapes=[pltpu.VMEM((L, GLA_V), F32), pltpu.VMEM((2, GLA_V, GLA_Q), F32)],
        compiler_params=_cparams(("parallel",)),
        name="gla",
    )(qk, g, v, go, onorm)


def _bias_kernel(rb_ref, o_ref):
    T = ATTN_TILE
    h = pl.program_id(0)
    nb = NUM_BUCKETS // 2
    max_exact = nb // 2
    lane = lax.broadcasted_iota(jnp.int32, (8, 2 * T), 1)
    for d in range(5):
        rel = (d - 2) * T + T - lane
        ret = jnp.where(rel > 0, nb, 0)
        n = jnp.abs(rel)
        large = max_exact + (jnp.log(jnp.maximum(n, 1).astype(F32) / max_exact)
                             / math.log(MAX_DISTANCE / max_exact) * (nb - max_exact)).astype(jnp.int32)
        large = jnp.minimum(large, nb - 1)
        bucket = ret + jnp.where(n < max_exact, n, large)
        for m in range(2):
            row = jnp.zeros((8, 2 * T), F32)
            for bkt in range(NUM_BUCKETS):
                row = jnp.where(bucket == bkt, rb_ref[(bkt * N_DIFF_HEADS + h) * 2 + m], row)
            rows = jnp.broadcast_to(row[0:1, :] * (-LOG2E), (T, 2 * T))
            o_ref[0, d, m] = pltpu.roll(rows, 0, 1, stride=1, stride_axis=0)[:, T:]


def _bias_tiles(rel_bias):
    T = ATTN_TILE
    return pl.pallas_call(
        _bias_kernel,
        grid=(N_DIFF_HEADS,),
        in_specs=[pl.BlockSpec(memory_space=pltpu.SMEM)],
        out_specs=pl.BlockSpec((1, 5, 2, T, T), lambda h: (h, 0, 0, 0, 0)),
        out_shape=jax.ShapeDtypeStruct((N_DIFF_HEADS, 5, 2, T, T), F32),
        compiler_params=_cparams(("parallel",)),
        name="t5_bias",
    )(rel_bias.reshape(-1))


def _stage_conv_window(u_ref, uw_ref):
    L = u_ref.shape[0]
    P = CONV_PAD
    rows = L // N_DIFF_HEADS
    h = pl.program_id(0)
    start = pl.multiple_of(h * rows, rows)
    uw_ref[P:P + rows, :] = u_ref[pl.ds(start, rows), :]
    above = u_ref[pl.ds(pl.multiple_of(jnp.maximum(start - P, 0), P), P), :]
    uw_ref[:P, :] = jnp.where(h > 0, above, 0.0)
    below = u_ref[pl.ds(pl.multiple_of(jnp.minimum(start + rows, L - P), P), P), :]
    uw_ref[P + rows:, :] = jnp.where(h < N_DIFF_HEADS - 1, below, 0.0)


def _conv_rows(r0, R, uw_ref, w_ref, b_ref, g_ref, beta_ref, y_ref):
    P = CONV_PAD
    off = P - CONV_K // 2
    win = uw_ref[pl.ds(r0, R + 2 * P), :]
    acc = jnp.zeros((R, CONV_CH), F32) + b_ref[...]
    for r in range(8):
        shifted = win if r == 0 else pltpu.roll(win, R + 2 * P - r, 0)
        for k in range(CONV_K):
            if (off + k) % 8 == r:
                a8 = (off + k) // 8
                acc = acc + w_ref[k:k + 1, :] * shifted[8 * a8:8 * a8 + R, :]
    mu = jnp.mean(acc, axis=-1, keepdims=True)
    xc = acc - mu
    y = xc * lax.rsqrt(jnp.mean(xc * xc, axis=-1, keepdims=True) + NORM_EPS) * g_ref[...] + beta_ref[...]
    y_ref[pl.ds(r0, R), :] = _silu(y)


def _attn_kernel(*refs, lam_init):
    bound_ref = refs[1]
    nq = refs[2].shape[0] // ATTN_TILE
    unshifted_ok = bound_ref[0] <= EXP2_SAFE_RANGE
    _stage_conv_window(refs[8], refs[-1])

    def sweep(unshifted, unroll):
        def body(qi, _):
            _attn_query_tile(qi, *refs, lam_init=lam_init, unshifted=unshifted)
            return 0
        lax.fori_loop(0, nq, body, 0, unroll=unroll)

    @pl.when(unshifted_ok)
    def _():
        sweep(True, ATTN_UNROLL)

    @pl.when(jnp.logical_not(unshifted_ok))
    def _():
        sweep(False, 1)


def _attn_query_tile(qi, rb_ref, bound_ref, q_ref, k_ref, vt_ref, band_ref, lam_ref, on_ref,
                     u_ref, cw_ref, cb_ref, cg_ref, cbeta_ref, o_ref, yc_ref,
                     m_ref, l_ref, acc_ref, uw_ref, *, lam_init, unshifted):
    T = ATTN_TILE
    L = k_ref.shape[0]
    nk = L // T
    h = pl.program_id(0)
    q = q_ref[pl.ds(pl.multiple_of(qi * T, T), T), :]
    qt = q.astype(F32).T
    chan = lax.broadcasted_iota(jnp.int32, qt.shape, 0)
    qm = [jnp.where(chan < DIFF_DQK, qt, 0.0).astype(BF16), jnp.where(chan >= DIFF_DQK, qt, 0.0).astype(BF16)]

    def side_const(bucket, m):
        return rb_ref[(bucket * N_DIFF_HEADS + h) * 2 + m] * LOG2E

    def tile_start(j):
        return pl.multiple_of(j * T, T)

    def unshifted_softmax():
        f_left = [jnp.exp2(jnp.full((1, 1), side_const(NUM_BUCKETS // 2 - 1, m), F32)).astype(BF16)
                  for m in range(2)]
        f_right = [jnp.exp2(jnp.full((1, 1), side_const(NUM_BUCKETS - 1, m), F32)).astype(BF16)
                   for m in range(2)]
        band_idx = [jnp.where(qi == 0, 4, 1), 2, jnp.where(qi == nk - 1, 0, 3)]
        l = [jnp.zeros((1, T), F32) for _ in range(2)]
        acc = [jnp.zeros((DIFF_DV, T), F32) for _ in range(2)]

        def scores(t):
            kt = k_ref[pl.ds(tile_start((qi + (2 + t)) % nk), T), :]
            return [_dot(kt, qm[m]) for m in range(2)]

        pending = [scores(t) for t in range(ATTN_LOOKAHEAD)]
        for t in range(nk):
            j = (qi + (2 + t)) % nk
            vt = vt_ref[:, pl.ds(tile_start(j), T)]
            s_cur = pending.pop(0)
            if t + ATTN_LOOKAHEAD < nk:
                pending.append(scores(t + ATTN_LOOKAHEAD))
            for m in range(2):
                if t >= nk - 3:
                    p = jnp.exp2(s_cur[m] - band_ref[0, band_idx[t - (nk - 3)], m])
                    l[m] = l[m] + jnp.sum(p, axis=0, keepdims=True)
                    acc[m] = acc[m] + _dot(vt, p.astype(BF16))
                else:
                    f = jnp.where(j < qi, f_left[m], f_right[m])
                    p = jnp.exp2(s_cur[m])
                    l[m] = l[m] + f.astype(F32) * jnp.sum(p, axis=0, keepdims=True)
                    acc[m] = acc[m] + _dot(vt * f, p.astype(BF16))
        for m in range(2):
            l_ref[m] = l[m]
            acc_ref[m] = acc[m]

    def online_softmax():
        m_ref[...] = jnp.full(m_ref.shape, -jnp.inf, F32)
        l_ref[...] = jnp.zeros(l_ref.shape, F32)
        acc_ref[...] = jnp.zeros(acc_ref.shape, F32)

        def body(j, _):
            r0 = tile_start(j)
            kt = k_ref[pl.ds(r0, T), :]
            vt = vt_ref[:, pl.ds(r0, T)]
            idx = jnp.clip(j - qi, -2, 2) + 2
            ss = [_dot(kt, qm[m]) for m in range(2)]
            for m in range(2):
                s = ss[m] - band_ref[0, idx, m]
                m_old = m_ref[m]
                m_new = jnp.maximum(m_old, jnp.max(s, axis=0, keepdims=True))
                alpha = jnp.exp2(m_old - m_new)
                p = jnp.exp2(s - m_new)
                l_ref[m] = alpha * l_ref[m] + jnp.sum(p, axis=0, keepdims=True)
                acc_ref[m] = alpha * acc_ref[m] + _dot(vt, p.astype(BF16))
                m_ref[m] = m_new
            return 0

        lax.fori_loop(0, nk, body, 0)

    if unshifted:
        unshifted_softmax()
    else:
        online_softmax()

    conv_rows = L // N_DIFF_HEADS // nk
    _conv_rows(pl.multiple_of(qi * conv_rows, conv_rows), conv_rows, uw_ref, cw_ref, cb_ref, cg_ref, cbeta_ref, yc_ref)
    lp = lam_ref[...]
    lam = (jnp.exp(jnp.sum(lp[0:1] * lp[1:2], axis=-1, keepdims=True))
           - jnp.exp(jnp.sum(lp[2:3] * lp[3:4], axis=-1, keepdims=True)) + lam_init)
    o = acc_ref[0] / l_ref[0] - lam * (acc_ref[1] / l_ref[1])
    y = o * lax.rsqrt(jnp.mean(o * o, axis=0, keepdims=True) + NORM_EPS) * on_ref[...] * (1.0 - lam_init)
    o_ref[pl.ds(pl.multiple_of(qi * T, T), T), :] = y.T.astype(o_ref.dtype)


def _diff_attn(rel_bias_flat, score_bound, dq, dk, dvt, band, lam_p, onorm, cu, conv_w, conv_b, conv_g,
               conv_beta, batch, lam_init):
    n = dq.shape[0]
    L = n // batch
    T = ATTN_TILE
    share = L // N_DIFF_HEADS
    seq = pl.BlockSpec((L, 2 * DIFF_DQK), lambda h, b: (b, h))

    def const(shape):
        return pl.BlockSpec(shape, lambda h, b: (0,) * len(shape))

    return pl.pallas_call(
        functools.partial(_attn_kernel, lam_init=lam_init),
        grid=(N_DIFF_HEADS, batch),
        in_specs=[pl.BlockSpec(memory_space=pltpu.SMEM),
                  pl.BlockSpec(memory_space=pltpu.SMEM),
                  seq, seq,
                  pl.BlockSpec((DIFF_DV, L), lambda h, b: (h, b)),
                  pl.BlockSpec((1, 5, 2, T, T), lambda h, b: (h, 0, 0, 0, 0)),
                  const((4, DIFF_DQK)), const((DIFF_DV, 1)),
                  pl.BlockSpec((L, CONV_CH), lambda h, b: (b, 0)),
                  const((CONV_K + 1, CONV_CH)), const((1, CONV_CH)), const((1, CONV_CH)), const((1, CONV_CH))],
        out_specs=[pl.BlockSpec((L, DIFF_DV), lambda h, b: (b, h)),
                   pl.BlockSpec((share, CONV_CH), lambda h, b: (b * N_DIFF_HEADS + h, 0))],
        out_shape=[jax.ShapeDtypeStruct((n, DIFF_V), BF16), jax.ShapeDtypeStruct((n, CONV_CH), F32)],
        scratch_shapes=[pltpu.VMEM((2, 1, T), F32), pltpu.VMEM((2, 1, T), F32),
                        pltpu.VMEM((2, DIFF_DV, T), F32),
                        pltpu.VMEM((share + 2 * CONV_PAD, CONV_CH), F32)],
        compiler_params=_cparams(("parallel", "parallel")),
        name="diff_attn",
    )(rel_bias_flat, score_bound, dq, dk, dvt, band, lam_p, onorm, cu, conv_w, conv_b, conv_g, conv_beta)


def _post_kernel(x_ref, ya_ref, yb_ref, yc_ref, p_ref, wo_hbm, fg_ref, wg_hbm, wu_hbm, wd_hbm,
                 pg_ref, pwg_hbm, pwp_hbm, o_ref,
                 wo_ref, wg_ref, wu_ref, wd_ref, pwg_ref, pwp_ref, wide_ref, wide_sem, tall_ref, tall_sem,
                 *, layer):
    @pl.when(pl.program_id(0) == 0)
    def _():
        _fetch_as_bf16(wo_hbm, layer, wo_ref, tall_ref, tall_sem)
        _fetch_as_bf16(wg_hbm, layer, wg_ref, wide_ref, wide_sem)
        _fetch_as_bf16(wu_hbm, layer, wu_ref, wide_ref, wide_sem)
        _fetch_as_bf16(wd_hbm, layer, wd_ref, tall_ref, tall_sem)
        _fetch_as_bf16(pwg_hbm, layer, pwg_ref, tall_ref, tall_sem)
        _fetch_as_bf16(pwp_hbm, layer, pwp_ref, tall_ref, tall_sem)

    mix = _dot(ya_ref[...].astype(BF16), wo_ref[:GLA_V, :])
    mix = mix + _dot(yb_ref[...].astype(BF16), wo_ref[GLA_V:GLA_V + DIFF_V, :])
    mix = mix + _dot(yc_ref[...].astype(BF16), wo_ref[GLA_V + DIFF_V:, :])
    x = _swiglu_half_step(x_ref[...] + mix, fg_ref, wg_ref, wu_ref, wd_ref)
    h = _rms_rows(x, pg_ref[...]).astype(BF16)
    gate = _sigmoid(_dot(h, pwg_ref[...]))
    o_ref[...] = x + gate * _dot(p_ref[...].astype(BF16), pwp_ref[...])


def _post(x, ya, yb, yc, p, w_out, ffn_gain, wg, wu, wd, ple_gain, ple_wg, ple_wp, layer):
    n = x.shape[0]
    tm = TOKEN_TILE

    def row(width):
        return pl.BlockSpec((tm, width), lambda i: (i, 0))

    return pl.pallas_call(
        functools.partial(_post_kernel, layer=layer),
        grid=(n // tm,),
        in_specs=[row(D_MODEL), row(GLA_V), row(DIFF_V), row(CONV_CH),
                  pl.BlockSpec((None, tm, PLE_DIM), lambda i: (layer, i, 0)),
                  _hbm_spec(), _const_spec((1, D_MODEL)), _hbm_spec(), _hbm_spec(), _hbm_spec(),
                  _const_spec((1, D_MODEL)), _hbm_spec(), _hbm_spec()],
        out_specs=row(D_MODEL),
        out_shape=jax.ShapeDtypeStruct((n, D_MODEL), F32),
        scratch_shapes=[pltpu.VMEM((D_MODEL, D_MODEL), BF16), pltpu.VMEM((D_MODEL, D_FF), BF16),
                        pltpu.VMEM((D_MODEL, D_FF), BF16), pltpu.VMEM((D_FF, D_MODEL), BF16),
                        pltpu.VMEM((D_MODEL, D_MODEL), BF16), pltpu.VMEM((PLE_DIM, D_MODEL), BF16)]
        + _stage_scratch(D_FF) + _stage_scratch(D_MODEL),
        compiler_params=_cparams(("arbitrary",)),
        name="post",
    )(x, ya, yb, yc, p, w_out, ffn_gain, wg, wu, wd, ple_gain, ple_wg, ple_wp)


def _pad_up(up, row0):
    return jnp.zeros((128, GLA_Q), F32).at[row0:row0 + GLA_RANK].set(up).astype(BF16)


def kernel(x, p, ffn1_norm, ffn1_w_gate, ffn1_w_up, ffn1_w_down, mix_norm, w_in, w_out, gla_gk_up_f, gla_gk_bias_f, gla_gk_up_b, gla_gk_bias_b, gla_out_norm, diff_q_norm, diff_k_norm, diff_lambda, diff_out_norm, rel_bias, conv_dw_w, conv_dw_b, conv_norm_g, conv_norm_b, ffn2_norm, ffn2_w_gate, ffn2_w_up, ffn2_w_down, ple_norm, ple_w_gate, ple_w_proj):
    B, L, _ = x.shape
    depth = w_in.shape[0]
    n = B * L
    xs = x.reshape(n, D_MODEL)
    rb_flat = rel_bias.reshape(-1)
    band = _bias_tiles(rel_bias)

    def row(v):
        return v.reshape(1, -1)

    ffn1_w = (ffn1_w_gate, ffn1_w_up, ffn1_w_down)
    ffn2_w = (ffn2_w_gate, ffn2_w_up, ffn2_w_down)
    w_in_b = w_in.astype(BF16)
    p_rows = p.reshape(depth, n, PLE_DIM)

    for i in range(depth):
        xs = _ffn(xs, row(ffn1_norm[i]), *ffn1_w, i)

        gqk, gg, gv, go, dq, dk, dvt, cu = _inproj(
            xs, row(mix_norm[i]), w_in_b,
            _pad_up(gla_gk_up_f[i], 0), row(gla_gk_bias_f[i]),
            _pad_up(gla_gk_up_b[i], GLA_RANK), row(gla_gk_bias_b[i]),
            row(jnp.tile(diff_q_norm[i], DIFF_QK // DIFF_DQK)), row(jnp.tile(diff_k_norm[i], DIFF_QK // DIFF_DQK)),
            i)

        y_gla = _gla(gqk, gg, gv, go, row(jnp.tile(gla_out_norm[i], N_GLA_HEADS)), B)
        lam_init = 0.8 - 0.6 * math.exp(-0.3 * i)
        score_bound = LOG2E * (DIFF_DQK ** 0.5 * jnp.max(jnp.abs(diff_q_norm[i])) * jnp.max(jnp.abs(diff_k_norm[i]))
                               + jnp.max(jnp.abs(rel_bias)))
        conv_w = jnp.concatenate([conv_dw_w[i], jnp.zeros((1, CONV_CH), F32)], axis=0)
        y_diff, y_conv = _diff_attn(rb_flat, score_bound.reshape(1), dq, dk, dvt, band, diff_lambda[i],
                                    diff_out_norm[i].reshape(-1, 1), cu, conv_w, row(conv_dw_b[i]),
                                    row(conv_norm_g[i]), row(conv_norm_b[i]), B, lam_init)

        xs = _post(xs, y_gla, y_diff, y_conv, p_rows, w_out, row(ffn2_norm[i]), *ffn2_w,
                   row(ple_norm[i]), ple_w_gate, ple_w_proj, i)
    return xs.reshape(B, L, D_MODEL)
```

```python
import functools
import math

import jax
import jax.numpy as jnp
from jax import lax
from jax.experimental import pallas as pl
from jax.experimental.pallas import tpu as pltpu

F32 = jnp.float32
BF16 = jnp.bfloat16

D_MODEL = 1024
N_GLA_HEADS = 4
GLA_DK = 32
GLA_DV = 64
GLA_Q = N_GLA_HEADS * GLA_DK
GLA_V = N_GLA_HEADS * GLA_DV
GLA_RANK = 16
GATE_TEMP = 16.0
GLA_CHUNK = 64
GLA_BLOCK = 16
GLA_CUM_CHUNKS = 4
DIFF_DQK = 64
N_DIFF_HEADS = 4
DIFF_DV = 128
DIFF_QK = 512
DIFF_V = 512
NUM_BUCKETS = 32
MAX_DISTANCE = 128
CONV_CH = 256
CONV_K = 31
D_FF = 2816
PLE_DIM = 256
NORM_EPS = 1e-6
LOG2E = math.log2(math.e)
EXP2_SAFE_RANGE = 64.0
D_IN = 2848

TOKEN_TILE = 512
INPROJ_TILE = 1024
INPROJ_SUB = 256
FF_CHUNKS = (1536, 1280)
ATTN_TILE = 256
ATTN_LOOKAHEAD = 4
ATTN_UNROLL = 4
CONV_PAD = 16
V7X_VMEM_BYTES = 64 * 1024 * 1024
VMEM_LIMIT = V7X_VMEM_BYTES * 7 // 8
WEIGHT_CHUNK_ROWS_WIDE = 128
WEIGHT_CHUNK_ROWS_TALL = 256
WEIGHT_STAGE_SLOTS = 3


def _cparams(sem):
    return pltpu.CompilerParams(dimension_semantics=sem, vmem_limit_bytes=VMEM_LIMIT)


def _dot(a, b):
    return jnp.dot(a, b, preferred_element_type=F32)


def _dot_nt(a, b):
    return lax.dot_general(a, b, (((1,), (1,)), ((), ())), preferred_element_type=F32)


def _rms_rows(x, gain):
    return x * lax.rsqrt(jnp.mean(x * x, axis=-1, keepdims=True) + NORM_EPS) * gain


def _sigmoid(x):
    return 1.0 / (1.0 + jnp.exp(-x))


def _silu(x):
    return x * _sigmoid(x)


def _const_spec(shape):
    return pl.BlockSpec(shape, lambda *_: (0,) * len(shape), pipeline_mode=pl.Buffered(1))


def _layer_spec(shape, layer):
    return pl.BlockSpec((None,) + shape, lambda *_: (layer,) + (0,) * len(shape), pipeline_mode=pl.Buffered(1))


def _hbm_spec():
    return pl.BlockSpec(memory_space=pl.ANY)


def _stage_scratch(cols):
    rows = WEIGHT_CHUNK_ROWS_WIDE if cols > D_MODEL else WEIGHT_CHUNK_ROWS_TALL
    return [pltpu.VMEM((WEIGHT_STAGE_SLOTS, rows, cols), F32), pltpu.SemaphoreType.DMA((WEIGHT_STAGE_SLOTS,))]


def _fetch_as_bf16(src_hbm, layer, dst_ref, stage_ref, sem_ref):
    slots = stage_ref.shape[0]
    chunk = min(stage_ref.shape[1], dst_ref.shape[0])
    assert dst_ref.shape[0] % chunk == 0
    n = dst_ref.shape[0] // chunk

    def dma(c):
        return pltpu.make_async_copy(src_hbm.at[layer, pl.ds(c * chunk, chunk), :],
                                     stage_ref.at[c % slots, pl.ds(0, chunk), :], sem_ref.at[c % slots])

    for c in range(min(slots - 1, n)):
        dma(c).start()
    for c in range(n):
        if c + slots - 1 < n:
            dma(c + slots - 1).start()
        dma(c).wait()
        dst_ref[c * chunk:(c + 1) * chunk, :] = stage_ref[c % slots, :chunk, :].astype(BF16)


def _swiglu_half_step(x, g_ref, wg_ref, wu_ref, wd_ref):
    h = _rms_rows(x, g_ref[...]).astype(BF16)
    acc = jnp.zeros(x.shape, F32)
    c0 = 0
    for width in FF_CHUNKS:
        gate = _dot(h, wg_ref[:, c0:c0 + width])
        up = _dot(h, wu_ref[:, c0:c0 + width])
        a = (_silu(gate) * up).astype(BF16)
        acc = acc + _dot(a, wd_ref[c0:c0 + width, :])
        c0 += width
    return x + 0.5 * acc


def _ffn_kernel(x_ref, g_ref, wg_hbm, wu_hbm, wd_hbm, o_ref,
                wg_ref, wu_ref, wd_ref, wide_ref, wide_sem, tall_ref, tall_sem, *, layer):
    @pl.when(pl.program_id(0) == 0)
    def _():
        _fetch_as_bf16(wg_hbm, layer, wg_ref, wide_ref, wide_sem)
        _fetch_as_bf16(wu_hbm, layer, wu_ref, wide_ref, wide_sem)
        _fetch_as_bf16(wd_hbm, layer, wd_ref, tall_ref, tall_sem)

    o_ref[...] = _swiglu_half_step(x_ref[...], g_ref, wg_ref, wu_ref, wd_ref)


def _ffn(x, gain, wg, wu, wd, layer):
    n = x.shape[0]
    row = pl.BlockSpec((TOKEN_TILE, D_MODEL), lambda i: (i, 0))
    return pl.pallas_call(
        functools.partial(_ffn_kernel, layer=layer),
        grid=(n // TOKEN_TILE,),
        in_specs=[row, _const_spec((1, D_MODEL)), _hbm_spec(), _hbm_spec(), _hbm_spec()],
        out_specs=row,
        out_shape=jax.ShapeDtypeStruct((n, D_MODEL), F32),
        scratch_shapes=[pltpu.VMEM((D_MODEL, D_FF), BF16), pltpu.VMEM((D_MODEL, D_FF), BF16),
                        pltpu.VMEM((D_FF, D_MODEL), BF16)] + _stage_scratch(D_FF) + _stage_scratch(D_MODEL),
        compiler_params=_cparams(("arbitrary",)),
        name="ffn",
    )(x, gain, wg, wu, wd)


def _block_diag_ones(n, blk, dtype):
    r = lax.broadcasted_iota(jnp.int32, (n, n), 0) // blk
    c = lax.broadcasted_iota(jnp.int32, (n, n), 1) // blk
    return jnp.where(r == c, 1.0, 0.0).astype(dtype)


def _group_mean_sq(x, blk):
    n = x.shape[-1]
    return _dot((x * x).astype(BF16), _block_diag_ones(n, blk, BF16)) * (1.0 / blk)


def _log_sigmoid(z):
    return jnp.minimum(z, 0.0) - jnp.log(1.0 + jnp.exp(-jnp.abs(z)))


def _inproj_kernel(x_ref, g_ref, w_ref, upf_ref, bf_ref, upb_ref, bb_ref, qn_ref, kn_ref,
                   gqk_ref, gg_ref, gv_ref, go_ref, dq_ref, dk_ref, dvt_ref, cu_ref):
    tm = x_ref.shape[0]
    nsub = tm // INPROJ_SUB

    def project(s):
        rows = slice(s * INPROJ_SUB, (s + 1) * INPROJ_SUB)
        h = _rms_rows(x_ref[rows, :], g_ref[...]).astype(BF16)
        return _dot(h, w_ref[...])

    def prepare(s, proj):
        rows = slice(s * INPROJ_SUB, (s + 1) * INPROJ_SUB)
        gqk_ref[rows, :GLA_Q] = proj[:, :GLA_Q] * (GLA_DK ** -0.5)
        gqk_ref[rows, GLA_Q:] = proj[:, GLA_Q:2 * GLA_Q]
        gv_ref[rows, :] = proj[:, 256:512].astype(BF16)
        go_ref[rows, :] = _silu(proj[:, 512:768]).astype(BF16)
        low = proj[:, 768:896].astype(BF16)
        zf = _dot(low, upf_ref[...]) + bf_ref[...]
        zb = _dot(low, upb_ref[...]) + bb_ref[...]
        gg_ref[rows, :GLA_Q] = _log_sigmoid(zf) / GATE_TEMP
        gg_ref[rows, GLA_Q:] = _log_sigmoid(zb) / GATE_TEMP
        rest = proj[:, 768 + 2 * GLA_RANK:]
        dq = rest[:, :512]
        dq_ref[rows, :] = (dq * lax.rsqrt(_group_mean_sq(dq, DIFF_DQK) + NORM_EPS) * qn_ref[...]
                           * (DIFF_DQK ** -0.5 * LOG2E)).astype(BF16)
        dk = rest[:, 512:1024]
        dk_ref[rows, :] = (dk * lax.rsqrt(_group_mean_sq(dk, DIFF_DQK) + NORM_EPS) * kn_ref[...]).astype(BF16)
        dvt_ref[:, rows] = rest[:, 1024:1536].T.astype(BF16)
        cu_ref[rows, :] = rest[:, 1536:1792] * _sigmoid(rest[:, 1792:2048])

    nxt = project(0)
    for s in range(nsub):
        cur = nxt
        if s + 1 < nsub:
            nxt = project(s + 1)
        prepare(s, cur)


def _inproj(x, gain, w, upf, bf, upb, bb, qn, kn, layer):
    n = x.shape[0]
    tm = INPROJ_TILE

    def row(width):
        return pl.BlockSpec((tm, width), lambda i: (i, 0))

    outs = [(2 * GLA_Q, F32), (2 * GLA_Q, F32), (GLA_V, BF16), (GLA_V, BF16),
            (DIFF_QK, BF16), (DIFF_QK, BF16), None, (CONV_CH, F32)]
    dvt_spec = pl.BlockSpec((DIFF_V, tm), lambda i: (0, i))
    return pl.pallas_call(
        _inproj_kernel,
        grid=(n // tm,),
        in_specs=[row(D_MODEL), _const_spec((1, D_MODEL)), _layer_spec((D_MODEL, D_IN), layer),
                  _const_spec((128, GLA_Q)), _const_spec((1, GLA_Q)),
                  _const_spec((128, GLA_Q)), _const_spec((1, GLA_Q)),
                  _const_spec((1, DIFF_QK)), _const_spec((1, DIFF_QK))],
        out_specs=[row(o[0]) if o else dvt_spec for o in outs],
        out_shape=[jax.ShapeDtypeStruct((n, o[0]), o[1]) if o else jax.ShapeDtypeStruct((DIFF_V, n), BF16)
                   for o in outs],
        compiler_params=_cparams(("parallel",)),
        name="inproj",
    )(x, gain, w, upf, bf, upb, bb, qn, kn)


def _gla_block_kernel(qk_ref, g_ref, v_ref, go_ref, on_ref, y_ref, of_ref, st_ref):
    C = GLA_CHUNK
    G = GLA_BLOCK
    R = G * C
    L = qk_ref.shape[0]
    nblk = L // R
    H = N_GLA_HEADS

    def iota(shape, d):
        return lax.broadcasted_iota(jnp.int32, shape, d)

    kmask = iota((H * C, GLA_Q), 0) // C == iota((H * C, GLA_Q), 1) // GLA_DK
    vmask = iota((H * C, GLA_V), 0) // C == iota((H * C, GLA_V), 1) // GLA_DV
    smask = iota((GLA_V, GLA_Q), 0) // GLA_DV == iota((GLA_V, GLA_Q), 1) // GLA_DK
    pos_a = iota((C, H * C), 0)
    pos_b = iota((C, H * C), 1) % C
    RC = GLA_CUM_CHUNKS * C
    ri, ci = iota((RC, RC), 0), iota((RC, RC), 1)
    same_chunk = ri // C == ci // C
    cums = [jnp.where(jnp.logical_and(same_chunk, ci <= ri), 1.0, 0.0).astype(BF16),
            jnp.where(jnp.logical_and(same_chunk, ci >= ri), 1.0, 0.0).astype(BF16)]
    amasks = [pos_b <= pos_a, pos_b >= pos_a]
    edges = [C - 1, 0]
    orders = [list(range(G)), list(range(G - 1, -1, -1))]

    st_ref[...] = jnp.zeros(st_ref.shape, F32)

    def step(t, _):
        r0 = [pl.multiple_of(t * R, R), pl.multiple_of((nblk - 1 - t) * R, R)]
        rows = [pl.ds(r0[0], R), pl.ds(r0[1], R)]
        dirs = (0, 1)
        b, k, v, qt = {}, {}, {}, {}
        for d in dirs:
            g = g_ref[rows[d], d * GLA_Q:(d + 1) * GLA_Q]
            g_hi = g.astype(BF16)
            g_lo = (g - g_hi.astype(F32)).astype(BF16)
            b[d] = jnp.concatenate([_dot(cums[d], g_hi[i:i + RC]) + _dot(cums[d], g_lo[i:i + RC])
                                    for i in range(0, R, RC)], axis=0)
        a, ds, decay = {}, {}, {}
        for d in dirs:
            q = qk_ref[rows[d], :GLA_Q]
            k[d] = qk_ref[rows[d], GLA_Q:]
            v[d] = v_ref[rows[d], :].astype(F32)
            qt[d] = (q * jnp.exp(b[d])).astype(BF16)
            kt = k[d] * jnp.exp(-b[d])
            for c in range(G):
                sl = slice(c * C, (c + 1) * C)
                b_c = b[d][sl]
                b_edge = b_c[edges[d]:edges[d] + 1, :]
                decay[d, c] = jnp.exp(b_edge)
                kend = (k[d][sl] * jnp.exp(b_edge - b_c)).astype(BF16)
                kstack = jnp.where(kmask, jnp.concatenate([kt[sl]] * H, axis=0), 0.0).astype(BF16)
                a[d, c] = jnp.where(amasks[d], _dot_nt(qt[d][sl], kstack), 0.0).astype(BF16)
                ds[d, c] = _dot(v[d][sl].T.astype(BF16), kend)
        for d in dirs:
            s = st_ref[d]
            s_in = {}
            for c in orders[d]:
                s_in[c] = s.astype(BF16)
                s = s * decay[d, c] + jnp.where(smask, ds[d, c], 0.0)
            st_ref[d] = s
            out_ref = y_ref if d else of_ref
            for c in range(G):
                sl = slice(c * C, (c + 1) * C)
                vstack = jnp.where(vmask, jnp.concatenate([v[d][sl]] * H, axis=0), 0.0).astype(BF16)
                o = _dot(a[d, c], vstack) + _dot_nt(qt[d][sl], s_in[c])
                out_ref[pl.ds(pl.multiple_of(r0[d] + c * C, C), C), :] = o
        return 0

    lax.fori_loop(0, nblk, step, 0)

    norm_bd = _block_diag_ones(GLA_V, GLA_DV, BF16)

    def finish(t, _):
        rows = pl.ds(pl.multiple_of(t * R, R), R)
        tot = of_ref[rows, :] + y_ref[rows, :]
        ms = _dot((tot * tot).astype(BF16), norm_bd) * (1.0 / GLA_DV)
        y_ref[rows, :] = tot * lax.rsqrt(ms + NORM_EPS) * on_ref[...] * go_ref[rows, :].astype(F32)
        return 0

    lax.fori_loop(0, nblk, finish, 0)


def _gla(qk, g, v, go, onorm, batch):
    n = qk.shape[0]
    L = n // batch

    def seq(width):
        return pl.BlockSpec((L, width), lambda b: (b, 0))

    return pl.pallas_call(
        _gla_block_kernel,
        grid=(batch,),
        in_specs=[seq(2 * GLA_Q), seq(2 * GLA_Q), seq(GLA_V), seq(GLA_V), _const_spec((1, GLA_V))],
        out_specs=seq(GLA_V),
        out_shape=jax.ShapeDtypeStruct((n, GLA_V), F32),
        scratch_shapes=[pltpu.VMEM((L, GLA_V), F32), pltpu.VMEM((2, GLA_V, GLA_Q), F32)],
        compiler_params=_cparams(("parallel",)),
        name="gla",
    )(qk, g, v, go, onorm)


def _bias_kernel(rb_ref, o_ref):
    T = ATTN_TILE
    h = pl.program_id(0)
    nb = NUM_BUCKETS // 2
    max_exact = nb // 2
    lane = lax.broadcasted_iota(jnp.int32, (8, 2 * T), 1)
    for d in range(5):
        rel = (d - 2) * T + T - lane
        ret = jnp.where(rel > 0, nb, 0)
        n = jnp.abs(rel)
        large = max_exact + (jnp.log(jnp.maximum(n, 1).astype(F32) / max_exact)
                             / math.log(MAX_DISTANCE / max_exact) * (nb - max_exact)).astype(jnp.int32)
        large = jnp.minimum(large, nb - 1)
        bucket = ret + jnp.where(n < max_exact, n, large)
        for m in range(2):
            row = jnp.zeros((8, 2 * T), F32)
            for bkt in range(NUM_BUCKETS):
                row = jnp.where(bucket == bkt, rb_ref[(bkt * N_DIFF_HEADS + h) * 2 + m], row)
            rows = jnp.broadcast_to(row[0:1, :] * (-LOG2E), (T, 2 * T))
            o_ref[0, d, m] = pltpu.roll(rows, 0, 1, stride=1, stride_axis=0)[:, T:]


def _bias_tiles(rel_bias):
    T = ATTN_TILE
    return pl.pallas_call(
        _bias_kernel,
        grid=(N_DIFF_HEADS,),
        in_specs=[pl.BlockSpec(memory_space=pltpu.SMEM)],
        out_specs=pl.BlockSpec((1, 5, 2, T, T), lambda h: (h, 0, 0, 0, 0)),
        out_shape=jax.ShapeDtypeStruct((N_DIFF_HEADS, 5, 2, T, T), F32),
        compiler_params=_cparams(("parallel",)),
        name="t5_bias",
    )(rel_bias.reshape(-1))


def _stage_conv_window(u_ref, uw_ref):
    L = u_ref.shape[0]
    P = CONV_PAD
    rows = L // N_DIFF_HEADS
    h = pl.program_id(0)
    start = pl.multiple_of(h * rows, rows)
    uw_ref[P:P + rows, :] = u_ref[pl.ds(start, rows), :]
    above = u_ref[pl.ds(pl.multiple_of(jnp.maximum(start - P, 0), P), P), :]
    uw_ref[:P, :] = jnp.where(h > 0, above, 0.0)
    below = u_ref[pl.ds(pl.multiple_of(jnp.minimum(start + rows, L - P), P), P), :]
    uw_ref[P + rows:, :] = jnp.where(h < N_DIFF_HEADS - 1, below, 0.0)


def _conv_rows(r0, R, uw_ref, w_ref, b_ref, g_ref, beta_ref, y_ref):
    P = CONV_PAD
    off = P - CONV_K // 2
    win = uw_ref[pl.ds(r0, R + 2 * P), :]
    acc = jnp.zeros((R, CONV_CH), F32) + b_ref[...]
    for r in range(8):
        shifted = win if r == 0 else pltpu.roll(win, R + 2 * P - r, 0)
        for k in range(CONV_K):
            if (off + k) % 8 == r:
                a8 = (off + k) // 8
                acc = acc + w_ref[k:k + 1, :] * shifted[8 * a8:8 * a8 + R, :]
    mu = jnp.mean(acc, axis=-1, keepdims=True)
    xc = acc - mu
    y = xc * lax.rsqrt(jnp.mean(xc * xc, axis=-1, keepdims=True) + NORM_EPS) * g_ref[...] + beta_ref[...]
    y_ref[pl.ds(r0, R), :] = _silu(y)


def _attn_kernel(*refs, lam_init):
    bound_ref = refs[1]
    nq = refs[2].shape[0] // ATTN_TILE
    unshifted_ok = bound_ref[0] <= EXP2_SAFE_RANGE
    _stage_conv_window(refs[8], refs[-1])

    def sweep(unshifted, unroll):
        def body(qi, _):
            _attn_query_tile(qi, *refs, lam_init=lam_init, unshifted=unshifted)
            return 0
        lax.fori_loop(0, nq, body, 0, unroll=unroll)

    @pl.when(unshifted_ok)
    def _():
        sweep(True, ATTN_UNROLL)

    @pl.when(jnp.logical_not(unshifted_ok))
    def _():
        sweep(False, 1)


def _attn_query_tile(qi, rb_ref, bound_ref, q_ref, k_ref, vt_ref, band_ref, lam_ref, on_ref,
                     u_ref, cw_ref, cb_ref, cg_ref, cbeta_ref, o_ref, yc_ref,
                     m_ref, l_ref, acc_ref, uw_ref, *, lam_init, unshifted):
    T = ATTN_TILE
    L = k_ref.shape[0]
    nk = L // T
    h = pl.program_id(0)
    q = q_ref[pl.ds(pl.multiple_of(qi * T, T), T), :]
    qt = q.astype(F32).T
    chan = lax.broadcasted_iota(jnp.int32, qt.shape, 0)
    qm = [jnp.where(chan < DIFF_DQK, qt, 0.0).astype(BF16), jnp.where(chan >= DIFF_DQK, qt, 0.0).astype(BF16)]

    def side_const(bucket, m):
        return rb_ref[(bucket * N_DIFF_HEADS + h) * 2 + m] * LOG2E

    def tile_start(j):
        return pl.multiple_of(j * T, T)

    def unshifted_softmax():
        f_left = [jnp.exp2(jnp.full((1, 1), side_const(NUM_BUCKETS // 2 - 1, m), F32)).astype(BF16)
                  for m in range(2)]
        f_right = [jnp.exp2(jnp.full((1, 1), side_const(NUM_BUCKETS - 1, m), F32)).astype(BF16)
                   for m in range(2)]
        band_idx = [jnp.where(qi == 0, 4, 1), 2, jnp.where(qi == nk - 1, 0, 3)]
        l = [jnp.zeros((1, T), F32) for _ in range(2)]
        acc = [jnp.zeros((DIFF_DV, T), F32) for _ in range(2)]

        def scores(t):
            kt = k_ref[pl.ds(tile_start((qi + (2 + t)) % nk), T), :]
            return [_dot(kt, qm[m]) for m in range(2)]

        pending = [scores(t) for t in range(ATTN_LOOKAHEAD)]
        for t in range(nk):
            j = (qi + (2 + t)) % nk
            vt = vt_ref[:, pl.ds(tile_start(j), T)]
            s_cur = pending.pop(0)
            if t + ATTN_LOOKAHEAD < nk:
                pending.append(scores(t + ATTN_LOOKAHEAD))
            for m in range(2):
                if t >= nk - 3:
                    p = jnp.exp2(s_cur[m] - band_ref[0, band_idx[t - (nk - 3)], m])
                    l[m] = l[m] + jnp.sum(p, axis=0, keepdims=True)
                    acc[m] = acc[m] + _dot(vt, p.astype(BF16))
                else:
                    f = jnp.where(j < qi, f_left[m], f_right[m])
                    p = jnp.exp2(s_cur[m])
                    l[m] = l[m] + f.astype(F32) * jnp.sum(p, axis=0, keepdims=True)
                    acc[m] = acc[m] + _dot(vt * f, p.astype(BF16))
        for m in range(2):
            l_ref[m] = l[m]
            acc_ref[m] = acc[m]

    def online_softmax():
        m_ref[...] = jnp.full(m_ref.shape, -jnp.inf, F32)
        l_ref[...] = jnp.zeros(l_ref.shape, F32)
        acc_ref[...] = jnp.zeros(acc_ref.shape, F32)

        def body(j, _):
            r0 = tile_start(j)
            kt = k_ref[pl.ds(r0, T), :]
            vt = vt_ref[:, pl.ds(r0, T)]
            idx = jnp.clip(j - qi, -2, 2) + 2
            ss = [_dot(kt, qm[m]) for m in range(2)]
            for m in range(2):
                s = ss[m] - band_ref[0, idx, m]
                m_old = m_ref[m]
                m_new = jnp.maximum(m_old, jnp.max(s, axis=0, keepdims=True))
                alpha = jnp.exp2(m_old - m_new)
                p = jnp.exp2(s - m_new)
                l_ref[m] = alpha * l_ref[m] + jnp.sum(p, axis=0, keepdims=True)
                acc_ref[m] = alpha * acc_ref[m] + _dot(vt, p.astype(BF16))
                m_ref[m] = m_new
            return 0

        lax.fori_loop(0, nk, body, 0)

    if unshifted:
        unshifted_softmax()
    else:
        online_softmax()

    conv_rows = L // N_DIFF_HEADS // nk
    _conv_rows(pl.multiple_of(qi * conv_rows, conv_rows), conv_rows, uw_ref, cw_ref, cb_ref, cg_ref, cbeta_ref, yc_ref)
    lp = lam_ref[...]
    lam = (jnp.exp(jnp.sum(lp[0:1] * lp[1:2], axis=-1, keepdims=True))
           - jnp.exp(jnp.sum(lp[2:3] * lp[3:4], axis=-1, keepdims=True)) + lam_init)
    o = acc_ref[0] / l_ref[0] - lam * (acc_ref[1] / l_ref[1])
    y = o * lax.rsqrt(jnp.mean(o * o, axis=0, keepdims=True) + NORM_EPS) * on_ref[...] * (1.0 - lam_init)
    o_ref[pl.ds(pl.multiple_of(qi * T, T), T), :] = y.T.astype(o_ref.dtype)


def _diff_attn(rel_bias_flat, score_bound, dq, dk, dvt, band, lam_p, onorm, cu, conv_w, conv_b, conv_g,
               conv_beta, batch, lam_init):
    n = dq.shape[0]
    L = n // batch
    T = ATTN_TILE
    share = L // N_DIFF_HEADS
    seq = pl.BlockSpec((L, 2 * DIFF_DQK), lambda h, b: (b, h))

    def const(shape):
        return pl.BlockSpec(shape, lambda h, b: (0,) * len(shape))

    return pl.pallas_call(
        functools.partial(_attn_kernel, lam_init=lam_init),
        grid=(N_DIFF_HEADS, batch),
        in_specs=[pl.BlockSpec(memory_space=pltpu.SMEM),
                  pl.BlockSpec(memory_space=pltpu.SMEM),
                  seq, seq,
                  pl.BlockSpec((DIFF_DV, L), lambda h, b: (h, b)),
                  pl.BlockSpec((1, 5, 2, T, T), lambda h, b: (h, 0, 0, 0, 0)),
                  const((4, DIFF_DQK)), const((DIFF_DV, 1)),
                  pl.BlockSpec((L, CONV_CH), lambda h, b: (b, 0)),
                  const((CONV_K + 1, CONV_CH)), const((1, CONV_CH)), const((1, CONV_CH)), const((1, CONV_CH))],
        out_specs=[pl.BlockSpec((L, DIFF_DV), lambda h, b: (b, h)),
                   pl.BlockSpec((share, CONV_CH), lambda h, b: (b * N_DIFF_HEADS + h, 0))],
        out_shape=[jax.ShapeDtypeStruct((n, DIFF_V), BF16), jax.ShapeDtypeStruct((n, CONV_CH), F32)],
        scratch_shapes=[pltpu.VMEM((2, 1, T), F32), pltpu.VMEM((2, 1, T), F32),
                        pltpu.VMEM((2, DIFF_DV, T), F32),
                        pltpu.VMEM((share + 2 * CONV_PAD, CONV_CH), F32)],
        compiler_params=_cparams(("parallel", "parallel")),
        name="diff_attn",
    )(rel_bias_flat, score_bound, dq, dk, dvt, band, lam_p, onorm, cu, conv_w, conv_b, conv_g, conv_beta)


def _post_kernel(x_ref, ya_ref, yb_ref, yc_ref, p_ref, wo_hbm, fg_ref, wg_hbm, wu_hbm, wd_hbm,
                 pg_ref, pwg_hbm, pwp_hbm, o_ref,
                 wo_ref, wg_ref, wu_ref, wd_ref, pwg_ref, pwp_ref, wide_ref, wide_sem, tall_ref, tall_sem,
                 *, layer):
    @pl.when(pl.program_id(0) == 0)
    def _():
        _fetch_as_bf16(wo_hbm, layer, wo_ref, tall_ref, tall_sem)
        _fetch_as_bf16(wg_hbm, layer, wg_ref, wide_ref, wide_sem)
        _fetch_as_bf16(wu_hbm, layer, wu_ref, wide_ref, wide_sem)
        _fetch_as_bf16(wd_hbm, layer, wd_ref, tall_ref, tall_sem)
        _fetch_as_bf16(pwg_hbm, layer, pwg_ref, tall_ref, tall_sem)
        _fetch_as_bf16(pwp_hbm, layer, pwp_ref, tall_ref, tall_sem)

    mix = _dot(ya_ref[...].astype(BF16), wo_ref[:GLA_V, :])
    mix = mix + _dot(yb_ref[...].astype(BF16), wo_ref[GLA_V:GLA_V + DIFF_V, :])
    mix = mix + _dot(yc_ref[...].astype(BF16), wo_ref[GLA_V + DIFF_V:, :])
    x = _swiglu_half_step(x_ref[...] + mix, fg_ref, wg_ref, wu_ref, wd_ref)
    h = _rms_rows(x, pg_ref[...]).astype(BF16)
    gate = _sigmoid(_dot(h, pwg_ref[...]))
    o_ref[...] = x + gate * _dot(p_ref[...].astype(BF16), pwp_ref[...])


def _post(x, ya, yb, yc, p, w_out, ffn_gain, wg, wu, wd, ple_gain, ple_wg, ple_wp, layer):
    n = x.shape[0]
    tm = TOKEN_TILE

    def row(width):
        return pl.BlockSpec((tm, width), lambda i: (i, 0))

    return pl.pallas_call(
        functools.partial(_post_kernel, layer=layer),
        grid=(n // tm,),
        in_specs=[row(D_MODEL), row(GLA_V), row(DIFF_V), row(CONV_CH),
                  pl.BlockSpec((None, tm, PLE_DIM), lambda i: (layer, i, 0)),
                  _hbm_spec(), _const_spec((1, D_MODEL)), _hbm_spec(), _hbm_spec(), _hbm_spec(),
                  _const_spec((1, D_MODEL)), _hbm_spec(), _hbm_spec()],
        out_specs=row(D_MODEL),
        out_shape=jax.ShapeDtypeStruct((n, D_MODEL), F32),
        scratch_shapes=[pltpu.VMEM((D_MODEL, D_MODEL), BF16), pltpu.VMEM((D_MODEL, D_FF), BF16),
                        pltpu.VMEM((D_MODEL, D_FF), BF16), pltpu.VMEM((D_FF, D_MODEL), BF16),
                        pltpu.VMEM((D_MODEL, D_MODEL), BF16), pltpu.VMEM((PLE_DIM, D_MODEL), BF16)]
        + _stage_scratch(D_FF) + _stage_scratch(D_MODEL),
        compiler_params=_cparams(("arbitrary",)),
        name="post",
    )(x, ya, yb, yc, p, w_out, ffn_gain, wg, wu, wd, ple_gain, ple_wg, ple_wp)


def _pad_up(up, row0):
    return jnp.zeros((128, GLA_Q), F32).at[row0:row0 + GLA_RANK].set(up).astype(BF16)


def kernel(x, p, ffn1_norm, ffn1_w_gate, ffn1_w_up, ffn1_w_down, mix_norm, w_in, w_out, gla_gk_up_f, gla_gk_bias_f, gla_gk_up_b, gla_gk_bias_b, gla_out_norm, diff_q_norm, diff_k_norm, diff_lambda, diff_out_norm, rel_bias, conv_dw_w, conv_dw_b, conv_norm_g, conv_norm_b, ffn2_norm, ffn2_w_gate, ffn2_w_up, ffn2_w_down, ple_norm, ple_w_gate, ple_w_proj):
    B, L, _ = x.shape
    depth = w_in.shape[0]
    n = B * L
    xs = x.reshape(n, D_MODEL)
    rb_flat = rel_bias.reshape(-1)
    band = _bias_tiles(rel_bias)

    def row(v):
        return v.reshape(1, -1)

    ffn1_w = (ffn1_w_gate, ffn1_w_up, ffn1_w_down)
    ffn2_w = (ffn2_w_gate, ffn2_w_up, ffn2_w_down)
    w_in_b = w_in.astype(BF16)
    p_rows = p.reshape(depth, n, PLE_DIM)

    for i in range(depth):
        xs = _ffn(xs, row(ffn1_norm[i]), *ffn1_w, i)

        gqk, gg, gv, go, dq, dk, dvt, cu = _inproj(
            xs, row(mix_norm[i]), w_in_b,
            _pad_up(gla_gk_up_f[i], 0), row(gla_gk_bias_f[i]),
            _pad_up(gla_gk_up_b[i], GLA_RANK), row(gla_gk_bias_b[i]),
            row(jnp.tile(diff_q_norm[i], DIFF_QK // DIFF_DQK)), row(jnp.tile(diff_k_norm[i], DIFF_QK // DIFF_DQK)),
            i)

        y_gla = _gla(gqk, gg, gv, go, row(jnp.tile(gla_out_norm[i], N_GLA_HEADS)), B)
        lam_init = 0.8 - 0.6 * math.exp(-0.3 * i)
        score_bound = LOG2E * (DIFF_DQK ** 0.5 * jnp.max(jnp.abs(diff_q_norm[i])) * jnp.max(jnp.abs(diff_k_norm[i]))
                               + jnp.max(jnp.abs(rel_bias)))
        conv_w = jnp.concatenate([conv_dw_w[i], jnp.zeros((1, CONV_CH), F32)], axis=0)
        y_diff, y_conv = _diff_attn(rb_flat, score_bound.reshape(1), dq, dk, dvt, band, diff_lambda[i],
                                    diff_out_norm[i].reshape(-1, 1), cu, conv_w, row(conv_dw_b[i]),
                                    row(conv_norm_g[i]), row(conv_norm_b[i]), B, lam_init)

        xs = _post(xs, y_gla, y_diff, y_conv, p_rows, w_out, row(ffn2_norm[i]), *ffn2_w,
                   row(ple_norm[i]), ple_w_gate, ple_w_proj, i)
    return xs.reshape(B, L, D_MODEL)
```

```python
import functools
import math

import jax
import jax.numpy as jnp
from jax import lax
from jax.experimental import pallas as pl
from jax.experimental.pallas import tpu as pltpu

F32 = jnp.float32
BF16 = jnp.bfloat16

D_MODEL = 1024
N_GLA_HEADS = 4
GLA_DK = 32
GLA_DV = 64
GLA_Q = N_GLA_HEADS * GLA_DK
GLA_V = N_GLA_HEADS * GLA_DV
GLA_RANK = 16
GATE_TEMP = 16.0
GLA_CHUNK = 64
GLA_BLOCK = 16
GLA_CUM_CHUNKS = 4
DIFF_DQK = 64
N_DIFF_HEADS = 4
DIFF_DV = 128
DIFF_QK = 512
DIFF_V = 512
NUM_BUCKETS = 32
MAX_DISTANCE = 128
CONV_CH = 256
CONV_K = 31
D_FF = 2816
PLE_DIM = 256
NORM_EPS = 1e-6
LOG2E = math.log2(math.e)
EXP2_SAFE_RANGE = 64.0
D_IN = 2848

TOKEN_TILE = 512
INPROJ_TILE = 1024
INPROJ_SUB = 256
FF_CHUNKS = (1536, 1280)
ATTN_TILE = 256
ATTN_LOOKAHEAD = 3
ATTN_UNROLL = 4
CONV_PAD = 16
V7X_VMEM_BYTES = 64 * 1024 * 1024
VMEM_LIMIT = V7X_VMEM_BYTES * 7 // 8
WEIGHT_CHUNK_ROWS_WIDE = 128
WEIGHT_CHUNK_ROWS_TALL = 256
WEIGHT_STAGE_SLOTS = 3


def _cparams(sem):
    return pltpu.CompilerParams(dimension_semantics=sem, vmem_limit_bytes=VMEM_LIMIT)


def _dot(a, b):
    return jnp.dot(a, b, preferred_element_type=F32)


def _dot_nt(a, b):
    return lax.dot_general(a, b, (((1,), (1,)), ((), ())), preferred_element_type=F32)


def _rms_rows(x, gain):
    return x * lax.rsqrt(jnp.mean(x * x, axis=-1, keepdims=True) + NORM_EPS) * gain


def _sigmoid(x):
    return 1.0 / (1.0 + jnp.exp(-x))


def _silu(x):
    return x * _sigmoid(x)


def _const_spec(shape):
    return pl.BlockSpec(shape, lambda *_: (0,) * len(shape), pipeline_mode=pl.Buffered(1))


def _layer_spec(shape, layer):
    return pl.BlockSpec((None,) + shape, lambda *_: (layer,) + (0,) * len(shape), pipeline_mode=pl.Buffered(1))


def _hbm_spec():
    return pl.BlockSpec(memory_space=pl.ANY)


def _stage_scratch(cols):
    rows = WEIGHT_CHUNK_ROWS_WIDE if cols > D_MODEL else WEIGHT_CHUNK_ROWS_TALL
    return [pltpu.VMEM((WEIGHT_STAGE_SLOTS, rows, cols), F32), pltpu.SemaphoreType.DMA((WEIGHT_STAGE_SLOTS,))]


def _fetch_as_bf16(src_hbm, layer, dst_ref, stage_ref, sem_ref):
    slots = stage_ref.shape[0]
    chunk = min(stage_ref.shape[1], dst_ref.shape[0])
    assert dst_ref.shape[0] % chunk == 0
    n = dst_ref.shape[0] // chunk

    def dma(c):
        return pltpu.make_async_copy(src_hbm.at[layer, pl.ds(c * chunk, chunk), :],
                                     stage_ref.at[c % slots, pl.ds(0, chunk), :], sem_ref.at[c % slots])

    for c in range(min(slots - 1, n)):
        dma(c).start()
    for c in range(n):
        if c + slots - 1 < n:
            dma(c + slots - 1).start()
        dma(c).wait()
        dst_ref[c * chunk:(c + 1) * chunk, :] = stage_ref[c % slots, :chunk, :].astype(BF16)


def _swiglu_half_step(x, g_ref, wg_ref, wu_ref, wd_ref):
    h = _rms_rows(x, g_ref[...]).astype(BF16)
    acc = jnp.zeros(x.shape, F32)
    c0 = 0
    for width in FF_CHUNKS:
        gate = _dot(h, wg_ref[:, c0:c0 + width])
        up = _dot(h, wu_ref[:, c0:c0 + width])
        a = (_silu(gate) * up).astype(BF16)
        acc = acc + _dot(a, wd_ref[c0:c0 + width, :])
        c0 += width
    return x + 0.5 * acc


def _ffn_kernel(x_ref, g_ref, wg_hbm, wu_hbm, wd_hbm, o_ref,
                wg_ref, wu_ref, wd_ref, wide_ref, wide_sem, tall_ref, tall_sem, *, layer):
    @pl.when(pl.program_id(0) == 0)
    def _():
        _fetch_as_bf16(wg_hbm, layer, wg_ref, wide_ref, wide_sem)
        _fetch_as_bf16(wu_hbm, layer, wu_ref, wide_ref, wide_sem)
        _fetch_as_bf16(wd_hbm, layer, wd_ref, tall_ref, tall_sem)

    o_ref[...] = _swiglu_half_step(x_ref[...], g_ref, wg_ref, wu_ref, wd_ref)


def _ffn(x, gain, wg, wu, wd, layer):
    n = x.shape[0]
    row = pl.BlockSpec((TOKEN_TILE, D_MODEL), lambda i: (i, 0))
    return pl.pallas_call(
        functools.partial(_ffn_kernel, layer=layer),
        grid=(n // TOKEN_TILE,),
        in_specs=[row, _const_spec((1, D_MODEL)), _hbm_spec(), _hbm_spec(), _hbm_spec()],
        out_specs=row,
        out_shape=jax.ShapeDtypeStruct((n, D_MODEL), F32),
        scratch_shapes=[pltpu.VMEM((D_MODEL, D_FF), BF16), pltpu.VMEM((D_MODEL, D_FF), BF16),
                        pltpu.VMEM((D_FF, D_MODEL), BF16)] + _stage_scratch(D_FF) + _stage_scratch(D_MODEL),
        compiler_params=_cparams(("arbitrary",)),
        name="ffn",
    )(x, gain, wg, wu, wd)


def _block_diag_ones(n, blk, dtype):
    r = lax.broadcasted_iota(jnp.int32, (n, n), 0) // blk
    c = lax.broadcasted_iota(jnp.int32, (n, n), 1) // blk
    return jnp.where(r == c, 1.0, 0.0).astype(dtype)


def _group_mean_sq(x, blk):
    n = x.shape[-1]
    return _dot((x * x).astype(BF16), _block_diag_ones(n, blk, BF16)) * (1.0 / blk)


def _log_sigmoid(z):
    return jnp.minimum(z, 0.0) - jnp.log(1.0 + jnp.exp(-jnp.abs(z)))


def _inproj_kernel(x_ref, g_ref, w_ref, upf_ref, bf_ref, upb_ref, bb_ref, qn_ref, kn_ref,
                   gqk_ref, gg_ref, gv_ref, go_ref, dq_ref, dk_ref, dvt_ref, cu_ref):
    tm = x_ref.shape[0]
    nsub = tm // INPROJ_SUB

    def project(s):
        rows = slice(s * INPROJ_SUB, (s + 1) * INPROJ_SUB)
        h = _rms_rows(x_ref[rows, :], g_ref[...]).astype(BF16)
        return _dot(h, w_ref[...])

    def prepare(s, proj):
        rows = slice(s * INPROJ_SUB, (s + 1) * INPROJ_SUB)
        gqk_ref[rows, :GLA_Q] = proj[:, :GLA_Q] * (GLA_DK ** -0.5)
        gqk_ref[rows, GLA_Q:] = proj[:, GLA_Q:2 * GLA_Q]
        gv_ref[rows, :] = proj[:, 256:512].astype(BF16)
        go_ref[rows, :] = _silu(proj[:, 512:768]).astype(BF16)
        low = proj[:, 768:896].astype(BF16)
        zf = _dot(low, upf_ref[...]) + bf_ref[...]
        zb = _dot(low, upb_ref[...]) + bb_ref[...]
        gg_ref[rows, :GLA_Q] = _log_sigmoid(zf) / GATE_TEMP
        gg_ref[rows, GLA_Q:] = _log_sigmoid(zb) / GATE_TEMP
        rest = proj[:, 768 + 2 * GLA_RANK:]
        dq = rest[:, :512]
        dq_ref[rows, :] = (dq * lax.rsqrt(_group_mean_sq(dq, DIFF_DQK) + NORM_EPS) * qn_ref[...]
                           * (DIFF_DQK ** -0.5 * LOG2E)).astype(BF16)
        dk = rest[:, 512:1024]
        dk_ref[rows, :] = (dk * lax.rsqrt(_group_mean_sq(dk, DIFF_DQK) + NORM_EPS) * kn_ref[...]).astype(BF16)
        dvt_ref[:, rows] = rest[:, 1024:1536].T.astype(BF16)
        cu_ref[rows, :] = rest[:, 1536:1792] * _sigmoid(rest[:, 1792:2048])

    nxt = project(0)
    for s in range(nsub):
        cur = nxt
        if s + 1 < nsub:
            nxt = project(s + 1)
        prepare(s, cur)


def _inproj(x, gain, w, upf, bf, upb, bb, qn, kn, layer):
    n = x.shape[0]
    tm = INPROJ_TILE

    def row(width):
        return pl.BlockSpec((tm, width), lambda i: (i, 0))

    outs = [(2 * GLA_Q, F32), (2 * GLA_Q, F32), (GLA_V, BF16), (GLA_V, BF16),
            (DIFF_QK, BF16), (DIFF_QK, BF16), None, (CONV_CH, F32)]
    dvt_spec = pl.BlockSpec((DIFF_V, tm), lambda i: (0, i))
    return pl.pallas_call(
        _inproj_kernel,
        grid=(n // tm,),
        in_specs=[row(D_MODEL), _const_spec((1, D_MODEL)), _layer_spec((D_MODEL, D_IN), layer),
                  _const_spec((128, GLA_Q)), _const_spec((1, GLA_Q)),
                  _const_spec((128, GLA_Q)), _const_spec((1, GLA_Q)),
                  _const_spec((1, DIFF_QK)), _const_spec((1, DIFF_QK))],
        out_specs=[row(o[0]) if o else dvt_spec for o in outs],
        out_shape=[jax.ShapeDtypeStruct((n, o[0]), o[1]) if o else jax.ShapeDtypeStruct((DIFF_V, n), BF16)
                   for o in outs],
        compiler_params=_cparams(("parallel",)),
        name="inproj",
    )(x, gain, w, upf, bf, upb, bb, qn, kn)


def _gla_block_kernel(qk_ref, g_ref, v_ref, go_ref, on_ref, y_ref, of_ref, st_ref):
    C = GLA_CHUNK
    G = GLA_BLOCK
    R = G * C
    L = qk_ref.shape[0]
    nblk = L // R
    H = N_GLA_HEADS

    def iota(shape, d):
        return lax.broadcasted_iota(jnp.int32, shape, d)

    kmask = iota((H * C, GLA_Q), 0) // C == iota((H * C, GLA_Q), 1) // GLA_DK
    vmask = iota((H * C, GLA_V), 0) // C == iota((H * C, GLA_V), 1) // GLA_DV
    smask = iota((GLA_V, GLA_Q), 0) // GLA_DV == iota((GLA_V, GLA_Q), 1) // GLA_DK
    pos_a = iota((C, H * C), 0)
    pos_b = iota((C, H * C), 1) % C
    RC = GLA_CUM_CHUNKS * C
    ri, ci = iota((RC, RC), 0), iota((RC, RC), 1)
    same_chunk = ri // C == ci // C
    cums = [jnp.where(jnp.logical_and(same_chunk, ci <= ri), 1.0, 0.0).astype(BF16),
            jnp.where(jnp.logical_and(same_chunk, ci >= ri), 1.0, 0.0).astype(BF16)]
    amasks = [pos_b <= pos_a, pos_b >= pos_a]
    edges = [C - 1, 0]
    orders = [list(range(G)), list(range(G - 1, -1, -1))]

    st_ref[...] = jnp.zeros(st_ref.shape, F32)

    def step(t, _):
        r0 = [pl.multiple_of(t * R, R), pl.multiple_of((nblk - 1 - t) * R, R)]
        rows = [pl.ds(r0[0], R), pl.ds(r0[1], R)]
        dirs = (0, 1)
        b, k, v, qt = {}, {}, {}, {}
        for d in dirs:
            g = g_ref[rows[d], d * GLA_Q:(d + 1) * GLA_Q]
            g_hi = g.astype(BF16)
            g_lo = (g - g_hi.astype(F32)).astype(BF16)
            b[d] = jnp.concatenate([_dot(cums[d], g_hi[i:i + RC]) + _dot(cums[d], g_lo[i:i + RC])
                                    for i in range(0, R, RC)], axis=0)
        a, ds, decay = {}, {}, {}
        for d in dirs:
            q = qk_ref[rows[d], :GLA_Q]
            k[d] = qk_ref[rows[d], GLA_Q:]
            v[d] = v_ref[rows[d], :].astype(F32)
            qt[d] = (q * jnp.exp(b[d])).astype(BF16)
            kt = k[d] * jnp.exp(-b[d])
            for c in range(G):
                sl = slice(c * C, (c + 1) * C)
                b_c = b[d][sl]
                b_edge = b_c[edges[d]:edges[d] + 1, :]
                decay[d, c] = jnp.exp(b_edge)
                kend = (k[d][sl] * jnp.exp(b_edge - b_c)).astype(BF16)
                kstack = jnp.where(kmask, jnp.concatenate([kt[sl]] * H, axis=0), 0.0).astype(BF16)
                a[d, c] = jnp.where(amasks[d], _dot_nt(qt[d][sl], kstack), 0.0).astype(BF16)
                ds[d, c] = _dot(v[d][sl].T.astype(BF16), kend)
        for d in dirs:
            s = st_ref[d]
            s_in = {}
            for c in orders[d]:
                s_in[c] = s.astype(BF16)
                s = s * decay[d, c] + jnp.where(smask, ds[d, c], 0.0)
            st_ref[d] = s
            out_ref = y_ref if d else of_ref
            for c in range(G):
                sl = slice(c * C, (c + 1) * C)
                vstack = jnp.where(vmask, jnp.concatenate([v[d][sl]] * H, axis=0), 0.0).astype(BF16)
                o = _dot(a[d, c], vstack) + _dot_nt(qt[d][sl], s_in[c])
                out_ref[pl.ds(pl.multiple_of(r0[d] + c * C, C), C), :] = o
        return 0

    lax.fori_loop(0, nblk, step, 0)

    norm_bd = _block_diag_ones(GLA_V, GLA_DV, BF16)

    def finish(t, _):
        rows = pl.ds(pl.multiple_of(t * R, R), R)
        tot = of_ref[rows, :] + y_ref[rows, :]
        ms = _dot((tot * tot).astype(BF16), norm_bd) * (1.0 / GLA_DV)
        y_ref[rows, :] = tot * lax.rsqrt(ms + NORM_EPS) * on_ref[...] * go_ref[rows, :].astype(F32)
        return 0

    lax.fori_loop(0, nblk, finish, 0)


def _gla(qk, g, v, go, onorm, batch):
    n = qk.shape[0]
    L = n // batch

    def seq(width):
        return pl.BlockSpec((L, width), lambda b: (b, 0))

    return pl.pallas_call(
        _gla_block_kernel,
        grid=(batch,),
        in_specs=[seq(2 * GLA_Q), seq(2 * GLA_Q), seq(GLA_V), seq(GLA_V), _const_spec((1, GLA_V))],
        out_specs=seq(GLA_V),
        out_shape=jax.ShapeDtypeStruct((n, GLA_V), F32),
        scratch_shapes=[pltpu.VMEM((L, GLA_V), F32), pltpu.VMEM((2, GLA_V, GLA_Q), F32)],
        compiler_params=_cparams(("parallel",)),
        name="gla",
    )(qk, g, v, go, onorm)


def _bias_kernel(rb_ref, o_ref):
    T = ATTN_TILE
    h = pl.program_id(0)
    nb = NUM_BUCKETS // 2
    max_exact = nb // 2
    lane = lax.broadcasted_iota(jnp.int32, (8, 2 * T), 1)
    for d in range(5):
        rel = (d - 2) * T + T - lane
        ret = jnp.where(rel > 0, nb, 0)
        n = jnp.abs(rel)
        large = max_exact + (jnp.log(jnp.maximum(n, 1).astype(F32) / max_exact)
                             / math.log(MAX_DISTANCE / max_exact) * (nb - max_exact)).astype(jnp.int32)
        large = jnp.minimum(large, nb - 1)
        bucket = ret + jnp.where(n < max_exact, n, large)
        for m in range(2):
            row = jnp.zeros((8, 2 * T), F32)
            for bkt in range(NUM_BUCKETS):
                row = jnp.where(bucket == bkt, rb_ref[(bkt * N_DIFF_HEADS + h) * 2 + m], row)
            rows = jnp.broadcast_to(row[0:1, :] * (-LOG2E), (T, 2 * T))
            o_ref[0, d, m] = pltpu.roll(rows, 0, 1, stride=1, stride_axis=0)[:, T:]


def _bias_tiles(rel_bias):
    T = ATTN_TILE
    return pl.pallas_call(
        _bias_kernel,
        grid=(N_DIFF_HEADS,),
        in_specs=[pl.BlockSpec(memory_space=pltpu.SMEM)],
        out_specs=pl.BlockSpec((1, 5, 2, T, T), lambda h: (h, 0, 0, 0, 0)),
        out_shape=jax.ShapeDtypeStruct((N_DIFF_HEADS, 5, 2, T, T), F32),
        compiler_params=_cparams(("parallel",)),
        name="t5_bias",
    )(rel_bias.reshape(-1))


def _stage_conv_window(u_ref, uw_ref):
    L = u_ref.shape[0]
    P = CONV_PAD
    rows = L // N_DIFF_HEADS
    h = pl.program_id(0)
    start = pl.multiple_of(h * rows, rows)
    uw_ref[P:P + rows, :] = u_ref[pl.ds(start, rows), :]
    above = u_ref[pl.ds(pl.multiple_of(jnp.maximum(start - P, 0), P), P), :]
    uw_ref[:P, :] = jnp.where(h > 0, above, 0.0)
    below = u_ref[pl.ds(pl.multiple_of(jnp.minimum(start + rows, L - P), P), P), :]
    uw_ref[P + rows:, :] = jnp.where(h < N_DIFF_HEADS - 1, below, 0.0)


def _conv_rows(r0, R, uw_ref, w_ref, b_ref, g_ref, beta_ref, y_ref):
    P = CONV_PAD
    off = P - CONV_K // 2
    win = uw_ref[pl.ds(r0, R + 2 * P), :]
    acc = jnp.zeros((R, CONV_CH), F32) + b_ref[...]
    for r in range(8):
        shifted = win if r == 0 else pltpu.roll(win, R + 2 * P - r, 0)
        for k in range(CONV_K):
            if (off + k) % 8 == r:
                a8 = (off + k) // 8
                acc = acc + w_ref[k:k + 1, :] * shifted[8 * a8:8 * a8 + R, :]
    mu = jnp.mean(acc, axis=-1, keepdims=True)
    xc = acc - mu
    y = xc * lax.rsqrt(jnp.mean(xc * xc, axis=-1, keepdims=True) + NORM_EPS) * g_ref[...] + beta_ref[...]
    y_ref[pl.ds(r0, R), :] = _silu(y)


def _attn_kernel(*refs, lam_init):
    bound_ref = refs[1]
    nq = refs[2].shape[0] // ATTN_TILE
    unshifted_ok = bound_ref[0] <= EXP2_SAFE_RANGE
    _stage_conv_window(refs[8], refs[-1])

    def sweep(unshifted, group):
        conv_rows = refs[8].shape[0] // N_DIFF_HEADS // nq * group

        def body(g, _):
            _conv_rows(pl.multiple_of(g * conv_rows, conv_rows), conv_rows, refs[-1], *refs[9:13], refs[14])
            for u in range(group):
                _attn_query_tile(g * group + u, *refs, lam_init=lam_init, unshifted=unshifted)
            return 0
        lax.fori_loop(0, nq // group, body, 0)

    @pl.when(unshifted_ok)
    def _():
        sweep(True, ATTN_UNROLL)

    @pl.when(jnp.logical_not(unshifted_ok))
    def _():
        sweep(False, 1)


def _attn_query_tile(qi, rb_ref, bound_ref, q_ref, k_ref, vt_ref, band_ref, lam_ref, on_ref,
                     u_ref, cw_ref, cb_ref, cg_ref, cbeta_ref, o_ref, yc_ref,
                     m_ref, l_ref, acc_ref, uw_ref, *, lam_init, unshifted):
    T = ATTN_TILE
    L = k_ref.shape[0]
    nk = L // T
    h = pl.program_id(0)
    q = q_ref[pl.ds(pl.multiple_of(qi * T, T), T), :]
    qt = q.astype(F32).T
    chan = lax.broadcasted_iota(jnp.int32, qt.shape, 0)
    qm = [jnp.where(chan < DIFF_DQK, qt, 0.0).astype(BF16), jnp.where(chan >= DIFF_DQK, qt, 0.0).astype(BF16)]

    def side_const(bucket, m):
        return rb_ref[(bucket * N_DIFF_HEADS + h) * 2 + m] * LOG2E

    def tile_start(j):
        return pl.multiple_of(j * T, T)

    def unshifted_softmax():
        f_left = [jnp.exp2(jnp.full((1, 1), side_const(NUM_BUCKETS // 2 - 1, m), F32)).astype(BF16)
                  for m in range(2)]
        f_right = [jnp.exp2(jnp.full((1, 1), side_const(NUM_BUCKETS - 1, m), F32)).astype(BF16)
                   for m in range(2)]
        band_idx = [jnp.where(qi == 0, 4, 1), 2, jnp.where(qi == nk - 1, 0, 3)]
        l = [jnp.zeros((1, T), F32) for _ in range(2)]
        acc = [jnp.zeros((DIFF_DV, T), F32) for _ in range(2)]

        def scores(t):
            kt = k_ref[pl.ds(tile_start((qi + (2 + t)) % nk), T), :]
            return [_dot(kt, qm[m]) for m in range(2)]

        pending = [scores(t) for t in range(ATTN_LOOKAHEAD)]
        for t in range(nk):
            j = (qi + (2 + t)) % nk
            vt = vt_ref[:, pl.ds(tile_start(j), T)]
            s_cur = pending.pop(0)
            if t + ATTN_LOOKAHEAD < nk:
                pending.append(scores(t + ATTN_LOOKAHEAD))
            for m in range(2):
                if t >= nk - 3:
                    p = jnp.exp2(s_cur[m] - band_ref[0, band_idx[t - (nk - 3)], m])
                    l[m] = l[m] + jnp.sum(p, axis=0, keepdims=True)
                    acc[m] = acc[m] + _dot(vt, p.astype(BF16))
                else:
                    f = jnp.where(j < qi, f_left[m], f_right[m])
                    p = jnp.exp2(s_cur[m])
                    l[m] = l[m] + f.astype(F32) * jnp.sum(p, axis=0, keepdims=True)
                    acc[m] = acc[m] + _dot(vt * f, p.astype(BF16))
        for m in range(2):
            l_ref[m] = l[m]
            acc_ref[m] = acc[m]

    def online_softmax():
        m_ref[...] = jnp.full(m_ref.shape, -jnp.inf, F32)
        l_ref[...] = jnp.zeros(l_ref.shape, F32)
        acc_ref[...] = jnp.zeros(acc_ref.shape, F32)

        def body(j, _):
            r0 = tile_start(j)
            kt = k_ref[pl.ds(r0, T), :]
            vt = vt_ref[:, pl.ds(r0, T)]
            idx = jnp.clip(j - qi, -2, 2) + 2
            ss = [_dot(kt, qm[m]) for m in range(2)]
            for m in range(2):
                s = ss[m] - band_ref[0, idx, m]
                m_old = m_ref[m]
                m_new = jnp.maximum(m_old, jnp.max(s, axis=0, keepdims=True))
                alpha = jnp.exp2(m_old - m_new)
                p = jnp.exp2(s - m_new)
                l_ref[m] = alpha * l_ref[m] + jnp.sum(p, axis=0, keepdims=True)
                acc_ref[m] = alpha * acc_ref[m] + _dot(vt, p.astype(BF16))
                m_ref[m] = m_new
            return 0

        lax.fori_loop(0, nk, body, 0)

    if unshifted:
        unshifted_softmax()
    else:
        online_softmax()

    lp = lam_ref[...]
    lam = (jnp.exp(jnp.sum(lp[0:1] * lp[1:2], axis=-1, keepdims=True))
           - jnp.exp(jnp.sum(lp[2:3] * lp[3:4], axis=-1, keepdims=True)) + lam_init)
    o = acc_ref[0] / l_ref[0] - lam * (acc_ref[1] / l_ref[1])
    y = o * lax.rsqrt(jnp.mean(o * o, axis=0, keepdims=True) + NORM_EPS) * on_ref[...] * (1.0 - lam_init)
    o_ref[pl.ds(pl.multiple_of(qi * T, T), T), :] = y.T.astype(o_ref.dtype)


def _diff_attn(rel_bias_flat, score_bound, dq, dk, dvt, band, lam_p, onorm, cu, conv_w, conv_b, conv_g,
               conv_beta, batch, lam_init):
    n = dq.shape[0]
    L = n // batch
    T = ATTN_TILE
    share = L // N_DIFF_HEADS
    seq = pl.BlockSpec((L, 2 * DIFF_DQK), lambda h, b: (b, h))

    def const(shape):
        return pl.BlockSpec(shape, lambda h, b: (0,) * len(shape))

    return pl.pallas_call(
        functools.partial(_attn_kernel, lam_init=lam_init),
        grid=(N_DIFF_HEADS, batch),
        in_specs=[pl.BlockSpec(memory_space=pltpu.SMEM),
                  pl.BlockSpec(memory_space=pltpu.SMEM),
                  seq, seq,
                  pl.BlockSpec((DIFF_DV, L), lambda h, b: (h, b)),
                  pl.BlockSpec((1, 5, 2, T, T), lambda h, b: (h, 0, 0, 0, 0)),
                  const((4, DIFF_DQK)), const((DIFF_DV, 1)),
                  pl.BlockSpec((L, CONV_CH), lambda h, b: (b, 0)),
                  const((CONV_K + 1, CONV_CH)), const((1, CONV_CH)), const((1, CONV_CH)), const((1, CONV_CH))],
        out_specs=[pl.BlockSpec((L, DIFF_DV), lambda h, b: (b, h)),
                   pl.BlockSpec((share, CONV_CH), lambda h, b: (b * N_DIFF_HEADS + h, 0))],
        out_shape=[jax.ShapeDtypeStruct((n, DIFF_V), BF16), jax.ShapeDtypeStruct((n, CONV_CH), F32)],
        scratch_shapes=[pltpu.VMEM((2, 1, T), F32), pltpu.VMEM((2, 1, T), F32),
                        pltpu.VMEM((2, DIFF_DV, T), F32),
                        pltpu.VMEM((share + 2 * CONV_PAD, CONV_CH), F32)],
        compiler_params=_cparams(("parallel", "parallel")),
        name="diff_attn",
    )(rel_bias_flat, score_bound, dq, dk, dvt, band, lam_p, onorm, cu, conv_w, conv_b, conv_g, conv_beta)


def _post_kernel(x_ref, ya_ref, yb_ref, yc_ref, p_ref, wo_hbm, fg_ref, wg_hbm, wu_hbm, wd_hbm,
                 pg_ref, pwg_hbm, pwp_hbm, o_ref,
                 wo_ref, wg_ref, wu_ref, wd_ref, pwg_ref, pwp_ref, wide_ref, wide_sem, tall_ref, tall_sem,
                 *, layer):
    @pl.when(pl.program_id(0) == 0)
    def _():
        _fetch_as_bf16(wo_hbm, layer, wo_ref, tall_ref, tall_sem)
        _fetch_as_bf16(wg_hbm, layer, wg_ref, wide_ref, wide_sem)
        _fetch_as_bf16(wu_hbm, layer, wu_ref, wide_ref, wide_sem)
        _fetch_as_bf16(wd_hbm, layer, wd_ref, tall_ref, tall_sem)
        _fetch_as_bf16(pwg_hbm, layer, pwg_ref, tall_ref, tall_sem)
        _fetch_as_bf16(pwp_hbm, layer, pwp_ref, tall_ref, tall_sem)

    mix = _dot(ya_ref[...].astype(BF16), wo_ref[:GLA_V, :])
    mix = mix + _dot(yb_ref[...].astype(BF16), wo_ref[GLA_V:GLA_V + DIFF_V, :])
    mix = mix + _dot(yc_ref[...].astype(BF16), wo_ref[GLA_V + DIFF_V:, :])
    x = _swiglu_half_step(x_ref[...] + mix, fg_ref, wg_ref, wu_ref, wd_ref)
    h = _rms_rows(x, pg_ref[...]).astype(BF16)
    gate = _sigmoid(_dot(h, pwg_ref[...]))
    o_ref[...] = x + gate * _dot(p_ref[...].astype(BF16), pwp_ref[...])


def _post(x, ya, yb, yc, p, w_out, ffn_gain, wg, wu, wd, ple_gain, ple_wg, ple_wp, layer):
    n = x.shape[0]
    tm = TOKEN_TILE

    def row(width):
        return pl.BlockSpec((tm, width), lambda i: (i, 0))

    return pl.pallas_call(
        functools.partial(_post_kernel, layer=layer),
        grid=(n // tm,),
        in_specs=[row(D_MODEL), row(GLA_V), row(DIFF_V), row(CONV_CH),
                  pl.BlockSpec((None, tm, PLE_DIM), lambda i: (layer, i, 0)),
                  _hbm_spec(), _const_spec((1, D_MODEL)), _hbm_spec(), _hbm_spec(), _hbm_spec(),
                  _const_spec((1, D_MODEL)), _hbm_spec(), _hbm_spec()],
        out_specs=row(D_MODEL),
        out_shape=jax.ShapeDtypeStruct((n, D_MODEL), F32),
        scratch_shapes=[pltpu.VMEM((D_MODEL, D_MODEL), BF16), pltpu.VMEM((D_MODEL, D_FF), BF16),
                        pltpu.VMEM((D_MODEL, D_FF), BF16), pltpu.VMEM((D_FF, D_MODEL), BF16),
                        pltpu.VMEM((D_MODEL, D_MODEL), BF16), pltpu.VMEM((PLE_DIM, D_MODEL), BF16)]
        + _stage_scratch(D_FF) + _stage_scratch(D_MODEL),
        compiler_params=_cparams(("arbitrary",)),
        name="post",
    )(x, ya, yb, yc, p, w_out, ffn_gain, wg, wu, wd, ple_gain, ple_wg, ple_wp)


def _pad_up(up, row0):
    return jnp.zeros((128, GLA_Q), F32).at[row0:row0 + GLA_RANK].set(up).astype(BF16)


def kernel(x, p, ffn1_norm, ffn1_w_gate, ffn1_w_up, ffn1_w_down, mix_norm, w_in, w_out, gla_gk_up_f, gla_gk_bias_f, gla_gk_up_b, gla_gk_bias_b, gla_out_norm, diff_q_norm, diff_k_norm, diff_lambda, diff_out_norm, rel_bias, conv_dw_w, conv_dw_b, conv_norm_g, conv_norm_b, ffn2_norm, ffn2_w_gate, ffn2_w_up, ffn2_w_down, ple_norm, ple_w_gate, ple_w_proj):
    B, L, _ = x.shape
    depth = w_in.shape[0]
    n = B * L
    xs = x.reshape(n, D_MODEL)
    rb_flat = rel_bias.reshape(-1)
    band = _bias_tiles(rel_bias)

    def row(v):
        return v.reshape(1, -1)

    ffn1_w = (ffn1_w_gate, ffn1_w_up, ffn1_w_down)
    ffn2_w = (ffn2_w_gate, ffn2_w_up, ffn2_w_down)
    w_in_b = w_in.astype(BF16)
    p_rows = p.reshape(depth, n, PLE_DIM)

    for i in range(depth):
        xs = _ffn(xs, row(ffn1_norm[i]), *ffn1_w, i)

        gqk, gg, gv, go, dq, dk, dvt, cu = _inproj(
            xs, row(mix_norm[i]), w_in_b,
            _pad_up(gla_gk_up_f[i], 0), row(gla_gk_bias_f[i]),
            _pad_up(gla_gk_up_b[i], GLA_RANK), row(gla_gk_bias_b[i]),
            row(jnp.tile(diff_q_norm[i], DIFF_QK // DIFF_DQK)), row(jnp.tile(diff_k_norm[i], DIFF_QK // DIFF_DQK)),
            i)

        y_gla = _gla(gqk, gg, gv, go, row(jnp.tile(gla_out_norm[i], N_GLA_HEADS)), B)
        lam_init = 0.8 - 0.6 * math.exp(-0.3 * i)
        score_bound = LOG2E * (DIFF_DQK ** 0.5 * jnp.max(jnp.abs(diff_q_norm[i])) * jnp.max(jnp.abs(diff_k_norm[i]))
                               + jnp.max(jnp.abs(rel_bias)))
        conv_w = jnp.concatenate([conv_dw_w[i], jnp.zeros((1, CONV_CH), F32)], axis=0)
        y_diff, y_conv = _diff_attn(rb_flat, score_bound.reshape(1), dq, dk, dvt, band, diff_lambda[i],
                                    diff_out_norm[i].reshape(-1, 1), cu, conv_w, row(conv_dw_b[i]),
                                    row(conv_norm_g[i]), row(conv_norm_b[i]), B, lam_init)

        xs = _post(xs, y_gla, y_diff, y_conv, p_rows, w_out, row(ffn2_norm[i]), *ffn2_w,
                   row(ple_norm[i]), ple_w_gate, ple_w_proj, i)
    return xs.reshape(B, L, D_MODEL)
```

```python
import functools
import math

import jax
import jax.numpy as jnp
from jax import lax
from jax.experimental import pallas as pl
from jax.experimental.pallas import tpu as pltpu

F32 = jnp.float32
BF16 = jnp.bfloat16

D_MODEL = 1024
N_GLA_HEADS = 4
GLA_DK = 32
GLA_DV = 64
GLA_Q = N_GLA_HEADS * GLA_DK
GLA_V = N_GLA_HEADS * GLA_DV
GLA_RANK = 16
GATE_TEMP = 16.0
GLA_CHUNK = 64
GLA_BLOCK = 16
GLA_CUM_CHUNKS = 4
DIFF_DQK = 64
N_DIFF_HEADS = 4
DIFF_DV = 128
DIFF_QK = 512
DIFF_V = 512
NUM_BUCKETS = 32
MAX_DISTANCE = 128
CONV_CH = 256
CONV_K = 31
D_FF = 2816
PLE_DIM = 256
NORM_EPS = 1e-6
LOG2E = math.log2(math.e)
EXP2_SAFE_RANGE = 64.0
D_IN = 2848

TOKEN_TILE = 512
INPROJ_TILE = 1024
INPROJ_SUB = 256
FF_CHUNKS = (1536, 1280)
ATTN_TILE = 256
ATTN_LOOKAHEAD = 3
ATTN_UNROLL = 4
CONV_PAD = 16
V7X_VMEM_BYTES = 64 * 1024 * 1024
VMEM_LIMIT = V7X_VMEM_BYTES * 7 // 8
WEIGHT_CHUNK_ROWS_WIDE = 128
WEIGHT_CHUNK_ROWS_TALL = 256
WEIGHT_STAGE_SLOTS = 3


def _cparams(sem):
    return pltpu.CompilerParams(dimension_semantics=sem, vmem_limit_bytes=VMEM_LIMIT)


def _dot(a, b):
    return jnp.dot(a, b, preferred_element_type=F32)


def _dot_nt(a, b):
    return lax.dot_general(a, b, (((1,), (1,)), ((), ())), preferred_element_type=F32)


def _rms_rows(x, gain):
    return x * lax.rsqrt(jnp.mean(x * x, axis=-1, keepdims=True) + NORM_EPS) * gain


def _sigmoid(x):
    return 1.0 / (1.0 + jnp.exp(-x))


def _silu(x):
    return x * _sigmoid(x)


def _const_spec(shape):
    return pl.BlockSpec(shape, lambda *_: (0,) * len(shape), pipeline_mode=pl.Buffered(1))


def _layer_spec(shape, layer):
    return pl.BlockSpec((None,) + shape, lambda *_: (layer,) + (0,) * len(shape), pipeline_mode=pl.Buffered(1))


def _hbm_spec():
    return pl.BlockSpec(memory_space=pl.ANY)


def _stage_scratch(cols):
    rows = WEIGHT_CHUNK_ROWS_WIDE if cols > D_MODEL else WEIGHT_CHUNK_ROWS_TALL
    return [pltpu.VMEM((WEIGHT_STAGE_SLOTS, rows, cols), F32), pltpu.SemaphoreType.DMA((WEIGHT_STAGE_SLOTS,))]


def _fetch_as_bf16(src_hbm, layer, dst_ref, stage_ref, sem_ref):
    slots = stage_ref.shape[0]
    chunk = min(stage_ref.shape[1], dst_ref.shape[0])
    assert dst_ref.shape[0] % chunk == 0
    n = dst_ref.shape[0] // chunk

    def dma(c):
        return pltpu.make_async_copy(src_hbm.at[layer, pl.ds(c * chunk, chunk), :],
                                     stage_ref.at[c % slots, pl.ds(0, chunk), :], sem_ref.at[c % slots])

    for c in range(min(slots - 1, n)):
        dma(c).start()
    for c in range(n):
        if c + slots - 1 < n:
            dma(c + slots - 1).start()
        dma(c).wait()
        dst_ref[c * chunk:(c + 1) * chunk, :] = stage_ref[c % slots, :chunk, :].astype(BF16)


def _swiglu_half_step(x, g_ref, wg_ref, wu_ref, wd_ref):
    h = _rms_rows(x, g_ref[...]).astype(BF16)
    acc = jnp.zeros(x.shape, F32)
    c0 = 0
    for width in FF_CHUNKS:
        gate = _dot(h, wg_ref[:, c0:c0 + width])
        up = _dot(h, wu_ref[:, c0:c0 + width])
        a = (_silu(gate) * up).astype(BF16)
        acc = acc + _dot(a, wd_ref[c0:c0 + width, :])
        c0 += width
    return x + 0.5 * acc


def _ffn_kernel(x_ref, g_ref, wg_hbm, wu_hbm, wd_hbm, o_ref,
                wg_ref, wu_ref, wd_ref, wide_ref, wide_sem, tall_ref, tall_sem, *, layer):
    @pl.when(pl.program_id(0) == 0)
    def _():
        _fetch_as_bf16(wg_hbm, layer, wg_ref, wide_ref, wide_sem)
        _fetch_as_bf16(wu_hbm, layer, wu_ref, wide_ref, wide_sem)
        _fetch_as_bf16(wd_hbm, layer, wd_ref, tall_ref, tall_sem)

    o_ref[...] = _swiglu_half_step(x_ref[...], g_ref, wg_ref, wu_ref, wd_ref)


def _ffn(x, gain, wg, wu, wd, layer):
    n = x.shape[0]
    row = pl.BlockSpec((TOKEN_TILE, D_MODEL), lambda i: (i, 0))
    return pl.pallas_call(
        functools.partial(_ffn_kernel, layer=layer),
        grid=(n // TOKEN_TILE,),
        in_specs=[row, _const_spec((1, D_MODEL)), _hbm_spec(), _hbm_spec(), _hbm_spec()],
        out_specs=row,
        out_shape=jax.ShapeDtypeStruct((n, D_MODEL), F32),
        scratch_shapes=[pltpu.VMEM((D_MODEL, D_FF), BF16), pltpu.VMEM((D_MODEL, D_FF), BF16),
                        pltpu.VMEM((D_FF, D_MODEL), BF16)] + _stage_scratch(D_FF) + _stage_scratch(D_MODEL),
        compiler_params=_cparams(("arbitrary",)),
        name="ffn",
    )(x, gain, wg, wu, wd)


def _block_diag_ones(n, blk, dtype):
    r = lax.broadcasted_iota(jnp.int32, (n, n), 0) // blk
    c = lax.broadcasted_iota(jnp.int32, (n, n), 1) // blk
    return jnp.where(r == c, 1.0, 0.0).astype(dtype)


def _group_mean_sq(x, blk):
    n = x.shape[-1]
    return _dot((x * x).astype(BF16), _block_diag_ones(n, blk, BF16)) * (1.0 / blk)


def _log_sigmoid(z):
    return jnp.minimum(z, 0.0) - jnp.log(1.0 + jnp.exp(-jnp.abs(z)))


def _inproj_kernel(x_ref, g_ref, w_ref, upf_ref, bf_ref, upb_ref, bb_ref, qn_ref, kn_ref,
                   gqk_ref, gg_ref, gv_ref, go_ref, dq_ref, dk_ref, dvt_ref, cu_ref):
    tm = x_ref.shape[0]
    nsub = tm // INPROJ_SUB

    def project(s):
        rows = slice(s * INPROJ_SUB, (s + 1) * INPROJ_SUB)
        h = _rms_rows(x_ref[rows, :], g_ref[...]).astype(BF16)
        return _dot(h, w_ref[...])

    def prepare(s, proj):
        rows = slice(s * INPROJ_SUB, (s + 1) * INPROJ_SUB)
        gqk_ref[rows, :GLA_Q] = proj[:, :GLA_Q] * (GLA_DK ** -0.5)
        gqk_ref[rows, GLA_Q:] = proj[:, GLA_Q:2 * GLA_Q]
        gv_ref[rows, :] = proj[:, 256:512].astype(BF16)
        go_ref[rows, :] = _silu(proj[:, 512:768]).astype(BF16)
        low = proj[:, 768:896].astype(BF16)
        zf = _dot(low, upf_ref[...]) + bf_ref[...]
        zb = _dot(low, upb_ref[...]) + bb_ref[...]
        gg_ref[rows, :GLA_Q] = _log_sigmoid(zf) / GATE_TEMP
        gg_ref[rows, GLA_Q:] = _log_sigmoid(zb) / GATE_TEMP
        rest = proj[:, 768 + 2 * GLA_RANK:]
        dq = rest[:, :512]
        dq_ref[rows, :] = (dq * lax.rsqrt(_group_mean_sq(dq, DIFF_DQK) + NORM_EPS) * qn_ref[...]
                           * (DIFF_DQK ** -0.5 * LOG2E)).astype(BF16)
        dk = rest[:, 512:1024]
        dk_ref[rows, :] = (dk * lax.rsqrt(_group_mean_sq(dk, DIFF_DQK) + NORM_EPS) * kn_ref[...]).astype(BF16)
        dvt_ref[:, rows] = rest[:, 1024:1536].T.astype(BF16)
        cu_ref[rows, :] = rest[:, 1536:1792] * _sigmoid(rest[:, 1792:2048])

    nxt = project(0)
    for s in range(nsub):
        cur = nxt
        if s + 1 < nsub:
            nxt = project(s + 1)
        prepare(s, cur)


def _inproj(x, gain, w, upf, bf, upb, bb, qn, kn, layer):
    n = x.shape[0]
    tm = INPROJ_TILE

    def row(width):
        return pl.BlockSpec((tm, width), lambda i: (i, 0))

    outs = [(2 * GLA_Q, F32), (2 * GLA_Q, F32), (GLA_V, BF16), (GLA_V, BF16),
            (DIFF_QK, BF16), (DIFF_QK, BF16), None, (CONV_CH, F32)]
    dvt_spec = pl.BlockSpec((DIFF_V, tm), lambda i: (0, i))
    return pl.pallas_call(
        _inproj_kernel,
        grid=(n // tm,),
        in_specs=[row(D_MODEL), _const_spec((1, D_MODEL)), _layer_spec((D_MODEL, D_IN), layer),
                  _const_spec((128, GLA_Q)), _const_spec((1, GLA_Q)),
                  _const_spec((128, GLA_Q)), _const_spec((1, GLA_Q)),
                  _const_spec((1, DIFF_QK)), _const_spec((1, DIFF_QK))],
        out_specs=[row(o[0]) if o else dvt_spec for o in outs],
        out_shape=[jax.ShapeDtypeStruct((n, o[0]), o[1]) if o else jax.ShapeDtypeStruct((DIFF_V, n), BF16)
                   for o in outs],
        compiler_params=_cparams(("parallel",)),
        name="inproj",
    )(x, gain, w, upf, bf, upb, bb, qn, kn)


def _gla_block_kernel(qk_ref, g_ref, v_ref, go_ref, on_ref, y_ref, of_ref, st_ref):
    C = GLA_CHUNK
    G = GLA_BLOCK
    R = G * C
    L = qk_ref.shape[0]
    nblk = L // R
    H = N_GLA_HEADS

    def iota(shape, d):
        return lax.broadcasted_iota(jnp.int32, shape, d)

    kmask = iota((H * C, GLA_Q), 0) // C == iota((H * C, GLA_Q), 1) // GLA_DK
    vmask = iota((H * C, GLA_V), 0) // C == iota((H * C, GLA_V), 1) // GLA_DV
    smask = iota((GLA_V, GLA_Q), 0) // GLA_DV == iota((GLA_V, GLA_Q), 1) // GLA_DK
    pos_a = iota((C, H * C), 0)
    pos_b = iota((C, H * C), 1) % C
    RC = GLA_CUM_CHUNKS * C
    ri, ci = iota((RC, RC), 0), iota((RC, RC), 1)
    same_chunk = ri // C == ci // C
    cums = [jnp.where(jnp.logical_and(same_chunk, ci <= ri), 1.0, 0.0).astype(BF16),
            jnp.where(jnp.logical_and(same_chunk, ci >= ri), 1.0, 0.0).astype(BF16)]
    amasks = [pos_b <= pos_a, pos_b >= pos_a]
    edges = [C - 1, 0]
    orders = [list(range(G)), list(range(G - 1, -1, -1))]

    st_ref[...] = jnp.zeros(st_ref.shape, F32)

    def step(t, _):
        r0 = [pl.multiple_of(t * R, R), pl.multiple_of((nblk - 1 - t) * R, R)]
        rows = [pl.ds(r0[0], R), pl.ds(r0[1], R)]
        dirs = (0, 1)
        b, k, v, qt = {}, {}, {}, {}
        for d in dirs:
            g = g_ref[rows[d], d * GLA_Q:(d + 1) * GLA_Q]
            g_hi = g.astype(BF16)
            g_lo = (g - g_hi.astype(F32)).astype(BF16)
            b[d] = jnp.concatenate([_dot(cums[d], g_hi[i:i + RC]) + _dot(cums[d], g_lo[i:i + RC])
                                    for i in range(0, R, RC)], axis=0)
        a, ds, decay = {}, {}, {}
        for d in dirs:
            q = qk_ref[rows[d], :GLA_Q]
            k[d] = qk_ref[rows[d], GLA_Q:]
            v[d] = v_ref[rows[d], :].astype(F32)
            qt[d] = (q * jnp.exp(b[d])).astype(BF16)
            kt = k[d] * jnp.exp(-b[d])
            for c in range(G):
                sl = slice(c * C, (c + 1) * C)
                b_c = b[d][sl]
                b_edge = b_c[edges[d]:edges[d] + 1, :]
                decay[d, c] = jnp.exp(b_edge)
                kend = (k[d][sl] * jnp.exp(b_edge - b_c)).astype(BF16)
                kstack = jnp.where(kmask, jnp.concatenate([kt[sl]] * H, axis=0), 0.0).astype(BF16)
                a[d, c] = jnp.where(amasks[d], _dot_nt(qt[d][sl], kstack), 0.0).astype(BF16)
                ds[d, c] = _dot(v[d][sl].T.astype(BF16), kend)
        for d in dirs:
            s = st_ref[d]
            s_in = {}
            for c in orders[d]:
                s_in[c] = s.astype(BF16)
                s = s * decay[d, c] + jnp.where(smask, ds[d, c], 0.0)
            st_ref[d] = s
            out_ref = y_ref if d else of_ref
            for c in range(G):
                sl = slice(c * C, (c + 1) * C)
                vstack = jnp.where(vmask, jnp.concatenate([v[d][sl]] * H, axis=0), 0.0).astype(BF16)
                o = _dot(a[d, c], vstack) + _dot_nt(qt[d][sl], s_in[c])
                out_ref[pl.ds(pl.multiple_of(r0[d] + c * C, C), C), :] = o
        return 0

    lax.fori_loop(0, nblk, step, 0)

    norm_bd = _block_diag_ones(GLA_V, GLA_DV, BF16)

    def finish(t, _):
        rows = pl.ds(pl.multiple_of(t * R, R), R)
        tot = of_ref[rows, :] + y_ref[rows, :]
        ms = _dot((tot * tot).astype(BF16), norm_bd) * (1.0 / GLA_DV)
        y_ref[rows, :] = tot * lax.rsqrt(ms + NORM_EPS) * on_ref[...] * go_ref[rows, :].astype(F32)
        return 0

    lax.fori_loop(0, nblk, finish, 0)


def _gla(qk, g, v, go, onorm, batch):
    n = qk.shape[0]
    L = n // batch

    def seq(width):
        return pl.BlockSpec((L, width), lambda b: (b, 0))

    return pl.pallas_call(
        _gla_block_kernel,
        grid=(batch,),
        in_specs=[seq(2 * GLA_Q), seq(2 * GLA_Q), seq(GLA_V), seq(GLA_V), _const_spec((1, GLA_V))],
        out_specs=seq(GLA_V),
        out_shape=jax.ShapeDtypeStruct((n, GLA_V), F32),
        scratch_shapes=[pltpu.VMEM((L, GLA_V), F32), pltpu.VMEM((2, GLA_V, GLA_Q), F32)],
        compiler_params=_cparams(("parallel",)),
        name="gla",
    )(qk, g, v, go, onorm)


def _bias_kernel(rb_ref, o_ref):
    T = ATTN_TILE
    h = pl.program_id(0)
    nb = NUM_BUCKETS // 2
    max_exact = nb // 2
    lane = lax.broadcasted_iota(jnp.int32, (8, 2 * T), 1)
    for d in range(5):
        rel = (d - 2) * T + T - lane
        ret = jnp.where(rel > 0, nb, 0)
        n = jnp.abs(rel)
        large = max_exact + (jnp.log(jnp.maximum(n, 1).astype(F32) / max_exact)
                             / math.log(MAX_DISTANCE / max_exact) * (nb - max_exact)).astype(jnp.int32)
        large = jnp.minimum(large, nb - 1)
        bucket = ret + jnp.where(n < max_exact, n, large)
        for m in range(2):
            row = jnp.zeros((8, 2 * T), F32)
            for bkt in range(NUM_BUCKETS):
                row = jnp.where(bucket == bkt, rb_ref[(bkt * N_DIFF_HEADS + h) * 2 + m], row)
            rows = jnp.broadcast_to(row[0:1, :] * (-LOG2E), (T, 2 * T))
            o_ref[0, d, m] = pltpu.roll(rows, 0, 1, stride=1, stride_axis=0)[:, T:]


def _bias_tiles(rel_bias):
    T = ATTN_TILE
    return pl.pallas_call(
        _bias_kernel,
        grid=(N_DIFF_HEADS,),
        in_specs=[pl.BlockSpec(memory_space=pltpu.SMEM)],
        out_specs=pl.BlockSpec((1, 5, 2, T, T), lambda h: (h, 0, 0, 0, 0)),
        out_shape=jax.ShapeDtypeStruct((N_DIFF_HEADS, 5, 2, T, T), F32),
        compiler_params=_cparams(("parallel",)),
        name="t5_bias",
    )(rel_bias.reshape(-1))


def _stage_conv_window(u_ref, uw_ref):
    L = u_ref.shape[0]
    P = CONV_PAD
    rows = L // N_DIFF_HEADS
    h = pl.program_id(0)
    start = pl.multiple_of(h * rows, rows)
    uw_ref[P:P + rows, :] = u_ref[pl.ds(start, rows), :]
    above = u_ref[pl.ds(pl.multiple_of(jnp.maximum(start - P, 0), P), P), :]
    uw_ref[:P, :] = jnp.where(h > 0, above, 0.0)
    below = u_ref[pl.ds(pl.multiple_of(jnp.minimum(start + rows, L - P), P), P), :]
    uw_ref[P + rows:, :] = jnp.where(h < N_DIFF_HEADS - 1, below, 0.0)


def _conv_rows(r0, R, uw_ref, w_ref, b_ref, g_ref, beta_ref, y_ref):
    P = CONV_PAD
    off = P - CONV_K // 2
    win = uw_ref[pl.ds(r0, R + 2 * P), :]
    acc = jnp.zeros((R, CONV_CH), F32) + b_ref[...]
    for r in range(8):
        shifted = win if r == 0 else pltpu.roll(win, R + 2 * P - r, 0)
        for k in range(CONV_K):
            if (off + k) % 8 == r:
                a8 = (off + k) // 8
                acc = acc + w_ref[k:k + 1, :] * shifted[8 * a8:8 * a8 + R, :]
    mu = jnp.mean(acc, axis=-1, keepdims=True)
    xc = acc - mu
    y = xc * lax.rsqrt(jnp.mean(xc * xc, axis=-1, keepdims=True) + NORM_EPS) * g_ref[...] + beta_ref[...]
    y_ref[pl.ds(r0, R), :] = _silu(y)


def _attn_kernel(*refs, lam_init):
    bound_ref = refs[1]
    nq = refs[2].shape[0] // ATTN_TILE
    unshifted_ok = bound_ref[0] <= EXP2_SAFE_RANGE
    _stage_conv_window(refs[8], refs[-1])

    def sweep(unshifted, group):
        conv_rows = refs[8].shape[0] // N_DIFF_HEADS // nq * group

        def body(g, _):
            for u in range(group):
                _attn_query_tile(g * group + u, *refs, lam_init=lam_init, unshifted=unshifted)
            _conv_rows(pl.multiple_of(g * conv_rows, conv_rows), conv_rows, refs[-1], *refs[9:13], refs[14])
            return 0
        lax.fori_loop(0, nq // group, body, 0)

    @pl.when(unshifted_ok)
    def _():
        sweep(True, ATTN_UNROLL)

    @pl.when(jnp.logical_not(unshifted_ok))
    def _():
        sweep(False, 1)


def _attn_query_tile(qi, rb_ref, bound_ref, q_ref, k_ref, vt_ref, band_ref, lam_ref, on_ref,
                     u_ref, cw_ref, cb_ref, cg_ref, cbeta_ref, o_ref, yc_ref,
                     m_ref, l_ref, acc_ref, uw_ref, *, lam_init, unshifted):
    T = ATTN_TILE
    L = k_ref.shape[0]
    nk = L // T
    h = pl.program_id(0)
    q = q_ref[pl.ds(pl.multiple_of(qi * T, T), T), :]
    qt = q.astype(F32).T
    chan = lax.broadcasted_iota(jnp.int32, qt.shape, 0)
    qm = [jnp.where(chan < DIFF_DQK, qt, 0.0).astype(BF16), jnp.where(chan >= DIFF_DQK, qt, 0.0).astype(BF16)]

    def side_const(bucket, m):
        return rb_ref[(bucket * N_DIFF_HEADS + h) * 2 + m] * LOG2E

    def tile_start(j):
        return pl.multiple_of(j * T, T)

    def unshifted_softmax():
        f_left = [jnp.exp2(jnp.full((1, 1), side_const(NUM_BUCKETS // 2 - 1, m), F32)).astype(BF16)
                  for m in range(2)]
        f_right = [jnp.exp2(jnp.full((1, 1), side_const(NUM_BUCKETS - 1, m), F32)).astype(BF16)
                   for m in range(2)]
        band_idx = [jnp.where(qi == 0, 4, 1), 2, jnp.where(qi == nk - 1, 0, 3)]
        l = [jnp.zeros((1, T), F32) for _ in range(2)]
        acc = [jnp.zeros((DIFF_DV, T), F32) for _ in range(2)]

        def scores(t):
            kt = k_ref[pl.ds(tile_start((qi + (2 + t)) % nk), T), :]
            return [_dot(kt, qm[m]) for m in range(2)]

        pending = [scores(t) for t in range(ATTN_LOOKAHEAD)]
        for t in range(nk):
            j = (qi + (2 + t)) % nk
            vt = vt_ref[:, pl.ds(tile_start(j), T)]
            s_cur = pending.pop(0)
            if t + ATTN_LOOKAHEAD < nk:
                pending.append(scores(t + ATTN_LOOKAHEAD))
            for m in range(2):
                if t >= nk - 3:
                    p = jnp.exp2(s_cur[m] - band_ref[0, band_idx[t - (nk - 3)], m])
                    l[m] = l[m] + jnp.sum(p, axis=0, keepdims=True)
                    acc[m] = acc[m] + _dot(vt, p.astype(BF16))
                else:
                    f = jnp.where(j < qi, f_left[m], f_right[m])
                    p = jnp.exp2(s_cur[m])
                    l[m] = l[m] + f.astype(F32) * jnp.sum(p, axis=0, keepdims=True)
                    acc[m] = acc[m] + _dot(vt * f, p.astype(BF16))
        for m in range(2):
            l_ref[m] = l[m]
            acc_ref[m] = acc[m]

    def online_softmax():
        m_ref[...] = jnp.full(m_ref.shape, -jnp.inf, F32)
        l_ref[...] = jnp.zeros(l_ref.shape, F32)
        acc_ref[...] = jnp.zeros(acc_ref.shape, F32)

        def body(j, _):
            r0 = tile_start(j)
            kt = k_ref[pl.ds(r0, T), :]
            vt = vt_ref[:, pl.ds(r0, T)]
            idx = jnp.clip(j - qi, -2, 2) + 2
            ss = [_dot(kt, qm[m]) for m in range(2)]
            for m in range(2):
                s = ss[m] - band_ref[0, idx, m]
                m_old = m_ref[m]
                m_new = jnp.maximum(m_old, jnp.max(s, axis=0, keepdims=True))
                alpha = jnp.exp2(m_old - m_new)
                p = jnp.exp2(s - m_new)
                l_ref[m] = alpha * l_ref[m] + jnp.sum(p, axis=0, keepdims=True)
                acc_ref[m] = alpha * acc_ref[m] + _dot(vt, p.astype(BF16))
                m_ref[m] = m_new
            return 0

        lax.fori_loop(0, nk, body, 0)

    if unshifted:
        unshifted_softmax()
    else:
        online_softmax()

    lp = lam_ref[...]
    lam = (jnp.exp(jnp.sum(lp[0:1] * lp[1:2], axis=-1, keepdims=True))
           - jnp.exp(jnp.sum(lp[2:3] * lp[3:4], axis=-1, keepdims=True)) + lam_init)
    o = acc_ref[0] / l_ref[0] - lam * (acc_ref[1] / l_ref[1])
    y = o * lax.rsqrt(jnp.mean(o * o, axis=0, keepdims=True) + NORM_EPS) * on_ref[...] * (1.0 - lam_init)
    o_ref[pl.ds(pl.multiple_of(qi * T, T), T), :] = y.T.astype(o_ref.dtype)


def _diff_attn(rel_bias_flat, score_bound, dq, dk, dvt, band, lam_p, onorm, cu, conv_w, conv_b, conv_g,
               conv_beta, batch, lam_init):
    n = dq.shape[0]
    L = n // batch
    T = ATTN_TILE
    share = L // N_DIFF_HEADS
    seq = pl.BlockSpec((L, 2 * DIFF_DQK), lambda h, b: (b, h))

    def const(shape):
        return pl.BlockSpec(shape, lambda h, b: (0,) * len(shape))

    return pl.pallas_call(
        functools.partial(_attn_kernel, lam_init=lam_init),
        grid=(N_DIFF_HEADS, batch),
        in_specs=[pl.BlockSpec(memory_space=pltpu.SMEM),
                  pl.BlockSpec(memory_space=pltpu.SMEM),
                  seq, seq,
                  pl.BlockSpec((DIFF_DV, L), lambda h, b: (h, b)),
                  pl.BlockSpec((1, 5, 2, T, T), lambda h, b: (h, 0, 0, 0, 0)),
                  const((4, DIFF_DQK)), const((DIFF_DV, 1)),
                  pl.BlockSpec((L, CONV_CH), lambda h, b: (b, 0)),
                  const((CONV_K + 1, CONV_CH)), const((1, CONV_CH)), const((1, CONV_CH)), const((1, CONV_CH))],
        out_specs=[pl.BlockSpec((L, DIFF_DV), lambda h, b: (b, h)),
                   pl.BlockSpec((share, CONV_CH), lambda h, b: (b * N_DIFF_HEADS + h, 0))],
        out_shape=[jax.ShapeDtypeStruct((n, DIFF_V), BF16), jax.ShapeDtypeStruct((n, CONV_CH), F32)],
        scratch_shapes=[pltpu.VMEM((2, 1, T), F32), pltpu.VMEM((2, 1, T), F32),
                        pltpu.VMEM((2, DIFF_DV, T), F32),
                        pltpu.VMEM((share + 2 * CONV_PAD, CONV_CH), F32)],
        compiler_params=_cparams(("parallel", "parallel")),
        name="diff_attn",
    )(rel_bias_flat, score_bound, dq, dk, dvt, band, lam_p, onorm, cu, conv_w, conv_b, conv_g, conv_beta)


def _post_kernel(x_ref, ya_ref, yb_ref, yc_ref, p_ref, wo_hbm, fg_ref, wg_hbm, wu_hbm, wd_hbm,
                 pg_ref, pwg_hbm, pwp_hbm, o_ref,
                 wo_ref, wg_ref, wu_ref, wd_ref, pwg_ref, pwp_ref, wide_ref, wide_sem, tall_ref, tall_sem,
                 *, layer):
    @pl.when(pl.program_id(0) == 0)
    def _():
        _fetch_as_bf16(wo_hbm, layer, wo_ref, tall_ref, tall_sem)
        _fetch_as_bf16(wg_hbm, layer, wg_ref, wide_ref, wide_sem)
        _fetch_as_bf16(wu_hbm, layer, wu_ref, wide_ref, wide_sem)
        _fetch_as_bf16(wd_hbm, layer, wd_ref, tall_ref, tall_sem)
        _fetch_as_bf16(pwg_hbm, layer, pwg_ref, tall_ref, tall_sem)
        _fetch_as_bf16(pwp_hbm, layer, pwp_ref, tall_ref, tall_sem)

    mix = _dot(ya_ref[...].astype(BF16), wo_ref[:GLA_V, :])
    mix = mix + _dot(yb_ref[...].astype(BF16), wo_ref[GLA_V:GLA_V + DIFF_V, :])
    mix = mix + _dot(yc_ref[...].astype(BF16), wo_ref[GLA_V + DIFF_V:, :])
    x = _swiglu_half_step(x_ref[...] + mix, fg_ref, wg_ref, wu_ref, wd_ref)
    h = _rms_rows(x, pg_ref[...]).astype(BF16)
    gate = _sigmoid(_dot(h, pwg_ref[...]))
    o_ref[...] = x + gate * _dot(p_ref[...].astype(BF16), pwp_ref[...])


def _post(x, ya, yb, yc, p, w_out, ffn_gain, wg, wu, wd, ple_gain, ple_wg, ple_wp, layer):
    n = x.shape[0]
    tm = TOKEN_TILE

    def row(width):
        return pl.BlockSpec((tm, width), lambda i: (i, 0))

    return pl.pallas_call(
        functools.partial(_post_kernel, layer=layer),
        grid=(n // tm,),
        in_specs=[row(D_MODEL), row(GLA_V), row(DIFF_V), row(CONV_CH),
                  pl.BlockSpec((None, tm, PLE_DIM), lambda i: (layer, i, 0)),
                  _hbm_spec(), _const_spec((1, D_MODEL)), _hbm_spec(), _hbm_spec(), _hbm_spec(),
                  _const_spec((1, D_MODEL)), _hbm_spec(), _hbm_spec()],
        out_specs=row(D_MODEL),
        out_shape=jax.ShapeDtypeStruct((n, D_MODEL), F32),
        scratch_shapes=[pltpu.VMEM((D_MODEL, D_MODEL), BF16), pltpu.VMEM((D_MODEL, D_FF), BF16),
                        pltpu.VMEM((D_MODEL, D_FF), BF16), pltpu.VMEM((D_FF, D_MODEL), BF16),
                        pltpu.VMEM((D_MODEL, D_MODEL), BF16), pltpu.VMEM((PLE_DIM, D_MODEL), BF16)]
        + _stage_scratch(D_FF) + _stage_scratch(D_MODEL),
        compiler_params=_cparams(("arbitrary",)),
        name="post",
    )(x, ya, yb, yc, p, w_out, ffn_gain, wg, wu, wd, ple_gain, ple_wg, ple_wp)


def _pad_up(up, row0):
    return jnp.zeros((128, GLA_Q), F32).at[row0:row0 + GLA_RANK].set(up).astype(BF16)


def kernel(x, p, ffn1_norm, ffn1_w_gate, ffn1_w_up, ffn1_w_down, mix_norm, w_in, w_out, gla_gk_up_f, gla_gk_bias_f, gla_gk_up_b, gla_gk_bias_b, gla_out_norm, diff_q_norm, diff_k_norm, diff_lambda, diff_out_norm, rel_bias, conv_dw_w, conv_dw_b, conv_norm_g, conv_norm_b, ffn2_norm, ffn2_w_gate, ffn2_w_up, ffn2_w_down, ple_norm, ple_w_gate, ple_w_proj):
    B, L, _ = x.shape
    depth = w_in.shape[0]
    n = B * L
    xs = x.reshape(n, D_MODEL)
    rb_flat = rel_bias.reshape(-1)
    band = _bias_tiles(rel_bias)

    def row(v):
        return v.reshape(1, -1)

    ffn1_w = (ffn1_w_gate, ffn1_w_up, ffn1_w_down)
    ffn2_w = (ffn2_w_gate, ffn2_w_up, ffn2_w_down)
    w_in_b = w_in.astype(BF16)
    p_rows = p.reshape(depth, n, PLE_DIM)

    for i in range(depth):
        xs = _ffn(xs, row(ffn1_norm[i]), *ffn1_w, i)

        gqk, gg, gv, go, dq, dk, dvt, cu = _inproj(
            xs, row(mix_norm[i]), w_in_b,
            _pad_up(gla_gk_up_f[i], 0), row(gla_gk_bias_f[i]),
            _pad_up(gla_gk_up_b[i], GLA_RANK), row(gla_gk_bias_b[i]),
            row(jnp.tile(diff_q_norm[i], DIFF_QK // DIFF_DQK)), row(jnp.tile(diff_k_norm[i], DIFF_QK // DIFF_DQK)),
            i)

        y_gla = _gla(gqk, gg, gv, go, row(jnp.tile(gla_out_norm[i], N_GLA_HEADS)), B)
        lam_init = 0.8 - 0.6 * math.exp(-0.3 * i)
        score_bound = LOG2E * (DIFF_DQK ** 0.5 * jnp.max(jnp.abs(diff_q_norm[i])) * jnp.max(jnp.abs(diff_k_norm[i]))
                               + jnp.max(jnp.abs(rel_bias)))
        conv_w = jnp.concatenate([conv_dw_w[i], jnp.zeros((1, CONV_CH), F32)], axis=0)
        y_diff, y_conv = _diff_attn(rb_flat, score_bound.reshape(1), dq, dk, dvt, band, diff_lambda[i],
                                    diff_out_norm[i].reshape(-1, 1), cu, conv_w, row(conv_dw_b[i]),
                                    row(conv_norm_g[i]), row(conv_norm_b[i]), B, lam_init)

        xs = _post(xs, y_gla, y_diff, y_conv, p_rows, w_out, row(ffn2_norm[i]), *ffn2_w,
                   row(ple_norm[i]), ple_w_gate, ple_w_proj, i)
    return xs.reshape(B, L, D_MODEL)
```

```python
import functools
import math

import jax
import jax.numpy as jnp
from jax import lax
from jax.experimental import pallas as pl
from jax.experimental.pallas import tpu as pltpu

F32 = jnp.float32
BF16 = jnp.bfloat16

D_MODEL = 1024
N_GLA_HEADS = 4
GLA_DK = 32
GLA_DV = 64
GLA_Q = N_GLA_HEADS * GLA_DK
GLA_V = N_GLA_HEADS * GLA_DV
GLA_RANK = 16
GATE_TEMP = 16.0
GLA_CHUNK = 64
GLA_BLOCK = 16
GLA_CUM_CHUNKS = 4
DIFF_DQK = 64
N_DIFF_HEADS = 4
DIFF_DV = 128
DIFF_QK = 512
DIFF_V = 512
NUM_BUCKETS = 32
MAX_DISTANCE = 128
CONV_CH = 256
CONV_K = 31
D_FF = 2816
PLE_DIM = 256
NORM_EPS = 1e-6
LOG2E = math.log2(math.e)
EXP2_SAFE_RANGE = 64.0
D_IN = 2848

TOKEN_TILE = 512
INPROJ_TILE = 1024
INPROJ_SUB = 256
FF_CHUNKS = (1536, 1280)
ATTN_TILE = 256
ATTN_LOOKAHEAD = 3
ATTN_UNROLL = 4
CONV_PAD = 16
V7X_VMEM_BYTES = 64 * 1024 * 1024
MXU_TILE = 256
VMEM_LIMIT = V7X_VMEM_BYTES * 7 // 8
WEIGHT_CHUNK_ROWS_WIDE = 128
WEIGHT_CHUNK_ROWS_TALL = 256
WEIGHT_STAGE_SLOTS = 3


def _cparams(sem):
    return pltpu.CompilerParams(dimension_semantics=sem, vmem_limit_bytes=VMEM_LIMIT)


def _dot(a, b):
    return jnp.dot(a, b, preferred_element_type=F32)


def _dot_nt(a, b):
    return lax.dot_general(a, b, (((1,), (1,)), ((), ())), preferred_element_type=F32)


def _rms_rows(x, gain):
    return x * lax.rsqrt(jnp.mean(x * x, axis=-1, keepdims=True) + NORM_EPS) * gain


def _sigmoid(x):
    return 1.0 / (1.0 + jnp.exp(-x))


def _silu(x):
    return x * _sigmoid(x)


def _const_spec(shape):
    return pl.BlockSpec(shape, lambda *_: (0,) * len(shape), pipeline_mode=pl.Buffered(1))


def _layer_spec(shape, layer):
    return pl.BlockSpec((None,) + shape, lambda *_: (layer,) + (0,) * len(shape), pipeline_mode=pl.Buffered(1))


def _hbm_spec():
    return pl.BlockSpec(memory_space=pl.ANY)


def _stage_scratch(cols):
    rows = WEIGHT_CHUNK_ROWS_WIDE if cols > D_MODEL else WEIGHT_CHUNK_ROWS_TALL
    return [pltpu.VMEM((WEIGHT_STAGE_SLOTS, rows, cols), F32), pltpu.SemaphoreType.DMA((WEIGHT_STAGE_SLOTS,))]


def _fetch_as_bf16(src_hbm, layer, dst_ref, stage_ref, sem_ref):
    slots = stage_ref.shape[0]
    chunk = min(stage_ref.shape[1], dst_ref.shape[0])
    assert dst_ref.shape[0] % chunk == 0
    n = dst_ref.shape[0] // chunk

    def dma(c):
        return pltpu.make_async_copy(src_hbm.at[layer, pl.ds(c * chunk, chunk), :],
                                     stage_ref.at[c % slots, pl.ds(0, chunk), :], sem_ref.at[c % slots])

    for c in range(min(slots - 1, n)):
        dma(c).start()
    for c in range(n):
        if c + slots - 1 < n:
            dma(c + slots - 1).start()
        dma(c).wait()
        dst_ref[c * chunk:(c + 1) * chunk, :] = stage_ref[c % slots, :chunk, :].astype(BF16)


def _swiglu_half_step(x, g_ref, wg_ref, wu_ref, wd_ref):
    h = _rms_rows(x, g_ref[...]).astype(BF16)
    acc = jnp.zeros(x.shape, F32)
    c0 = 0
    for width in FF_CHUNKS:
        gate = _dot(h, wg_ref[:, c0:c0 + width])
        up = _dot(h, wu_ref[:, c0:c0 + width])
        a = (_silu(gate) * up).astype(BF16)
        acc = acc + _dot(a, wd_ref[c0:c0 + width, :])
        c0 += width
    return x + 0.5 * acc


def _ffn_kernel(x_ref, g_ref, wg_hbm, wu_hbm, wd_hbm, o_ref,
                wg_ref, wu_ref, wd_ref, wide_ref, wide_sem, tall_ref, tall_sem, *, layer):
    @pl.when(pl.program_id(0) == 0)
    def _():
        _fetch_as_bf16(wg_hbm, layer, wg_ref, wide_ref, wide_sem)
        _fetch_as_bf16(wu_hbm, layer, wu_ref, wide_ref, wide_sem)
        _fetch_as_bf16(wd_hbm, layer, wd_ref, tall_ref, tall_sem)

    o_ref[...] = _swiglu_half_step(x_ref[...], g_ref, wg_ref, wu_ref, wd_ref)


def _ffn(x, gain, wg, wu, wd, layer):
    n = x.shape[0]
    row = pl.BlockSpec((TOKEN_TILE, D_MODEL), lambda i: (i, 0))
    return pl.pallas_call(
        functools.partial(_ffn_kernel, layer=layer),
        grid=(n // TOKEN_TILE,),
        in_specs=[row, _const_spec((1, D_MODEL)), _hbm_spec(), _hbm_spec(), _hbm_spec()],
        out_specs=row,
        out_shape=jax.ShapeDtypeStruct((n, D_MODEL), F32),
        scratch_shapes=[pltpu.VMEM((D_MODEL, D_FF), BF16), pltpu.VMEM((D_MODEL, D_FF), BF16),
                        pltpu.VMEM((D_FF, D_MODEL), BF16)] + _stage_scratch(D_FF) + _stage_scratch(D_MODEL),
        compiler_params=_cparams(("arbitrary",)),
        name="ffn",
    )(x, gain, wg, wu, wd)


def _block_diag_ones(n, blk, dtype):
    r = lax.broadcasted_iota(jnp.int32, (n, n), 0) // blk
    c = lax.broadcasted_iota(jnp.int32, (n, n), 1) // blk
    return jnp.where(r == c, 1.0, 0.0).astype(dtype)


def _group_mean_sq(x, blk):
    sq = (x * x).astype(BF16)
    ones = _block_diag_ones(MXU_TILE, blk, BF16)
    parts = [_dot(sq[:, c:c + MXU_TILE], ones) for c in range(0, x.shape[-1], MXU_TILE)]
    return jnp.concatenate(parts, axis=-1) * (1.0 / blk)


def _log_sigmoid(z):
    return jnp.minimum(z, 0.0) - jnp.log(1.0 + jnp.exp(-jnp.abs(z)))


def _inproj_kernel(x_ref, g_ref, w_ref, upf_ref, bf_ref, upb_ref, bb_ref, qn_ref, kn_ref,
                   gqk_ref, gg_ref, gv_ref, go_ref, dq_ref, dk_ref, dvt_ref, cu_ref):
    tm = x_ref.shape[0]
    nsub = tm // INPROJ_SUB

    def project(s):
        rows = slice(s * INPROJ_SUB, (s + 1) * INPROJ_SUB)
        h = _rms_rows(x_ref[rows, :], g_ref[...]).astype(BF16)
        return _dot(h, w_ref[...])

    def prepare(s, proj):
        rows = slice(s * INPROJ_SUB, (s + 1) * INPROJ_SUB)
        gqk_ref[rows, :GLA_Q] = proj[:, :GLA_Q] * (GLA_DK ** -0.5)
        gqk_ref[rows, GLA_Q:] = proj[:, GLA_Q:2 * GLA_Q]
        gv_ref[rows, :] = proj[:, 256:512].astype(BF16)
        go_ref[rows, :] = _silu(proj[:, 512:768]).astype(BF16)
        low = proj[:, 768:896].astype(BF16)
        zf = _dot(low, upf_ref[...]) + bf_ref[...]
        zb = _dot(low, upb_ref[...]) + bb_ref[...]
        gg_ref[rows, :GLA_Q] = _log_sigmoid(zf) / GATE_TEMP
        gg_ref[rows, GLA_Q:] = _log_sigmoid(zb) / GATE_TEMP
        rest = proj[:, 768 + 2 * GLA_RANK:]
        dq = rest[:, :512]
        dq_ref[rows, :] = (dq * lax.rsqrt(_group_mean_sq(dq, DIFF_DQK) + NORM_EPS) * qn_ref[...]
                           * (DIFF_DQK ** -0.5 * LOG2E)).astype(BF16)
        dk = rest[:, 512:1024]
        dk_ref[rows, :] = (dk * lax.rsqrt(_group_mean_sq(dk, DIFF_DQK) + NORM_EPS) * kn_ref[...]).astype(BF16)
        dvt_ref[:, rows] = rest[:, 1024:1536].T.astype(BF16)
        cu_ref[rows, :] = rest[:, 1536:1792] * _sigmoid(rest[:, 1792:2048])

    nxt = project(0)
    for s in range(nsub):
        cur = nxt
        if s + 1 < nsub:
            nxt = project(s + 1)
        prepare(s, cur)


def _inproj(x, gain, w, upf, bf, upb, bb, qn, kn, layer):
    n = x.shape[0]
    tm = INPROJ_TILE

    def row(width):
        return pl.BlockSpec((tm, width), lambda i: (i, 0))

    outs = [(2 * GLA_Q, F32), (2 * GLA_Q, F32), (GLA_V, BF16), (GLA_V, BF16),
            (DIFF_QK, BF16), (DIFF_QK, BF16), None, (CONV_CH, F32)]
    dvt_spec = pl.BlockSpec((DIFF_V, tm), lambda i: (0, i))
    return pl.pallas_call(
        _inproj_kernel,
        grid=(n // tm,),
        in_specs=[row(D_MODEL), _const_spec((1, D_MODEL)), _layer_spec((D_MODEL, D_IN), layer),
                  _const_spec((128, GLA_Q)), _const_spec((1, GLA_Q)),
                  _const_spec((128, GLA_Q)), _const_spec((1, GLA_Q)),
                  _const_spec((1, DIFF_QK)), _const_spec((1, DIFF_QK))],
        out_specs=[row(o[0]) if o else dvt_spec for o in outs],
        out_shape=[jax.ShapeDtypeStruct((n, o[0]), o[1]) if o else jax.ShapeDtypeStruct((DIFF_V, n), BF16)
                   for o in outs],
        compiler_params=_cparams(("parallel",)),
        name="inproj",
    )(x, gain, w, upf, bf, upb, bb, qn, kn)


def _gla_block_kernel(qk_ref, g_ref, v_ref, go_ref, on_ref, y_ref, of_ref, st_ref):
    C = GLA_CHUNK
    G = GLA_BLOCK
    R = G * C
    L = qk_ref.shape[0]
    nblk = L // R
    H = N_GLA_HEADS

    def iota(shape, d):
        return lax.broadcasted_iota(jnp.int32, shape, d)

    kmask = iota((H * C, GLA_Q), 0) // C == iota((H * C, GLA_Q), 1) // GLA_DK
    vmask = iota((H * C, GLA_V), 0) // C == iota((H * C, GLA_V), 1) // GLA_DV
    smask = iota((GLA_V, GLA_Q), 0) // GLA_DV == iota((GLA_V, GLA_Q), 1) // GLA_DK
    pos_a = iota((C, H * C), 0)
    pos_b = iota((C, H * C), 1) % C
    RC = GLA_CUM_CHUNKS * C
    ri, ci = iota((RC, RC), 0), iota((RC, RC), 1)
    same_chunk = ri // C == ci // C
    cums = [jnp.where(jnp.logical_and(same_chunk, ci <= ri), 1.0, 0.0).astype(BF16),
            jnp.where(jnp.logical_and(same_chunk, ci >= ri), 1.0, 0.0).astype(BF16)]
    amasks = [pos_b <= pos_a, pos_b >= pos_a]
    edges = [C - 1, 0]
    orders = [list(range(G)), list(range(G - 1, -1, -1))]

    st_ref[...] = jnp.zeros(st_ref.shape, F32)

    def step(t, _):
        r0 = [pl.multiple_of(t * R, R), pl.multiple_of((nblk - 1 - t) * R, R)]
        rows = [pl.ds(r0[0], R), pl.ds(r0[1], R)]
        dirs = (0, 1)
        b, k, v, qt = {}, {}, {}, {}
        for d in dirs:
            g = g_ref[rows[d], d * GLA_Q:(d + 1) * GLA_Q]
            g_hi = g.astype(BF16)
            g_lo = (g - g_hi.astype(F32)).astype(BF16)
            b[d] = jnp.concatenate([_dot(cums[d], g_hi[i:i + RC]) + _dot(cums[d], g_lo[i:i + RC])
                                    for i in range(0, R, RC)], axis=0)
        a, ds, decay = {}, {}, {}
        for d in dirs:
            q = qk_ref[rows[d], :GLA_Q]
            k[d] = qk_ref[rows[d], GLA_Q:]
            v[d] = v_ref[rows[d], :].astype(F32)
            qt[d] = (q * jnp.exp(b[d])).astype(BF16)
            kt = k[d] * jnp.exp(-b[d])
            for c in range(G):
                sl = slice(c * C, (c + 1) * C)
                b_c = b[d][sl]
                b_edge = b_c[edges[d]:edges[d] + 1, :]
                decay[d, c] = jnp.exp(b_edge)
                kend = (k[d][sl] * jnp.exp(b_edge - b_c)).astype(BF16)
                kstack = jnp.where(kmask, jnp.concatenate([kt[sl]] * H, axis=0), 0.0).astype(BF16)
                a[d, c] = jnp.where(amasks[d], _dot_nt(qt[d][sl], kstack), 0.0).astype(BF16)
                ds[d, c] = _dot(v[d][sl].T.astype(BF16), kend)
        for d in dirs:
            s = st_ref[d]
            s_in = {}
            for c in orders[d]:
                s_in[c] = s.astype(BF16)
                s = s * decay[d, c] + jnp.where(smask, ds[d, c], 0.0)
            st_ref[d] = s
            out_ref = y_ref if d else of_ref
            for c in range(G):
                sl = slice(c * C, (c + 1) * C)
                vstack = jnp.where(vmask, jnp.concatenate([v[d][sl]] * H, axis=0), 0.0).astype(BF16)
                o = _dot(a[d, c], vstack) + _dot_nt(qt[d][sl], s_in[c])
                out_ref[pl.ds(pl.multiple_of(r0[d] + c * C, C), C), :] = o
        return 0

    lax.fori_loop(0, nblk, step, 0, unroll=2)

    norm_bd = _block_diag_ones(GLA_V, GLA_DV, BF16)

    def finish(t, _):
        rows = pl.ds(pl.multiple_of(t * R, R), R)
        tot = of_ref[rows, :] + y_ref[rows, :]
        ms = _dot((tot * tot).astype(BF16), norm_bd) * (1.0 / GLA_DV)
        y_ref[rows, :] = tot * lax.rsqrt(ms + NORM_EPS) * on_ref[...] * go_ref[rows, :].astype(F32)
        return 0

    lax.fori_loop(0, nblk, finish, 0)


def _gla(qk, g, v, go, onorm, batch):
    n = qk.shape[0]
    L = n // batch

    def seq(width):
        return pl.BlockSpec((L, width), lambda b: (b, 0))

    return pl.pallas_call(
        _gla_block_kernel,
        grid=(batch,),
        in_specs=[seq(2 * GLA_Q), seq(2 * GLA_Q), seq(GLA_V), seq(GLA_V), _const_spec((1, GLA_V))],
        out_specs=seq(GLA_V),
        out_shape=jax.ShapeDtypeStruct((n, GLA_V), F32),
        scratch_shapes=[pltpu.VMEM((L, GLA_V), F32), pltpu.VMEM((2, GLA_V, GLA_Q), F32)],
        compiler_params=_cparams(("parallel",)),
        name="gla",
    )(qk, g, v, go, onorm)


def _bias_kernel(rb_ref, o_ref):
    T = ATTN_TILE
    h = pl.program_id(0)
    nb = NUM_BUCKETS // 2
    max_exact = nb // 2
    lane = lax.broadcasted_iota(jnp.int32, (8, 2 * T), 1)
    for d in range(5):
        rel = (d - 2) * T + T - lane
        ret = jnp.where(rel > 0, nb, 0)
        n = jnp.abs(rel)
        large = max_exact + (jnp.log(jnp.maximum(n, 1).astype(F32) / max_exact)
                             / math.log(MAX_DISTANCE / max_exact) * (nb - max_exact)).astype(jnp.int32)
        large = jnp.minimum(large, nb - 1)
        bucket = ret + jnp.where(n < max_exact, n, large)
        for m in range(2):
            row = jnp.zeros((8, 2 * T), F32)
            for bkt in range(NUM_BUCKETS):
                row = jnp.where(bucket == bkt, rb_ref[(bkt * N_DIFF_HEADS + h) * 2 + m], row)
            rows = jnp.broadcast_to(row[0:1, :] * (-LOG2E), (T, 2 * T))
            o_ref[0, d, m] = pltpu.roll(rows, 0, 1, stride=1, stride_axis=0)[:, T:]


def _bias_tiles(rel_bias):
    T = ATTN_TILE
    return pl.pallas_call(
        _bias_kernel,
        grid=(N_DIFF_HEADS,),
        in_specs=[pl.BlockSpec(memory_space=pltpu.SMEM)],
        out_specs=pl.BlockSpec((1, 5, 2, T, T), lambda h: (h, 0, 0, 0, 0)),
        out_shape=jax.ShapeDtypeStruct((N_DIFF_HEADS, 5, 2, T, T), F32),
        compiler_params=_cparams(("parallel",)),
        name="t5_bias",
    )(rel_bias.reshape(-1))


def _stage_conv_window(u_ref, uw_ref):
    L = u_ref.shape[0]
    P = CONV_PAD
    rows = L // N_DIFF_HEADS
    h = pl.program_id(0)
    start = pl.multiple_of(h * rows, rows)
    uw_ref[P:P + rows, :] = u_ref[pl.ds(start, rows), :]
    above = u_ref[pl.ds(pl.multiple_of(jnp.maximum(start - P, 0), P), P), :]
    uw_ref[:P, :] = jnp.where(h > 0, above, 0.0)
    below = u_ref[pl.ds(pl.multiple_of(jnp.minimum(start + rows, L - P), P), P), :]
    uw_ref[P + rows:, :] = jnp.where(h < N_DIFF_HEADS - 1, below, 0.0)


def _conv_rows(r0, R, uw_ref, w_ref, b_ref, g_ref, beta_ref, y_ref):
    P = CONV_PAD
    off = P - CONV_K // 2
    win = uw_ref[pl.ds(r0, R + 2 * P), :]
    acc = jnp.zeros((R, CONV_CH), F32) + b_ref[...]
    for r in range(8):
        shifted = win if r == 0 else pltpu.roll(win, R + 2 * P - r, 0)
        for k in range(CONV_K):
            if (off + k) % 8 == r:
                a8 = (off + k) // 8
                acc = acc + w_ref[k:k + 1, :] * shifted[8 * a8:8 * a8 + R, :]
    mu = jnp.mean(acc, axis=-1, keepdims=True)
    xc = acc - mu
    y = xc * lax.rsqrt(jnp.mean(xc * xc, axis=-1, keepdims=True) + NORM_EPS) * g_ref[...] + beta_ref[...]
    y_ref[pl.ds(r0, R), :] = _silu(y)


def _attn_kernel(*refs, lam_init):
    bound_ref = refs[1]
    nq = refs[2].shape[0] // ATTN_TILE
    unshifted_ok = bound_ref[0] <= EXP2_SAFE_RANGE
    _stage_conv_window(refs[8], refs[-1])

    def sweep(unshifted, group):
        conv_rows = refs[8].shape[0] // N_DIFF_HEADS // nq * group

        def body(g, _):
            for u in range(group):
                _attn_query_tile(g * group + u, *refs, lam_init=lam_init, unshifted=unshifted)
            _conv_rows(pl.multiple_of(g * conv_rows, conv_rows), conv_rows, refs[-1], *refs[9:13], refs[14])
            return 0
        lax.fori_loop(0, nq // group, body, 0)

    @pl.when(unshifted_ok)
    def _():
        sweep(True, ATTN_UNROLL)

    @pl.when(jnp.logical_not(unshifted_ok))
    def _():
        sweep(False, 1)


def _attn_query_tile(qi, rb_ref, bound_ref, q_ref, k_ref, vt_ref, band_ref, lam_ref, on_ref,
                     u_ref, cw_ref, cb_ref, cg_ref, cbeta_ref, o_ref, yc_ref,
                     m_ref, l_ref, acc_ref, uw_ref, *, lam_init, unshifted):
    T = ATTN_TILE
    L = k_ref.shape[0]
    nk = L // T
    h = pl.program_id(0)
    q = q_ref[pl.ds(pl.multiple_of(qi * T, T), T), :]
    qt = q.astype(F32).T
    chan = lax.broadcasted_iota(jnp.int32, qt.shape, 0)
    qm = [jnp.where(chan < DIFF_DQK, qt, 0.0).astype(BF16), jnp.where(chan >= DIFF_DQK, qt, 0.0).astype(BF16)]

    def side_const(bucket, m):
        return rb_ref[(bucket * N_DIFF_HEADS + h) * 2 + m] * LOG2E

    def tile_start(j):
        return pl.multiple_of(j * T, T)

    def unshifted_softmax():
        f_left = [jnp.exp2(jnp.full((1, 1), side_const(NUM_BUCKETS // 2 - 1, m), F32)).astype(BF16)
                  for m in range(2)]
        f_right = [jnp.exp2(jnp.full((1, 1), side_const(NUM_BUCKETS - 1, m), F32)).astype(BF16)
                   for m in range(2)]
        band_idx = [jnp.where(qi == 0, 4, 1), 2, jnp.where(qi == nk - 1, 0, 3)]
        l = [jnp.zeros((1, T), F32) for _ in range(2)]
        acc = [jnp.zeros((DIFF_DV, T), F32) for _ in range(2)]

        def scores(t):
            kt = k_ref[pl.ds(tile_start((qi + (2 + t)) % nk), T), :]
            return [_dot(kt, qm[m]) for m in range(2)]

        pending = [scores(t) for t in range(ATTN_LOOKAHEAD)]
        for t in range(nk):
            j = (qi + (2 + t)) % nk
            vt = vt_ref[:, pl.ds(tile_start(j), T)]
            s_cur = pending.pop(0)
            if t + ATTN_LOOKAHEAD < nk:
                pending.append(scores(t + ATTN_LOOKAHEAD))
            for m in range(2):
                if t >= nk - 3:
                    p = jnp.exp2(s_cur[m] - band_ref[0, band_idx[t - (nk - 3)], m])
                    l[m] = l[m] + jnp.sum(p, axis=0, keepdims=True)
                    acc[m] = acc[m] + _dot(vt, p.astype(BF16))
                else:
                    f = jnp.where(j < qi, f_left[m], f_right[m])
                    p = jnp.exp2(s_cur[m])
                    l[m] = l[m] + f.astype(F32) * jnp.sum(p, axis=0, keepdims=True)
                    acc[m] = acc[m] + _dot(vt * f, p.astype(BF16))
        for m in range(2):
            l_ref[m] = l[m]
            acc_ref[m] = acc[m]

    def online_softmax():
        m_ref[...] = jnp.full(m_ref.shape, -jnp.inf, F32)
        l_ref[...] = jnp.zeros(l_ref.shape, F32)
        acc_ref[...] = jnp.zeros(acc_ref.shape, F32)

        def body(j, _):
            r0 = tile_start(j)
            kt = k_ref[pl.ds(r0, T), :]
            vt = vt_ref[:, pl.ds(r0, T)]
            idx = jnp.clip(j - qi, -2, 2) + 2
            ss = [_dot(kt, qm[m]) for m in range(2)]
            for m in range(2):
                s = ss[m] - band_ref[0, idx, m]
                m_old = m_ref[m]
                m_new = jnp.maximum(m_old, jnp.max(s, axis=0, keepdims=True))
                alpha = jnp.exp2(m_old - m_new)
                p = jnp.exp2(s - m_new)
                l_ref[m] = alpha * l_ref[m] + jnp.sum(p, axis=0, keepdims=True)
                acc_ref[m] = alpha * acc_ref[m] + _dot(vt, p.astype(BF16))
                m_ref[m] = m_new
            return 0

        lax.fori_loop(0, nk, body, 0)

    if unshifted:
        unshifted_softmax()
    else:
        online_softmax()

    lp = lam_ref[...]
    lam = (jnp.exp(jnp.sum(lp[0:1] * lp[1:2], axis=-1, keepdims=True))
           - jnp.exp(jnp.sum(lp[2:3] * lp[3:4], axis=-1, keepdims=True)) + lam_init)
    o = acc_ref[0] / l_ref[0] - lam * (acc_ref[1] / l_ref[1])
    y = o * lax.rsqrt(jnp.mean(o * o, axis=0, keepdims=True) + NORM_EPS) * on_ref[...] * (1.0 - lam_init)
    o_ref[pl.ds(pl.multiple_of(qi * T, T), T), :] = y.T.astype(o_ref.dtype)


def _diff_attn(rel_bias_flat, score_bound, dq, dk, dvt, band, lam_p, onorm, cu, conv_w, conv_b, conv_g,
               conv_beta, batch, lam_init):
    n = dq.shape[0]
    L = n // batch
    T = ATTN_TILE
    share = L // N_DIFF_HEADS
    seq = pl.BlockSpec((L, 2 * DIFF_DQK), lambda h, b: (b, h))

    def const(shape):
        return pl.BlockSpec(shape, lambda h, b: (0,) * len(shape))

    return pl.pallas_call(
        functools.partial(_attn_kernel, lam_init=lam_init),
        grid=(N_DIFF_HEADS, batch),
        in_specs=[pl.BlockSpec(memory_space=pltpu.SMEM),
                  pl.BlockSpec(memory_space=pltpu.SMEM),
                  seq, seq,
                  pl.BlockSpec((DIFF_DV, L), lambda h, b: (h, b)),
                  pl.BlockSpec((1, 5, 2, T, T), lambda h, b: (h, 0, 0, 0, 0)),
                  const((4, DIFF_DQK)), const((DIFF_DV, 1)),
                  pl.BlockSpec((L, CONV_CH), lambda h, b: (b, 0)),
                  const((CONV_K + 1, CONV_CH)), const((1, CONV_CH)), const((1, CONV_CH)), const((1, CONV_CH))],
        out_specs=[pl.BlockSpec((L, DIFF_DV), lambda h, b: (b, h)),
                   pl.BlockSpec((share, CONV_CH), lambda h, b: (b * N_DIFF_HEADS + h, 0))],
        out_shape=[jax.ShapeDtypeStruct((n, DIFF_V), BF16), jax.ShapeDtypeStruct((n, CONV_CH), F32)],
        scratch_shapes=[pltpu.VMEM((2, 1, T), F32), pltpu.VMEM((2, 1, T), F32),
                        pltpu.VMEM((2, DIFF_DV, T), F32),
                        pltpu.VMEM((share + 2 * CONV_PAD, CONV_CH), F32)],
        compiler_params=_cparams(("parallel", "parallel")),
        name="diff_attn",
    )(rel_bias_flat, score_bound, dq, dk, dvt, band, lam_p, onorm, cu, conv_w, conv_b, conv_g, conv_beta)


def _post_kernel(x_ref, ya_ref, yb_ref, yc_ref, p_ref, wo_hbm, fg_ref, wg_hbm, wu_hbm, wd_hbm,
                 pg_ref, pwg_hbm, pwp_hbm, o_ref,
                 wo_ref, wg_ref, wu_ref, wd_ref, pwg_ref, pwp_ref, wide_ref, wide_sem, tall_ref, tall_sem,
                 *, layer):
    @pl.when(pl.program_id(0) == 0)
    def _():
        _fetch_as_bf16(wo_hbm, layer, wo_ref, tall_ref, tall_sem)
        _fetch_as_bf16(wg_hbm, layer, wg_ref, wide_ref, wide_sem)
        _fetch_as_bf16(wu_hbm, layer, wu_ref, wide_ref, wide_sem)
        _fetch_as_bf16(wd_hbm, layer, wd_ref, tall_ref, tall_sem)
        _fetch_as_bf16(pwg_hbm, layer, pwg_ref, tall_ref, tall_sem)
        _fetch_as_bf16(pwp_hbm, layer, pwp_ref, tall_ref, tall_sem)

    mix = _dot(ya_ref[...].astype(BF16), wo_ref[:GLA_V, :])
    mix = mix + _dot(yb_ref[...].astype(BF16), wo_ref[GLA_V:GLA_V + DIFF_V, :])
    mix = mix + _dot(yc_ref[...].astype(BF16), wo_ref[GLA_V + DIFF_V:, :])
    x = _swiglu_half_step(x_ref[...] + mix, fg_ref, wg_ref, wu_ref, wd_ref)
    h = _rms_rows(x, pg_ref[...]).astype(BF16)
    gate = _sigmoid(_dot(h, pwg_ref[...]))
    o_ref[...] = x + gate * _dot(p_ref[...].astype(BF16), pwp_ref[...])


def _post(x, ya, yb, yc, p, w_out, ffn_gain, wg, wu, wd, ple_gain, ple_wg, ple_wp, layer):
    n = x.shape[0]
    tm = TOKEN_TILE

    def row(width):
        return pl.BlockSpec((tm, width), lambda i: (i, 0))

    return pl.pallas_call(
        functools.partial(_post_kernel, layer=layer),
        grid=(n // tm,),
        in_specs=[row(D_MODEL), row(GLA_V), row(DIFF_V), row(CONV_CH),
                  pl.BlockSpec((None, tm, PLE_DIM), lambda i: (layer, i, 0)),
                  _hbm_spec(), _const_spec((1, D_MODEL)), _hbm_spec(), _hbm_spec(), _hbm_spec(),
                  _const_spec((1, D_MODEL)), _hbm_spec(), _hbm_spec()],
        out_specs=row(D_MODEL),
        out_shape=jax.ShapeDtypeStruct((n, D_MODEL), F32),
        scratch_shapes=[pltpu.VMEM((D_MODEL, D_MODEL), BF16), pltpu.VMEM((D_MODEL, D_FF), BF16),
                        pltpu.VMEM((D_MODEL, D_FF), BF16), pltpu.VMEM((D_FF, D_MODEL), BF16),
                        pltpu.VMEM((D_MODEL, D_MODEL), BF16), pltpu.VMEM((PLE_DIM, D_MODEL), BF16)]
        + _stage_scratch(D_FF) + _stage_scratch(D_MODEL),
        compiler_params=_cparams(("arbitrary",)),
        name="post",
    )(x, ya, yb, yc, p, w_out, ffn_gain, wg, wu, wd, ple_gain, ple_wg, ple_wp)


def _pad_up(up, row0):
    return jnp.zeros((128, GLA_Q), F32).at[row0:row0 + GLA_RANK].set(up).astype(BF16)


def kernel(x, p, ffn1_norm, ffn1_w_gate, ffn1_w_up, ffn1_w_down, mix_norm, w_in, w_out, gla_gk_up_f, gla_gk_bias_f, gla_gk_up_b, gla_gk_bias_b, gla_out_norm, diff_q_norm, diff_k_norm, diff_lambda, diff_out_norm, rel_bias, conv_dw_w, conv_dw_b, conv_norm_g, conv_norm_b, ffn2_norm, ffn2_w_gate, ffn2_w_up, ffn2_w_down, ple_norm, ple_w_gate, ple_w_proj):
    B, L, _ = x.shape
    depth = w_in.shape[0]
    n = B * L
    xs = x.reshape(n, D_MODEL)
    rb_flat = rel_bias.reshape(-1)
    band = _bias_tiles(rel_bias)

    def row(v):
        return v.reshape(1, -1)

    ffn1_w = (ffn1_w_gate, ffn1_w_up, ffn1_w_down)
    ffn2_w = (ffn2_w_gate, ffn2_w_up, ffn2_w_down)
    w_in_b = w_in.astype(BF16)
    p_rows = p.reshape(depth, n, PLE_DIM)

    for i in range(depth):
        xs = _ffn(xs, row(ffn1_norm[i]), *ffn1_w, i)

        gqk, gg, gv, go, dq, dk, dvt, cu = _inproj(
            xs, row(mix_norm[i]), w_in_b,
            _pad_up(gla_gk_up_f[i], 0), row(gla_gk_bias_f[i]),
            _pad_up(gla_gk_up_b[i], GLA_RANK), row(gla_gk_bias_b[i]),
            row(jnp.tile(diff_q_norm[i], DIFF_QK // DIFF_DQK)), row(jnp.tile(diff_k_norm[i], DIFF_QK // DIFF_DQK)),
            i)

        y_gla = _gla(gqk, gg, gv, go, row(jnp.tile(gla_out_norm[i], N_GLA_HEADS)), B)
        lam_init = 0.8 - 0.6 * math.exp(-0.3 * i)
        score_bound = LOG2E * (DIFF_DQK ** 0.5 * jnp.max(jnp.abs(diff_q_norm[i])) * jnp.max(jnp.abs(diff_k_norm[i]))
                               + jnp.max(jnp.abs(rel_bias)))
        conv_w = jnp.concatenate([conv_dw_w[i], jnp.zeros((1, CONV_CH), F32)], axis=0)
        y_diff, y_conv = _diff_attn(rb_flat, score_bound.reshape(1), dq, dk, dvt, band, diff_lambda[i],
                                    diff_out_norm[i].reshape(-1, 1), cu, conv_w, row(conv_dw_b[i]),
                                    row(conv_norm_g[i]), row(conv_norm_b[i]), B, lam_init)

        xs = _post(xs, y_gla, y_diff, y_conv, p_rows, w_out, row(ffn2_norm[i]), *ffn2_w,
                   row(ple_norm[i]), ple_w_gate, ple_w_proj, i)
    return xs.reshape(B, L, D_MODEL)
```
